```python
import jax, jax.numpy as jnp
from jax import lax
import numpy as np

D_MODEL = 1024
BATCH = 4
SEQ = 8192
DEPTH = 1

D_CONV = D_MODEL
CONV_WIDTH = 3
CONV_GROUPS = 8
D_POOL = D_MODEL
POOL_WINDOWS = (2, 4, 8, 16)
N_POOL_GROUPS = len(POOL_WINDOWS)
POOL_GROUP = D_POOL // N_POOL_GROUPS
IN_COLS = 3 * D_CONV + D_POOL + 2 * D_MODEL
IN_SPLITS = (D_CONV, 2 * D_CONV, 3 * D_CONV, 3 * D_CONV + D_POOL, 3 * D_CONV + D_POOL + D_MODEL)
N_GROUPS = 8
EXPERTS_PER_GROUP = 8
N_EXPERTS = N_GROUPS * EXPERTS_PER_GROUP
TOP_K = 2
D_EXPERT = 512
MOE_BLOCK = 128
LN_EPS = 1e-5
DEEPNORM_ALPHA = (2.0 * DEPTH) ** 0.25
DEEPNORM_BETA = (8.0 * DEPTH) ** -0.25
N_MOD = 6

kernel_name = "hybrid_conv_pool_hmoe_deepnorm_adaln"


def layer_norm(x, g, b):
    xf = x.astype(jnp.float32)
    mu = jnp.mean(xf, axis=-1, keepdims=True)
    var = jnp.mean(jnp.square(xf - mu), axis=-1, keepdims=True)
    y = (xf - mu) * lax.rsqrt(var + LN_EPS) * g.astype(jnp.float32) + b.astype(jnp.float32)
    return y.astype(x.dtype)


def causal_depthwise_conv(z, w):
    return lax.conv_general_dilated(
        z, w[:, None, :].astype(z.dtype), window_strides=(1,),
        padding=((CONV_WIDTH - 1, 0),), dimension_numbers=('NWC', 'WIO', 'NWC'),
        feature_group_count=z.shape[-1])


def causal_multiscale_pool(z):
    S = z.shape[1]
    zf = z.astype(jnp.float32)
    cs = jnp.pad(jnp.cumsum(zf, axis=1), ((0, 0), (1, 0), (0, 0)))
    t = jnp.arange(S)
    outs = []
    for gi, w in enumerate(POOL_WINDOWS):
        c = cs[:, :, gi * POOL_GROUP:(gi + 1) * POOL_GROUP]
        lower = jnp.pad(c[:, :S + 1 - w], ((0, 0), (w - 1, 0), (0, 0)))
        count = jnp.minimum(t + 1, w).astype(jnp.float32)[None, :, None]
        outs.append((c[:, 1:] - lower) / count - zf[:, :, gi * POOL_GROUP:(gi + 1) * POOL_GROUP])
    return jnp.stack(outs, axis=2).astype(z.dtype)


def hybrid_mixer(u, w_in, conv_w, w_out_conv, w_pool, pool_scale, w_o):
    proj = jnp.einsum('bsd,de->bse', u, w_in)
    a_val, a_b, a_c, p_in, g_a, g_b = jnp.split(proj, IN_SPLITS, axis=-1)
    y_a = causal_depthwise_conv(a_c * a_val, conv_w) * a_b
    y_a = jnp.einsum('bsc,cd->bsd', y_a, w_out_conv)
    pooled = causal_multiscale_pool(p_in)
    y_b = jnp.einsum('bsgc,gce->bsge', pooled, w_pool)
    y_b = y_b.reshape(u.shape[0], u.shape[1], D_POOL) * pool_scale
    merged = jax.nn.sigmoid(g_a) * y_a + jax.nn.sigmoid(g_b) * y_b
    return jnp.einsum('bsd,de->bse', merged, w_o)


def hierarchical_route(h, w_group, b_group, w_router, b_router):
    n = h.shape[0]
    hf = h.astype(jnp.float32)
    g_prob = jax.nn.softmax(hf @ w_group.astype(jnp.float32) + b_group.astype(jnp.float32), axis=-1)
    g_top_p, g_top = lax.top_k(g_prob, 1)
    e_logits = (hf @ w_router.astype(jnp.float32)).reshape(n, N_GROUPS, EXPERTS_PER_GROUP)
    e_logits = e_logits + b_router.astype(jnp.float32)
    e_logits = jnp.take_along_axis(e_logits, g_top[:, :, None], axis=1)[:, 0]
    e_prob = jax.nn.softmax(e_logits, axis=-1)
    e_top_p, e_top = lax.top_k(e_prob, TOP_K)
    e_top_p = e_top_p / jnp.sum(e_top_p, axis=-1, keepdims=True)
    return g_top * EXPERTS_PER_GROUP + e_top, g_top_p * e_top_p


def routed_experts(h, expert_idx, expert_w, w_gate, w_up, w_down):
    n = h.shape[0]
    n_assign = n * TOP_K
    n_blocks = -(-n_assign // MOE_BLOCK) + N_EXPERTS
    cap = n_blocks * MOE_BLOCK
    flat_e = expert_idx.reshape(-1).astype(jnp.int32)
    order = jnp.argsort(flat_e)
    sorted_e = flat_e[order]
    token_of = (order // TOP_K).astype(jnp.int32)
    counts = jnp.bincount(flat_e, length=N_EXPERTS)
    padded = (counts + MOE_BLOCK - 1) // MOE_BLOCK * MOE_BLOCK
    pad_end = jnp.cumsum(padded)
    pad_start = pad_end - padded
    seg_start = jnp.cumsum(counts) - counts
    dest = pad_start[sorted_e] + jnp.arange(n_assign, dtype=jnp.int32) - seg_start[sorted_e]
    buf_tok = jnp.zeros((cap,), jnp.int32).at[dest].set(token_of)
    buf_w = jnp.zeros((cap,), h.dtype).at[dest].set(expert_w.reshape(-1)[order].astype(h.dtype))
    block_expert = jnp.minimum(
        jnp.searchsorted(pad_end, jnp.arange(n_blocks) * MOE_BLOCK, side='right'), N_EXPERTS - 1)

    def expert_block(args):
        tok, e = args
        xb = h[tok]
        act = jax.nn.silu(xb @ w_gate[e]) * (xb @ w_up[e])
        return act @ w_down[e]

    rows = lax.map(expert_block, (buf_tok.reshape(n_blocks, MOE_BLOCK), block_expert))
    rows = rows.reshape(cap, h.shape[1]) * buf_w[:, None]
    return jax.ops.segment_sum(rows, buf_tok, num_segments=n)


def setup_inputs(seed: int = 0) -> dict:
    key = jax.random.key(seed)
    ks = jax.random.split(key, 24)
    L = DEPTH
    beta = DEEPNORM_BETA

    def nrm(k, shape, s):
        return jax.random.normal(k, shape, jnp.float32) * s

    col_scale = jnp.concatenate([
        jnp.full((D_CONV,), beta, jnp.float32), jnp.ones((2 * D_CONV,), jnp.float32),
        jnp.full((D_POOL,), beta, jnp.float32), jnp.ones((2 * D_MODEL,), jnp.float32)])
    return {
        "x": nrm(ks[0], (BATCH, SEQ, D_MODEL), 1.0),
        "c": nrm(ks[1], (BATCH, D_MODEL), 1.0),
        "w_ada": nrm(ks[2], (L, D_MODEL, N_MOD * D_MODEL), 0.1 * D_MODEL ** -0.5),
        "b_ada": nrm(ks[3], (L, N_MOD * D_MODEL), 0.01),
        "w_in": nrm(ks[4], (L, D_MODEL, IN_COLS), D_MODEL ** -0.5) * col_scale,
        "conv_w": nrm(ks[5], (L, CONV_WIDTH, D_CONV), CONV_WIDTH ** -0.5),
        "w_out_conv": nrm(ks[6], (L, D_CONV, D_MODEL), beta * D_CONV ** -0.5),
        "w_pool": nrm(ks[7], (L, N_POOL_GROUPS, POOL_GROUP, POOL_GROUP), beta * POOL_GROUP ** -0.5),
        "pool_scale": 1.0 + nrm(ks[8], (L, D_POOL), 0.1),
        "w_o": nrm(ks[9], (L, D_MODEL, D_MODEL), beta * D_MODEL ** -0.5),
        "ln1_g": 1.0 + nrm(ks[10], (L, D_MODEL), 0.05),
        "ln1_b": nrm(ks[11], (L, D_MODEL), 0.02),
        "w_group": nrm(ks[12], (L, D_MODEL, N_GROUPS), D_MODEL ** -0.5),
        "b_group": nrm(ks[13], (L, N_GROUPS), 0.01),
        "w_router": nrm(ks[14], (L, D_MODEL, N_EXPERTS), D_MODEL ** -0.5),
        "b_router": nrm(ks[15], (L, N_GROUPS, EXPERTS_PER_GROUP), 0.01),
        "w_gate": nrm(ks[16], (L, N_EXPERTS, D_MODEL, D_EXPERT), D_MODEL ** -0.5),
        "w_up": nrm(ks[17], (L, N_EXPERTS, D_MODEL, D_EXPERT), beta * D_MODEL ** -0.5),
        "w_down": nrm(ks[18], (L, N_EXPERTS, D_EXPERT, D_MODEL), beta * D_EXPERT ** -0.5),
        "ln2_g": 1.0 + nrm(ks[19], (L, D_MODEL), 0.05),
        "ln2_b": nrm(ks[20], (L, D_MODEL), 0.02),
    }


def reference(x, c, w_ada, b_ada, w_in, conv_w, w_out_conv, w_pool, pool_scale, w_o,
              ln1_g, ln1_b, w_group, b_group, w_router, b_router, w_gate, w_up, w_down,
              ln2_g, ln2_b):
    bsz, seq, d = x.shape
    for l in range(DEPTH):
        mod = (jax.nn.silu(c) @ w_ada[l] + b_ada[l]).reshape(bsz, N_MOD, d)[:, :, None, :]
        shift1, scale1, gate1, shift2, scale2, gate2 = [mod[:, i] for i in range(N_MOD)]
        u = x * (1.0 + scale1) + shift1
        mix = hybrid_mixer(u, w_in[l], conv_w[l], w_out_conv[l], w_pool[l], pool_scale[l], w_o[l])
        x = layer_norm(DEEPNORM_ALPHA * x + (1.0 + gate1) * mix, ln1_g[l], ln1_b[l])
        h = (x * (1.0 + scale2) + shift2).reshape(bsz * seq, d)
        idx, wts = hierarchical_route(h, w_group[l], b_group[l], w_router[l], b_router[l])
        y = routed_experts(h, idx, wts, w_gate[l], w_up[l], w_down[l]).reshape(bsz, seq, d)
        x = layer_norm(DEEPNORM_ALPHA * x + (1.0 + gate2) * y, ln2_g[l], ln2_b[l])
    return x
```

```python
import functools

import jax
import jax.numpy as jnp
from jax import lax
from jax.experimental import pallas as pl
from jax.experimental.pallas import tpu as pltpu

F32 = jnp.float32
BF16 = jnp.bfloat16
I32 = jnp.int32

LANES = 128
SUBLANES = 8
VMEM_LIMIT_BYTES = 56 * 1024 * 1024

N_POOL_GROUPS = 4
POOL_WINDOWS = (2, 4, 8, 16)
CONV_WIDTH = 3
N_GROUPS = 8
EXPERTS_PER_GROUP = 8
N_EXPERTS = N_GROUPS * EXPERTS_PER_GROUP
TOP_K = 2
N_MOD = 6
LN_EPS = 1e-5
DEPTH = 1
DEEPNORM_ALPHA = (2.0 * DEPTH) ** 0.25

MIX_ROWS = 512
EXPERT_ROWS = 256
DISPATCH_ROWS = 2048
COMBINE_ROWS = 512
HALO = SUBLANES


def _split_bf16(a):
    hi = a.astype(BF16)
    lo = (a - hi.astype(F32)).astype(BF16)
    return hi, lo


def _dot(a, b):
    return jnp.dot(a, b, preferred_element_type=F32)


def _dot3(a, b):
    a_hi, a_lo = _split_bf16(a)
    b_hi, b_lo = _split_bf16(b)
    return _dot(a_hi, b_hi) + _dot(a_lo, b_hi) + _dot(a_hi, b_lo)


def _layer_norm(r, g, b):
    mu = jnp.mean(r, axis=-1, keepdims=True)
    d = r - mu
    var = jnp.mean(d * d, axis=-1, keepdims=True)
    return d * lax.rsqrt(var + LN_EPS) * g + b


def _rows_to_tiles(ref, val):
    rows = val.shape[0]
    for c in range(SUBLANES):
        ref[pl.ds(c, rows, stride=SUBLANES), :] = val[:, c * LANES:(c + 1) * LANES]


def _tiles_to_rows(ref, rows):
    return jnp.concatenate(
        [ref[pl.ds(c, rows, stride=SUBLANES), :] for c in range(SUBLANES)], axis=1)


def _ada_kernel(c_ref, w_ref, b_ref, o_ref):
    cc = c_ref[...]
    act = cc * jax.nn.sigmoid(cc)
    o_ref[...] = _dot3(act, w_ref[...]) + b_ref[...]


def _ada(c_pad, w_ada, b_ada):
    rows, d = c_pad.shape
    n_out = w_ada.shape[1]
    return pl.pallas_call(
        _ada_kernel,
        grid=(n_out // d,),
        in_specs=[
            pl.BlockSpec((rows, d), lambda j: (0, 0)),
            pl.BlockSpec((d, d), lambda j: (0, j)),
            pl.BlockSpec((1, d), lambda j: (0, j)),
        ],
        out_specs=pl.BlockSpec((rows, d), lambda j: (0, j)),
        out_shape=jax.ShapeDtypeStruct((rows, n_out), F32),
        compiler_params=pltpu.CompilerParams(
            dimension_semantics=("arbitrary",), vmem_limit_bytes=VMEM_LIMIT_BYTES),
        name="ada",
    )(c_pad, w_ada, b_ada)


def _mix_kernel(x_ref, mod_ref, win_ref, convw_ref, woc_ref, wpool_ref, pscale_ref, wo_ref,
                ln1g_ref, ln1b_ref, wr_ref, rb_ref,
                x1_ref, hrow_ref, rint_ref, ewcol_ref, cnt_ref,
                cvbuf, zbuf, s2buf, s4buf, s8buf, base_ref):
    t_rows, d = x_ref.shape
    pg = d // N_POOL_GROUPS
    b = pl.program_id(0)
    s = pl.program_id(1)
    shift_bufs = (cvbuf, zbuf, s2buf, s4buf, s8buf)

    @pl.when(jnp.logical_and(b == 0, s == 0))
    def _():
        base_ref[...] = jnp.zeros_like(base_ref)

    @pl.when(s == 0)
    def _():
        for buf in shift_bufs:
            buf[0:HALO, :] = jnp.zeros((HALO, buf.shape[1]), F32)

    @pl.when(s > 0)
    def _():
        for buf in shift_bufs:
            buf[0:HALO, :] = buf[t_rows:t_rows + HALO, :]

    def shifted(buf, k):
        return buf[HALO - k:HALO - k + t_rows, :]

    xt = x_ref[...]
    shift1 = mod_ref[0, 0:1, :]
    scale1 = mod_ref[0, 1:2, :]
    gate1 = mod_ref[0, 2:3, :]
    shift2 = mod_ref[0, 3:4, :]
    scale2 = mod_ref[0, 4:5, :]

    u = (xt * (1.0 + scale1) + shift1).astype(BF16)

    def proj(i):
        return _dot(u, win_ref[:, i * d:(i + 1) * d])

    cv = proj(2) * proj(0)
    cvbuf[HALO:HALO + t_rows, :] = cv
    conv = (convw_ref[2:3, :] * cv + convw_ref[1:2, :] * shifted(cvbuf, 1)
            + convw_ref[0:1, :] * shifted(cvbuf, 2))
    y_a = _dot((conv * proj(1)).astype(BF16), woc_ref[...])

    z = proj(3)
    zbuf[HALO:HALO + t_rows, :] = z
    s2 = z + shifted(zbuf, 1)
    s2buf[HALO:HALO + t_rows, :] = s2[:, pg:]
    s4 = s2[:, pg:] + shifted(s2buf, 2)
    s4buf[HALO:HALO + t_rows, :] = s4[:, pg:]
    s8 = s4[:, pg:] + shifted(s4buf, 4)
    s8buf[HALO:HALO + t_rows, :] = s8[:, pg:]
    s16 = s8[:, pg:] + shifted(s8buf, 8)
    win_sums = (s2[:, :pg], s4[:, :pg], s8[:, :pg], s16)
    t_pos = s * t_rows + lax.broadcasted_iota(I32, (t_rows, 1), 0) + 1
    y_b_parts = []
    for gi, w in enumerate(POOL_WINDOWS):
        count = jnp.minimum(t_pos, w).astype(F32)
        pooled = win_sums[gi] / count - z[:, gi * pg:(gi + 1) * pg]
        y_b_parts.append(_dot(pooled.astype(BF16), wpool_ref[gi]))
    y_b = jnp.concatenate(y_b_parts, axis=1) * pscale_ref[...]

    merged = jax.nn.sigmoid(proj(4)) * y_a + jax.nn.sigmoid(proj(5)) * y_b
    mix = _dot(merged.astype(BF16), wo_ref[...])

    x1 = _layer_norm(DEEPNORM_ALPHA * xt + (1.0 + gate1) * mix, ln1g_ref[...], ln1b_ref[...])
    x1_ref[...] = x1
    h = x1 * (1.0 + scale2) + shift2
    _rows_to_tiles(hrow_ref, h)

    lt = _dot3(h, wr_ref[...]).T + rb_ref[...]
    iota8 = lax.broadcasted_iota(I32, (SUBLANES, t_rows), 0)
    lg = lt[N_EXPERTS:N_EXPERTS + N_GROUPS, :]
    g_max = jnp.max(lg, axis=0, keepdims=True)
    g_top = jnp.min(jnp.where(lg == g_max, iota8, N_GROUPS), axis=0, keepdims=True)
    g_top_p = 1.0 / jnp.sum(jnp.exp(lg - g_max), axis=0, keepdims=True)
    sel = jnp.zeros((EXPERTS_PER_GROUP, t_rows), F32)
    for g in range(N_GROUPS):
        sel = jnp.where(g_top == g, lt[g * EXPERTS_PER_GROUP:(g + 1) * EXPERTS_PER_GROUP, :], sel)
    m1 = jnp.max(sel, axis=0, keepdims=True)
    i1 = jnp.min(jnp.where(sel == m1, iota8, EXPERTS_PER_GROUP), axis=0, keepdims=True)
    sel2 = jnp.where(iota8 == i1, -jnp.inf, sel)
    m2 = jnp.max(sel2, axis=0, keepdims=True)
    i2 = jnp.min(jnp.where(sel2 == m2, iota8, EXPERTS_PER_GROUP), axis=0, keepdims=True)
    ratio = jnp.exp(m2 - m1)
    p1 = 1.0 / (1.0 + ratio)
    w1 = g_top_p * p1
    w2 = g_top_p * (ratio * p1)
    e1 = g_top * EXPERTS_PER_GROUP + i1
    e2 = g_top * EXPERTS_PER_GROUP + i2

    iota_e = lax.broadcasted_iota(I32, (N_EXPERTS, t_rows), 0)
    hit1 = iota_e == e1
    hit2 = iota_e == e2
    onehot = jnp.where(jnp.logical_or(hit1, hit2), 1.0, 0.0)
    before = (lax.broadcasted_iota(I32, (t_rows, t_rows), 0)
              < lax.broadcasted_iota(I32, (t_rows, t_rows), 1))
    prefix = _dot(onehot.astype(BF16), jnp.where(before, 1.0, 0.0).astype(BF16))
    total = prefix + base_ref[...]
    r1 = jnp.sum(jnp.where(hit1, total, 0.0), axis=0, keepdims=True).astype(I32)
    r2 = jnp.sum(jnp.where(hit2, total, 0.0), axis=0, keepdims=True).astype(I32)
    new_base = base_ref[...] + jnp.sum(onehot, axis=1, keepdims=True)
    base_ref[...] = new_base
    cnt_ref[...] = jnp.broadcast_to(new_base, cnt_ref.shape).astype(I32)

    rint_ref[...] = jnp.where(iota8 == 0, e1, jnp.where(iota8 == 1, e2,
                              jnp.where(iota8 == 2, r1, jnp.where(iota8 == 3, r2, 0))))
    iota_l = lax.broadcasted_iota(I32, (LANES, t_rows), 0)
    ewcol_ref[...] = jnp.where(iota_l == 0, w1, jnp.where(iota_l == 1, w2, 0.0)).T


def _mix(x2d, mod, w_in, conv_w, w_oc, w_pool, pool_scale, w_o, ln1_g, ln1_b, wr, rb,
         *, batch, seq):
    n, d = x2d.shape
    t = MIX_ROWS
    n_s = seq // t
    const = lambda shape: pl.BlockSpec(shape, lambda b, s: (0,) * len(shape),
                                       pipeline_mode=pl.Buffered(1))
    row_blk = lambda b, s: (b * n_s + s, 0)
    pg = d // N_POOL_GROUPS
    return pl.pallas_call(
        _mix_kernel,
        grid=(batch, n_s),
        in_specs=[
            pl.BlockSpec((t, d), row_blk),
            pl.BlockSpec((1, N_MOD, d), lambda b, s: (b, 0, 0)),
            const(w_in.shape), const(conv_w.shape), const(w_oc.shape), const(w_pool.shape),
            const(pool_scale.shape), const(w_o.shape), const(ln1_g.shape), const(ln1_b.shape),
            const(wr.shape), const(rb.shape),
        ],
        out_specs=[
            pl.BlockSpec((t, d), row_blk),
            pl.BlockSpec((t * SUBLANES, LANES), row_blk),
            pl.BlockSpec((SUBLANES, t), lambda b, s: (0, b * n_s + s)),
            pl.BlockSpec((t, LANES), row_blk),
            pl.BlockSpec((N_EXPERTS, LANES), lambda b, s: (0, 0)),
        ],
        out_shape=[
            jax.ShapeDtypeStruct((n, d), F32),
            jax.ShapeDtypeStruct((n * SUBLANES, LANES), F32),
            jax.ShapeDtypeStruct((SUBLANES, n), I32),
            jax.ShapeDtypeStruct((n, LANES), F32),
            jax.ShapeDtypeStruct((N_EXPERTS, LANES), I32),
        ],
        scratch_shapes=[
            pltpu.VMEM((t + HALO, d), F32),
            pltpu.VMEM((t + HALO, d), F32),
            pltpu.VMEM((t + HALO, 3 * pg), F32),
            pltpu.VMEM((t + HALO, 2 * pg), F32),
            pltpu.VMEM((t + HALO, pg), F32),
            pltpu.VMEM((N_EXPERTS, 1), F32),
        ],
        compiler_params=pltpu.CompilerParams(
            dimension_semantics=("arbitrary", "arbitrary"), vmem_limit_bytes=VMEM_LIMIT_BYTES),
        name="mix",
    )(x2d, mod, w_in, conv_w, w_oc, w_pool, pool_scale, w_o, ln1_g, ln1_b, wr, rb)


def _row_copy(src_hbm, src_row, dst_hbm, dst_row, sem):
    return pltpu.make_async_copy(
        src_hbm.at[pl.ds(pl.multiple_of(src_row * SUBLANES, SUBLANES), SUBLANES), :],
        dst_hbm.at[pl.ds(pl.multiple_of(dst_row * SUBLANES, SUBLANES), SUBLANES), :],
        sem)


def _dispatch_kernel(pstart_ref, rint_ref, hrow_hbm, xs_in_hbm, xs_hbm, sem):
    del xs_in_hbm
    n_tok = rint_ref.shape[1]
    tok0 = pl.program_id(0) * n_tok

    def issue(t, carry):
        for k in range(TOP_K):
            dest = pstart_ref[rint_ref[k, t]] + rint_ref[TOP_K + k, t]
            _row_copy(hrow_hbm, tok0 + t, xs_hbm, dest, sem).start()
        return carry

    lax.fori_loop(0, n_tok, issue, 0)

    def drain(t, carry):
        for k in range(TOP_K):
            _row_copy(hrow_hbm, 0, xs_hbm, 0, sem).wait()
        return carry

    lax.fori_loop(0, n_tok, drain, 0)


def _dispatch(pad_start, rint, hrow, xs_zero):
    n = rint.shape[1]
    td = min(DISPATCH_ROWS, n)
    grid_spec = pltpu.PrefetchScalarGridSpec(
        num_scalar_prefetch=1,
        grid=(n // td,),
        in_specs=[
            pl.BlockSpec((SUBLANES, td), lambda i, ps: (0, i), memory_space=pltpu.SMEM),
            pl.BlockSpec(memory_space=pl.ANY),
            pl.BlockSpec(memory_space=pl.ANY),
        ],
        out_specs=pl.BlockSpec(memory_space=pl.ANY),
        scratch_shapes=[pltpu.SemaphoreType.DMA(())],
    )
    return pl.pallas_call(
        _dispatch_kernel,
        grid_spec=grid_spec,
        out_shape=jax.ShapeDtypeStruct(xs_zero.shape, F32),
        input_output_aliases={3: 0},
        compiler_params=pltpu.CompilerParams(
            dimension_semantics=("arbitrary",), has_side_effects=True),
        name="dispatch",
    )(pad_start, rint, hrow, xs_zero)


def _expert_kernel(bexp_ref, nused_ref, xs_ref, wg_ref, wu_ref, wd_ref, ys_ref,
                   wg_bf, wu_bf, wd_bf):
    j = pl.program_id(0)
    rows = xs_ref.shape[0] // SUBLANES
    prev = bexp_ref[jnp.maximum(j - 1, 0)]
    active = j < nused_ref[0]

    @pl.when(jnp.logical_and(active, jnp.logical_or(j == 0, bexp_ref[j] != prev)))
    def _():
        wg_bf[...] = wg_ref[0].astype(BF16)
        wu_bf[...] = wu_ref[0].astype(BF16)
        wd_bf[...] = wd_ref[0].astype(BF16)

    @pl.when(active)
    def _():
        xb = _tiles_to_rows(xs_ref, rows).astype(BF16)
        g = _dot(xb, wg_bf[...])
        u = _dot(xb, wu_bf[...])
        act = (g * jax.nn.sigmoid(g) * u).astype(BF16)
        _rows_to_tiles(ys_ref, _dot(act, wd_bf[...]))


def _experts(block_expert, n_used, xs, w_gate, w_up, w_down):
    n_blocks = block_expert.shape[0]
    bm = EXPERT_ROWS
    _, d, de = w_gate.shape
    blk = lambda j, be, nu: (jnp.minimum(j, nu[0] - 1), 0)
    wblk = lambda j, be, nu: (be[j], 0, 0)
    grid_spec = pltpu.PrefetchScalarGridSpec(
        num_scalar_prefetch=2,
        grid=(n_blocks,),
        in_specs=[
            pl.BlockSpec((bm * SUBLANES, LANES), blk),
            pl.BlockSpec((1, d, de), wblk),
            pl.BlockSpec((1, d, de), wblk),
            pl.BlockSpec((1, de, d), wblk),
        ],
        out_specs=pl.BlockSpec((bm * SUBLANES, LANES), blk),
        scratch_shapes=[
            pltpu.VMEM((d, de), BF16),
            pltpu.VMEM((d, de), BF16),
            pltpu.VMEM((de, d), BF16),
        ],
    )
    return pl.pallas_call(
        _expert_kernel,
        grid_spec=grid_spec,
        out_shape=jax.ShapeDtypeStruct(xs.shape, F32),
        compiler_params=pltpu.CompilerParams(
            dimension_semantics=("arbitrary",), vmem_limit_bytes=VMEM_LIMIT_BYTES),
        name="experts",
    )(block_expert, n_used, xs, w_gate, w_up, w_down)


def _combine_kernel(pstart_ref, rint_ref, x1_ref, mod_ref, ln2g_ref, ln2b_ref, ewcol_ref, ys_hbm,
                    out_ref, ybuf0, ybuf1, sem):
    t_rows = x1_ref.shape[0]
    ybufs = (ybuf0, ybuf1)

    def issue(t, carry):
        for k in range(TOP_K):
            src = pstart_ref[rint_ref[k, t]] + rint_ref[TOP_K + k, t]
            _row_copy(ys_hbm, src, ybufs[k], t, sem).start()
        return carry

    lax.fori_loop(0, t_rows, issue, 0)

    def drain(t, carry):
        for k in range(TOP_K):
            _row_copy(ys_hbm, 0, ybufs[k], 0, sem).wait()
        return carry

    lax.fori_loop(0, t_rows, drain, 0)

    ew = ewcol_ref[...]
    y = (ew[:, 0:1] * _tiles_to_rows(ybuf0, t_rows)
         + ew[:, 1:2] * _tiles_to_rows(ybuf1, t_rows))
    gate2 = mod_ref[0, 5:6, :]
    out_ref[...] = _layer_norm(DEEPNORM_ALPHA * x1_ref[...] + (1.0 + gate2) * y,
                               ln2g_ref[...], ln2b_ref[...])


def _combine(pad_start, rint, x1, mod, ln2_g, ln2_b, ewcol, ys, *, seq):
    n, d = x1.shape
    t = min(COMBINE_ROWS, seq)
    per_seq = seq // t
    row_blk = lambda i, ps: (i, 0)
    grid_spec = pltpu.PrefetchScalarGridSpec(
        num_scalar_prefetch=1,
        grid=(n // t,),
        in_specs=[
            pl.BlockSpec((SUBLANES, t), lambda i, ps: (0, i), memory_space=pltpu.SMEM),
            pl.BlockSpec((t, d), row_blk),
            pl.BlockSpec((1, N_MOD, d), lambda i, ps: (i // per_seq, 0, 0)),
            pl.BlockSpec((1, d), lambda i, ps: (0, 0)),
            pl.BlockSpec((1, d), lambda i, ps: (0, 0)),
            pl.BlockSpec((t, LANES), row_blk),
            pl.BlockSpec(memory_space=pl.ANY),
        ],
        out_specs=pl.BlockSpec((t, d), row_blk),
        scratch_shapes=[
            pltpu.VMEM((t * SUBLANES, LANES), F32),
            pltpu.VMEM((t * SUBLANES, LANES), F32),
            pltpu.SemaphoreType.DMA(()),
        ],
    )
    return pl.pallas_call(
        _combine_kernel,
        grid_spec=grid_spec,
        out_shape=jax.ShapeDtypeStruct((n, d), F32),
        compiler_params=pltpu.CompilerParams(
            dimension_semantics=("arbitrary",), vmem_limit_bytes=VMEM_LIMIT_BYTES),
        name="combine",
    )(pad_start, rint, x1, mod, ln2_g, ln2_b, ewcol, ys)


def _block_layout(counts, n_assign):
    bm = EXPERT_ROWS
    n_blocks = -(-n_assign // bm) + N_EXPERTS
    padded = (counts + bm - 1) // bm * bm
    pad_end = jnp.cumsum(padded)
    pad_start = (pad_end - padded).astype(I32)
    n_used = (pad_end[-1] // bm).astype(I32)
    blk_row = jnp.minimum(jnp.arange(n_blocks, dtype=I32), n_used - 1) * bm
    block_expert = jnp.minimum(
        jnp.searchsorted(pad_end, blk_row, side="right"), N_EXPERTS - 1).astype(I32)
    return pad_start, block_expert, n_used.reshape(1), n_blocks


def kernel(x, c, w_ada, b_ada, w_in, conv_w, w_out_conv, w_pool, pool_scale, w_o, ln1_g, ln1_b,
           w_group, b_group, w_router, b_router, w_gate, w_up, w_down, ln2_g, ln2_b):
    bsz, seq, d = x.shape
    n = bsz * seq
    assert d == SUBLANES * LANES and seq % MIX_ROWS == 0 and w_ada.shape[0] == DEPTH
    x2d = x.reshape(n, d)
    c_pad = jnp.pad(c, ((0, -bsz % SUBLANES), (0, 0)))
    for l in range(DEPTH):
        mod = _ada(c_pad, w_ada[l], b_ada[l][None, :])[:bsz].reshape(bsz, N_MOD, d)
        wr = jnp.concatenate(
            [w_router[l], w_group[l],
             jnp.zeros((d, LANES - N_EXPERTS - N_GROUPS), F32)], axis=1)
        rb = jnp.concatenate(
            [b_router[l].reshape(-1), b_group[l],
             jnp.zeros((LANES - N_EXPERTS - N_GROUPS,), F32)])[:, None]
        x1, hrow, rint, ewcol, cnt = _mix(
            x2d, mod, w_in[l].astype(BF16), conv_w[l], w_out_conv[l].astype(BF16),
            w_pool[l].astype(BF16), pool_scale[l][None, :], w_o[l].astype(BF16),
            ln1_g[l][None, :], ln1_b[l][None, :], wr, rb, batch=bsz, seq=seq)
        pad_start, block_expert, n_used, n_blocks = _block_layout(cnt[:, 0], n * TOP_K)
        xs_zero = jnp.zeros((n_blocks * EXPERT_ROWS * SUBLANES, LANES), F32)
        xs = _dispatch(pad_start, rint, hrow, xs_zero)
        ys = _experts(block_expert, n_used, xs, w_gate[l], w_up[l], w_down[l])
        x2d = _combine(pad_start, rint, x1, mod, ln2_g[l][None, :], ln2_b[l][None, :],
                       ewcol, ys, seq=seq)
    return x2d.reshape(bsz, seq, d)
```

```python
import functools

import jax
import jax.numpy as jnp
from jax import lax
from jax.experimental import pallas as pl
from jax.experimental.pallas import tpu as pltpu

F32 = jnp.float32
BF16 = jnp.bfloat16
I32 = jnp.int32

LANES = 128
SUBLANES = 8
VMEM_LIMIT_BYTES = 56 * 1024 * 1024

N_POOL_GROUPS = 4
POOL_WINDOWS = (2, 4, 8, 16)
CONV_WIDTH = 3
N_GROUPS = 8
EXPERTS_PER_GROUP = 8
N_EXPERTS = N_GROUPS * EXPERTS_PER_GROUP
TOP_K = 2
N_MOD = 6
LN_EPS = 1e-5
DEPTH = 1
DEEPNORM_ALPHA = (2.0 * DEPTH) ** 0.25

MIX_ROWS = 512
EXPERT_ROWS = 256
DISPATCH_ROWS = 2048
COMBINE_ROWS = 512
HALO = SUBLANES
ISSUE_UNROLL = 8


def _split_bf16(a):
    hi = a.astype(BF16)
    lo = (a - hi.astype(F32)).astype(BF16)
    return hi, lo


def _dot(a, b):
    return jnp.dot(a, b, preferred_element_type=F32)


def _dot3(a, b):
    a_hi, a_lo = _split_bf16(a)
    b_hi, b_lo = _split_bf16(b)
    return _dot(a_hi, b_hi) + _dot(a_lo, b_hi) + _dot(a_hi, b_lo)


def _layer_norm(r, g, b):
    mu = jnp.mean(r, axis=-1, keepdims=True)
    d = r - mu
    var = jnp.mean(d * d, axis=-1, keepdims=True)
    return d * lax.rsqrt(var + LN_EPS) * g + b


def _rows_to_tiles(ref, val):
    rows = val.shape[0]
    for c in range(SUBLANES):
        ref[pl.ds(c, rows, stride=SUBLANES), :] = val[:, c * LANES:(c + 1) * LANES]


def _tiles_to_rows(ref, rows):
    return jnp.concatenate(
        [ref[pl.ds(c, rows, stride=SUBLANES), :] for c in range(SUBLANES)], axis=1)


def _ada_kernel(c_ref, w_ref, b_ref, o_ref):
    cc = c_ref[...]
    act = cc * jax.nn.sigmoid(cc)
    o_ref[...] = _dot3(act, w_ref[...]) + b_ref[...]


def _ada(c_pad, w_ada, b_ada):
    rows, d = c_pad.shape
    n_out = w_ada.shape[1]
    return pl.pallas_call(
        _ada_kernel,
        grid=(n_out // d,),
        in_specs=[
            pl.BlockSpec((rows, d), lambda j: (0, 0)),
            pl.BlockSpec((d, d), lambda j: (0, j)),
            pl.BlockSpec((1, d), lambda j: (0, j)),
        ],
        out_specs=pl.BlockSpec((rows, d), lambda j: (0, j)),
        out_shape=jax.ShapeDtypeStruct((rows, n_out), F32),
        compiler_params=pltpu.CompilerParams(
            dimension_semantics=("arbitrary",), vmem_limit_bytes=VMEM_LIMIT_BYTES),
        name="ada",
    )(c_pad, w_ada, b_ada)


def _mix_kernel(x_ref, mod_ref, win_ref, convw_ref, woc_ref, wpool_ref, pscale_ref, wo_ref,
                ln1g_ref, ln1b_ref, wr_ref, rb_ref,
                x1_ref, hrow_ref, rint_ref, ewcol_ref, cnt_ref,
                cvbuf, zbuf, s2buf, s4buf, s8buf, base_ref):
    t_rows, d = x_ref.shape
    pg = d // N_POOL_GROUPS
    b = pl.program_id(0)
    s = pl.program_id(1)
    shift_bufs = (cvbuf, zbuf, s2buf, s4buf, s8buf)

    @pl.when(jnp.logical_and(b == 0, s == 0))
    def _():
        base_ref[...] = jnp.zeros_like(base_ref)

    @pl.when(s == 0)
    def _():
        for buf in shift_bufs:
            buf[0:HALO, :] = jnp.zeros((HALO, buf.shape[1]), F32)

    @pl.when(s > 0)
    def _():
        for buf in shift_bufs:
            buf[0:HALO, :] = buf[t_rows:t_rows + HALO, :]

    def shifted(buf, k):
        return buf[HALO - k:HALO - k + t_rows, :]

    xt = x_ref[...]
    shift1 = mod_ref[0, 0:1, :]
    scale1 = mod_ref[0, 1:2, :]
    gate1 = mod_ref[0, 2:3, :]
    shift2 = mod_ref[0, 3:4, :]
    scale2 = mod_ref[0, 4:5, :]

    u = (xt * (1.0 + scale1) + shift1).astype(BF16)

    def proj(i):
        return _dot(u, win_ref[:, i * d:(i + 1) * d])

    cv = proj(2) * proj(0)
    cvbuf[HALO:HALO + t_rows, :] = cv
    conv = (convw_ref[2:3, :] * cv + convw_ref[1:2, :] * shifted(cvbuf, 1)
            + convw_ref[0:1, :] * shifted(cvbuf, 2))
    y_a = _dot((conv * proj(1)).astype(BF16), woc_ref[...])

    z = proj(3)
    zbuf[HALO:HALO + t_rows, :] = z
    s2 = z + shifted(zbuf, 1)
    s2buf[HALO:HALO + t_rows, :] = s2[:, pg:]
    s4 = s2[:, pg:] + shifted(s2buf, 2)
    s4buf[HALO:HALO + t_rows, :] = s4[:, pg:]
    s8 = s4[:, pg:] + shifted(s4buf, 4)
    s8buf[HALO:HALO + t_rows, :] = s8[:, pg:]
    s16 = s8[:, pg:] + shifted(s8buf, 8)
    win_sums = (s2[:, :pg], s4[:, :pg], s8[:, :pg], s16)
    t_pos = s * t_rows + lax.broadcasted_iota(I32, (t_rows, 1), 0) + 1
    y_b_parts = []
    for gi, w in enumerate(POOL_WINDOWS):
        count = jnp.minimum(t_pos, w).astype(F32)
        pooled = win_sums[gi] / count - z[:, gi * pg:(gi + 1) * pg]
        y_b_parts.append(_dot(pooled.astype(BF16), wpool_ref[gi]))
    y_b = jnp.concatenate(y_b_parts, axis=1) * pscale_ref[...]

    merged = jax.nn.sigmoid(proj(4)) * y_a + jax.nn.sigmoid(proj(5)) * y_b
    mix = _dot(merged.astype(BF16), wo_ref[...])

    x1 = _layer_norm(DEEPNORM_ALPHA * xt + (1.0 + gate1) * mix, ln1g_ref[...], ln1b_ref[...])
    x1_ref[...] = x1
    h = x1 * (1.0 + scale2) + shift2
    _rows_to_tiles(hrow_ref, h)

    lt = _dot3(h, wr_ref[...]).T + rb_ref[...]
    iota8 = lax.broadcasted_iota(I32, (SUBLANES, t_rows), 0)
    lg = lt[N_EXPERTS:N_EXPERTS + N_GROUPS, :]
    g_max = jnp.max(lg, axis=0, keepdims=True)
    g_top = jnp.min(jnp.where(lg == g_max, iota8, N_GROUPS), axis=0, keepdims=True)
    g_top_p = 1.0 / jnp.sum(jnp.exp(lg - g_max), axis=0, keepdims=True)
    sel = jnp.zeros((EXPERTS_PER_GROUP, t_rows), F32)
    for g in range(N_GROUPS):
        sel = jnp.where(g_top == g, lt[g * EXPERTS_PER_GROUP:(g + 1) * EXPERTS_PER_GROUP, :], sel)
    m1 = jnp.max(sel, axis=0, keepdims=True)
    i1 = jnp.min(jnp.where(sel == m1, iota8, EXPERTS_PER_GROUP), axis=0, keepdims=True)
    sel2 = jnp.where(iota8 == i1, -jnp.inf, sel)
    m2 = jnp.max(sel2, axis=0, keepdims=True)
    i2 = jnp.min(jnp.where(sel2 == m2, iota8, EXPERTS_PER_GROUP), axis=0, keepdims=True)
    ratio = jnp.exp(m2 - m1)
    p1 = 1.0 / (1.0 + ratio)
    w1 = g_top_p * p1
    w2 = g_top_p * (ratio * p1)
    e1 = g_top * EXPERTS_PER_GROUP + i1
    e2 = g_top * EXPERTS_PER_GROUP + i2

    iota_e = lax.broadcasted_iota(I32, (N_EXPERTS, t_rows), 0)
    hit1 = iota_e == e1
    hit2 = iota_e == e2
    onehot = jnp.where(jnp.logical_or(hit1, hit2), 1.0, 0.0)
    before = (lax.broadcasted_iota(I32, (t_rows, t_rows), 0)
              < lax.broadcasted_iota(I32, (t_rows, t_rows), 1))
    prefix = _dot(onehot.astype(BF16), jnp.where(before, 1.0, 0.0).astype(BF16))
    total = prefix + base_ref[...]
    r1 = jnp.sum(jnp.where(hit1, total, 0.0), axis=0, keepdims=True).astype(I32)
    r2 = jnp.sum(jnp.where(hit2, total, 0.0), axis=0, keepdims=True).astype(I32)
    new_base = base_ref[...] + jnp.sum(onehot, axis=1, keepdims=True)
    base_ref[...] = new_base
    cnt_ref[...] = jnp.broadcast_to(new_base, cnt_ref.shape).astype(I32)

    rint_ref[...] = jnp.where(iota8 == 0, e1, jnp.where(iota8 == 1, e2,
                              jnp.where(iota8 == 2, r1, jnp.where(iota8 == 3, r2, 0))))
    iota_l = lax.broadcasted_iota(I32, (LANES, t_rows), 0)
    ewcol_ref[...] = jnp.where(iota_l == 0, w1, jnp.where(iota_l == 1, w2, 0.0)).T


def _mix(x2d, mod, w_in, conv_w, w_oc, w_pool, pool_scale, w_o, ln1_g, ln1_b, wr, rb,
         *, batch, seq):
    n, d = x2d.shape
    t = MIX_ROWS
    n_s = seq // t
    const = lambda shape: pl.BlockSpec(shape, lambda b, s: (0,) * len(shape),
                                       pipeline_mode=pl.Buffered(1))
    row_blk = lambda b, s: (b * n_s + s, 0)
    pg = d // N_POOL_GROUPS
    return pl.pallas_call(
        _mix_kernel,
        grid=(batch, n_s),
        in_specs=[
            pl.BlockSpec((t, d), row_blk),
            pl.BlockSpec((1, N_MOD, d), lambda b, s: (b, 0, 0)),
            const(w_in.shape), const(conv_w.shape), const(w_oc.shape), const(w_pool.shape),
            const(pool_scale.shape), const(w_o.shape), const(ln1_g.shape), const(ln1_b.shape),
            const(wr.shape), const(rb.shape),
        ],
        out_specs=[
            pl.BlockSpec((t, d), row_blk),
            pl.BlockSpec((t * SUBLANES, LANES), row_blk),
            pl.BlockSpec((SUBLANES, t), lambda b, s: (0, b * n_s + s)),
            pl.BlockSpec((t, LANES), row_blk),
            pl.BlockSpec((N_EXPERTS, LANES), lambda b, s: (0, 0)),
        ],
        out_shape=[
            jax.ShapeDtypeStruct((n, d), F32),
            jax.ShapeDtypeStruct((n * SUBLANES, LANES), F32),
            jax.ShapeDtypeStruct((SUBLANES, n), I32),
            jax.ShapeDtypeStruct((n, LANES), F32),
            jax.ShapeDtypeStruct((N_EXPERTS, LANES), I32),
        ],
        scratch_shapes=[
            pltpu.VMEM((t + HALO, d), F32),
            pltpu.VMEM((t + HALO, d), F32),
            pltpu.VMEM((t + HALO, 3 * pg), F32),
            pltpu.VMEM((t + HALO, 2 * pg), F32),
            pltpu.VMEM((t + HALO, pg), F32),
            pltpu.VMEM((N_EXPERTS, 1), F32),
        ],
        compiler_params=pltpu.CompilerParams(
            dimension_semantics=("arbitrary", "arbitrary"), vmem_limit_bytes=VMEM_LIMIT_BYTES),
        name="mix",
    )(x2d, mod, w_in, conv_w, w_oc, w_pool, pool_scale, w_o, ln1_g, ln1_b, wr, rb)


def _tile_rows(ref, row, n_rows=1):
    return ref.at[pl.ds(pl.multiple_of(row * SUBLANES, SUBLANES), n_rows * SUBLANES), :]


def _dispatch_kernel(pstart_ref, tail_ref, rint_ref, hrow_ref, xs_hbm, zero_blk, sem):
    n_tok = rint_ref.shape[1]
    blk_tok = zero_blk.shape[0] // SUBLANES

    @pl.when(pl.program_id(0) == 0)
    def _():
        zero_blk[...] = jnp.zeros_like(zero_blk)

        def fill_copy(e):
            return pltpu.make_async_copy(zero_blk, _tile_rows(xs_hbm, tail_ref[e], blk_tok), sem)

        def fill(e, carry):
            @pl.when(tail_ref[e] >= 0)
            def _():
                fill_copy(e).start()
            return carry

        def fill_done(e, carry):
            @pl.when(tail_ref[e] >= 0)
            def _():
                fill_copy(e).wait()
            return carry

        lax.fori_loop(0, N_EXPERTS, fill, 0)
        lax.fori_loop(0, N_EXPERTS, fill_done, 0)

    def issue(i, carry):
        for j in range(ISSUE_UNROLL):
            t = i * ISSUE_UNROLL + j
            for k in range(TOP_K):
                dest = pstart_ref[rint_ref[k, t]] + rint_ref[TOP_K + k, t]
                pltpu.make_async_copy(_tile_rows(hrow_ref, t), _tile_rows(xs_hbm, dest), sem).start()
        return carry

    lax.fori_loop(0, n_tok // ISSUE_UNROLL, issue, 0)
    for k in range(TOP_K):
        pltpu.make_async_copy(hrow_ref, _tile_rows(xs_hbm, 0, n_tok), sem).wait()


def _dispatch(pad_start, tail_row, rint, hrow, n_blocks):
    n = rint.shape[1]
    td = min(DISPATCH_ROWS, n)
    grid_spec = pltpu.PrefetchScalarGridSpec(
        num_scalar_prefetch=2,
        grid=(n // td,),
        in_specs=[
            pl.BlockSpec((SUBLANES, td), lambda i, ps, tl: (0, i), memory_space=pltpu.SMEM),
            pl.BlockSpec((td * SUBLANES, LANES), lambda i, ps, tl: (i, 0)),
        ],
        out_specs=pl.BlockSpec(memory_space=pl.ANY),
        scratch_shapes=[
            pltpu.VMEM((EXPERT_ROWS * SUBLANES, LANES), F32),
            pltpu.SemaphoreType.DMA(()),
        ],
    )
    return pl.pallas_call(
        _dispatch_kernel,
        grid_spec=grid_spec,
        out_shape=jax.ShapeDtypeStruct((n_blocks * EXPERT_ROWS * SUBLANES, LANES), F32),
        compiler_params=pltpu.CompilerParams(
            dimension_semantics=("arbitrary",), vmem_limit_bytes=VMEM_LIMIT_BYTES,
            has_side_effects=True),
        name="dispatch",
    )(pad_start, tail_row, rint, hrow)


def _expert_kernel(bexp_ref, nused_ref, xs_ref, wg_ref, wu_ref, wd_ref, ys_ref,
                   wg_bf, wu_bf, wd_bf):
    j = pl.program_id(0)
    rows = xs_ref.shape[0] // SUBLANES
    prev = bexp_ref[jnp.maximum(j - 1, 0)]
    active = j < nused_ref[0]

    @pl.when(jnp.logical_and(active, jnp.logical_or(j == 0, bexp_ref[j] != prev)))
    def _():
        wg_bf[...] = wg_ref[0].astype(BF16)
        wu_bf[...] = wu_ref[0].astype(BF16)
        wd_bf[...] = wd_ref[0].astype(BF16)

    @pl.when(active)
    def _():
        xb = _tiles_to_rows(xs_ref, rows).astype(BF16)
        g = _dot(xb, wg_bf[...])
        u = _dot(xb, wu_bf[...])
        act = (g * jax.nn.sigmoid(g) * u).astype(BF16)
        _rows_to_tiles(ys_ref, _dot(act, wd_bf[...]))


def _experts(block_expert, n_used, xs, w_gate, w_up, w_down):
    n_blocks = block_expert.shape[0]
    bm = EXPERT_ROWS
    _, d, de = w_gate.shape
    blk = lambda j, be, nu: (jnp.minimum(j, nu[0] - 1), 0)
    wblk = lambda j, be, nu: (be[j], 0, 0)
    grid_spec = pltpu.PrefetchScalarGridSpec(
        num_scalar_prefetch=2,
        grid=(n_blocks,),
        in_specs=[
            pl.BlockSpec((bm * SUBLANES, LANES), blk),
            pl.BlockSpec((1, d, de), wblk),
            pl.BlockSpec((1, d, de), wblk),
            pl.BlockSpec((1, de, d), wblk),
        ],
        out_specs=pl.BlockSpec((bm * SUBLANES, LANES), blk),
        scratch_shapes=[
            pltpu.VMEM((d, de), BF16),
            pltpu.VMEM((d, de), BF16),
            pltpu.VMEM((de, d), BF16),
        ],
    )
    return pl.pallas_call(
        _expert_kernel,
        grid_spec=grid_spec,
        out_shape=jax.ShapeDtypeStruct(xs.shape, F32),
        compiler_params=pltpu.CompilerParams(
            dimension_semantics=("arbitrary",), vmem_limit_bytes=VMEM_LIMIT_BYTES),
        name="experts",
    )(block_expert, n_used, xs, w_gate, w_up, w_down)


def _combine_kernel(pstart_ref, rint_ref, x1_ref, mod_ref, ln2g_ref, ln2b_ref, ewcol_ref, ys_hbm,
                    out_ref, ybuf0, ybuf1, sem):
    t_rows = x1_ref.shape[0]
    ybufs = (ybuf0, ybuf1)

    def issue(i, carry):
        for j in range(ISSUE_UNROLL):
            t = i * ISSUE_UNROLL + j
            for k in range(TOP_K):
                src = pstart_ref[rint_ref[k, t]] + rint_ref[TOP_K + k, t]
                pltpu.make_async_copy(_tile_rows(ys_hbm, src), _tile_rows(ybufs[k], t), sem).start()
        return carry

    lax.fori_loop(0, t_rows // ISSUE_UNROLL, issue, 0)
    for k in range(TOP_K):
        pltpu.make_async_copy(_tile_rows(ys_hbm, 0, t_rows), ybufs[k], sem).wait()

    ew = ewcol_ref[...]
    y = (ew[:, 0:1] * _tiles_to_rows(ybuf0, t_rows)
         + ew[:, 1:2] * _tiles_to_rows(ybuf1, t_rows))
    gate2 = mod_ref[0, 5:6, :]
    out_ref[...] = _layer_norm(DEEPNORM_ALPHA * x1_ref[...] + (1.0 + gate2) * y,
                               ln2g_ref[...], ln2b_ref[...])


def _combine(pad_start, rint, x1, mod, ln2_g, ln2_b, ewcol, ys, *, seq):
    n, d = x1.shape
    t = min(COMBINE_ROWS, seq)
    per_seq = seq // t
    row_blk = lambda i, ps: (i, 0)
    grid_spec = pltpu.PrefetchScalarGridSpec(
        num_scalar_prefetch=1,
        grid=(n // t,),
        in_specs=[
            pl.BlockSpec((SUBLANES, t), lambda i, ps: (0, i), memory_space=pltpu.SMEM),
            pl.BlockSpec((t, d), row_blk),
            pl.BlockSpec((1, N_MOD, d), lambda i, ps: (i // per_seq, 0, 0)),
            pl.BlockSpec((1, d), lambda i, ps: (0, 0)),
            pl.BlockSpec((1, d), lambda i, ps: (0, 0)),
            pl.BlockSpec((t, LANES), row_blk),
            pl.BlockSpec(memory_space=pl.ANY),
        ],
        out_specs=pl.BlockSpec((t, d), row_blk),
        scratch_shapes=[
            pltpu.VMEM((t * SUBLANES, LANES), F32),
            pltpu.VMEM((t * SUBLANES, LANES), F32),
            pltpu.SemaphoreType.DMA(()),
        ],
    )
    return pl.pallas_call(
        _combine_kernel,
        grid_spec=grid_spec,
        out_shape=jax.ShapeDtypeStruct((n, d), F32),
        compiler_params=pltpu.CompilerParams(
            dimension_semantics=("arbitrary",), vmem_limit_bytes=VMEM_LIMIT_BYTES),
        name="combine",
    )(pad_start, rint, x1, mod, ln2_g, ln2_b, ewcol, ys)


def _block_layout(counts, n_assign):
    bm = EXPERT_ROWS
    n_blocks = -(-n_assign // bm) + N_EXPERTS
    padded = (counts + bm - 1) // bm * bm
    pad_end = jnp.cumsum(padded)
    pad_start = (pad_end - padded).astype(I32)
    n_used = (pad_end[-1] // bm).astype(I32)
    blk_row = jnp.minimum(jnp.arange(n_blocks, dtype=I32), n_used - 1) * bm
    block_expert = jnp.minimum(
        jnp.sum(pad_end[None, :] <= blk_row[:, None], axis=1), N_EXPERTS - 1).astype(I32)
    tail_row = jnp.where(padded > 0, pad_end - bm, -1).astype(I32)
    return pad_start, tail_row, block_expert, n_used.reshape(1), n_blocks


def kernel(x, c, w_ada, b_ada, w_in, conv_w, w_out_conv, w_pool, pool_scale, w_o, ln1_g, ln1_b,
           w_group, b_group, w_router, b_router, w_gate, w_up, w_down, ln2_g, ln2_b):
    bsz, seq, d = x.shape
    n = bsz * seq
    assert d == SUBLANES * LANES and seq % MIX_ROWS == 0 and w_ada.shape[0] == DEPTH
    x2d = x.reshape(n, d)
    c_pad = jnp.pad(c, ((0, -bsz % SUBLANES), (0, 0)))
    for l in range(DEPTH):
        mod = _ada(c_pad, w_ada[l], b_ada[l][None, :])[:bsz].reshape(bsz, N_MOD, d)
        wr = jnp.concatenate(
            [w_router[l], w_group[l],
             jnp.zeros((d, LANES - N_EXPERTS - N_GROUPS), F32)], axis=1)
        rb = jnp.concatenate(
            [b_router[l].reshape(-1), b_group[l],
             jnp.zeros((LANES - N_EXPERTS - N_GROUPS,), F32)])[:, None]
        x1, hrow, rint, ewcol, cnt = _mix(
            x2d, mod, w_in[l].astype(BF16), conv_w[l], w_out_conv[l].astype(BF16),
            w_pool[l].astype(BF16), pool_scale[l][None, :], w_o[l].astype(BF16),
            ln1_g[l][None, :], ln1_b[l][None, :], wr, rb, batch=bsz, seq=seq)
        pad_start, tail_row, block_expert, n_used, n_blocks = _block_layout(
            cnt[:, 0], n * TOP_K)
        xs = _dispatch(pad_start, tail_row, rint, hrow, n_blocks)
        ys = _experts(block_expert, n_used, xs, w_gate[l], w_up[l], w_down[l])
        x2d = _combine(pad_start, rint, x1, mod, ln2_g[l][None, :], ln2_b[l][None, :],
                       ewcol, ys, seq=seq)
    return x2d.reshape(bsz, seq, d)
```

```python
import functools

import jax
import jax.numpy as jnp
from jax import lax
from jax.experimental import pallas as pl
from jax.experimental.pallas import tpu as pltpu

F32 = jnp.float32
BF16 = jnp.bfloat16
I32 = jnp.int32

LANES = 128
SUBLANES = 8
VMEM_LIMIT_BYTES = 56 * 1024 * 1024

N_POOL_GROUPS = 4
POOL_WINDOWS = (2, 4, 8, 16)
CONV_WIDTH = 3
N_GROUPS = 8
EXPERTS_PER_GROUP = 8
N_EXPERTS = N_GROUPS * EXPERTS_PER_GROUP
TOP_K = 2
N_MOD = 6
LN_EPS = 1e-5
DEPTH = 1
DEEPNORM_ALPHA = (2.0 * DEPTH) ** 0.25

MIX_ROWS = 512
EXPERT_ROWS = 256
DISPATCH_ROWS = 2048
COMBINE_ROWS = 512
HALO = SUBLANES
ISSUE_UNROLL = 8


def _split_bf16(a):
    hi = a.astype(BF16)
    lo = (a - hi.astype(F32)).astype(BF16)
    return hi, lo


def _dot(a, b):
    return jnp.dot(a, b, preferred_element_type=F32)


def _dot3(a, b):
    a_hi, a_lo = _split_bf16(a)
    b_hi, b_lo = _split_bf16(b)
    return _dot(a_hi, b_hi) + _dot(a_lo, b_hi) + _dot(a_hi, b_lo)


def _layer_norm(r, g, b):
    mu = jnp.mean(r, axis=-1, keepdims=True)
    d = r - mu
    var = jnp.mean(d * d, axis=-1, keepdims=True)
    return d * lax.rsqrt(var + LN_EPS) * g + b


def _rows_to_tiles(ref, val):
    rows = val.shape[0]
    for c in range(SUBLANES):
        ref[pl.ds(c, rows, stride=SUBLANES), :] = val[:, c * LANES:(c + 1) * LANES]


def _tiles_to_rows(ref, rows):
    return jnp.concatenate(
        [ref[pl.ds(c, rows, stride=SUBLANES), :] for c in range(SUBLANES)], axis=1)


def _ada_kernel(c_ref, w_ref, b_ref, o_ref):
    cc = c_ref[...]
    act = cc * jax.nn.sigmoid(cc)
    o_ref[...] = _dot3(act, w_ref[...]) + b_ref[...]


def _ada(c_pad, w_ada, b_ada):
    rows, d = c_pad.shape
    n_out = w_ada.shape[1]
    return pl.pallas_call(
        _ada_kernel,
        grid=(n_out // d,),
        in_specs=[
            pl.BlockSpec((rows, d), lambda j: (0, 0)),
            pl.BlockSpec((d, d), lambda j: (0, j)),
            pl.BlockSpec((1, d), lambda j: (0, j)),
        ],
        out_specs=pl.BlockSpec((rows, d), lambda j: (0, j)),
        out_shape=jax.ShapeDtypeStruct((rows, n_out), F32),
        compiler_params=pltpu.CompilerParams(
            dimension_semantics=("arbitrary",), vmem_limit_bytes=VMEM_LIMIT_BYTES),
        name="ada",
    )(c_pad, w_ada, b_ada)


def _mix_kernel(x_ref, mod_ref, win_ref, convw_ref, woc_ref, wpool_ref, pscale_ref, wo_ref,
                ln1g_ref, ln1b_ref, wr_ref, rb_ref,
                x1_ref, hrow_ref, rint_ref, ewcol_ref, cnt_ref,
                cvbuf, zbuf, s2buf, s4buf, s8buf, base_ref):
    t_rows, d = x_ref.shape
    pg = d // N_POOL_GROUPS
    b = pl.program_id(0)
    s = pl.program_id(1)
    shift_bufs = (cvbuf, zbuf, s2buf, s4buf, s8buf)

    @pl.when(jnp.logical_and(b == 0, s == 0))
    def _():
        base_ref[...] = jnp.zeros_like(base_ref)

    @pl.when(s == 0)
    def _():
        for buf in shift_bufs:
            buf[0:HALO, :] = jnp.zeros((HALO, buf.shape[1]), F32)

    @pl.when(s > 0)
    def _():
        for buf in shift_bufs:
            buf[0:HALO, :] = buf[t_rows:t_rows + HALO, :]

    def shifted(buf, k):
        return buf[HALO - k:HALO - k + t_rows, :]

    xt = x_ref[...]
    shift1 = mod_ref[0, 0:1, :]
    scale1 = mod_ref[0, 1:2, :]
    gate1 = mod_ref[0, 2:3, :]
    shift2 = mod_ref[0, 3:4, :]
    scale2 = mod_ref[0, 4:5, :]

    u = (xt * (1.0 + scale1) + shift1).astype(BF16)

    def proj(i):
        return _dot(u, win_ref[:, i * d:(i + 1) * d])

    cv = proj(2) * proj(0)
    cvbuf[HALO:HALO + t_rows, :] = cv
    conv = (convw_ref[2:3, :] * cv + convw_ref[1:2, :] * shifted(cvbuf, 1)
            + convw_ref[0:1, :] * shifted(cvbuf, 2))
    y_a = _dot((conv * proj(1)).astype(BF16), woc_ref[...])

    z = proj(3)
    zbuf[HALO:HALO + t_rows, :] = z
    s2 = z + shifted(zbuf, 1)
    s2buf[HALO:HALO + t_rows, :] = s2[:, pg:]
    s4 = s2[:, pg:] + shifted(s2buf, 2)
    s4buf[HALO:HALO + t_rows, :] = s4[:, pg:]
    s8 = s4[:, pg:] + shifted(s4buf, 4)
    s8buf[HALO:HALO + t_rows, :] = s8[:, pg:]
    s16 = s8[:, pg:] + shifted(s8buf, 8)
    win_sums = (s2[:, :pg], s4[:, :pg], s8[:, :pg], s16)
    t_pos = s * t_rows + lax.broadcasted_iota(I32, (t_rows, 1), 0) + 1
    y_b_parts = []
    for gi, w in enumerate(POOL_WINDOWS):
        count = jnp.minimum(t_pos, w).astype(F32)
        pooled = win_sums[gi] / count - z[:, gi * pg:(gi + 1) * pg]
        y_b_parts.append(_dot(pooled.astype(BF16), wpool_ref[gi]))
    y_b = jnp.concatenate(y_b_parts, axis=1) * pscale_ref[...]

    merged = jax.nn.sigmoid(proj(4)) * y_a + jax.nn.sigmoid(proj(5)) * y_b
    mix = _dot(merged.astype(BF16), wo_ref[...])

    x1 = _layer_norm(DEEPNORM_ALPHA * xt + (1.0 + gate1) * mix, ln1g_ref[...], ln1b_ref[...])
    x1_ref[...] = x1
    h = x1 * (1.0 + scale2) + shift2
    _rows_to_tiles(hrow_ref, h)

    lt = _dot3(h, wr_ref[...]).T + rb_ref[...]
    iota8 = lax.broadcasted_iota(I32, (SUBLANES, t_rows), 0)
    lg = lt[N_EXPERTS:N_EXPERTS + N_GROUPS, :]
    g_max = jnp.max(lg, axis=0, keepdims=True)
    g_top = jnp.min(jnp.where(lg == g_max, iota8, N_GROUPS), axis=0, keepdims=True)
    g_top_p = 1.0 / jnp.sum(jnp.exp(lg - g_max), axis=0, keepdims=True)
    sel = jnp.zeros((EXPERTS_PER_GROUP, t_rows), F32)
    for g in range(N_GROUPS):
        sel = jnp.where(g_top == g, lt[g * EXPERTS_PER_GROUP:(g + 1) * EXPERTS_PER_GROUP, :], sel)
    m1 = jnp.max(sel, axis=0, keepdims=True)
    i1 = jnp.min(jnp.where(sel == m1, iota8, EXPERTS_PER_GROUP), axis=0, keepdims=True)
    sel2 = jnp.where(iota8 == i1, -jnp.inf, sel)
    m2 = jnp.max(sel2, axis=0, keepdims=True)
    i2 = jnp.min(jnp.where(sel2 == m2, iota8, EXPERTS_PER_GROUP), axis=0, keepdims=True)
    ratio = jnp.exp(m2 - m1)
    p1 = 1.0 / (1.0 + ratio)
    w1 = g_top_p * p1
    w2 = g_top_p * (ratio * p1)
    e1 = g_top * EXPERTS_PER_GROUP + i1
    e2 = g_top * EXPERTS_PER_GROUP + i2

    iota_e = lax.broadcasted_iota(I32, (N_EXPERTS, t_rows), 0)
    hit1 = iota_e == e1
    hit2 = iota_e == e2
    onehot = jnp.where(jnp.logical_or(hit1, hit2), 1.0, 0.0)
    before = (lax.broadcasted_iota(I32, (t_rows, t_rows), 0)
              < lax.broadcasted_iota(I32, (t_rows, t_rows), 1))
    prefix = _dot(onehot.astype(BF16), jnp.where(before, 1.0, 0.0).astype(BF16))
    total = prefix + base_ref[...]
    r1 = jnp.sum(jnp.where(hit1, total, 0.0), axis=0, keepdims=True).astype(I32)
    r2 = jnp.sum(jnp.where(hit2, total, 0.0), axis=0, keepdims=True).astype(I32)
    new_base = base_ref[...] + jnp.sum(onehot, axis=1, keepdims=True)
    base_ref[...] = new_base
    cnt_ref[...] = jnp.broadcast_to(new_base, cnt_ref.shape).astype(I32)

    rint_ref[...] = jnp.where(iota8 == 0, e1, jnp.where(iota8 == 1, e2,
                              jnp.where(iota8 == 2, r1, jnp.where(iota8 == 3, r2, 0))))
    iota_l = lax.broadcasted_iota(I32, (LANES, t_rows), 0)
    ewcol_ref[...] = jnp.where(iota_l == 0, w1, jnp.where(iota_l == 1, w2, 0.0)).T


def _mix(x2d, mod, w_in, conv_w, w_oc, w_pool, pool_scale, w_o, ln1_g, ln1_b, wr, rb,
         *, batch, seq):
    n, d = x2d.shape
    t = MIX_ROWS
    n_s = seq // t
    const = lambda shape: pl.BlockSpec(shape, lambda b, s: (0,) * len(shape),
                                       pipeline_mode=pl.Buffered(1))
    row_blk = lambda b, s: (b * n_s + s, 0)
    pg = d // N_POOL_GROUPS
    return pl.pallas_call(
        _mix_kernel,
        grid=(batch, n_s),
        in_specs=[
            pl.BlockSpec((t, d), row_blk),
            pl.BlockSpec((1, N_MOD, d), lambda b, s: (b, 0, 0)),
            const(w_in.shape), const(conv_w.shape), const(w_oc.shape), const(w_pool.shape),
            const(pool_scale.shape), const(w_o.shape), const(ln1_g.shape), const(ln1_b.shape),
            const(wr.shape), const(rb.shape),
        ],
        out_specs=[
            pl.BlockSpec((t, d), row_blk),
            pl.BlockSpec((t * SUBLANES, LANES), row_blk),
            pl.BlockSpec((SUBLANES, t), lambda b, s: (0, b * n_s + s)),
            pl.BlockSpec((t, LANES), row_blk),
            pl.BlockSpec((N_EXPERTS, LANES), lambda b, s: (0, 0)),
        ],
        out_shape=[
            jax.ShapeDtypeStruct((n, d), F32),
            jax.ShapeDtypeStruct((n * SUBLANES, LANES), F32),
            jax.ShapeDtypeStruct((SUBLANES, n), I32),
            jax.ShapeDtypeStruct((n, LANES), F32),
            jax.ShapeDtypeStruct((N_EXPERTS, LANES), I32),
        ],
        scratch_shapes=[
            pltpu.VMEM((t + HALO, d), F32),
            pltpu.VMEM((t + HALO, d), F32),
            pltpu.VMEM((t + HALO, 3 * pg), F32),
            pltpu.VMEM((t + HALO, 2 * pg), F32),
            pltpu.VMEM((t + HALO, pg), F32),
            pltpu.VMEM((N_EXPERTS, 1), F32),
        ],
        compiler_params=pltpu.CompilerParams(
            dimension_semantics=("arbitrary", "arbitrary"), vmem_limit_bytes=VMEM_LIMIT_BYTES),
        name="mix",
    )(x2d, mod, w_in, conv_w, w_oc, w_pool, pool_scale, w_o, ln1_g, ln1_b, wr, rb)


def _tile_rows(ref, row, n_rows=1):
    return ref.at[pl.ds(pl.multiple_of(row * SUBLANES, SUBLANES), n_rows * SUBLANES), :]


def _dispatch_kernel(pstart_ref, tail_ref, rint_ref, hrow_ref, xs_hbm, zero_blk, sem):
    n_tok = rint_ref.shape[1]
    blk_tok = zero_blk.shape[0] // SUBLANES

    @pl.when(pl.program_id(0) == 0)
    def _():
        zero_blk[...] = jnp.zeros_like(zero_blk)

        def fill_copy(e):
            return pltpu.make_async_copy(zero_blk, _tile_rows(xs_hbm, tail_ref[e], blk_tok), sem)

        def fill(e, carry):
            @pl.when(tail_ref[e] >= 0)
            def _():
                fill_copy(e).start()
            return carry

        def fill_done(e, carry):
            @pl.when(tail_ref[e] >= 0)
            def _():
                fill_copy(e).wait()
            return carry

        lax.fori_loop(0, N_EXPERTS, fill, 0)
        lax.fori_loop(0, N_EXPERTS, fill_done, 0)

    def issue(i, carry):
        for j in range(ISSUE_UNROLL):
            t = i * ISSUE_UNROLL + j
            for k in range(TOP_K):
                dest = pstart_ref[rint_ref[k, t]] + rint_ref[TOP_K + k, t]
                pltpu.make_async_copy(_tile_rows(hrow_ref, t), _tile_rows(xs_hbm, dest), sem).start()
        return carry

    lax.fori_loop(0, n_tok // ISSUE_UNROLL, issue, 0)
    for k in range(TOP_K):
        pltpu.make_async_copy(hrow_ref, _tile_rows(xs_hbm, 0, n_tok), sem).wait()


def _dispatch(pad_start, tail_row, rint, hrow, n_blocks):
    n = rint.shape[1]
    td = min(DISPATCH_ROWS, n)
    grid_spec = pltpu.PrefetchScalarGridSpec(
        num_scalar_prefetch=2,
        grid=(n // td,),
        in_specs=[
            pl.BlockSpec((SUBLANES, td), lambda i, ps, tl: (0, i), memory_space=pltpu.SMEM),
            pl.BlockSpec((td * SUBLANES, LANES), lambda i, ps, tl: (i, 0)),
        ],
        out_specs=pl.BlockSpec(memory_space=pl.ANY),
        scratch_shapes=[
            pltpu.VMEM((EXPERT_ROWS * SUBLANES, LANES), F32),
            pltpu.SemaphoreType.DMA(()),
        ],
    )
    return pl.pallas_call(
        _dispatch_kernel,
        grid_spec=grid_spec,
        out_shape=jax.ShapeDtypeStruct((n_blocks * EXPERT_ROWS * SUBLANES, LANES), F32),
        compiler_params=pltpu.CompilerParams(
            dimension_semantics=("arbitrary",), vmem_limit_bytes=VMEM_LIMIT_BYTES,
            has_side_effects=True),
        name="dispatch",
    )(pad_start, tail_row, rint, hrow)


def _expert_kernel(blk0_ref, nblk_ref, wsel_ref, nused_ref, xs_hbm, wg_ref, wu_ref, wd_ref, ys_hbm,
                   xbuf0, xbuf1, ybuf0, ybuf1, wg_bf, wu_bf, wd_bf, sem_in, sem_out):
    del wsel_ref
    e = pl.program_id(0)
    blk_tok = xbuf0.shape[0] // SUBLANES
    xbufs = (xbuf0, xbuf1)
    ybufs = (ybuf0, ybuf1)
    n_used = nused_ref[0]

    def in_copy(g, slot):
        return pltpu.make_async_copy(
            _tile_rows(xs_hbm, g * blk_tok, blk_tok), xbufs[slot], sem_in.at[slot])

    def out_copy(g, slot):
        return pltpu.make_async_copy(
            ybufs[slot], _tile_rows(ys_hbm, g * blk_tok, blk_tok), sem_out.at[slot])

    @pl.when(e == 0)
    def _():
        in_copy(0, 0).start()

    @pl.when(nblk_ref[e] > 0)
    def _():
        wg_bf[...] = wg_ref[0].astype(BF16)
        wu_bf[...] = wu_ref[0].astype(BF16)
        wd_bf[...] = wd_ref[0].astype(BF16)

    def block(j, carry):
        g = blk0_ref[e] + j
        for slot in range(2):
            @pl.when(g % 2 == slot)
            def _():
                in_copy(g, slot).wait()

                @pl.when(g + 1 < n_used)
                def _():
                    in_copy(g + 1, 1 - slot).start()

                @pl.when(g >= 2)
                def _():
                    out_copy(g - 2, slot).wait()

                xb = _tiles_to_rows(xbufs[slot], blk_tok).astype(BF16)
                gate = _dot(xb, wg_bf[...])
                up = _dot(xb, wu_bf[...])
                act = (gate * jax.nn.sigmoid(gate) * up).astype(BF16)
                _rows_to_tiles(ybufs[slot], _dot(act, wd_bf[...]))
                out_copy(g, slot).start()
        return carry

    lax.fori_loop(0, nblk_ref[e], block, 0)

    @pl.when(e == pl.num_programs(0) - 1)
    def _():
        for slot in range(2):
            @pl.when(n_used > slot)
            def _():
                g_last = n_used - 1 - ((n_used - 1 - slot) % 2)
                out_copy(g_last, slot).wait()


def _experts(blk0, nblk, wsel, n_used, xs, w_gate, w_up, w_down):
    bm = EXPERT_ROWS
    n_exp, d, de = w_gate.shape
    wblk = lambda e, b0, nb, ws, nu: (ws[e], 0, 0)
    grid_spec = pltpu.PrefetchScalarGridSpec(
        num_scalar_prefetch=4,
        grid=(n_exp,),
        in_specs=[
            pl.BlockSpec(memory_space=pl.ANY),
            pl.BlockSpec((1, d, de), wblk),
            pl.BlockSpec((1, d, de), wblk),
            pl.BlockSpec((1, de, d), wblk),
        ],
        out_specs=pl.BlockSpec(memory_space=pl.ANY),
        scratch_shapes=[
            pltpu.VMEM((bm * SUBLANES, LANES), F32),
            pltpu.VMEM((bm * SUBLANES, LANES), F32),
            pltpu.VMEM((bm * SUBLANES, LANES), F32),
            pltpu.VMEM((bm * SUBLANES, LANES), F32),
            pltpu.VMEM((d, de), BF16),
            pltpu.VMEM((d, de), BF16),
            pltpu.VMEM((de, d), BF16),
            pltpu.SemaphoreType.DMA((2,)),
            pltpu.SemaphoreType.DMA((2,)),
        ],
    )
    return pl.pallas_call(
        _expert_kernel,
        grid_spec=grid_spec,
        out_shape=jax.ShapeDtypeStruct(xs.shape, F32),
        compiler_params=pltpu.CompilerParams(
            dimension_semantics=("arbitrary",), vmem_limit_bytes=VMEM_LIMIT_BYTES,
            has_side_effects=True),
        name="experts",
    )(blk0, nblk, wsel, n_used, xs, w_gate, w_up, w_down)


def _combine_kernel(pstart_ref, rint_ref, rint_next_ref, x1_ref, mod_ref, ln2g_ref, ln2b_ref,
                    ewcol_ref, ys_hbm, out_ref, ybuf00, ybuf01, ybuf10, ybuf11, sem):
    i = pl.program_id(0)
    t_rows = x1_ref.shape[0]
    ybufs = ((ybuf00, ybuf01), (ybuf10, ybuf11))

    def gather(idx_ref, slot):
        def issue(it, carry):
            for j in range(ISSUE_UNROLL):
                t = it * ISSUE_UNROLL + j
                for k in range(TOP_K):
                    src = pstart_ref[idx_ref[k, t]] + idx_ref[TOP_K + k, t]
                    pltpu.make_async_copy(
                        _tile_rows(ys_hbm, src), _tile_rows(ybufs[slot][k], t), sem.at[slot]).start()
            return carry

        lax.fori_loop(0, t_rows // ISSUE_UNROLL, issue, 0)

    @pl.when(i == 0)
    def _():
        gather(rint_ref, 0)

    for slot in range(2):
        @pl.when(i % 2 == slot)
        def _():
            @pl.when(i + 1 < pl.num_programs(0))
            def _():
                gather(rint_next_ref, 1 - slot)

            for k in range(TOP_K):
                pltpu.make_async_copy(
                    _tile_rows(ys_hbm, 0, t_rows), ybufs[slot][k], sem.at[slot]).wait()
            ew = ewcol_ref[...]
            y = (ew[:, 0:1] * _tiles_to_rows(ybufs[slot][0], t_rows)
                 + ew[:, 1:2] * _tiles_to_rows(ybufs[slot][1], t_rows))
            gate2 = mod_ref[0, 5:6, :]
            out_ref[...] = _layer_norm(DEEPNORM_ALPHA * x1_ref[...] + (1.0 + gate2) * y,
                                       ln2g_ref[...], ln2b_ref[...])


def _combine(pad_start, rint, x1, mod, ln2_g, ln2_b, ewcol, ys, *, seq):
    n, d = x1.shape
    t = min(COMBINE_ROWS, seq)
    per_seq = seq // t
    n_steps = n // t
    row_blk = lambda i, ps: (i, 0)
    grid_spec = pltpu.PrefetchScalarGridSpec(
        num_scalar_prefetch=1,
        grid=(n_steps,),
        in_specs=[
            pl.BlockSpec((SUBLANES, t), lambda i, ps: (0, i), memory_space=pltpu.SMEM),
            pl.BlockSpec((SUBLANES, t), lambda i, ps: (0, jnp.minimum(i + 1, n_steps - 1)),
                         memory_space=pltpu.SMEM),
            pl.BlockSpec((t, d), row_blk),
            pl.BlockSpec((1, N_MOD, d), lambda i, ps: (i // per_seq, 0, 0)),
            pl.BlockSpec((1, d), lambda i, ps: (0, 0)),
            pl.BlockSpec((1, d), lambda i, ps: (0, 0)),
            pl.BlockSpec((t, LANES), row_blk),
            pl.BlockSpec(memory_space=pl.ANY),
        ],
        out_specs=pl.BlockSpec((t, d), row_blk),
        scratch_shapes=[pltpu.VMEM((t * SUBLANES, LANES), F32)] * (2 * TOP_K)
        + [pltpu.SemaphoreType.DMA((2,))],
    )
    return pl.pallas_call(
        _combine_kernel,
        grid_spec=grid_spec,
        out_shape=jax.ShapeDtypeStruct((n, d), F32),
        compiler_params=pltpu.CompilerParams(
            dimension_semantics=("arbitrary",), vmem_limit_bytes=VMEM_LIMIT_BYTES),
        name="combine",
    )(pad_start, rint, rint, x1, mod, ln2_g, ln2_b, ewcol, ys)


def _block_layout(counts, n_assign):
    bm = EXPERT_ROWS
    n_blocks = -(-n_assign // bm) + N_EXPERTS
    padded = (counts + bm - 1) // bm * bm
    pad_end = jnp.cumsum(padded)
    pad_start = (pad_end - padded).astype(I32)
    n_used = (pad_end[-1] // bm).astype(I32)
    tail_row = jnp.where(padded > 0, pad_end - bm, -1).astype(I32)
    ids = jnp.arange(N_EXPERTS, dtype=I32)
    prev_used = lax.cummax(jnp.where(padded > 0, ids, -1))
    first_used = jnp.argmax(padded > 0).astype(I32)
    weight_sel = jnp.where(prev_used >= 0, prev_used, first_used).astype(I32)
    return (pad_start, tail_row, pad_start // bm, (padded // bm).astype(I32), weight_sel,
            n_used.reshape(1), n_blocks)


def kernel(x, c, w_ada, b_ada, w_in, conv_w, w_out_conv, w_pool, pool_scale, w_o, ln1_g, ln1_b,
           w_group, b_group, w_router, b_router, w_gate, w_up, w_down, ln2_g, ln2_b):
    bsz, seq, d = x.shape
    n = bsz * seq
    assert d == SUBLANES * LANES and seq % MIX_ROWS == 0 and w_ada.shape[0] == DEPTH
    x2d = x.reshape(n, d)
    c_pad = jnp.pad(c, ((0, -bsz % SUBLANES), (0, 0)))
    for l in range(DEPTH):
        mod = _ada(c_pad, w_ada[l], b_ada[l][None, :])[:bsz].reshape(bsz, N_MOD, d)
        wr = jnp.concatenate(
            [w_router[l], w_group[l],
             jnp.zeros((d, LANES - N_EXPERTS - N_GROUPS), F32)], axis=1)
        rb = jnp.concatenate(
            [b_router[l].reshape(-1), b_group[l],
             jnp.zeros((LANES - N_EXPERTS - N_GROUPS,), F32)])[:, None]
        x1, hrow, rint, ewcol, cnt = _mix(
            x2d, mod, w_in[l].astype(BF16), conv_w[l], w_out_conv[l].astype(BF16),
            w_pool[l].astype(BF16), pool_scale[l][None, :], w_o[l].astype(BF16),
            ln1_g[l][None, :], ln1_b[l][None, :], wr, rb, batch=bsz, seq=seq)
        pad_start, tail_row, blk0, nblk, weight_sel, n_used, n_blocks = _block_layout(
            cnt[:, 0], n * TOP_K)
        xs = _dispatch(pad_start, tail_row, rint, hrow, n_blocks)
        ys = _experts(blk0, nblk, weight_sel, n_used, xs, w_gate[l], w_up[l], w_down[l])
        x2d = _combine(pad_start, rint, x1, mod, ln2_g[l][None, :], ln2_b[l][None, :],
                       ewcol, ys, seq=seq)
    return x2d.reshape(bsz, seq, d)
```

```python
import functools

import jax
import jax.numpy as jnp
from jax import lax
from jax.experimental import pallas as pl
from jax.experimental.pallas import tpu as pltpu

F32 = jnp.float32
BF16 = jnp.bfloat16
I32 = jnp.int32

LANES = 128
SUBLANES = 8
VMEM_LIMIT_BYTES = 56 * 1024 * 1024

N_POOL_GROUPS = 4
POOL_WINDOWS = (2, 4, 8, 16)
CONV_WIDTH = 3
N_GROUPS = 8
EXPERTS_PER_GROUP = 8
N_EXPERTS = N_GROUPS * EXPERTS_PER_GROUP
TOP_K = 2
N_MOD = 6
LN_EPS = 1e-5
DEPTH = 1
DEEPNORM_ALPHA = (2.0 * DEPTH) ** 0.25

MIX_ROWS = 512
EXPERT_ROWS = 256
DISPATCH_ROWS = 2048
COMBINE_ROWS = 512
HALO = SUBLANES
ISSUE_UNROLL = 8


def _split_bf16(a):
    hi = a.astype(BF16)
    lo = (a - hi.astype(F32)).astype(BF16)
    return hi, lo


def _dot(a, b):
    return jnp.dot(a, b, preferred_element_type=F32)


def _dot3(a, b):
    a_hi, a_lo = _split_bf16(a)
    b_hi, b_lo = _split_bf16(b)
    return _dot(a_hi, b_hi) + _dot(a_lo, b_hi) + _dot(a_hi, b_lo)


def _layer_norm(r, g, b):
    mu = jnp.mean(r, axis=-1, keepdims=True)
    d = r - mu
    var = jnp.mean(d * d, axis=-1, keepdims=True)
    return d * lax.rsqrt(var + LN_EPS) * g + b


def _rows_to_tiles(ref, val):
    rows = val.shape[0]
    for c in range(SUBLANES):
        ref[pl.ds(c, rows, stride=SUBLANES), :] = val[:, c * LANES:(c + 1) * LANES]


def _tiles_to_rows(ref, rows):
    return jnp.concatenate(
        [ref[pl.ds(c, rows, stride=SUBLANES), :] for c in range(SUBLANES)], axis=1)


def _ada_kernel(c_ref, w_ref, b_ref, o_ref):
    cc = c_ref[...]
    act = cc * jax.nn.sigmoid(cc)
    o_ref[...] = _dot3(act, w_ref[...]) + b_ref[...]


def _ada(c_pad, w_ada, b_ada):
    rows, d = c_pad.shape
    n_out = w_ada.shape[1]
    return pl.pallas_call(
        _ada_kernel,
        grid=(n_out // d,),
        in_specs=[
            pl.BlockSpec((rows, d), lambda j: (0, 0)),
            pl.BlockSpec((d, d), lambda j: (0, j)),
            pl.BlockSpec((1, d), lambda j: (0, j)),
        ],
        out_specs=pl.BlockSpec((rows, d), lambda j: (0, j)),
        out_shape=jax.ShapeDtypeStruct((rows, n_out), F32),
        compiler_params=pltpu.CompilerParams(
            dimension_semantics=("arbitrary",), vmem_limit_bytes=VMEM_LIMIT_BYTES),
        name="ada",
    )(c_pad, w_ada, b_ada)


def _mix_kernel(x_ref, mod_ref, win_ref, convw_ref, woc_ref, wpool_ref, pscale_ref, wo_ref,
                ln1g_ref, ln1b_ref, wr_ref, rb_ref,
                x1_ref, h_ref, loc_ref, ewcol_ref, tab_ref, cnt_ref,
                cvbuf, zbuf, s2buf, s4buf, s8buf, base_ref):
    t_rows, d = x_ref.shape
    pg = d // N_POOL_GROUPS
    b = pl.program_id(0)
    s = pl.program_id(1)
    shift_bufs = (cvbuf, zbuf, s2buf, s4buf, s8buf)

    @pl.when(jnp.logical_and(b == 0, s == 0))
    def _():
        base_ref[...] = jnp.zeros_like(base_ref)

    @pl.when(s == 0)
    def _():
        for buf in shift_bufs:
            buf[0:HALO, :] = jnp.zeros((HALO, buf.shape[1]), F32)

    @pl.when(s > 0)
    def _():
        for buf in shift_bufs:
            buf[0:HALO, :] = buf[t_rows:t_rows + HALO, :]

    def shifted(buf, k):
        return buf[HALO - k:HALO - k + t_rows, :]

    xt = x_ref[...]
    shift1 = mod_ref[0, 0:1, :]
    scale1 = mod_ref[0, 1:2, :]
    gate1 = mod_ref[0, 2:3, :]
    shift2 = mod_ref[0, 3:4, :]
    scale2 = mod_ref[0, 4:5, :]

    u = (xt * (1.0 + scale1) + shift1).astype(BF16)

    def proj(i):
        return _dot(u, win_ref[:, i * d:(i + 1) * d])

    cv = proj(2) * proj(0)
    cvbuf[HALO:HALO + t_rows, :] = cv
    conv = (convw_ref[2:3, :] * cv + convw_ref[1:2, :] * shifted(cvbuf, 1)
            + convw_ref[0:1, :] * shifted(cvbuf, 2))
    y_a = _dot((conv * proj(1)).astype(BF16), woc_ref[...])

    z = proj(3)
    zbuf[HALO:HALO + t_rows, :] = z
    s2 = z + shifted(zbuf, 1)
    s2buf[HALO:HALO + t_rows, :] = s2[:, pg:]
    s4 = s2[:, pg:] + shifted(s2buf, 2)
    s4buf[HALO:HALO + t_rows, :] = s4[:, pg:]
    s8 = s4[:, pg:] + shifted(s4buf, 4)
    s8buf[HALO:HALO + t_rows, :] = s8[:, pg:]
    s16 = s8[:, pg:] + shifted(s8buf, 8)
    win_sums = (s2[:, :pg], s4[:, :pg], s8[:, :pg], s16)
    t_pos = s * t_rows + lax.broadcasted_iota(I32, (t_rows, 1), 0) + 1
    y_b_parts = []
    for gi, w in enumerate(POOL_WINDOWS):
        count = jnp.minimum(t_pos, w).astype(F32)
        pooled = win_sums[gi] / count - z[:, gi * pg:(gi + 1) * pg]
        y_b_parts.append(_dot(pooled.astype(BF16), wpool_ref[gi]))
    y_b = jnp.concatenate(y_b_parts, axis=1) * pscale_ref[...]

    merged = jax.nn.sigmoid(proj(4)) * y_a + jax.nn.sigmoid(proj(5)) * y_b
    mix = _dot(merged.astype(BF16), wo_ref[...])

    x1 = _layer_norm(DEEPNORM_ALPHA * xt + (1.0 + gate1) * mix, ln1g_ref[...], ln1b_ref[...])
    x1_ref[...] = x1
    h = x1 * (1.0 + scale2) + shift2
    h_ref[...] = h.astype(BF16)

    lt = _dot3(h, wr_ref[...]).T + rb_ref[...]
    iota8 = lax.broadcasted_iota(I32, (SUBLANES, t_rows), 0)
    lg = lt[N_EXPERTS:N_EXPERTS + N_GROUPS, :]
    g_max = jnp.max(lg, axis=0, keepdims=True)
    g_top = jnp.min(jnp.where(lg == g_max, iota8, N_GROUPS), axis=0, keepdims=True)
    g_top_p = 1.0 / jnp.sum(jnp.exp(lg - g_max), axis=0, keepdims=True)
    sel = jnp.zeros((EXPERTS_PER_GROUP, t_rows), F32)
    for g in range(N_GROUPS):
        sel = jnp.where(g_top == g, lt[g * EXPERTS_PER_GROUP:(g + 1) * EXPERTS_PER_GROUP, :], sel)
    m1 = jnp.max(sel, axis=0, keepdims=True)
    i1 = jnp.min(jnp.where(sel == m1, iota8, EXPERTS_PER_GROUP), axis=0, keepdims=True)
    sel2 = jnp.where(iota8 == i1, -jnp.inf, sel)
    m2 = jnp.max(sel2, axis=0, keepdims=True)
    i2 = jnp.min(jnp.where(sel2 == m2, iota8, EXPERTS_PER_GROUP), axis=0, keepdims=True)
    ratio = jnp.exp(m2 - m1)
    p1 = 1.0 / (1.0 + ratio)
    w1 = g_top_p * p1
    w2 = g_top_p * (ratio * p1)
    e1 = g_top * EXPERTS_PER_GROUP + i1
    e2 = g_top * EXPERTS_PER_GROUP + i2

    iota_e = lax.broadcasted_iota(I32, (N_EXPERTS, t_rows), 0)
    hit1 = iota_e == e1
    hit2 = iota_e == e2
    onehot = jnp.where(jnp.logical_or(hit1, hit2), 1.0, 0.0)
    before = (lax.broadcasted_iota(I32, (t_rows, t_rows), 0)
              < lax.broadcasted_iota(I32, (t_rows, t_rows), 1))
    prefix = _dot(onehot.astype(BF16), jnp.where(before, 1.0, 0.0).astype(BF16))
    tile_cnt = jnp.sum(onehot, axis=1, keepdims=True)
    lower = (lax.broadcasted_iota(I32, (N_EXPERTS, N_EXPERTS), 1)
             < lax.broadcasted_iota(I32, (N_EXPERTS, N_EXPERTS), 0))
    lower = jnp.where(lower, 1.0, 0.0).astype(BF16)
    cnt_hi, cnt_lo = _split_bf16(jnp.broadcast_to(tile_cnt, (N_EXPERTS, LANES)))
    loc_start = (_dot(lower, cnt_hi) + _dot(lower, cnt_lo))[:, 0:1]
    pos = prefix + loc_start
    loc1 = jnp.sum(jnp.where(hit1, pos, 0.0), axis=0, keepdims=True)
    loc2 = jnp.sum(jnp.where(hit2, pos, 0.0), axis=0, keepdims=True)
    base = base_ref[...]
    base_ref[...] = base + tile_cnt
    cnt_ref[...] = jnp.broadcast_to(base + tile_cnt, cnt_ref.shape).astype(I32)

    diag = (lax.broadcasted_iota(I32, (N_EXPERTS, LANES), 0)
            == lax.broadcasted_iota(I32, (N_EXPERTS, LANES), 1))

    def to_lanes(col):
        return jnp.sum(jnp.where(diag, col, 0.0), axis=0, keepdims=True)

    iota8l = lax.broadcasted_iota(I32, (SUBLANES, LANES), 0)
    tab_ref[...] = jnp.where(
        iota8l == 0, to_lanes(tile_cnt),
        jnp.where(iota8l == 1, to_lanes(base),
                  jnp.where(iota8l == 2, to_lanes(loc_start), 0.0))).astype(I32)

    loc_ref[...] = jnp.where(iota8 == 0, loc1, jnp.where(iota8 == 1, loc2, 0.0)).astype(I32)
    iota_l = lax.broadcasted_iota(I32, (LANES, t_rows), 0)
    ewcol_ref[...] = jnp.where(
        iota_l == 0, w1, jnp.where(iota_l == 1, w2,
                                   jnp.where(iota_l == 2, loc1, jnp.where(iota_l == 3, loc2, 0.0)))).T


def _mix(x2d, mod, w_in, conv_w, w_oc, w_pool, pool_scale, w_o, ln1_g, ln1_b, wr, rb,
         *, batch, seq):
    n, d = x2d.shape
    t = MIX_ROWS
    n_s = seq // t
    const = lambda shape: pl.BlockSpec(shape, lambda b, s: (0,) * len(shape),
                                       pipeline_mode=pl.Buffered(1))
    row_blk = lambda b, s: (b * n_s + s, 0)
    pg = d // N_POOL_GROUPS
    return pl.pallas_call(
        _mix_kernel,
        grid=(batch, n_s),
        in_specs=[
            pl.BlockSpec((t, d), row_blk),
            pl.BlockSpec((1, N_MOD, d), lambda b, s: (b, 0, 0)),
            const(w_in.shape), const(conv_w.shape), const(w_oc.shape), const(w_pool.shape),
            const(pool_scale.shape), const(w_o.shape), const(ln1_g.shape), const(ln1_b.shape),
            const(wr.shape), const(rb.shape),
        ],
        out_specs=[
            pl.BlockSpec((t, d), row_blk),
            pl.BlockSpec((t, d), row_blk),
            pl.BlockSpec((SUBLANES, t), lambda b, s: (0, b * n_s + s)),
            pl.BlockSpec((t, LANES), row_blk),
            pl.BlockSpec((SUBLANES, LANES), row_blk),
            pl.BlockSpec((N_EXPERTS, LANES), lambda b, s: (0, 0)),
        ],
        out_shape=[
            jax.ShapeDtypeStruct((n, d), F32),
            jax.ShapeDtypeStruct((n, d), BF16),
            jax.ShapeDtypeStruct((SUBLANES, n), I32),
            jax.ShapeDtypeStruct((n, LANES), F32),
            jax.ShapeDtypeStruct((n // t * SUBLANES, LANES), I32),
            jax.ShapeDtypeStruct((N_EXPERTS, LANES), I32),
        ],
        scratch_shapes=[
            pltpu.VMEM((t + HALO, d), F32),
            pltpu.VMEM((t + HALO, d), F32),
            pltpu.VMEM((t + HALO, 3 * pg), F32),
            pltpu.VMEM((t + HALO, 2 * pg), F32),
            pltpu.VMEM((t + HALO, pg), F32),
            pltpu.VMEM((N_EXPERTS, 1), F32),
        ],
        compiler_params=pltpu.CompilerParams(
            dimension_semantics=("arbitrary", "arbitrary"), vmem_limit_bytes=VMEM_LIMIT_BYTES),
        name="mix",
    )(x2d, mod, w_in, conv_w, w_oc, w_pool, pool_scale, w_o, ln1_g, ln1_b, wr, rb)


def _tile_rows(ref, row, n_rows=1):
    return ref.at[pl.ds(pl.multiple_of(row * SUBLANES, SUBLANES), n_rows * SUBLANES), :]


TAB_LEN, TAB_RANK, TAB_LOC = 0, 1, 2
FILL_SEM = 2


def _strip_copies(tab_ref, pstart_ref, local_ref, sorted_hbm, sem, *, to_hbm):
    def strip(e, carry):
        n_rows = tab_ref[TAB_LEN, e]

        @pl.when(n_rows > 0)
        def _():
            local = _tile_rows(local_ref, tab_ref[TAB_LOC, e], n_rows)
            remote = _tile_rows(sorted_hbm, pstart_ref[e] + tab_ref[TAB_RANK, e], n_rows)
            src, dst = (local, remote) if to_hbm else (remote, local)
            pltpu.make_async_copy(src, dst, sem).start()
        return carry

    lax.fori_loop(0, N_EXPERTS, strip, 0)


def _dispatch_kernel(pstart_ref, tail_ref, tab_ref, loc_ref, h_ref, xs_hbm,
                     xloc0, xloc1, zero_blk, sem):
    i = pl.program_id(0)
    n_steps = pl.num_programs(0)
    t_rows = h_ref.shape[0]
    n_loc = TOP_K * t_rows
    blk_tok = zero_blk.shape[0] // SUBLANES
    xlocs = (xloc0, xloc1)

    @pl.when(i == 0)
    def _():
        zero_blk[...] = jnp.zeros_like(zero_blk)

        def fill_copy(e):
            return pltpu.make_async_copy(
                zero_blk, _tile_rows(xs_hbm, tail_ref[e], blk_tok), sem.at[FILL_SEM])

        def fill(e, carry):
            @pl.when(tail_ref[e] >= 0)
            def _():
                fill_copy(e).start()
            return carry

        def fill_done(e, carry):
            @pl.when(tail_ref[e] >= 0)
            def _():
                fill_copy(e).wait()
            return carry

        lax.fori_loop(0, N_EXPERTS, fill, 0)
        lax.fori_loop(0, N_EXPERTS, fill_done, 0)

    def strips_done(slot):
        pltpu.make_async_copy(xlocs[slot], _tile_rows(xs_hbm, 0, n_loc), sem.at[slot]).wait()

    for slot in range(2):
        @pl.when(i % 2 == slot)
        def _():
            @pl.when(i >= 2)
            def _():
                strips_done(slot)

            iota_p = lax.broadcasted_iota(I32, (n_loc, t_rows), 0)
            pick = jnp.logical_or(iota_p == loc_ref[0:1, :], iota_p == loc_ref[1:2, :])
            xsorted = _dot(jnp.where(pick, 1.0, 0.0).astype(BF16), h_ref[...])
            _rows_to_tiles(xlocs[slot], xsorted)
            _strip_copies(tab_ref, pstart_ref, xlocs[slot], xs_hbm, sem.at[slot], to_hbm=True)

    @pl.when(i == n_steps - 1)
    def _():
        for slot in range(2):
            @pl.when(n_steps > slot)
            def _():
                strips_done(slot)


def _dispatch(pad_start, tail_row, tab, loc, h, n_blocks):
    n, d = h.shape
    t = MIX_ROWS
    grid_spec = pltpu.PrefetchScalarGridSpec(
        num_scalar_prefetch=2,
        grid=(n // t,),
        in_specs=[
            pl.BlockSpec((SUBLANES, LANES), lambda i, ps, tl: (i, 0), memory_space=pltpu.SMEM),
            pl.BlockSpec((SUBLANES, t), lambda i, ps, tl: (0, i)),
            pl.BlockSpec((t, d), lambda i, ps, tl: (i, 0)),
        ],
        out_specs=pl.BlockSpec(memory_space=pl.ANY),
        scratch_shapes=[
            pltpu.VMEM((TOP_K * t * SUBLANES, LANES), F32),
            pltpu.VMEM((TOP_K * t * SUBLANES, LANES), F32),
            pltpu.VMEM((EXPERT_ROWS * SUBLANES, LANES), F32),
            pltpu.SemaphoreType.DMA((3,)),
        ],
    )
    return pl.pallas_call(
        _dispatch_kernel,
        grid_spec=grid_spec,
        out_shape=jax.ShapeDtypeStruct((n_blocks * EXPERT_ROWS * SUBLANES, LANES), F32),
        compiler_params=pltpu.CompilerParams(
            dimension_semantics=("arbitrary",), vmem_limit_bytes=VMEM_LIMIT_BYTES,
            has_side_effects=True),
        name="dispatch",
    )(pad_start, tail_row, tab, loc, h)


def _expert_kernel(blk0_ref, nblk_ref, wsel_ref, nused_ref, xs_hbm, wg_ref, wu_ref, wd_ref, ys_hbm,
                   xbuf0, xbuf1, ybuf0, ybuf1, wg_bf, wu_bf, wd_bf, sem_in, sem_out):
    del wsel_ref
    e = pl.program_id(0)
    blk_tok = xbuf0.shape[0] // SUBLANES
    xbufs = (xbuf0, xbuf1)
    ybufs = (ybuf0, ybuf1)
    n_used = nused_ref[0]

    def in_copy(g, slot):
        return pltpu.make_async_copy(
            _tile_rows(xs_hbm, g * blk_tok, blk_tok), xbufs[slot], sem_in.at[slot])

    def out_copy(g, slot):
        return pltpu.make_async_copy(
            ybufs[slot], _tile_rows(ys_hbm, g * blk_tok, blk_tok), sem_out.at[slot])

    @pl.when(e == 0)
    def _():
        in_copy(0, 0).start()

    @pl.when(nblk_ref[e] > 0)
    def _():
        wg_bf[...] = wg_ref[0].astype(BF16)
        wu_bf[...] = wu_ref[0].astype(BF16)
        wd_bf[...] = wd_ref[0].astype(BF16)

    def block(j, carry):
        g = blk0_ref[e] + j
        for slot in range(2):
            @pl.when(g % 2 == slot)
            def _():
                in_copy(g, slot).wait()

                @pl.when(g + 1 < n_used)
                def _():
                    in_copy(g + 1, 1 - slot).start()

                @pl.when(g >= 2)
                def _():
                    out_copy(g - 2, slot).wait()

                xb = _tiles_to_rows(xbufs[slot], blk_tok).astype(BF16)
                gate = _dot(xb, wg_bf[...])
                up = _dot(xb, wu_bf[...])
                act = (gate * jax.nn.sigmoid(gate) * up).astype(BF16)
                _rows_to_tiles(ybufs[slot], _dot(act, wd_bf[...]))
                out_copy(g, slot).start()
        return carry

    lax.fori_loop(0, nblk_ref[e], block, 0)

    @pl.when(e == pl.num_programs(0) - 1)
    def _():
        for slot in range(2):
            @pl.when(n_used > slot)
            def _():
                g_last = n_used - 1 - ((n_used - 1 - slot) % 2)
                out_copy(g_last, slot).wait()


def _experts(blk0, nblk, wsel, n_used, xs, w_gate, w_up, w_down):
    bm = EXPERT_ROWS
    n_exp, d, de = w_gate.shape
    wblk = lambda e, b0, nb, ws, nu: (ws[e], 0, 0)
    grid_spec = pltpu.PrefetchScalarGridSpec(
        num_scalar_prefetch=4,
        grid=(n_exp,),
        in_specs=[
            pl.BlockSpec(memory_space=pl.ANY),
            pl.BlockSpec((1, d, de), wblk),
            pl.BlockSpec((1, d, de), wblk),
            pl.BlockSpec((1, de, d), wblk),
        ],
        out_specs=pl.BlockSpec(memory_space=pl.ANY),
        scratch_shapes=[
            pltpu.VMEM((bm * SUBLANES, LANES), F32),
            pltpu.VMEM((bm * SUBLANES, LANES), F32),
            pltpu.VMEM((bm * SUBLANES, LANES), F32),
            pltpu.VMEM((bm * SUBLANES, LANES), F32),
            pltpu.VMEM((d, de), BF16),
            pltpu.VMEM((d, de), BF16),
            pltpu.VMEM((de, d), BF16),
            pltpu.SemaphoreType.DMA((2,)),
            pltpu.SemaphoreType.DMA((2,)),
        ],
    )
    return pl.pallas_call(
        _expert_kernel,
        grid_spec=grid_spec,
        out_shape=jax.ShapeDtypeStruct(xs.shape, F32),
        compiler_params=pltpu.CompilerParams(
            dimension_semantics=("arbitrary",), vmem_limit_bytes=VMEM_LIMIT_BYTES,
            has_side_effects=True),
        name="experts",
    )(blk0, nblk, wsel, n_used, xs, w_gate, w_up, w_down)


def _combine_kernel(pstart_ref, tab_ref, tab_next_ref, x1_ref, mod_ref, ln2g_ref, ln2b_ref,
                    ewcol_ref, ys_hbm, out_ref, ybuf0, ybuf1, sem):
    i = pl.program_id(0)
    t_rows = x1_ref.shape[0]
    n_loc = TOP_K * t_rows
    ybufs = (ybuf0, ybuf1)

    @pl.when(i == 0)
    def _():
        _strip_copies(tab_ref, pstart_ref, ybuf0, ys_hbm, sem.at[0], to_hbm=False)

    for slot in range(2):
        @pl.when(i % 2 == slot)
        def _():
            @pl.when(i + 1 < pl.num_programs(0))
            def _():
                _strip_copies(tab_next_ref, pstart_ref, ybufs[1 - slot], ys_hbm,
                              sem.at[1 - slot], to_hbm=False)

            pltpu.make_async_copy(_tile_rows(ys_hbm, 0, n_loc), ybufs[slot], sem.at[slot]).wait()
            ew = ewcol_ref[...]
            lane = lax.broadcasted_iota(I32, (t_rows, n_loc), 1).astype(F32)
            weights = (jnp.where(lane == ew[:, 2:3], ew[:, 0:1], 0.0)
                       + jnp.where(lane == ew[:, 3:4], ew[:, 1:2], 0.0))
            y = _dot(weights.astype(BF16), _tiles_to_rows(ybufs[slot], n_loc).astype(BF16))
            gate2 = mod_ref[0, 5:6, :]
            out_ref[...] = _layer_norm(DEEPNORM_ALPHA * x1_ref[...] + (1.0 + gate2) * y,
                                       ln2g_ref[...], ln2b_ref[...])


def _combine(pad_start, tab, x1, mod, ln2_g, ln2_b, ewcol, ys, *, seq):
    n, d = x1.shape
    t = MIX_ROWS
    per_seq = seq // t
    n_steps = n // t
    row_blk = lambda i, ps: (i, 0)
    grid_spec = pltpu.PrefetchScalarGridSpec(
        num_scalar_prefetch=1,
        grid=(n_steps,),
        in_specs=[
            pl.BlockSpec((SUBLANES, LANES), row_blk, memory_space=pltpu.SMEM),
            pl.BlockSpec((SUBLANES, LANES), lambda i, ps: (jnp.minimum(i + 1, n_steps - 1), 0),
                         memory_space=pltpu.SMEM),
            pl.BlockSpec((t, d), row_blk),
            pl.BlockSpec((1, N_MOD, d), lambda i, ps: (i // per_seq, 0, 0)),
            pl.BlockSpec((1, d), lambda i, ps: (0, 0)),
            pl.BlockSpec((1, d), lambda i, ps: (0, 0)),
            pl.BlockSpec((t, LANES), row_blk),
            pl.BlockSpec(memory_space=pl.ANY),
        ],
        out_specs=pl.BlockSpec((t, d), row_blk),
        scratch_shapes=[
            pltpu.VMEM((TOP_K * t * SUBLANES, LANES), F32),
            pltpu.VMEM((TOP_K * t * SUBLANES, LANES), F32),
            pltpu.SemaphoreType.DMA((2,)),
        ],
    )
    return pl.pallas_call(
        _combine_kernel,
        grid_spec=grid_spec,
        out_shape=jax.ShapeDtypeStruct((n, d), F32),
        compiler_params=pltpu.CompilerParams(
            dimension_semantics=("arbitrary",), vmem_limit_bytes=VMEM_LIMIT_BYTES),
        name="combine",
    )(pad_start, tab, tab, x1, mod, ln2_g, ln2_b, ewcol, ys)


def _block_layout(counts, n_assign):
    bm = EXPERT_ROWS
    n_blocks = -(-n_assign // bm) + N_EXPERTS
    padded = (counts + bm - 1) // bm * bm
    pad_end = jnp.cumsum(padded)
    pad_start = (pad_end - padded).astype(I32)
    n_used = (pad_end[-1] // bm).astype(I32)
    tail_row = jnp.where(padded > 0, pad_end - bm, -1).astype(I32)
    ids = jnp.arange(N_EXPERTS, dtype=I32)
    prev_used = lax.cummax(jnp.where(padded > 0, ids, -1))
    first_used = jnp.argmax(padded > 0).astype(I32)
    weight_sel = jnp.where(prev_used >= 0, prev_used, first_used).astype(I32)
    return (pad_start, tail_row, pad_start // bm, (padded // bm).astype(I32), weight_sel,
            n_used.reshape(1), n_blocks)


def kernel(x, c, w_ada, b_ada, w_in, conv_w, w_out_conv, w_pool, pool_scale, w_o, ln1_g, ln1_b,
           w_group, b_group, w_router, b_router, w_gate, w_up, w_down, ln2_g, ln2_b):
    bsz, seq, d = x.shape
    n = bsz * seq
    assert d == SUBLANES * LANES and seq % MIX_ROWS == 0 and w_ada.shape[0] == DEPTH
    x2d = x.reshape(n, d)
    c_pad = jnp.pad(c, ((0, -bsz % SUBLANES), (0, 0)))
    for l in range(DEPTH):
        mod = _ada(c_pad, w_ada[l], b_ada[l][None, :])[:bsz].reshape(bsz, N_MOD, d)
        wr = jnp.concatenate(
            [w_router[l], w_group[l],
             jnp.zeros((d, LANES - N_EXPERTS - N_GROUPS), F32)], axis=1)
        rb = jnp.concatenate(
            [b_router[l].reshape(-1), b_group[l],
             jnp.zeros((LANES - N_EXPERTS - N_GROUPS,), F32)])[:, None]
        x1, h, loc, ewcol, tab, cnt = _mix(
            x2d, mod, w_in[l].astype(BF16), conv_w[l], w_out_conv[l].astype(BF16),
            w_pool[l].astype(BF16), pool_scale[l][None, :], w_o[l].astype(BF16),
            ln1_g[l][None, :], ln1_b[l][None, :], wr, rb, batch=bsz, seq=seq)
        pad_start, tail_row, blk0, nblk, weight_sel, n_used, n_blocks = _block_layout(
            cnt[:, 0], n * TOP_K)
        xs = _dispatch(pad_start, tail_row, tab, loc, h, n_blocks)
        ys = _experts(blk0, nblk, weight_sel, n_used, xs, w_gate[l], w_up[l], w_down[l])
        x2d = _combine(pad_start, tab, x1, mod, ln2_g[l][None, :], ln2_b[l][None, :],
                       ewcol, ys, seq=seq)
    return x2d.reshape(bsz, seq, d)
```

```python
import functools

import jax
import jax.numpy as jnp
from jax import lax
from jax.experimental import pallas as pl
from jax.experimental.pallas import tpu as pltpu

F32 = jnp.float32
BF16 = jnp.bfloat16
I32 = jnp.int32

LANES = 128
SUBLANES = 8
VMEM_LIMIT_BYTES = 56 * 1024 * 1024

N_POOL_GROUPS = 4
POOL_WINDOWS = (2, 4, 8, 16)
CONV_WIDTH = 3
N_GROUPS = 8
EXPERTS_PER_GROUP = 8
N_EXPERTS = N_GROUPS * EXPERTS_PER_GROUP
TOP_K = 2
N_MOD = 6
LN_EPS = 1e-5
DEPTH = 1
DEEPNORM_ALPHA = (2.0 * DEPTH) ** 0.25

MIX_ROWS = 512
EXPERT_ROWS = 256
HALO = SUBLANES


def _split_bf16(a):
    hi = a.astype(BF16)
    lo = (a - hi.astype(F32)).astype(BF16)
    return hi, lo


def _dot(a, b):
    return jnp.dot(a, b, preferred_element_type=F32)


def _dot3(a, b):
    a_hi, a_lo = _split_bf16(a)
    b_hi, b_lo = _split_bf16(b)
    return _dot(a_hi, b_hi) + _dot(a_lo, b_hi) + _dot(a_hi, b_lo)


def _layer_norm(r, g, b):
    mu = jnp.mean(r, axis=-1, keepdims=True)
    d = r - mu
    var = jnp.mean(d * d, axis=-1, keepdims=True)
    return d * lax.rsqrt(var + LN_EPS) * g + b


def _rows_to_tiles(ref, val):
    rows = val.shape[0]
    for c in range(SUBLANES):
        ref[pl.ds(c, rows, stride=SUBLANES), :] = val[:, c * LANES:(c + 1) * LANES]


def _tiles_to_rows(ref, rows):
    return jnp.concatenate(
        [ref[pl.ds(c, rows, stride=SUBLANES), :] for c in range(SUBLANES)], axis=1)


def _ada_kernel(c_ref, w_ref, b_ref, o_ref):
    cc = c_ref[...]
    act = cc * jax.nn.sigmoid(cc)
    o_ref[...] = _dot3(act, w_ref[...]) + b_ref[...]


def _ada(c_pad, w_ada, b_ada):
    rows, d = c_pad.shape
    n_out = w_ada.shape[1]
    return pl.pallas_call(
        _ada_kernel,
        grid=(n_out // d,),
        in_specs=[
            pl.BlockSpec((rows, d), lambda j: (0, 0)),
            pl.BlockSpec((d, d), lambda j: (0, j)),
            pl.BlockSpec((1, d), lambda j: (0, j)),
        ],
        out_specs=pl.BlockSpec((rows, d), lambda j: (0, j)),
        out_shape=jax.ShapeDtypeStruct((rows, n_out), F32),
        compiler_params=pltpu.CompilerParams(
            dimension_semantics=("arbitrary",), vmem_limit_bytes=VMEM_LIMIT_BYTES),
        name="ada",
    )(c_pad, w_ada, b_ada)


def _mix_kernel(x_ref, mod_ref, win_ref, convw_ref, woc_ref, wpool_ref, pscale_ref, wo_ref,
                ln1g_ref, ln1b_ref, wr_ref, rb_ref,
                x1_ref, h_ref, loc_ref, ewcol_ref, tab_ref, cnt_ref,
                cvbuf, zbuf, s2buf, s4buf, s8buf, base_ref):
    t_rows, d = x_ref.shape
    pg = d // N_POOL_GROUPS
    b = pl.program_id(0)
    s = pl.program_id(1)
    shift_bufs = (cvbuf, zbuf, s2buf, s4buf, s8buf)

    @pl.when(jnp.logical_and(b == 0, s == 0))
    def _():
        base_ref[...] = jnp.zeros_like(base_ref)

    @pl.when(s == 0)
    def _():
        for buf in shift_bufs:
            buf[0:HALO, :] = jnp.zeros((HALO, buf.shape[1]), F32)

    @pl.when(s > 0)
    def _():
        for buf in shift_bufs:
            buf[0:HALO, :] = buf[t_rows:t_rows + HALO, :]

    def shifted(buf, k):
        return buf[HALO - k:HALO - k + t_rows, :]

    xt = x_ref[...]
    shift1 = mod_ref[0, 0:1, :]
    scale1 = mod_ref[0, 1:2, :]
    gate1 = mod_ref[0, 2:3, :]
    shift2 = mod_ref[0, 3:4, :]
    scale2 = mod_ref[0, 4:5, :]

    u = (xt * (1.0 + scale1) + shift1).astype(BF16)

    def proj(i):
        return _dot(u, win_ref[:, i * d:(i + 1) * d])

    cv = proj(2) * proj(0)
    cvbuf[HALO:HALO + t_rows, :] = cv
    conv = (convw_ref[2:3, :] * cv + convw_ref[1:2, :] * shifted(cvbuf, 1)
            + convw_ref[0:1, :] * shifted(cvbuf, 2))
    y_a = _dot((conv * proj(1)).astype(BF16), woc_ref[...])

    z = proj(3)
    zbuf[HALO:HALO + t_rows, :] = z
    s2 = z + shifted(zbuf, 1)
    s2buf[HALO:HALO + t_rows, :] = s2[:, pg:]
    s4 = s2[:, pg:] + shifted(s2buf, 2)
    s4buf[HALO:HALO + t_rows, :] = s4[:, pg:]
    s8 = s4[:, pg:] + shifted(s4buf, 4)
    s8buf[HALO:HALO + t_rows, :] = s8[:, pg:]
    s16 = s8[:, pg:] + shifted(s8buf, 8)
    win_sums = (s2[:, :pg], s4[:, :pg], s8[:, :pg], s16)
    t_pos = s * t_rows + lax.broadcasted_iota(I32, (t_rows, 1), 0) + 1
    y_b_parts = []
    for gi, w in enumerate(POOL_WINDOWS):
        count = jnp.minimum(t_pos, w).astype(F32)
        pooled = win_sums[gi] / count - z[:, gi * pg:(gi + 1) * pg]
        y_b_parts.append(_dot(pooled.astype(BF16), wpool_ref[gi]))
    y_b = jnp.concatenate(y_b_parts, axis=1) * pscale_ref[...]

    merged = jax.nn.sigmoid(proj(4)) * y_a + jax.nn.sigmoid(proj(5)) * y_b
    mix = _dot(merged.astype(BF16), wo_ref[...])

    x1 = _layer_norm(DEEPNORM_ALPHA * xt + (1.0 + gate1) * mix, ln1g_ref[...], ln1b_ref[...])
    x1_ref[...] = x1
    h = x1 * (1.0 + scale2) + shift2
    h_ref[...] = h.astype(BF16)

    lt = _dot3(h, wr_ref[...]).T + rb_ref[...]
    iota8 = lax.broadcasted_iota(I32, (SUBLANES, t_rows), 0)
    lg = lt[N_EXPERTS:N_EXPERTS + N_GROUPS, :]
    g_max = jnp.max(lg, axis=0, keepdims=True)
    g_top = jnp.min(jnp.where(lg == g_max, iota8, N_GROUPS), axis=0, keepdims=True)
    g_top_p = 1.0 / jnp.sum(jnp.exp(lg - g_max), axis=0, keepdims=True)
    sel = jnp.zeros((EXPERTS_PER_GROUP, t_rows), F32)
    for g in range(N_GROUPS):
        sel = jnp.where(g_top == g, lt[g * EXPERTS_PER_GROUP:(g + 1) * EXPERTS_PER_GROUP, :], sel)
    m1 = jnp.max(sel, axis=0, keepdims=True)
    i1 = jnp.min(jnp.where(sel == m1, iota8, EXPERTS_PER_GROUP), axis=0, keepdims=True)
    sel2 = jnp.where(iota8 == i1, -jnp.inf, sel)
    m2 = jnp.max(sel2, axis=0, keepdims=True)
    i2 = jnp.min(jnp.where(sel2 == m2, iota8, EXPERTS_PER_GROUP), axis=0, keepdims=True)
    ratio = jnp.exp(m2 - m1)
    p1 = 1.0 / (1.0 + ratio)
    w1 = g_top_p * p1
    w2 = g_top_p * (ratio * p1)
    e1 = g_top * EXPERTS_PER_GROUP + i1
    e2 = g_top * EXPERTS_PER_GROUP + i2

    iota_e = lax.broadcasted_iota(I32, (N_EXPERTS, t_rows), 0)
    hit1 = iota_e == e1
    hit2 = iota_e == e2
    onehot = jnp.where(jnp.logical_or(hit1, hit2), 1.0, 0.0)
    before = (lax.broadcasted_iota(I32, (t_rows, t_rows), 0)
              < lax.broadcasted_iota(I32, (t_rows, t_rows), 1))
    prefix = _dot(onehot.astype(BF16), jnp.where(before, 1.0, 0.0).astype(BF16))
    tile_cnt = jnp.sum(onehot, axis=1, keepdims=True)
    lower = (lax.broadcasted_iota(I32, (N_EXPERTS, N_EXPERTS), 1)
             < lax.broadcasted_iota(I32, (N_EXPERTS, N_EXPERTS), 0))
    lower = jnp.where(lower, 1.0, 0.0).astype(BF16)
    cnt_hi, cnt_lo = _split_bf16(jnp.broadcast_to(tile_cnt, (N_EXPERTS, LANES)))
    loc_start = (_dot(lower, cnt_hi) + _dot(lower, cnt_lo))[:, 0:1]
    pos = prefix + loc_start
    loc1 = jnp.sum(jnp.where(hit1, pos, 0.0), axis=0, keepdims=True)
    loc2 = jnp.sum(jnp.where(hit2, pos, 0.0), axis=0, keepdims=True)
    base = base_ref[...]
    base_ref[...] = base + tile_cnt
    cnt_ref[...] = jnp.broadcast_to(base + tile_cnt, cnt_ref.shape).astype(I32)

    diag = (lax.broadcasted_iota(I32, (N_EXPERTS, LANES), 0)
            == lax.broadcasted_iota(I32, (N_EXPERTS, LANES), 1))

    def to_lanes(col):
        return jnp.sum(jnp.where(diag, col, 0.0), axis=0, keepdims=True)

    iota8l = lax.broadcasted_iota(I32, (SUBLANES, LANES), 0)
    tab_ref[...] = jnp.where(
        iota8l == 0, to_lanes(tile_cnt),
        jnp.where(iota8l == 1, to_lanes(base),
                  jnp.where(iota8l == 2, to_lanes(loc_start), 0.0))).astype(I32)

    loc_ref[...] = jnp.where(iota8 == 0, loc1, jnp.where(iota8 == 1, loc2, 0.0)).astype(I32)
    iota_l = lax.broadcasted_iota(I32, (LANES, t_rows), 0)
    ewcol_ref[...] = jnp.where(
        iota_l == 0, w1, jnp.where(iota_l == 1, w2,
                                   jnp.where(iota_l == 2, loc1, jnp.where(iota_l == 3, loc2, 0.0)))).T


def _mix(x2d, mod, w_in, conv_w, w_oc, w_pool, pool_scale, w_o, ln1_g, ln1_b, wr, rb,
         *, batch, seq):
    n, d = x2d.shape
    t = MIX_ROWS
    n_s = seq // t
    const = lambda shape: pl.BlockSpec(shape, lambda b, s: (0,) * len(shape),
                                       pipeline_mode=pl.Buffered(1))
    row_blk = lambda b, s: (b * n_s + s, 0)
    pg = d // N_POOL_GROUPS
    return pl.pallas_call(
        _mix_kernel,
        grid=(batch, n_s),
        in_specs=[
            pl.BlockSpec((t, d), row_blk),
            pl.BlockSpec((1, N_MOD, d), lambda b, s: (b, 0, 0)),
            const(w_in.shape), const(conv_w.shape), const(w_oc.shape), const(w_pool.shape),
            const(pool_scale.shape), const(w_o.shape), const(ln1_g.shape), const(ln1_b.shape),
            const(wr.shape), const(rb.shape),
        ],
        out_specs=[
            pl.BlockSpec((t, d), row_blk),
            pl.BlockSpec((t, d), row_blk),
            pl.BlockSpec((SUBLANES, t), lambda b, s: (0, b * n_s + s)),
            pl.BlockSpec((t, LANES), row_blk),
            pl.BlockSpec((SUBLANES, LANES), row_blk),
            pl.BlockSpec((N_EXPERTS, LANES), lambda b, s: (0, 0)),
        ],
        out_shape=[
            jax.ShapeDtypeStruct((n, d), F32),
            jax.ShapeDtypeStruct((n, d), BF16),
            jax.ShapeDtypeStruct((SUBLANES, n), I32),
            jax.ShapeDtypeStruct((n, LANES), F32),
            jax.ShapeDtypeStruct((n // t * SUBLANES, LANES), I32),
            jax.ShapeDtypeStruct((N_EXPERTS, LANES), I32),
        ],
        scratch_shapes=[
            pltpu.VMEM((t + HALO, d), F32),
            pltpu.VMEM((t + HALO, d), F32),
            pltpu.VMEM((t + HALO, 3 * pg), F32),
            pltpu.VMEM((t + HALO, 2 * pg), F32),
            pltpu.VMEM((t + HALO, pg), F32),
            pltpu.VMEM((N_EXPERTS, 1), F32),
        ],
        compiler_params=pltpu.CompilerParams(
            dimension_semantics=("arbitrary", "arbitrary"), vmem_limit_bytes=VMEM_LIMIT_BYTES),
        name="mix",
    )(x2d, mod, w_in, conv_w, w_oc, w_pool, pool_scale, w_o, ln1_g, ln1_b, wr, rb)


def _tile_rows(ref, row, n_rows=1):
    return ref.at[pl.ds(pl.multiple_of(row * SUBLANES, SUBLANES), n_rows * SUBLANES), :]


TAB_LEN, TAB_RANK, TAB_LOC = 0, 1, 2
FILL_SEM = 2


def _strip_copies(tab_ref, pstart_ref, local_ref, sorted_hbm, sem, *, to_hbm):
    def strip(e, carry):
        n_rows = tab_ref[TAB_LEN, e]

        @pl.when(n_rows > 0)
        def _():
            local = _tile_rows(local_ref, tab_ref[TAB_LOC, e], n_rows)
            remote = _tile_rows(sorted_hbm, pstart_ref[e] + tab_ref[TAB_RANK, e], n_rows)
            src, dst = (local, remote) if to_hbm else (remote, local)
            pltpu.make_async_copy(src, dst, sem).start()
        return carry

    lax.fori_loop(0, N_EXPERTS, strip, 0)


def _dispatch_kernel(pstart_ref, padrow_ref, padlen_ref, nused_ref, tab_ref, loc_ref, h_ref, xs_hbm,
                     xloc0, xloc1, zero_blk, sem):
    i = pl.program_id(0)
    n_steps = pl.num_programs(0)
    t_rows = h_ref.shape[0]
    n_loc = TOP_K * t_rows
    blk_tok = zero_blk.shape[0] // SUBLANES
    n_blocks = xs_hbm.shape[0] // zero_blk.shape[0]
    xlocs = (xloc0, xloc1)

    def zero_fills(act):
        def pad_rows(e, carry):
            @pl.when(padlen_ref[e] > 0)
            def _():
                act(pltpu.make_async_copy(
                    _tile_rows(zero_blk, 0, padlen_ref[e]),
                    _tile_rows(xs_hbm, padrow_ref[e], padlen_ref[e]), sem.at[FILL_SEM]))
            return carry

        def unused_block(b, carry):
            act(pltpu.make_async_copy(
                zero_blk, _tile_rows(xs_hbm, b * blk_tok, blk_tok), sem.at[FILL_SEM]))
            return carry

        lax.fori_loop(0, N_EXPERTS, pad_rows, 0)
        lax.fori_loop(nused_ref[0], n_blocks, unused_block, 0)

    @pl.when(i == 0)
    def _():
        zero_blk[...] = jnp.zeros_like(zero_blk)
        zero_fills(lambda copy: copy.start())

    def strips_done(slot):
        pltpu.make_async_copy(xlocs[slot], _tile_rows(xs_hbm, 0, n_loc), sem.at[slot]).wait()

    for slot in range(2):
        @pl.when(i % 2 == slot)
        def _():
            @pl.when(i >= 2)
            def _():
                strips_done(slot)

            iota_p = lax.broadcasted_iota(I32, (n_loc, t_rows), 0)
            pick = jnp.logical_or(iota_p == loc_ref[0:1, :], iota_p == loc_ref[1:2, :])
            xsorted = _dot(jnp.where(pick, 1.0, 0.0).astype(BF16), h_ref[...])
            _rows_to_tiles(xlocs[slot], xsorted)
            _strip_copies(tab_ref, pstart_ref, xlocs[slot], xs_hbm, sem.at[slot], to_hbm=True)

    @pl.when(i == n_steps - 1)
    def _():
        for slot in range(2):
            @pl.when(n_steps > slot)
            def _():
                strips_done(slot)
        zero_fills(lambda copy: copy.wait())


def _dispatch(pad_start, pad_row, pad_len, n_used, tab, loc, h, n_blocks):
    n, d = h.shape
    t = MIX_ROWS
    grid_spec = pltpu.PrefetchScalarGridSpec(
        num_scalar_prefetch=4,
        grid=(n // t,),
        in_specs=[
            pl.BlockSpec((SUBLANES, LANES), lambda i, *_: (i, 0), memory_space=pltpu.SMEM),
            pl.BlockSpec((SUBLANES, t), lambda i, *_: (0, i)),
            pl.BlockSpec((t, d), lambda i, *_: (i, 0)),
        ],
        out_specs=pl.BlockSpec(memory_space=pl.ANY),
        scratch_shapes=[
            pltpu.VMEM((TOP_K * t * SUBLANES, LANES), F32),
            pltpu.VMEM((TOP_K * t * SUBLANES, LANES), F32),
            pltpu.VMEM((EXPERT_ROWS * SUBLANES, LANES), F32),
            pltpu.SemaphoreType.DMA((3,)),
        ],
    )
    return pl.pallas_call(
        _dispatch_kernel,
        grid_spec=grid_spec,
        out_shape=jax.ShapeDtypeStruct((n_blocks * EXPERT_ROWS * SUBLANES, LANES), F32),
        compiler_params=pltpu.CompilerParams(
            dimension_semantics=("arbitrary",), vmem_limit_bytes=VMEM_LIMIT_BYTES,
            has_side_effects=True),
        name="dispatch",
    )(pad_start, pad_row, pad_len, n_used, tab, loc, h)


IN_SLOTS = 4
OUT_SLOTS = 2


def _expert_kernel(blk0_ref, nblk_ref, wsel_ref, nused_ref, xs_hbm, wg_ref, wu_ref, wd_ref, ys_hbm,
                   xbuf0, xbuf1, xbuf2, xbuf3, ybuf0, ybuf1, wg_bf, wu_bf, wd_bf, sem_in, sem_out):
    del wsel_ref
    e = pl.program_id(0)
    blk_tok = xbuf0.shape[0] // SUBLANES
    xbufs = (xbuf0, xbuf1, xbuf2, xbuf3)
    ybufs = (ybuf0, ybuf1)
    n_used = nused_ref[0]
    ahead = IN_SLOTS - 1

    def in_copy(g, slot):
        return pltpu.make_async_copy(
            _tile_rows(xs_hbm, g * blk_tok, blk_tok), xbufs[slot], sem_in.at[slot])

    def out_copy(g, slot):
        return pltpu.make_async_copy(
            ybufs[slot], _tile_rows(ys_hbm, g * blk_tok, blk_tok), sem_out.at[slot])

    @pl.when(e == 0)
    def _():
        for g in range(ahead):
            @pl.when(g < n_used)
            def _():
                in_copy(g, g).start()

    @pl.when(nblk_ref[e] > 0)
    def _():
        wg_bf[...] = wg_ref[0].astype(BF16)
        wu_bf[...] = wu_ref[0].astype(BF16)
        wd_bf[...] = wd_ref[0].astype(BF16)

    def block(j, carry):
        g = blk0_ref[e] + j
        for slot in range(IN_SLOTS):
            oslot = slot % OUT_SLOTS

            @pl.when(g % IN_SLOTS == slot)
            def _():
                in_copy(g, slot).wait()

                @pl.when(g + ahead < n_used)
                def _():
                    in_copy(g + ahead, (slot + ahead) % IN_SLOTS).start()

                @pl.when(g >= OUT_SLOTS)
                def _():
                    out_copy(g - OUT_SLOTS, oslot).wait()

                xb = _tiles_to_rows(xbufs[slot], blk_tok).astype(BF16)
                gate = _dot(xb, wg_bf[...])
                up = _dot(xb, wu_bf[...])
                act = (gate * jax.nn.sigmoid(gate) * up).astype(BF16)
                _rows_to_tiles(ybufs[oslot], _dot(act, wd_bf[...]))
                out_copy(g, oslot).start()
        return carry

    lax.fori_loop(0, nblk_ref[e], block, 0)

    @pl.when(e == pl.num_programs(0) - 1)
    def _():
        for oslot in range(OUT_SLOTS):
            @pl.when(n_used > oslot)
            def _():
                g_last = n_used - 1 - ((n_used - 1 - oslot) % OUT_SLOTS)
                out_copy(g_last, oslot).wait()


def _experts(blk0, nblk, wsel, n_used, xs, w_gate, w_up, w_down):
    bm = EXPERT_ROWS
    n_exp, d, de = w_gate.shape
    wblk = lambda e, b0, nb, ws, nu: (ws[e], 0, 0)
    grid_spec = pltpu.PrefetchScalarGridSpec(
        num_scalar_prefetch=4,
        grid=(n_exp,),
        in_specs=[
            pl.BlockSpec(memory_space=pl.ANY),
            pl.BlockSpec((1, d, de), wblk),
            pl.BlockSpec((1, d, de), wblk),
            pl.BlockSpec((1, de, d), wblk),
        ],
        out_specs=pl.BlockSpec(memory_space=pl.ANY),
        scratch_shapes=[pltpu.VMEM((bm * SUBLANES, LANES), F32)] * (IN_SLOTS + OUT_SLOTS) + [
            pltpu.VMEM((d, de), BF16),
            pltpu.VMEM((d, de), BF16),
            pltpu.VMEM((de, d), BF16),
            pltpu.SemaphoreType.DMA((IN_SLOTS,)),
            pltpu.SemaphoreType.DMA((OUT_SLOTS,)),
        ],
    )
    n_prefetch = 4
    return pl.pallas_call(
        _expert_kernel,
        grid_spec=grid_spec,
        out_shape=jax.ShapeDtypeStruct(xs.shape, F32),
        input_output_aliases={n_prefetch: 0},
        compiler_params=pltpu.CompilerParams(
            dimension_semantics=("arbitrary",), vmem_limit_bytes=VMEM_LIMIT_BYTES,
            has_side_effects=True),
        name="experts",
    )(blk0, nblk, wsel, n_used, xs, w_gate, w_up, w_down)


def _combine_kernel(pstart_ref, tab_ref, tab_next_ref, x1_ref, mod_ref, ln2g_ref, ln2b_ref,
                    ewcol_ref, ys_hbm, out_ref, ybuf0, ybuf1, sem):
    i = pl.program_id(0)
    t_rows = x1_ref.shape[0]
    n_loc = TOP_K * t_rows
    ybufs = (ybuf0, ybuf1)

    @pl.when(i == 0)
    def _():
        _strip_copies(tab_ref, pstart_ref, ybuf0, ys_hbm, sem.at[0], to_hbm=False)

    for slot in range(2):
        @pl.when(i % 2 == slot)
        def _():
            @pl.when(i + 1 < pl.num_programs(0))
            def _():
                _strip_copies(tab_next_ref, pstart_ref, ybufs[1 - slot], ys_hbm,
                              sem.at[1 - slot], to_hbm=False)

            pltpu.make_async_copy(_tile_rows(ys_hbm, 0, n_loc), ybufs[slot], sem.at[slot]).wait()
            ew = ewcol_ref[...]
            lane = lax.broadcasted_iota(I32, (t_rows, n_loc), 1).astype(F32)
            weights = (jnp.where(lane == ew[:, 2:3], ew[:, 0:1], 0.0)
                       + jnp.where(lane == ew[:, 3:4], ew[:, 1:2], 0.0))
            y = _dot(weights.astype(BF16), _tiles_to_rows(ybufs[slot], n_loc).astype(BF16))
            gate2 = mod_ref[0, 5:6, :]
            out_ref[...] = _layer_norm(DEEPNORM_ALPHA * x1_ref[...] + (1.0 + gate2) * y,
                                       ln2g_ref[...], ln2b_ref[...])


def _combine(pad_start, tab, x1, mod, ln2_g, ln2_b, ewcol, ys, *, seq):
    n, d = x1.shape
    t = MIX_ROWS
    per_seq = seq // t
    n_steps = n // t
    row_blk = lambda i, ps: (i, 0)
    grid_spec = pltpu.PrefetchScalarGridSpec(
        num_scalar_prefetch=1,
        grid=(n_steps,),
        in_specs=[
            pl.BlockSpec((SUBLANES, LANES), row_blk, memory_space=pltpu.SMEM),
            pl.BlockSpec((SUBLANES, LANES), lambda i, ps: (jnp.minimum(i + 1, n_steps - 1), 0),
                         memory_space=pltpu.SMEM),
            pl.BlockSpec((t, d), row_blk),
            pl.BlockSpec((1, N_MOD, d), lambda i, ps: (i // per_seq, 0, 0)),
            pl.BlockSpec((1, d), lambda i, ps: (0, 0)),
            pl.BlockSpec((1, d), lambda i, ps: (0, 0)),
            pl.BlockSpec((t, LANES), row_blk),
            pl.BlockSpec(memory_space=pl.ANY),
        ],
        out_specs=pl.BlockSpec((t, d), row_blk),
        scratch_shapes=[
            pltpu.VMEM((TOP_K * t * SUBLANES, LANES), F32),
            pltpu.VMEM((TOP_K * t * SUBLANES, LANES), F32),
            pltpu.SemaphoreType.DMA((2,)),
        ],
    )
    return pl.pallas_call(
        _combine_kernel,
        grid_spec=grid_spec,
        out_shape=jax.ShapeDtypeStruct((n, d), F32),
        compiler_params=pltpu.CompilerParams(
            dimension_semantics=("arbitrary",), vmem_limit_bytes=VMEM_LIMIT_BYTES),
        name="combine",
    )(pad_start, tab, tab, x1, mod, ln2_g, ln2_b, ewcol, ys)


def _block_layout(counts, n_assign):
    bm = EXPERT_ROWS
    n_blocks = -(-n_assign // bm) + N_EXPERTS
    padded = (counts + bm - 1) // bm * bm
    pad_end = jnp.cumsum(padded)
    pad_start = (pad_end - padded).astype(I32)
    n_used = (pad_end[-1] // bm).astype(I32)
    pad_row = (pad_start + counts).astype(I32)
    pad_len = (padded - counts).astype(I32)
    ids = jnp.arange(N_EXPERTS, dtype=I32)
    prev_used = lax.cummax(jnp.where(padded > 0, ids, -1))
    first_used = jnp.argmax(padded > 0).astype(I32)
    weight_sel = jnp.where(prev_used >= 0, prev_used, first_used).astype(I32)
    return (pad_start, pad_row, pad_len, pad_start // bm, (padded // bm).astype(I32), weight_sel,
            n_used.reshape(1), n_blocks)


def kernel(x, c, w_ada, b_ada, w_in, conv_w, w_out_conv, w_pool, pool_scale, w_o, ln1_g, ln1_b,
           w_group, b_group, w_router, b_router, w_gate, w_up, w_down, ln2_g, ln2_b):
    bsz, seq, d = x.shape
    n = bsz * seq
    assert d == SUBLANES * LANES and seq % MIX_ROWS == 0 and w_ada.shape[0] == DEPTH
    x2d = x.reshape(n, d)
    c_pad = jnp.pad(c, ((0, -bsz % SUBLANES), (0, 0)))
    for l in range(DEPTH):
        mod = _ada(c_pad, w_ada[l], b_ada[l][None, :])[:bsz].reshape(bsz, N_MOD, d)
        wr = jnp.concatenate(
            [w_router[l], w_group[l],
             jnp.zeros((d, LANES - N_EXPERTS - N_GROUPS), F32)], axis=1)
        rb = jnp.concatenate(
            [b_router[l].reshape(-1), b_group[l],
             jnp.zeros((LANES - N_EXPERTS - N_GROUPS,), F32)])[:, None]
        x1, h, loc, ewcol, tab, cnt = _mix(
            x2d, mod, w_in[l].astype(BF16), conv_w[l], w_out_conv[l].astype(BF16),
            w_pool[l].astype(BF16), pool_scale[l][None, :], w_o[l].astype(BF16),
            ln1_g[l][None, :], ln1_b[l][None, :], wr, rb, batch=bsz, seq=seq)
        pad_start, pad_row, pad_len, blk0, nblk, weight_sel, n_used, n_blocks = _block_layout(
            cnt[:, 0], n * TOP_K)
        xs = _dispatch(pad_start, pad_row, pad_len, n_used, tab, loc, h, n_blocks)
        ys = _experts(blk0, nblk, weight_sel, n_used, xs, w_gate[l], w_up[l], w_down[l])
        x2d = _combine(pad_start, tab, x1, mod, ln2_g[l][None, :], ln2_b[l][None, :],
                       ewcol, ys, seq=seq)
    return x2d.reshape(bsz, seq, d)
```

```python
import functools

import jax
import jax.numpy as jnp
from jax import lax
from jax.experimental import pallas as pl
from jax.experimental.pallas import tpu as pltpu

F32 = jnp.float32
BF16 = jnp.bfloat16
I32 = jnp.int32

LANES = 128
SUBLANES = 8
VMEM_LIMIT_BYTES = 56 * 1024 * 1024

N_POOL_GROUPS = 4
POOL_WINDOWS = (2, 4, 8, 16)
CONV_WIDTH = 3
N_GROUPS = 8
EXPERTS_PER_GROUP = 8
N_EXPERTS = N_GROUPS * EXPERTS_PER_GROUP
TOP_K = 2
N_MOD = 6
LN_EPS = 1e-5
DEPTH = 1
DEEPNORM_ALPHA = (2.0 * DEPTH) ** 0.25

MIX_ROWS = 512
EXPERT_ROWS = 256
HALO = SUBLANES


def _split_bf16(a):
    hi = a.astype(BF16)
    lo = (a - hi.astype(F32)).astype(BF16)
    return hi, lo


def _dot(a, b):
    return jnp.dot(a, b, preferred_element_type=F32)


def _dot3(a, b):
    a_hi, a_lo = _split_bf16(a)
    b_hi, b_lo = _split_bf16(b)
    return _dot(a_hi, b_hi) + _dot(a_lo, b_hi) + _dot(a_hi, b_lo)


def _dot3_wide(a, b):
    a_hi, a_lo = _split_bf16(a)
    n = b.shape[1]
    b_both = jnp.concatenate(_split_bf16(b), axis=1)
    hi = _dot(a_hi, b_both)
    lo = _dot(a_lo, b_both)
    return hi[:, :n] + hi[:, n:] + lo[:, :n] + lo[:, n:]


def _layer_norm(r, g, b):
    mu = jnp.mean(r, axis=-1, keepdims=True)
    d = r - mu
    var = jnp.mean(d * d, axis=-1, keepdims=True)
    return d * lax.rsqrt(var + LN_EPS) * g + b


def _rows_to_tiles(ref, val):
    rows = val.shape[0]
    for c in range(SUBLANES):
        ref[pl.ds(c, rows, stride=SUBLANES), :] = val[:, c * LANES:(c + 1) * LANES]


def _tiles_to_rows(ref, rows):
    return jnp.concatenate(
        [ref[pl.ds(c, rows, stride=SUBLANES), :] for c in range(SUBLANES)], axis=1)


def _ada_kernel(c_ref, w_ref, b_ref, o_ref):
    cc = c_ref[...]
    act = cc * jax.nn.sigmoid(cc)
    o_ref[...] = _dot3(act, w_ref[...]) + b_ref[...]


def _ada(c_pad, w_ada, b_ada):
    rows, d = c_pad.shape
    n_out = w_ada.shape[1]
    return pl.pallas_call(
        _ada_kernel,
        grid=(n_out // d,),
        in_specs=[
            pl.BlockSpec((rows, d), lambda j: (0, 0)),
            pl.BlockSpec((d, d), lambda j: (0, j)),
            pl.BlockSpec((1, d), lambda j: (0, j)),
        ],
        out_specs=pl.BlockSpec((rows, d), lambda j: (0, j)),
        out_shape=jax.ShapeDtypeStruct((rows, n_out), F32),
        compiler_params=pltpu.CompilerParams(
            dimension_semantics=("arbitrary",), vmem_limit_bytes=VMEM_LIMIT_BYTES),
        name="ada",
    )(c_pad, w_ada, b_ada)


def _mix_kernel(x_ref, mod_ref, win_ref, convw_ref, woc_ref, wpool_ref, pscale_ref, wo_ref,
                ln1g_ref, ln1b_ref, wr_ref, rb_ref,
                x1_ref, h_ref, loc_ref, ewcol_ref, tab_ref, cnt_ref,
                cvbuf, zbuf, s2buf, s4buf, s8buf, base_ref, lt_buf, *, tiles_per_seq):
    t_rows, d = x_ref.shape
    pg = d // N_POOL_GROUPS
    step = pl.program_id(0)
    n_tiles = pl.num_programs(0) - 1
    s = jnp.minimum(step, n_tiles - 1) % tiles_per_seq
    shift_bufs = (cvbuf, zbuf, s2buf, s4buf, s8buf)

    @pl.when(step == 0)
    def _():
        base_ref[...] = jnp.zeros_like(base_ref)
        lt_buf[...] = jnp.zeros_like(lt_buf)

    @pl.when(s == 0)
    def _():
        for buf in shift_bufs:
            buf[0:HALO, :] = jnp.zeros((HALO, buf.shape[1]), F32)

    @pl.when(s > 0)
    def _():
        for buf in shift_bufs:
            buf[0:HALO, :] = buf[t_rows:t_rows + HALO, :]

    _route_tile(lt_buf[...], step > 0, base_ref, loc_ref, ewcol_ref, tab_ref, cnt_ref)

    def shifted(buf, k):
        return buf[HALO - k:HALO - k + t_rows, :]

    xt = x_ref[...]
    shift1 = mod_ref[0, 0:1, :]
    scale1 = mod_ref[0, 1:2, :]
    gate1 = mod_ref[0, 2:3, :]
    shift2 = mod_ref[0, 3:4, :]
    scale2 = mod_ref[0, 4:5, :]

    u = (xt * (1.0 + scale1) + shift1).astype(BF16)

    def proj(i):
        return _dot(u, win_ref[:, i * d:(i + 1) * d])

    cv = proj(2) * proj(0)
    cvbuf[HALO:HALO + t_rows, :] = cv
    conv = (convw_ref[2:3, :] * cv + convw_ref[1:2, :] * shifted(cvbuf, 1)
            + convw_ref[0:1, :] * shifted(cvbuf, 2))
    y_a = _dot((conv * proj(1)).astype(BF16), woc_ref[...])

    z = proj(3)
    zbuf[HALO:HALO + t_rows, :] = z
    s2 = z + shifted(zbuf, 1)
    s2buf[HALO:HALO + t_rows, :] = s2[:, pg:]
    s4 = s2[:, pg:] + shifted(s2buf, 2)
    s4buf[HALO:HALO + t_rows, :] = s4[:, pg:]
    s8 = s4[:, pg:] + shifted(s4buf, 4)
    s8buf[HALO:HALO + t_rows, :] = s8[:, pg:]
    s16 = s8[:, pg:] + shifted(s8buf, 8)
    win_sums = (s2[:, :pg], s4[:, :pg], s8[:, :pg], s16)
    t_pos = s * t_rows + lax.broadcasted_iota(I32, (t_rows, 1), 0) + 1
    y_b_parts = []
    for gi, w in enumerate(POOL_WINDOWS):
        count = jnp.minimum(t_pos, w).astype(F32)
        pooled = win_sums[gi] / count - z[:, gi * pg:(gi + 1) * pg]
        y_b_parts.append(_dot(pooled.astype(BF16), wpool_ref[gi]))
    y_b = jnp.concatenate(y_b_parts, axis=1) * pscale_ref[...]

    merged = jax.nn.sigmoid(proj(4)) * y_a + jax.nn.sigmoid(proj(5)) * y_b
    mix = _dot(merged.astype(BF16), wo_ref[...])

    x1 = _layer_norm(DEEPNORM_ALPHA * xt + (1.0 + gate1) * mix, ln1g_ref[...], ln1b_ref[...])
    h = x1 * (1.0 + scale2) + shift2
    lt_buf[...] = _dot3_wide(h, wr_ref[...]).T + rb_ref[...]

    @pl.when(step < n_tiles)
    def _():
        x1_ref[...] = x1
        h_ref[...] = h.astype(BF16)


def _route_tile(lt, valid, base_ref, loc_ref, ewcol_ref, tab_ref, cnt_ref):
    t_rows = lt.shape[1]
    iota8 = lax.broadcasted_iota(I32, (SUBLANES, t_rows), 0)
    lg = lt[N_EXPERTS:N_EXPERTS + N_GROUPS, :]
    g_max = jnp.max(lg, axis=0, keepdims=True)
    g_top = jnp.min(jnp.where(lg == g_max, iota8, N_GROUPS), axis=0, keepdims=True)
    g_top_p = 1.0 / jnp.sum(jnp.exp(lg - g_max), axis=0, keepdims=True)
    sel = jnp.zeros((EXPERTS_PER_GROUP, t_rows), F32)
    for g in range(N_GROUPS):
        sel = jnp.where(g_top == g, lt[g * EXPERTS_PER_GROUP:(g + 1) * EXPERTS_PER_GROUP, :], sel)
    m1 = jnp.max(sel, axis=0, keepdims=True)
    i1 = jnp.min(jnp.where(sel == m1, iota8, EXPERTS_PER_GROUP), axis=0, keepdims=True)
    sel2 = jnp.where(iota8 == i1, -jnp.inf, sel)
    m2 = jnp.max(sel2, axis=0, keepdims=True)
    i2 = jnp.min(jnp.where(sel2 == m2, iota8, EXPERTS_PER_GROUP), axis=0, keepdims=True)
    ratio = jnp.exp(m2 - m1)
    p1 = 1.0 / (1.0 + ratio)
    w1 = g_top_p * p1
    w2 = g_top_p * (ratio * p1)
    e1 = g_top * EXPERTS_PER_GROUP + i1
    e2 = g_top * EXPERTS_PER_GROUP + i2

    iota_e = lax.broadcasted_iota(I32, (N_EXPERTS, t_rows), 0)
    hit1 = jnp.logical_and(iota_e == e1, valid)
    hit2 = jnp.logical_and(iota_e == e2, valid)
    onehot = jnp.where(jnp.logical_or(hit1, hit2), 1.0, 0.0)
    before = (lax.broadcasted_iota(I32, (t_rows, t_rows), 0)
              < lax.broadcasted_iota(I32, (t_rows, t_rows), 1))
    prefix = _dot(onehot.astype(BF16), jnp.where(before, 1.0, 0.0).astype(BF16))
    tile_cnt = jnp.sum(onehot, axis=1, keepdims=True)
    lower = (lax.broadcasted_iota(I32, (N_EXPERTS, N_EXPERTS), 1)
             < lax.broadcasted_iota(I32, (N_EXPERTS, N_EXPERTS), 0))
    lower = jnp.where(lower, 1.0, 0.0).astype(BF16)
    cnt_hi, cnt_lo = _split_bf16(jnp.broadcast_to(tile_cnt, (N_EXPERTS, LANES)))
    loc_start = (_dot(lower, cnt_hi) + _dot(lower, cnt_lo))[:, 0:1]
    pos = prefix + loc_start
    loc1 = jnp.sum(jnp.where(hit1, pos, 0.0), axis=0, keepdims=True)
    loc2 = jnp.sum(jnp.where(hit2, pos, 0.0), axis=0, keepdims=True)
    base = base_ref[...]
    base_ref[...] = base + tile_cnt
    cnt_ref[...] = jnp.broadcast_to(base + tile_cnt, cnt_ref.shape).astype(I32)

    diag = (lax.broadcasted_iota(I32, (N_EXPERTS, LANES), 0)
            == lax.broadcasted_iota(I32, (N_EXPERTS, LANES), 1))

    def to_lanes(col):
        return jnp.sum(jnp.where(diag, col, 0.0), axis=0, keepdims=True)

    iota8l = lax.broadcasted_iota(I32, (SUBLANES, LANES), 0)
    tab_ref[...] = jnp.where(
        iota8l == 0, to_lanes(tile_cnt),
        jnp.where(iota8l == 1, to_lanes(base),
                  jnp.where(iota8l == 2, to_lanes(loc_start), 0.0))).astype(I32)

    loc_ref[...] = jnp.where(iota8 == 0, loc1, jnp.where(iota8 == 1, loc2, 0.0)).astype(I32)
    iota_l = lax.broadcasted_iota(I32, (LANES, t_rows), 0)
    ewcol_ref[...] = jnp.where(
        iota_l == 0, w1, jnp.where(iota_l == 1, w2,
                                   jnp.where(iota_l == 2, loc1, jnp.where(iota_l == 3, loc2, 0.0)))).T


def _mix(x2d, mod, w_in, conv_w, w_oc, w_pool, pool_scale, w_o, ln1_g, ln1_b, wr, rb,
         *, batch, seq):
    n, d = x2d.shape
    t = MIX_ROWS
    n_s = seq // t
    n_tiles = batch * n_s
    const = lambda shape: pl.BlockSpec(shape, lambda i: (0,) * len(shape),
                                       pipeline_mode=pl.Buffered(1))
    cur = lambda i: jnp.minimum(i, n_tiles - 1)
    prev = lambda i: jnp.maximum(i - 1, 0)
    pg = d // N_POOL_GROUPS
    return pl.pallas_call(
        functools.partial(_mix_kernel, tiles_per_seq=n_s),
        grid=(n_tiles + 1,),
        in_specs=[
            pl.BlockSpec((t, d), lambda i: (cur(i), 0)),
            pl.BlockSpec((1, N_MOD, d), lambda i: (cur(i) // n_s, 0, 0)),
            const(w_in.shape), const(conv_w.shape), const(w_oc.shape), const(w_pool.shape),
            const(pool_scale.shape), const(w_o.shape), const(ln1_g.shape), const(ln1_b.shape),
            const(wr.shape), const(rb.shape),
        ],
        out_specs=[
            pl.BlockSpec((t, d), lambda i: (cur(i), 0)),
            pl.BlockSpec((t, d), lambda i: (cur(i), 0)),
            pl.BlockSpec((SUBLANES, t), lambda i: (0, prev(i))),
            pl.BlockSpec((t, LANES), lambda i: (prev(i), 0)),
            pl.BlockSpec((SUBLANES, LANES), lambda i: (prev(i), 0)),
            pl.BlockSpec((N_EXPERTS, LANES), lambda i: (0, 0)),
        ],
        out_shape=[
            jax.ShapeDtypeStruct((n, d), F32),
            jax.ShapeDtypeStruct((n, d), BF16),
            jax.ShapeDtypeStruct((SUBLANES, n), I32),
            jax.ShapeDtypeStruct((n, LANES), F32),
            jax.ShapeDtypeStruct((n // t * SUBLANES, LANES), I32),
            jax.ShapeDtypeStruct((N_EXPERTS, LANES), I32),
        ],
        scratch_shapes=[
            pltpu.VMEM((t + HALO, d), F32),
            pltpu.VMEM((t + HALO, d), F32),
            pltpu.VMEM((t + HALO, 3 * pg), F32),
            pltpu.VMEM((t + HALO, 2 * pg), F32),
            pltpu.VMEM((t + HALO, pg), F32),
            pltpu.VMEM((N_EXPERTS, 1), F32),
            pltpu.VMEM((LANES, t), F32),
        ],
        compiler_params=pltpu.CompilerParams(
            dimension_semantics=("arbitrary",), vmem_limit_bytes=VMEM_LIMIT_BYTES),
        name="mix",
    )(x2d, mod, w_in, conv_w, w_oc, w_pool, pool_scale, w_o, ln1_g, ln1_b, wr, rb)


def _tile_rows(ref, row, n_rows=1):
    return ref.at[pl.ds(pl.multiple_of(row * SUBLANES, SUBLANES), n_rows * SUBLANES), :]


TAB_LEN, TAB_RANK, TAB_LOC = 0, 1, 2
FILL_SEM = 2


def _strip_copies(tab_ref, pstart_ref, local_ref, sorted_hbm, sem, *, to_hbm):
    def strip(e, carry):
        n_rows = tab_ref[TAB_LEN, e]

        @pl.when(n_rows > 0)
        def _():
            local = _tile_rows(local_ref, tab_ref[TAB_LOC, e], n_rows)
            remote = _tile_rows(sorted_hbm, pstart_ref[e] + tab_ref[TAB_RANK, e], n_rows)
            src, dst = (local, remote) if to_hbm else (remote, local)
            pltpu.make_async_copy(src, dst, sem).start()
        return carry

    lax.fori_loop(0, N_EXPERTS, strip, 0)


def _dispatch_kernel(pstart_ref, padrow_ref, padlen_ref, nused_ref, tab_ref, loc_ref, h_ref, xs_hbm,
                     xloc0, xloc1, zero_blk, sem):
    i = pl.program_id(0)
    n_steps = pl.num_programs(0)
    t_rows = h_ref.shape[0]
    n_loc = TOP_K * t_rows
    blk_tok = zero_blk.shape[0] // SUBLANES
    n_blocks = xs_hbm.shape[0] // zero_blk.shape[0]
    xlocs = (xloc0, xloc1)

    def zero_fills(act):
        def pad_rows(e, carry):
            @pl.when(padlen_ref[e] > 0)
            def _():
                act(pltpu.make_async_copy(
                    _tile_rows(zero_blk, 0, padlen_ref[e]),
                    _tile_rows(xs_hbm, padrow_ref[e], padlen_ref[e]), sem.at[FILL_SEM]))
            return carry

        def unused_block(b, carry):
            act(pltpu.make_async_copy(
                zero_blk, _tile_rows(xs_hbm, b * blk_tok, blk_tok), sem.at[FILL_SEM]))
            return carry

        lax.fori_loop(0, N_EXPERTS, pad_rows, 0)
        lax.fori_loop(nused_ref[0], n_blocks, unused_block, 0)

    @pl.when(i == 0)
    def _():
        zero_blk[...] = jnp.zeros_like(zero_blk)
        zero_fills(lambda copy: copy.start())

    def strips_done(slot):
        pltpu.make_async_copy(xlocs[slot], _tile_rows(xs_hbm, 0, n_loc), sem.at[slot]).wait()

    for slot in range(2):
        @pl.when(i % 2 == slot)
        def _():
            @pl.when(i >= 2)
            def _():
                strips_done(slot)

            iota_p = lax.broadcasted_iota(I32, (n_loc, t_rows), 0)
            pick = jnp.logical_or(iota_p == loc_ref[0:1, :], iota_p == loc_ref[1:2, :])
            xsorted = _dot(jnp.where(pick, 1.0, 0.0).astype(BF16), h_ref[...])
            _rows_to_tiles(xlocs[slot], xsorted)
            _strip_copies(tab_ref, pstart_ref, xlocs[slot], xs_hbm, sem.at[slot], to_hbm=True)

    @pl.when(i == n_steps - 1)
    def _():
        for slot in range(2):
            @pl.when(n_steps > slot)
            def _():
                strips_done(slot)
        zero_fills(lambda copy: copy.wait())


def _dispatch(pad_start, pad_row, pad_len, n_used, tab, loc, h, n_blocks):
    n, d = h.shape
    t = MIX_ROWS
    grid_spec = pltpu.PrefetchScalarGridSpec(
        num_scalar_prefetch=4,
        grid=(n // t,),
        in_specs=[
            pl.BlockSpec((SUBLANES, LANES), lambda i, *_: (i, 0), memory_space=pltpu.SMEM),
            pl.BlockSpec((SUBLANES, t), lambda i, *_: (0, i)),
            pl.BlockSpec((t, d), lambda i, *_: (i, 0)),
        ],
        out_specs=pl.BlockSpec(memory_space=pl.ANY),
        scratch_shapes=[
            pltpu.VMEM((TOP_K * t * SUBLANES, LANES), F32),
            pltpu.VMEM((TOP_K * t * SUBLANES, LANES), F32),
            pltpu.VMEM((EXPERT_ROWS * SUBLANES, LANES), F32),
            pltpu.SemaphoreType.DMA((3,)),
        ],
    )
    return pl.pallas_call(
        _dispatch_kernel,
        grid_spec=grid_spec,
        out_shape=jax.ShapeDtypeStruct((n_blocks * EXPERT_ROWS * SUBLANES, LANES), F32),
        compiler_params=pltpu.CompilerParams(
            dimension_semantics=("arbitrary",), vmem_limit_bytes=VMEM_LIMIT_BYTES,
            has_side_effects=True),
        name="dispatch",
    )(pad_start, pad_row, pad_len, n_used, tab, loc, h)


IN_SLOTS = 4
OUT_SLOTS = 2


def _expert_kernel(blk0_ref, nblk_ref, wsel_ref, nused_ref, xs_hbm, wg_ref, wu_ref, wd_ref, ys_hbm,
                   xbuf0, xbuf1, xbuf2, xbuf3, ybuf0, ybuf1, wg_bf, wu_bf, wd_bf, sem_in, sem_out):
    del wsel_ref
    e = pl.program_id(0)
    blk_tok = xbuf0.shape[0] // SUBLANES
    xbufs = (xbuf0, xbuf1, xbuf2, xbuf3)
    ybufs = (ybuf0, ybuf1)
    n_used = nused_ref[0]
    ahead = IN_SLOTS - 1

    def in_copy(g, slot):
        return pltpu.make_async_copy(
            _tile_rows(xs_hbm, g * blk_tok, blk_tok), xbufs[slot], sem_in.at[slot])

    def out_copy(g, slot):
        return pltpu.make_async_copy(
            ybufs[slot], _tile_rows(ys_hbm, g * blk_tok, blk_tok), sem_out.at[slot])

    @pl.when(e == 0)
    def _():
        for g in range(ahead):
            @pl.when(g < n_used)
            def _():
                in_copy(g, g).start()

    @pl.when(nblk_ref[e] > 0)
    def _():
        wg_bf[...] = wg_ref[0].astype(BF16)
        wu_bf[...] = wu_ref[0].astype(BF16)
        wd_bf[...] = wd_ref[0].astype(BF16)

    def block(j, carry):
        g = blk0_ref[e] + j
        for slot in range(IN_SLOTS):
            oslot = slot % OUT_SLOTS

            @pl.when(g % IN_SLOTS == slot)
            def _():
                in_copy(g, slot).wait()

                @pl.when(g + ahead < n_used)
                def _():
                    in_copy(g + ahead, (slot + ahead) % IN_SLOTS).start()

                @pl.when(g >= OUT_SLOTS)
                def _():
                    out_copy(g - OUT_SLOTS, oslot).wait()

                xb = _tiles_to_rows(xbufs[slot], blk_tok).astype(BF16)
                gate = _dot(xb, wg_bf[...])
                up = _dot(xb, wu_bf[...])
                act = (gate * jax.nn.sigmoid(gate) * up).astype(BF16)
                _rows_to_tiles(ybufs[oslot], _dot(act, wd_bf[...]))
                out_copy(g, oslot).start()
        return carry

    lax.fori_loop(0, nblk_ref[e], block, 0)

    @pl.when(e == pl.num_programs(0) - 1)
    def _():
        for oslot in range(OUT_SLOTS):
            @pl.when(n_used > oslot)
            def _():
                g_last = n_used - 1 - ((n_used - 1 - oslot) % OUT_SLOTS)
                out_copy(g_last, oslot).wait()


def _experts(blk0, nblk, wsel, n_used, xs, w_gate, w_up, w_down):
    bm = EXPERT_ROWS
    n_exp, d, de = w_gate.shape
    wblk = lambda e, b0, nb, ws, nu: (ws[e], 0, 0)
    grid_spec = pltpu.PrefetchScalarGridSpec(
        num_scalar_prefetch=4,
        grid=(n_exp,),
        in_specs=[
            pl.BlockSpec(memory_space=pl.ANY),
            pl.BlockSpec((1, d, de), wblk),
            pl.BlockSpec((1, d, de), wblk),
            pl.BlockSpec((1, de, d), wblk),
        ],
        out_specs=pl.BlockSpec(memory_space=pl.ANY),
        scratch_shapes=[pltpu.VMEM((bm * SUBLANES, LANES), F32)] * (IN_SLOTS + OUT_SLOTS) + [
            pltpu.VMEM((d, de), BF16),
            pltpu.VMEM((d, de), BF16),
            pltpu.VMEM((de, d), BF16),
            pltpu.SemaphoreType.DMA((IN_SLOTS,)),
            pltpu.SemaphoreType.DMA((OUT_SLOTS,)),
        ],
    )
    n_prefetch = 4
    return pl.pallas_call(
        _expert_kernel,
        grid_spec=grid_spec,
        out_shape=jax.ShapeDtypeStruct(xs.shape, F32),
        input_output_aliases={n_prefetch: 0},
        compiler_params=pltpu.CompilerParams(
            dimension_semantics=("arbitrary",), vmem_limit_bytes=VMEM_LIMIT_BYTES,
            has_side_effects=True),
        name="experts",
    )(blk0, nblk, wsel, n_used, xs, w_gate, w_up, w_down)


def _combine_kernel(pstart_ref, tab_ref, tab_next_ref, x1_ref, mod_ref, ln2g_ref, ln2b_ref,
                    ewcol_ref, ys_hbm, out_ref, ybuf0, ybuf1, sem):
    i = pl.program_id(0)
    t_rows = x1_ref.shape[0]
    n_loc = TOP_K * t_rows
    ybufs = (ybuf0, ybuf1)

    @pl.when(i == 0)
    def _():
        _strip_copies(tab_ref, pstart_ref, ybuf0, ys_hbm, sem.at[0], to_hbm=False)

    for slot in range(2):
        @pl.when(i % 2 == slot)
        def _():
            @pl.when(i + 1 < pl.num_programs(0))
            def _():
                _strip_copies(tab_next_ref, pstart_ref, ybufs[1 - slot], ys_hbm,
                              sem.at[1 - slot], to_hbm=False)

            pltpu.make_async_copy(_tile_rows(ys_hbm, 0, n_loc), ybufs[slot], sem.at[slot]).wait()
            ew = ewcol_ref[...]
            lane = lax.broadcasted_iota(I32, (t_rows, n_loc), 1).astype(F32)
            weights = (jnp.where(lane == ew[:, 2:3], ew[:, 0:1], 0.0)
                       + jnp.where(lane == ew[:, 3:4], ew[:, 1:2], 0.0))
            y = _dot(weights.astype(BF16), _tiles_to_rows(ybufs[slot], n_loc).astype(BF16))
            gate2 = mod_ref[0, 5:6, :]
            out_ref[...] = _layer_norm(DEEPNORM_ALPHA * x1_ref[...] + (1.0 + gate2) * y,
                                       ln2g_ref[...], ln2b_ref[...])


def _combine(pad_start, tab, x1, mod, ln2_g, ln2_b, ewcol, ys, *, seq):
    n, d = x1.shape
    t = MIX_ROWS
    per_seq = seq // t
    n_steps = n // t
    row_blk = lambda i, ps: (i, 0)
    grid_spec = pltpu.PrefetchScalarGridSpec(
        num_scalar_prefetch=1,
        grid=(n_steps,),
        in_specs=[
            pl.BlockSpec((SUBLANES, LANES), row_blk, memory_space=pltpu.SMEM),
            pl.BlockSpec((SUBLANES, LANES), lambda i, ps: (jnp.minimum(i + 1, n_steps - 1), 0),
                         memory_space=pltpu.SMEM),
            pl.BlockSpec((t, d), row_blk),
            pl.BlockSpec((1, N_MOD, d), lambda i, ps: (i // per_seq, 0, 0)),
            pl.BlockSpec((1, d), lambda i, ps: (0, 0)),
            pl.BlockSpec((1, d), lambda i, ps: (0, 0)),
            pl.BlockSpec((t, LANES), row_blk),
            pl.BlockSpec(memory_space=pl.ANY),
        ],
        out_specs=pl.BlockSpec((t, d), row_blk),
        scratch_shapes=[
            pltpu.VMEM((TOP_K * t * SUBLANES, LANES), F32),
            pltpu.VMEM((TOP_K * t * SUBLANES, LANES), F32),
            pltpu.SemaphoreType.DMA((2,)),
        ],
    )
    return pl.pallas_call(
        _combine_kernel,
        grid_spec=grid_spec,
        out_shape=jax.ShapeDtypeStruct((n, d), F32),
        compiler_params=pltpu.CompilerParams(
            dimension_semantics=("arbitrary",), vmem_limit_bytes=VMEM_LIMIT_BYTES),
        name="combine",
    )(pad_start, tab, tab, x1, mod, ln2_g, ln2_b, ewcol, ys)


def _block_layout(counts, n_assign):
    bm = EXPERT_ROWS
    n_blocks = -(-n_assign // bm) + N_EXPERTS
    padded = (counts + bm - 1) // bm * bm
    pad_end = jnp.cumsum(padded)
    pad_start = (pad_end - padded).astype(I32)
    n_used = (pad_end[-1] // bm).astype(I32)
    pad_row = (pad_start + counts).astype(I32)
    pad_len = (padded - counts).astype(I32)
    ids = jnp.arange(N_EXPERTS, dtype=I32)
    prev_used = lax.cummax(jnp.where(padded > 0, ids, -1))
    first_used = jnp.argmax(padded > 0).astype(I32)
    weight_sel = jnp.where(prev_used >= 0, prev_used, first_used).astype(I32)
    return (pad_start, pad_row, pad_len, pad_start // bm, (padded // bm).astype(I32), weight_sel,
            n_used.reshape(1), n_blocks)


def kernel(x, c, w_ada, b_ada, w_in, conv_w, w_out_conv, w_pool, pool_scale, w_o, ln1_g, ln1_b,
           w_group, b_group, w_router, b_router, w_gate, w_up, w_down, ln2_g, ln2_b):
    bsz, seq, d = x.shape
    n = bsz * seq
    assert d == SUBLANES * LANES and seq % MIX_ROWS == 0 and w_ada.shape[0] == DEPTH
    x2d = x.reshape(n, d)
    c_pad = jnp.pad(c, ((0, -bsz % SUBLANES), (0, 0)))
    for l in range(DEPTH):
        mod = _ada(c_pad, w_ada[l], b_ada[l][None, :])[:bsz].reshape(bsz, N_MOD, d)
        wr = jnp.concatenate(
            [w_router[l], w_group[l],
             jnp.zeros((d, LANES - N_EXPERTS - N_GROUPS), F32)], axis=1)
        rb = jnp.concatenate(
            [b_router[l].reshape(-1), b_group[l],
             jnp.zeros((LANES - N_EXPERTS - N_GROUPS,), F32)])[:, None]
        x1, h, loc, ewcol, tab, cnt = _mix(
            x2d, mod, w_in[l].astype(BF16), conv_w[l], w_out_conv[l].astype(BF16),
            w_pool[l].astype(BF16), pool_scale[l][None, :], w_o[l].astype(BF16),
            ln1_g[l][None, :], ln1_b[l][None, :], wr, rb, batch=bsz, seq=seq)
        pad_start, pad_row, pad_len, blk0, nblk, weight_sel, n_used, n_blocks = _block_layout(
            cnt[:, 0], n * TOP_K)
        xs = _dispatch(pad_start, pad_row, pad_len, n_used, tab, loc, h, n_blocks)
        ys = _experts(blk0, nblk, weight_sel, n_used, xs, w_gate[l], w_up[l], w_down[l])
        x2d = _combine(pad_start, tab, x1, mod, ln2_g[l][None, :], ln2_b[l][None, :],
                       ewcol, ys, seq=seq)
    return x2d.reshape(bsz, seq, d)
```

```python
import functools

import jax
import jax.numpy as jnp
from jax import lax
from jax.experimental import pallas as pl
from jax.experimental.pallas import tpu as pltpu

F32 = jnp.float32
BF16 = jnp.bfloat16
I32 = jnp.int32

LANES = 128
SUBLANES = 8
VMEM_LIMIT_BYTES = 56 * 1024 * 1024

N_POOL_GROUPS = 4
POOL_WINDOWS = (2, 4, 8, 16)
CONV_WIDTH = 3
N_GROUPS = 8
EXPERTS_PER_GROUP = 8
N_EXPERTS = N_GROUPS * EXPERTS_PER_GROUP
TOP_K = 2
N_MOD = 6
LN_EPS = 1e-5
DEPTH = 1
DEEPNORM_ALPHA = (2.0 * DEPTH) ** 0.25

MIX_ROWS = 512
EXPERT_ROWS = 256
HALO = SUBLANES
PACK_SUBLANES = 4


def _split_bf16(a):
    hi = a.astype(BF16)
    lo = (a - hi.astype(F32)).astype(BF16)
    return hi, lo


def _dot(a, b):
    return jnp.dot(a, b, preferred_element_type=F32)


def _dot3(a, b):
    a_hi, a_lo = _split_bf16(a)
    b_hi, b_lo = _split_bf16(b)
    return _dot(a_hi, b_hi) + _dot(a_lo, b_hi) + _dot(a_hi, b_lo)


def _dot3_wide(a, b):
    a_hi, a_lo = _split_bf16(a)
    n = b.shape[1]
    b_both = jnp.concatenate(_split_bf16(b), axis=1)
    hi = _dot(a_hi, b_both)
    lo = _dot(a_lo, b_both)
    return hi[:, :n] + hi[:, n:] + lo[:, :n] + lo[:, n:]


def _layer_norm(r, g, b):
    mu = jnp.mean(r, axis=-1, keepdims=True)
    d = r - mu
    var = jnp.mean(d * d, axis=-1, keepdims=True)
    return d * lax.rsqrt(var + LN_EPS) * g + b


def _packed_shape(rows):
    return (rows, PACK_SUBLANES, LANES)


def _plane(ref, s):
    rows = ref.shape[0]
    return ref.reshape(rows * PACK_SUBLANES, LANES).at[pl.ds(s, rows, stride=PACK_SUBLANES), :]


def _pack_rows(ref, val):
    half = val.shape[1] // 2
    bits = lax.bitcast_convert_type(val.astype(BF16).astype(F32), I32)
    for s in range(PACK_SUBLANES):
        hi = bits[:, s * LANES:(s + 1) * LANES]
        lo = bits[:, half + s * LANES:half + (s + 1) * LANES]
        _plane(ref, s)[...] = jnp.bitwise_or(hi, lax.shift_right_logical(lo, 16))


def _unpack_rows(ref):
    his, los = [], []
    for s in range(PACK_SUBLANES):
        word = _plane(ref, s)[...]
        his.append(lax.bitcast_convert_type(jnp.bitwise_and(word, jnp.int32(-65536)), F32))
        los.append(lax.bitcast_convert_type(lax.shift_left(word, 16), F32))
    return jnp.concatenate(his + los, axis=1).astype(BF16)


def _rows(ref, row, n_rows):
    return ref.at[pl.ds(row, n_rows)]


def _ada_kernel(c_ref, w_ref, b_ref, o_ref):
    cc = c_ref[...]
    act = cc * jax.nn.sigmoid(cc)
    o_ref[...] = _dot3(act, w_ref[...]) + b_ref[...]


def _ada(c_pad, w_ada, b_ada):
    rows, d = c_pad.shape
    n_out = w_ada.shape[1]
    return pl.pallas_call(
        _ada_kernel,
        grid=(n_out // d,),
        in_specs=[
            pl.BlockSpec((rows, d), lambda j: (0, 0)),
            pl.BlockSpec((d, d), lambda j: (0, j)),
            pl.BlockSpec((1, d), lambda j: (0, j)),
        ],
        out_specs=pl.BlockSpec((rows, d), lambda j: (0, j)),
        out_shape=jax.ShapeDtypeStruct((rows, n_out), F32),
        compiler_params=pltpu.CompilerParams(
            dimension_semantics=("arbitrary",), vmem_limit_bytes=VMEM_LIMIT_BYTES),
        name="ada",
    )(c_pad, w_ada, b_ada)


def _mix_kernel(x_ref, mod_ref, win_ref, convw_ref, woc_ref, wpool_ref, pscale_ref, wo_ref,
                ln1g_ref, ln1b_ref, wr_ref, rb_ref,
                x1_ref, h_ref, loc_ref, ewcol_ref, tab_ref, cnt_ref,
                cvbuf, zbuf, s2buf, s4buf, s8buf, base_ref, lt_buf, *, tiles_per_seq):
    t_rows, d = x_ref.shape
    pg = d // N_POOL_GROUPS
    step = pl.program_id(0)
    n_tiles = pl.num_programs(0) - 1
    s = jnp.minimum(step, n_tiles - 1) % tiles_per_seq
    shift_bufs = (cvbuf, zbuf, s2buf, s4buf, s8buf)

    @pl.when(step == 0)
    def _():
        base_ref[...] = jnp.zeros_like(base_ref)
        lt_buf[...] = jnp.zeros_like(lt_buf)

    @pl.when(s == 0)
    def _():
        for buf in shift_bufs:
            buf[0:HALO, :] = jnp.zeros((HALO, buf.shape[1]), F32)

    @pl.when(s > 0)
    def _():
        for buf in shift_bufs:
            buf[0:HALO, :] = buf[t_rows:t_rows + HALO, :]

    _route_tile(lt_buf[...], step > 0, base_ref, loc_ref, ewcol_ref, tab_ref, cnt_ref)

    def shifted(buf, k):
        return buf[HALO - k:HALO - k + t_rows, :]

    xt = x_ref[...]
    shift1 = mod_ref[0, 0:1, :]
    scale1 = mod_ref[0, 1:2, :]
    gate1 = mod_ref[0, 2:3, :]
    shift2 = mod_ref[0, 3:4, :]
    scale2 = mod_ref[0, 4:5, :]

    u = (xt * (1.0 + scale1) + shift1).astype(BF16)

    def proj(i):
        return _dot(u, win_ref[:, i * d:(i + 1) * d])

    cv = proj(2) * proj(0)
    cvbuf[HALO:HALO + t_rows, :] = cv
    conv = (convw_ref[2:3, :] * cv + convw_ref[1:2, :] * shifted(cvbuf, 1)
            + convw_ref[0:1, :] * shifted(cvbuf, 2))
    y_a = _dot((conv * proj(1)).astype(BF16), woc_ref[...])

    z = proj(3)
    zbuf[HALO:HALO + t_rows, :] = z
    s2 = z + shifted(zbuf, 1)
    s2buf[HALO:HALO + t_rows, :] = s2[:, pg:]
    s4 = s2[:, pg:] + shifted(s2buf, 2)
    s4buf[HALO:HALO + t_rows, :] = s4[:, pg:]
    s8 = s4[:, pg:] + shifted(s4buf, 4)
    s8buf[HALO:HALO + t_rows, :] = s8[:, pg:]
    s16 = s8[:, pg:] + shifted(s8buf, 8)
    win_sums = (s2[:, :pg], s4[:, :pg], s8[:, :pg], s16)
    t_pos = s * t_rows + lax.broadcasted_iota(I32, (t_rows, 1), 0) + 1
    y_b_parts = []
    for gi, w in enumerate(POOL_WINDOWS):
        count = jnp.minimum(t_pos, w).astype(F32)
        pooled = win_sums[gi] / count - z[:, gi * pg:(gi + 1) * pg]
        y_b_parts.append(_dot(pooled.astype(BF16), wpool_ref[gi]))
    y_b = jnp.concatenate(y_b_parts, axis=1) * pscale_ref[...]

    merged = jax.nn.sigmoid(proj(4)) * y_a + jax.nn.sigmoid(proj(5)) * y_b
    mix = _dot(merged.astype(BF16), wo_ref[...])

    x1 = _layer_norm(DEEPNORM_ALPHA * xt + (1.0 + gate1) * mix, ln1g_ref[...], ln1b_ref[...])
    h = x1 * (1.0 + scale2) + shift2
    lt_buf[...] = _dot3_wide(h, wr_ref[...]).T + rb_ref[...]

    @pl.when(step < n_tiles)
    def _():
        x1_ref[...] = x1
        h_ref[...] = h.astype(BF16)


def _route_tile(lt, valid, base_ref, loc_ref, ewcol_ref, tab_ref, cnt_ref):
    t_rows = lt.shape[1]
    iota8 = lax.broadcasted_iota(I32, (SUBLANES, t_rows), 0)
    lg = lt[N_EXPERTS:N_EXPERTS + N_GROUPS, :]
    g_max = jnp.max(lg, axis=0, keepdims=True)
    g_top = jnp.min(jnp.where(lg == g_max, iota8, N_GROUPS), axis=0, keepdims=True)
    g_top_p = 1.0 / jnp.sum(jnp.exp(lg - g_max), axis=0, keepdims=True)
    sel = jnp.zeros((EXPERTS_PER_GROUP, t_rows), F32)
    for g in range(N_GROUPS):
        sel = jnp.where(g_top == g, lt[g * EXPERTS_PER_GROUP:(g + 1) * EXPERTS_PER_GROUP, :], sel)
    m1 = jnp.max(sel, axis=0, keepdims=True)
    i1 = jnp.min(jnp.where(sel == m1, iota8, EXPERTS_PER_GROUP), axis=0, keepdims=True)
    sel2 = jnp.where(iota8 == i1, -jnp.inf, sel)
    m2 = jnp.max(sel2, axis=0, keepdims=True)
    i2 = jnp.min(jnp.where(sel2 == m2, iota8, EXPERTS_PER_GROUP), axis=0, keepdims=True)
    ratio = jnp.exp(m2 - m1)
    p1 = 1.0 / (1.0 + ratio)
    w1 = g_top_p * p1
    w2 = g_top_p * (ratio * p1)
    e1 = g_top * EXPERTS_PER_GROUP + i1
    e2 = g_top * EXPERTS_PER_GROUP + i2

    iota_e = lax.broadcasted_iota(I32, (N_EXPERTS, t_rows), 0)
    hit1 = jnp.logical_and(iota_e == e1, valid)
    hit2 = jnp.logical_and(iota_e == e2, valid)
    onehot = jnp.where(jnp.logical_or(hit1, hit2), 1.0, 0.0)
    before = (lax.broadcasted_iota(I32, (t_rows, t_rows), 0)
              < lax.broadcasted_iota(I32, (t_rows, t_rows), 1))
    prefix = _dot(onehot.astype(BF16), jnp.where(before, 1.0, 0.0).astype(BF16))
    tile_cnt = jnp.sum(onehot, axis=1, keepdims=True)
    lower = (lax.broadcasted_iota(I32, (N_EXPERTS, N_EXPERTS), 1)
             < lax.broadcasted_iota(I32, (N_EXPERTS, N_EXPERTS), 0))
    lower = jnp.where(lower, 1.0, 0.0).astype(BF16)
    cnt_hi, cnt_lo = _split_bf16(jnp.broadcast_to(tile_cnt, (N_EXPERTS, LANES)))
    loc_start = (_dot(lower, cnt_hi) + _dot(lower, cnt_lo))[:, 0:1]
    pos = prefix + loc_start
    loc1 = jnp.sum(jnp.where(hit1, pos, 0.0), axis=0, keepdims=True)
    loc2 = jnp.sum(jnp.where(hit2, pos, 0.0), axis=0, keepdims=True)
    base = base_ref[...]
    base_ref[...] = base + tile_cnt
    cnt_ref[...] = jnp.broadcast_to(base + tile_cnt, cnt_ref.shape).astype(I32)

    diag = (lax.broadcasted_iota(I32, (N_EXPERTS, LANES), 0)
            == lax.broadcasted_iota(I32, (N_EXPERTS, LANES), 1))

    def to_lanes(col):
        return jnp.sum(jnp.where(diag, col, 0.0), axis=0, keepdims=True)

    iota8l = lax.broadcasted_iota(I32, (SUBLANES, LANES), 0)
    tab_ref[...] = jnp.where(
        iota8l == 0, to_lanes(tile_cnt),
        jnp.where(iota8l == 1, to_lanes(base),
                  jnp.where(iota8l == 2, to_lanes(loc_start), 0.0))).astype(I32)

    loc_ref[...] = jnp.where(iota8 == 0, loc1, jnp.where(iota8 == 1, loc2, 0.0)).astype(I32)
    iota_l = lax.broadcasted_iota(I32, (LANES, t_rows), 0)
    ewcol_ref[...] = jnp.where(
        iota_l == 0, w1, jnp.where(iota_l == 1, w2,
                                   jnp.where(iota_l == 2, loc1, jnp.where(iota_l == 3, loc2, 0.0)))).T


def _mix(x2d, mod, w_in, conv_w, w_oc, w_pool, pool_scale, w_o, ln1_g, ln1_b, wr, rb,
         *, batch, seq):
    n, d = x2d.shape
    t = MIX_ROWS
    n_s = seq // t
    n_tiles = batch * n_s
    const = lambda shape: pl.BlockSpec(shape, lambda i: (0,) * len(shape),
                                       pipeline_mode=pl.Buffered(1))
    cur = lambda i: jnp.minimum(i, n_tiles - 1)
    prev = lambda i: jnp.maximum(i - 1, 0)
    pg = d // N_POOL_GROUPS
    return pl.pallas_call(
        functools.partial(_mix_kernel, tiles_per_seq=n_s),
        grid=(n_tiles + 1,),
        in_specs=[
            pl.BlockSpec((t, d), lambda i: (cur(i), 0)),
            pl.BlockSpec((1, N_MOD, d), lambda i: (cur(i) // n_s, 0, 0)),
            const(w_in.shape), const(conv_w.shape), const(w_oc.shape), const(w_pool.shape),
            const(pool_scale.shape), const(w_o.shape), const(ln1_g.shape), const(ln1_b.shape),
            const(wr.shape), const(rb.shape),
        ],
        out_specs=[
            pl.BlockSpec((t, d), lambda i: (cur(i), 0)),
            pl.BlockSpec((t, d), lambda i: (cur(i), 0)),
            pl.BlockSpec((SUBLANES, t), lambda i: (0, prev(i))),
            pl.BlockSpec((t, LANES), lambda i: (prev(i), 0)),
            pl.BlockSpec((SUBLANES, LANES), lambda i: (prev(i), 0)),
            pl.BlockSpec((N_EXPERTS, LANES), lambda i: (0, 0)),
        ],
        out_shape=[
            jax.ShapeDtypeStruct((n, d), F32),
            jax.ShapeDtypeStruct((n, d), BF16),
            jax.ShapeDtypeStruct((SUBLANES, n), I32),
            jax.ShapeDtypeStruct((n, LANES), F32),
            jax.ShapeDtypeStruct((n // t * SUBLANES, LANES), I32),
            jax.ShapeDtypeStruct((N_EXPERTS, LANES), I32),
        ],
        scratch_shapes=[
            pltpu.VMEM((t + HALO, d), F32),
            pltpu.VMEM((t + HALO, d), F32),
            pltpu.VMEM((t + HALO, 3 * pg), F32),
            pltpu.VMEM((t + HALO, 2 * pg), F32),
            pltpu.VMEM((t + HALO, pg), F32),
            pltpu.VMEM((N_EXPERTS, 1), F32),
            pltpu.VMEM((LANES, t), F32),
        ],
        compiler_params=pltpu.CompilerParams(
            dimension_semantics=("arbitrary",), vmem_limit_bytes=VMEM_LIMIT_BYTES),
        name="mix",
    )(x2d, mod, w_in, conv_w, w_oc, w_pool, pool_scale, w_o, ln1_g, ln1_b, wr, rb)


TAB_LEN, TAB_RANK, TAB_LOC = 0, 1, 2
FILL_SEM = 2


def _strip_copies(tab_ref, pstart_ref, local_ref, sorted_hbm, sem, *, to_hbm):
    def strip(e, carry):
        n_rows = tab_ref[TAB_LEN, e]

        @pl.when(n_rows > 0)
        def _():
            local = _rows(local_ref, tab_ref[TAB_LOC, e], n_rows)
            remote = _rows(sorted_hbm, pstart_ref[e] + tab_ref[TAB_RANK, e], n_rows)
            src, dst = (local, remote) if to_hbm else (remote, local)
            pltpu.make_async_copy(src, dst, sem).start()
        return carry

    lax.fori_loop(0, N_EXPERTS, strip, 0)


def _dispatch_kernel(pstart_ref, padrow_ref, padlen_ref, nused_ref, tab_ref, loc_ref, h_ref, xs_hbm,
                     xloc0, xloc1, zero_blk, sem):
    i = pl.program_id(0)
    n_steps = pl.num_programs(0)
    t_rows = h_ref.shape[0]
    n_loc = TOP_K * t_rows
    blk_tok = zero_blk.shape[0]
    n_blocks = xs_hbm.shape[0] // blk_tok
    xlocs = (xloc0, xloc1)

    def zero_fills(act):
        def pad_rows(e, carry):
            @pl.when(padlen_ref[e] > 0)
            def _():
                act(pltpu.make_async_copy(
                    _rows(zero_blk, 0, padlen_ref[e]),
                    _rows(xs_hbm, padrow_ref[e], padlen_ref[e]), sem.at[FILL_SEM]))
            return carry

        def unused_block(b, carry):
            act(pltpu.make_async_copy(
                zero_blk, _rows(xs_hbm, b * blk_tok, blk_tok), sem.at[FILL_SEM]))
            return carry

        lax.fori_loop(0, N_EXPERTS, pad_rows, 0)
        lax.fori_loop(nused_ref[0], n_blocks, unused_block, 0)

    @pl.when(i == 0)
    def _():
        zero_blk[...] = jnp.zeros_like(zero_blk)
        zero_fills(lambda copy: copy.start())

    def strips_done(slot):
        pltpu.make_async_copy(xlocs[slot], _rows(xs_hbm, 0, n_loc), sem.at[slot]).wait()

    for slot in range(2):
        @pl.when(i % 2 == slot)
        def _():
            @pl.when(i >= 2)
            def _():
                strips_done(slot)

            iota_p = lax.broadcasted_iota(I32, (n_loc, t_rows), 0)
            pick = jnp.logical_or(iota_p == loc_ref[0:1, :], iota_p == loc_ref[1:2, :])
            xsorted = _dot(jnp.where(pick, 1.0, 0.0).astype(BF16), h_ref[...])
            _pack_rows(xlocs[slot], xsorted)
            _strip_copies(tab_ref, pstart_ref, xlocs[slot], xs_hbm, sem.at[slot], to_hbm=True)

    @pl.when(i == n_steps - 1)
    def _():
        for slot in range(2):
            @pl.when(n_steps > slot)
            def _():
                strips_done(slot)
        zero_fills(lambda copy: copy.wait())


def _dispatch(pad_start, pad_row, pad_len, n_used, tab, loc, h, n_blocks):
    n, d = h.shape
    t = MIX_ROWS
    grid_spec = pltpu.PrefetchScalarGridSpec(
        num_scalar_prefetch=4,
        grid=(n // t,),
        in_specs=[
            pl.BlockSpec((SUBLANES, LANES), lambda i, *_: (i, 0), memory_space=pltpu.SMEM),
            pl.BlockSpec((SUBLANES, t), lambda i, *_: (0, i)),
            pl.BlockSpec((t, d), lambda i, *_: (i, 0)),
        ],
        out_specs=pl.BlockSpec(memory_space=pl.ANY),
        scratch_shapes=[
            pltpu.VMEM(_packed_shape(TOP_K * t), I32),
            pltpu.VMEM(_packed_shape(TOP_K * t), I32),
            pltpu.VMEM(_packed_shape(EXPERT_ROWS), I32),
            pltpu.SemaphoreType.DMA((3,)),
        ],
    )
    return pl.pallas_call(
        _dispatch_kernel,
        grid_spec=grid_spec,
        out_shape=jax.ShapeDtypeStruct(_packed_shape(n_blocks * EXPERT_ROWS), I32),
        compiler_params=pltpu.CompilerParams(
            dimension_semantics=("arbitrary",), vmem_limit_bytes=VMEM_LIMIT_BYTES,
            has_side_effects=True),
        name="dispatch",
    )(pad_start, pad_row, pad_len, n_used, tab, loc, h)


IN_SLOTS = 4
OUT_SLOTS = 2


def _expert_kernel(blk0_ref, nblk_ref, wsel_ref, nused_ref, xs_hbm, wg_ref, wu_ref, wd_ref, ys_hbm,
                   xbuf0, xbuf1, xbuf2, xbuf3, ybuf0, ybuf1, wg_bf, wu_bf, wd_bf, sem_in, sem_out):
    del wsel_ref
    e = pl.program_id(0)
    blk_tok = xbuf0.shape[0]
    xbufs = (xbuf0, xbuf1, xbuf2, xbuf3)
    ybufs = (ybuf0, ybuf1)
    n_used = nused_ref[0]
    ahead = IN_SLOTS - 1

    def in_copy(g, slot):
        return pltpu.make_async_copy(
            _rows(xs_hbm, g * blk_tok, blk_tok), xbufs[slot], sem_in.at[slot])

    def out_copy(g, slot):
        return pltpu.make_async_copy(
            ybufs[slot], _rows(ys_hbm, g * blk_tok, blk_tok), sem_out.at[slot])

    @pl.when(e == 0)
    def _():
        for g in range(ahead):
            @pl.when(g < n_used)
            def _():
                in_copy(g, g).start()

    @pl.when(nblk_ref[e] > 0)
    def _():
        wg_bf[...] = wg_ref[0].astype(BF16)
        wu_bf[...] = wu_ref[0].astype(BF16)
        wd_bf[...] = wd_ref[0].astype(BF16)

    def block(j, carry):
        g = blk0_ref[e] + j
        for slot in range(IN_SLOTS):
            oslot = slot % OUT_SLOTS

            @pl.when(g % IN_SLOTS == slot)
            def _():
                in_copy(g, slot).wait()

                @pl.when(g + ahead < n_used)
                def _():
                    in_copy(g + ahead, (slot + ahead) % IN_SLOTS).start()

                @pl.when(g >= OUT_SLOTS)
                def _():
                    out_copy(g - OUT_SLOTS, oslot).wait()

                xb = _unpack_rows(xbufs[slot])
                gate = _dot(xb, wg_bf[...])
                up = _dot(xb, wu_bf[...])
                act = (gate * jax.nn.sigmoid(gate) * up).astype(BF16)
                _pack_rows(ybufs[oslot], _dot(act, wd_bf[...]))
                out_copy(g, oslot).start()
        return carry

    lax.fori_loop(0, nblk_ref[e], block, 0)

    @pl.when(e == pl.num_programs(0) - 1)
    def _():
        for oslot in range(OUT_SLOTS):
            @pl.when(n_used > oslot)
            def _():
                g_last = n_used - 1 - ((n_used - 1 - oslot) % OUT_SLOTS)
                out_copy(g_last, oslot).wait()


def _experts(blk0, nblk, wsel, n_used, xs, w_gate, w_up, w_down):
    bm = EXPERT_ROWS
    n_exp, d, de = w_gate.shape
    wblk = lambda e, b0, nb, ws, nu: (ws[e], 0, 0)
    grid_spec = pltpu.PrefetchScalarGridSpec(
        num_scalar_prefetch=4,
        grid=(n_exp,),
        in_specs=[
            pl.BlockSpec(memory_space=pl.ANY),
            pl.BlockSpec((1, d, de), wblk),
            pl.BlockSpec((1, d, de), wblk),
            pl.BlockSpec((1, de, d), wblk),
        ],
        out_specs=pl.BlockSpec(memory_space=pl.ANY),
        scratch_shapes=[pltpu.VMEM(_packed_shape(bm), I32)] * (IN_SLOTS + OUT_SLOTS) + [
            pltpu.VMEM((d, de), BF16),
            pltpu.VMEM((d, de), BF16),
            pltpu.VMEM((de, d), BF16),
            pltpu.SemaphoreType.DMA((IN_SLOTS,)),
            pltpu.SemaphoreType.DMA((OUT_SLOTS,)),
        ],
    )
    n_prefetch = 4
    return pl.pallas_call(
        _expert_kernel,
        grid_spec=grid_spec,
        out_shape=jax.ShapeDtypeStruct(xs.shape, xs.dtype),
        input_output_aliases={n_prefetch: 0},
        compiler_params=pltpu.CompilerParams(
            dimension_semantics=("arbitrary",), vmem_limit_bytes=VMEM_LIMIT_BYTES,
            has_side_effects=True),
        name="experts",
    )(blk0, nblk, wsel, n_used, xs, w_gate, w_up, w_down)


def _combine_kernel(pstart_ref, tab_ref, tab_next_ref, x1_ref, mod_ref, ln2g_ref, ln2b_ref,
                    ewcol_ref, ys_hbm, out_ref, ybuf0, ybuf1, sem):
    i = pl.program_id(0)
    t_rows = x1_ref.shape[0]
    n_loc = TOP_K * t_rows
    ybufs = (ybuf0, ybuf1)

    @pl.when(i == 0)
    def _():
        _strip_copies(tab_ref, pstart_ref, ybuf0, ys_hbm, sem.at[0], to_hbm=False)

    for slot in range(2):
        @pl.when(i % 2 == slot)
        def _():
            @pl.when(i + 1 < pl.num_programs(0))
            def _():
                _strip_copies(tab_next_ref, pstart_ref, ybufs[1 - slot], ys_hbm,
                              sem.at[1 - slot], to_hbm=False)

            pltpu.make_async_copy(_rows(ys_hbm, 0, n_loc), ybufs[slot], sem.at[slot]).wait()
            ew = ewcol_ref[...]
            lane = lax.broadcasted_iota(I32, (t_rows, n_loc), 1).astype(F32)
            weights = (jnp.where(lane == ew[:, 2:3], ew[:, 0:1], 0.0)
                       + jnp.where(lane == ew[:, 3:4], ew[:, 1:2], 0.0))
            y = _dot(weights.astype(BF16), _unpack_rows(ybufs[slot]))
            gate2 = mod_ref[0, 5:6, :]
            out_ref[...] = _layer_norm(DEEPNORM_ALPHA * x1_ref[...] + (1.0 + gate2) * y,
                                       ln2g_ref[...], ln2b_ref[...])


def _combine(pad_start, tab, x1, mod, ln2_g, ln2_b, ewcol, ys, *, seq):
    n, d = x1.shape
    t = MIX_ROWS
    per_seq = seq // t
    n_steps = n // t
    row_blk = lambda i, ps: (i, 0)
    grid_spec = pltpu.PrefetchScalarGridSpec(
        num_scalar_prefetch=1,
        grid=(n_steps,),
        in_specs=[
            pl.BlockSpec((SUBLANES, LANES), row_blk, memory_space=pltpu.SMEM),
            pl.BlockSpec((SUBLANES, LANES), lambda i, ps: (jnp.minimum(i + 1, n_steps - 1), 0),
                         memory_space=pltpu.SMEM),
            pl.BlockSpec((t, d), row_blk),
            pl.BlockSpec((1, N_MOD, d), lambda i, ps: (i // per_seq, 0, 0)),
            pl.BlockSpec((1, d), lambda i, ps: (0, 0)),
            pl.BlockSpec((1, d), lambda i, ps: (0, 0)),
            pl.BlockSpec((t, LANES), row_blk),
            pl.BlockSpec(memory_space=pl.ANY),
        ],
        out_specs=pl.BlockSpec((t, d), row_blk),
        scratch_shapes=[
            pltpu.VMEM(_packed_shape(TOP_K * t), I32),
            pltpu.VMEM(_packed_shape(TOP_K * t), I32),
            pltpu.SemaphoreType.DMA((2,)),
        ],
    )
    return pl.pallas_call(
        _combine_kernel,
        grid_spec=grid_spec,
        out_shape=jax.ShapeDtypeStruct((n, d), F32),
        compiler_params=pltpu.CompilerParams(
            dimension_semantics=("arbitrary",), vmem_limit_bytes=VMEM_LIMIT_BYTES),
        name="combine",
    )(pad_start, tab, tab, x1, mod, ln2_g, ln2_b, ewcol, ys)


def _block_layout(counts, n_assign):
    bm = EXPERT_ROWS
    n_blocks = -(-n_assign // bm) + N_EXPERTS
    padded = (counts + bm - 1) // bm * bm
    pad_end = jnp.cumsum(padded)
    pad_start = (pad_end - padded).astype(I32)
    n_used = (pad_end[-1] // bm).astype(I32)
    pad_row = (pad_start + counts).astype(I32)
    pad_len = (padded - counts).astype(I32)
    ids = jnp.arange(N_EXPERTS, dtype=I32)
    prev_used = lax.cummax(jnp.where(padded > 0, ids, -1))
    first_used = jnp.argmax(padded > 0).astype(I32)
    weight_sel = jnp.where(prev_used >= 0, prev_used, first_used).astype(I32)
    return (pad_start, pad_row, pad_len, pad_start // bm, (padded // bm).astype(I32), weight_sel,
            n_used.reshape(1), n_blocks)


def kernel(x, c, w_ada, b_ada, w_in, conv_w, w_out_conv, w_pool, pool_scale, w_o, ln1_g, ln1_b,
           w_group, b_group, w_router, b_router, w_gate, w_up, w_down, ln2_g, ln2_b):
    bsz, seq, d = x.shape
    n = bsz * seq
    assert d == 2 * PACK_SUBLANES * LANES and seq % MIX_ROWS == 0 and w_ada.shape[0] == DEPTH
    x2d = x.reshape(n, d)
    c_pad = jnp.pad(c, ((0, -bsz % SUBLANES), (0, 0)))
    for l in range(DEPTH):
        mod = _ada(c_pad, w_ada[l], b_ada[l][None, :])[:bsz].reshape(bsz, N_MOD, d)
        wr = jnp.concatenate(
            [w_router[l], w_group[l],
             jnp.zeros((d, LANES - N_EXPERTS - N_GROUPS), F32)], axis=1)
        rb = jnp.concatenate(
            [b_router[l].reshape(-1), b_group[l],
             jnp.zeros((LANES - N_EXPERTS - N_GROUPS,), F32)])[:, None]
        x1, h, loc, ewcol, tab, cnt = _mix(
            x2d, mod, w_in[l].astype(BF16), conv_w[l], w_out_conv[l].astype(BF16),
            w_pool[l].astype(BF16), pool_scale[l][None, :], w_o[l].astype(BF16),
            ln1_g[l][None, :], ln1_b[l][None, :], wr, rb, batch=bsz, seq=seq)
        pad_start, pad_row, pad_len, blk0, nblk, weight_sel, n_used, n_blocks = _block_layout(
            cnt[:, 0], n * TOP_K)
        xs = _dispatch(pad_start, pad_row, pad_len, n_used, tab, loc, h, n_blocks)
        ys = _experts(blk0, nblk, weight_sel, n_used, xs, w_gate[l], w_up[l], w_down[l])
        x2d = _combine(pad_start, tab, x1, mod, ln2_g[l][None, :], ln2_b[l][None, :],
                       ewcol, ys, seq=seq)
    return x2d.reshape(bsz, seq, d)
```

```python
import functools

import jax
import jax.numpy as jnp
from jax import lax
from jax.experimental import pallas as pl
from jax.experimental.pallas import tpu as pltpu

F32 = jnp.float32
BF16 = jnp.bfloat16
I32 = jnp.int32

LANES = 128
SUBLANES = 8
VMEM_LIMIT_BYTES = 56 * 1024 * 1024

N_POOL_GROUPS = 4
POOL_WINDOWS = (2, 4, 8, 16)
CONV_WIDTH = 3
N_GROUPS = 8
EXPERTS_PER_GROUP = 8
N_EXPERTS = N_GROUPS * EXPERTS_PER_GROUP
TOP_K = 2
N_MOD = 6
LN_EPS = 1e-5
DEPTH = 1
DEEPNORM_ALPHA = (2.0 * DEPTH) ** 0.25

MIX_ROWS = 512
EXPERT_ROWS = 256
HALO = SUBLANES
PACK_SUBLANES = 4


def _split_bf16(a):
    hi = a.astype(BF16)
    lo = (a - hi.astype(F32)).astype(BF16)
    return hi, lo


def _dot(a, b):
    return jnp.dot(a, b, preferred_element_type=F32)


def _dot3(a, b):
    a_hi, a_lo = _split_bf16(a)
    b_hi, b_lo = _split_bf16(b)
    return _dot(a_hi, b_hi) + _dot(a_lo, b_hi) + _dot(a_hi, b_lo)


def _dot3_wide(a, b):
    a_hi, a_lo = _split_bf16(a)
    n = b.shape[1]
    b_both = jnp.concatenate(_split_bf16(b), axis=1)
    hi = _dot(a_hi, b_both)
    lo = _dot(a_lo, b_both)
    return hi[:, :n] + hi[:, n:] + lo[:, :n] + lo[:, n:]


def _layer_norm(r, g, b):
    mu = jnp.mean(r, axis=-1, keepdims=True)
    d = r - mu
    var = jnp.mean(d * d, axis=-1, keepdims=True)
    return d * lax.rsqrt(var + LN_EPS) * g + b


def _packed_shape(rows):
    return (rows, PACK_SUBLANES, LANES)


def _plane(ref, s):
    rows = ref.shape[0]
    return ref.reshape(rows * PACK_SUBLANES, LANES).at[pl.ds(s, rows, stride=PACK_SUBLANES), :]


def _pack_rows(ref, val):
    half = val.shape[1] // 2
    bits = lax.bitcast_convert_type(val.astype(BF16).astype(F32), I32)
    for s in range(PACK_SUBLANES):
        hi = bits[:, s * LANES:(s + 1) * LANES]
        lo = bits[:, half + s * LANES:half + (s + 1) * LANES]
        _plane(ref, s)[...] = jnp.bitwise_or(hi, lax.shift_right_logical(lo, 16))


def _unpack_rows(ref):
    his, los = [], []
    for s in range(PACK_SUBLANES):
        word = _plane(ref, s)[...]
        his.append(lax.bitcast_convert_type(jnp.bitwise_and(word, jnp.int32(-65536)), F32))
        los.append(lax.bitcast_convert_type(lax.shift_left(word, 16), F32))
    return jnp.concatenate(his + los, axis=1).astype(BF16)


def _rows(ref, row, n_rows):
    return ref.at[pl.ds(row, n_rows)]


def _ada_kernel(c_ref, w_ref, b_ref, o_ref):
    cc = c_ref[...]
    act = cc * jax.nn.sigmoid(cc)
    o_ref[...] = _dot3(act, w_ref[...]) + b_ref[...]


def _ada(c_pad, w_ada, b_ada):
    rows, d = c_pad.shape
    n_out = w_ada.shape[1]
    return pl.pallas_call(
        _ada_kernel,
        grid=(n_out // d,),
        in_specs=[
            pl.BlockSpec((rows, d), lambda j: (0, 0)),
            pl.BlockSpec((d, d), lambda j: (0, j)),
            pl.BlockSpec((1, d), lambda j: (0, j)),
        ],
        out_specs=pl.BlockSpec((rows, d), lambda j: (0, j)),
        out_shape=jax.ShapeDtypeStruct((rows, n_out), F32),
        compiler_params=pltpu.CompilerParams(
            dimension_semantics=("arbitrary",), vmem_limit_bytes=VMEM_LIMIT_BYTES),
        name="ada",
    )(c_pad, w_ada, b_ada)


def _mix_kernel(x_ref, mod_ref, win_ref, convw_ref, woc_ref, wpool_ref, pscale_ref, wo_ref,
                ln1g_ref, ln1b_ref, wr_ref, rb_ref,
                x1_ref, h_ref, loc_ref, ewcol_ref, tab_ref, cnt_ref,
                cvbuf, zbuf, s2buf, s4buf, s8buf, base_ref, lt_buf, *, tiles_per_seq):
    t_rows, d = x_ref.shape
    pg = d // N_POOL_GROUPS
    step = pl.program_id(0)
    n_tiles = pl.num_programs(0) - 1
    s = jnp.minimum(step, n_tiles - 1) % tiles_per_seq
    shift_bufs = (cvbuf, zbuf, s2buf, s4buf, s8buf)

    @pl.when(step == 0)
    def _():
        base_ref[...] = jnp.zeros_like(base_ref)
        lt_buf[...] = jnp.zeros_like(lt_buf)

    @pl.when(s == 0)
    def _():
        for buf in shift_bufs:
            buf[0:HALO, :] = jnp.zeros((HALO, buf.shape[1]), F32)

    @pl.when(s > 0)
    def _():
        for buf in shift_bufs:
            buf[0:HALO, :] = buf[t_rows:t_rows + HALO, :]

    _route_tile(lt_buf[...], step > 0, base_ref, loc_ref, ewcol_ref, tab_ref, cnt_ref)

    def shifted(buf, k):
        return buf[HALO - k:HALO - k + t_rows, :]

    xt = x_ref[...]
    shift1 = mod_ref[0, 0:1, :]
    scale1 = mod_ref[0, 1:2, :]
    gate1 = mod_ref[0, 2:3, :]
    shift2 = mod_ref[0, 3:4, :]
    scale2 = mod_ref[0, 4:5, :]

    u = (xt * (1.0 + scale1) + shift1).astype(BF16)

    def proj(i):
        return _dot(u, win_ref[:, i * d:(i + 1) * d])

    cv = proj(2) * proj(0)
    cvbuf[HALO:HALO + t_rows, :] = cv
    conv = (convw_ref[2:3, :] * cv + convw_ref[1:2, :] * shifted(cvbuf, 1)
            + convw_ref[0:1, :] * shifted(cvbuf, 2))
    y_a = _dot((conv * proj(1)).astype(BF16), woc_ref[...])

    z = proj(3)
    zbuf[HALO:HALO + t_rows, :] = z
    s2 = z + shifted(zbuf, 1)
    s2buf[HALO:HALO + t_rows, :] = s2[:, pg:]
    s4 = s2[:, pg:] + shifted(s2buf, 2)
    s4buf[HALO:HALO + t_rows, :] = s4[:, pg:]
    s8 = s4[:, pg:] + shifted(s4buf, 4)
    s8buf[HALO:HALO + t_rows, :] = s8[:, pg:]
    s16 = s8[:, pg:] + shifted(s8buf, 8)
    win_sums = (s2[:, :pg], s4[:, :pg], s8[:, :pg], s16)
    t_pos = s * t_rows + lax.broadcasted_iota(I32, (t_rows, 1), 0) + 1
    y_b_parts = []
    for gi, w in enumerate(POOL_WINDOWS):
        count = jnp.minimum(t_pos, w).astype(F32)
        pooled = win_sums[gi] / count - z[:, gi * pg:(gi + 1) * pg]
        y_b_parts.append(_dot(pooled.astype(BF16), wpool_ref[gi]))
    y_b = jnp.concatenate(y_b_parts, axis=1) * pscale_ref[...]

    merged = jax.nn.sigmoid(proj(4)) * y_a + jax.nn.sigmoid(proj(5)) * y_b
    mix = _dot(merged.astype(BF16), wo_ref[...])

    x1 = _layer_norm(DEEPNORM_ALPHA * xt + (1.0 + gate1) * mix, ln1g_ref[...], ln1b_ref[...])
    h = x1 * (1.0 + scale2) + shift2
    lt_buf[...] = _dot3_wide(h, wr_ref[...]).T + rb_ref[...]

    @pl.when(step < n_tiles)
    def _():
        x1_ref[...] = x1
        h_ref[...] = h.astype(BF16)


def _route_tile(lt, valid, base_ref, loc_ref, ewcol_ref, tab_ref, cnt_ref):
    t_rows = lt.shape[1]
    iota8 = lax.broadcasted_iota(I32, (SUBLANES, t_rows), 0)
    lg = lt[N_EXPERTS:N_EXPERTS + N_GROUPS, :]
    g_max = jnp.max(lg, axis=0, keepdims=True)
    g_top = jnp.min(jnp.where(lg == g_max, iota8, N_GROUPS), axis=0, keepdims=True)
    g_top_p = 1.0 / jnp.sum(jnp.exp(lg - g_max), axis=0, keepdims=True)
    sel = jnp.zeros((EXPERTS_PER_GROUP, t_rows), F32)
    for g in range(N_GROUPS):
        sel = jnp.where(g_top == g, lt[g * EXPERTS_PER_GROUP:(g + 1) * EXPERTS_PER_GROUP, :], sel)
    m1 = jnp.max(sel, axis=0, keepdims=True)
    i1 = jnp.min(jnp.where(sel == m1, iota8, EXPERTS_PER_GROUP), axis=0, keepdims=True)
    sel2 = jnp.where(iota8 == i1, -jnp.inf, sel)
    m2 = jnp.max(sel2, axis=0, keepdims=True)
    i2 = jnp.min(jnp.where(sel2 == m2, iota8, EXPERTS_PER_GROUP), axis=0, keepdims=True)
    ratio = jnp.exp(m2 - m1)
    p1 = 1.0 / (1.0 + ratio)
    w1 = g_top_p * p1
    w2 = g_top_p * (ratio * p1)
    e1 = g_top * EXPERTS_PER_GROUP + i1
    e2 = g_top * EXPERTS_PER_GROUP + i2

    iota_e = lax.broadcasted_iota(I32, (N_EXPERTS, t_rows), 0)
    hit1 = jnp.logical_and(iota_e == e1, valid)
    hit2 = jnp.logical_and(iota_e == e2, valid)
    onehot = jnp.where(jnp.logical_or(hit1, hit2), 1.0, 0.0)
    before = (lax.broadcasted_iota(I32, (t_rows, t_rows), 0)
              < lax.broadcasted_iota(I32, (t_rows, t_rows), 1))
    prefix = _dot(onehot.astype(BF16), jnp.where(before, 1.0, 0.0).astype(BF16))
    tile_cnt = jnp.sum(onehot, axis=1, keepdims=True)
    lower = (lax.broadcasted_iota(I32, (N_EXPERTS, N_EXPERTS), 1)
             < lax.broadcasted_iota(I32, (N_EXPERTS, N_EXPERTS), 0))
    lower = jnp.where(lower, 1.0, 0.0).astype(BF16)
    cnt_hi, cnt_lo = _split_bf16(jnp.broadcast_to(tile_cnt, (N_EXPERTS, LANES)))
    loc_start = (_dot(lower, cnt_hi) + _dot(lower, cnt_lo))[:, 0:1]
    pos = prefix + loc_start
    loc1 = jnp.sum(jnp.where(hit1, pos, 0.0), axis=0, keepdims=True)
    loc2 = jnp.sum(jnp.where(hit2, pos, 0.0), axis=0, keepdims=True)
    base = base_ref[...]
    base_ref[...] = base + tile_cnt
    cnt_ref[...] = jnp.broadcast_to(base + tile_cnt, cnt_ref.shape).astype(I32)

    diag = (lax.broadcasted_iota(I32, (N_EXPERTS, LANES), 0)
            == lax.broadcasted_iota(I32, (N_EXPERTS, LANES), 1))

    def to_lanes(col):
        return jnp.sum(jnp.where(diag, col, 0.0), axis=0, keepdims=True)

    iota8l = lax.broadcasted_iota(I32, (SUBLANES, LANES), 0)
    min_len = jnp.min(jnp.broadcast_to(tile_cnt, (N_EXPERTS, LANES)), axis=0, keepdims=True)
    tab_ref[...] = jnp.where(
        iota8l == TAB_LEN, to_lanes(tile_cnt),
        jnp.where(iota8l == TAB_RANK, to_lanes(base),
                  jnp.where(iota8l == TAB_LOC, to_lanes(loc_start),
                            jnp.where(iota8l == TAB_MIN_LEN, min_len, 0.0)))).astype(I32)

    loc_ref[...] = jnp.where(iota8 == 0, loc1, jnp.where(iota8 == 1, loc2, 0.0)).astype(I32)
    iota_l = lax.broadcasted_iota(I32, (LANES, t_rows), 0)
    ewcol_ref[...] = jnp.where(
        iota_l == 0, w1, jnp.where(iota_l == 1, w2,
                                   jnp.where(iota_l == 2, loc1, jnp.where(iota_l == 3, loc2, 0.0)))).T


def _mix(x2d, mod, w_in, conv_w, w_oc, w_pool, pool_scale, w_o, ln1_g, ln1_b, wr, rb,
         *, batch, seq):
    n, d = x2d.shape
    t = MIX_ROWS
    n_s = seq // t
    n_tiles = batch * n_s
    const = lambda shape: pl.BlockSpec(shape, lambda i: (0,) * len(shape),
                                       pipeline_mode=pl.Buffered(1))
    cur = lambda i: jnp.minimum(i, n_tiles - 1)
    prev = lambda i: jnp.maximum(i - 1, 0)
    pg = d // N_POOL_GROUPS
    return pl.pallas_call(
        functools.partial(_mix_kernel, tiles_per_seq=n_s),
        grid=(n_tiles + 1,),
        in_specs=[
            pl.BlockSpec((t, d), lambda i: (cur(i), 0)),
            pl.BlockSpec((1, N_MOD, d), lambda i: (cur(i) // n_s, 0, 0)),
            const(w_in.shape), const(conv_w.shape), const(w_oc.shape), const(w_pool.shape),
            const(pool_scale.shape), const(w_o.shape), const(ln1_g.shape), const(ln1_b.shape),
            const(wr.shape), const(rb.shape),
        ],
        out_specs=[
            pl.BlockSpec((t, d), lambda i: (cur(i), 0)),
            pl.BlockSpec((t, d), lambda i: (cur(i), 0)),
            pl.BlockSpec((SUBLANES, t), lambda i: (0, prev(i))),
            pl.BlockSpec((t, LANES), lambda i: (prev(i), 0)),
            pl.BlockSpec((SUBLANES, LANES), lambda i: (prev(i), 0)),
            pl.BlockSpec((N_EXPERTS, LANES), lambda i: (0, 0)),
        ],
        out_shape=[
            jax.ShapeDtypeStruct((n, d), F32),
            jax.ShapeDtypeStruct((n, d), BF16),
            jax.ShapeDtypeStruct((SUBLANES, n), I32),
            jax.ShapeDtypeStruct((n, LANES), F32),
            jax.ShapeDtypeStruct((n // t * SUBLANES, LANES), I32),
            jax.ShapeDtypeStruct((N_EXPERTS, LANES), I32),
        ],
        scratch_shapes=[
            pltpu.VMEM((t + HALO, d), F32),
            pltpu.VMEM((t + HALO, d), F32),
            pltpu.VMEM((t + HALO, 3 * pg), F32),
            pltpu.VMEM((t + HALO, 2 * pg), F32),
            pltpu.VMEM((t + HALO, pg), F32),
            pltpu.VMEM((N_EXPERTS, 1), F32),
            pltpu.VMEM((LANES, t), F32),
        ],
        compiler_params=pltpu.CompilerParams(
            dimension_semantics=("arbitrary",), vmem_limit_bytes=VMEM_LIMIT_BYTES),
        name="mix",
    )(x2d, mod, w_in, conv_w, w_oc, w_pool, pool_scale, w_o, ln1_g, ln1_b, wr, rb)


TAB_LEN, TAB_RANK, TAB_LOC, TAB_MIN_LEN = 0, 1, 2, 3
FILL_SEM = 2


def _strip_copies(tab_ref, pstart_ref, local_ref, sorted_hbm, sem, *, to_hbm):
    def start(e):
        n_rows = tab_ref[TAB_LEN, e]
        local = _rows(local_ref, tab_ref[TAB_LOC, e], n_rows)
        remote = _rows(sorted_hbm, pstart_ref[e] + tab_ref[TAB_RANK, e], n_rows)
        src, dst = (local, remote) if to_hbm else (remote, local)
        pltpu.make_async_copy(src, dst, sem).start()

    @pl.when(tab_ref[TAB_MIN_LEN, 0] > 0)
    def _():
        for e in range(N_EXPERTS):
            start(e)

    @pl.when(tab_ref[TAB_MIN_LEN, 0] <= 0)
    def _():
        def strip(e, carry):
            @pl.when(tab_ref[TAB_LEN, e] > 0)
            def _():
                start(e)
            return carry

        lax.fori_loop(0, N_EXPERTS, strip, 0)


def _dispatch_kernel(pstart_ref, padrow_ref, padlen_ref, nused_ref, tab_ref, loc_ref, h_ref, xs_hbm,
                     xloc0, xloc1, zero_blk, sem):
    i = pl.program_id(0)
    n_steps = pl.num_programs(0)
    t_rows = h_ref.shape[0]
    n_loc = TOP_K * t_rows
    blk_tok = zero_blk.shape[0]
    n_blocks = xs_hbm.shape[0] // blk_tok
    xlocs = (xloc0, xloc1)

    def zero_fills(act):
        def pad_rows(e, carry):
            @pl.when(padlen_ref[e] > 0)
            def _():
                act(pltpu.make_async_copy(
                    _rows(zero_blk, 0, padlen_ref[e]),
                    _rows(xs_hbm, padrow_ref[e], padlen_ref[e]), sem.at[FILL_SEM]))
            return carry

        def unused_block(b, carry):
            act(pltpu.make_async_copy(
                zero_blk, _rows(xs_hbm, b * blk_tok, blk_tok), sem.at[FILL_SEM]))
            return carry

        lax.fori_loop(0, N_EXPERTS, pad_rows, 0)
        lax.fori_loop(nused_ref[0], n_blocks, unused_block, 0)

    @pl.when(i == 0)
    def _():
        zero_blk[...] = jnp.zeros_like(zero_blk)
        zero_fills(lambda copy: copy.start())

    def strips_done(slot):
        pltpu.make_async_copy(xlocs[slot], _rows(xs_hbm, 0, n_loc), sem.at[slot]).wait()

    for slot in range(2):
        @pl.when(i % 2 == slot)
        def _():
            @pl.when(i >= 2)
            def _():
                strips_done(slot)

            iota_p = lax.broadcasted_iota(I32, (n_loc, t_rows), 0)
            pick = jnp.logical_or(iota_p == loc_ref[0:1, :], iota_p == loc_ref[1:2, :])
            xsorted = _dot(jnp.where(pick, 1.0, 0.0).astype(BF16), h_ref[...])
            _pack_rows(xlocs[slot], xsorted)
            _strip_copies(tab_ref, pstart_ref, xlocs[slot], xs_hbm, sem.at[slot], to_hbm=True)

    @pl.when(i == n_steps - 1)
    def _():
        for slot in range(2):
            @pl.when(n_steps > slot)
            def _():
                strips_done(slot)
        zero_fills(lambda copy: copy.wait())


def _dispatch(pad_start, pad_row, pad_len, n_used, tab, loc, h, n_blocks):
    n, d = h.shape
    t = MIX_ROWS
    grid_spec = pltpu.PrefetchScalarGridSpec(
        num_scalar_prefetch=4,
        grid=(n // t,),
        in_specs=[
            pl.BlockSpec((SUBLANES, LANES), lambda i, *_: (i, 0), memory_space=pltpu.SMEM),
            pl.BlockSpec((SUBLANES, t), lambda i, *_: (0, i)),
            pl.BlockSpec((t, d), lambda i, *_: (i, 0)),
        ],
        out_specs=pl.BlockSpec(memory_space=pl.ANY),
        scratch_shapes=[
            pltpu.VMEM(_packed_shape(TOP_K * t), I32),
            pltpu.VMEM(_packed_shape(TOP_K * t), I32),
            pltpu.VMEM(_packed_shape(EXPERT_ROWS), I32),
            pltpu.SemaphoreType.DMA((3,)),
        ],
    )
    return pl.pallas_call(
        _dispatch_kernel,
        grid_spec=grid_spec,
        out_shape=jax.ShapeDtypeStruct(_packed_shape(n_blocks * EXPERT_ROWS), I32),
        compiler_params=pltpu.CompilerParams(
            dimension_semantics=("arbitrary",), vmem_limit_bytes=VMEM_LIMIT_BYTES,
            has_side_effects=True),
        name="dispatch",
    )(pad_start, pad_row, pad_len, n_used, tab, loc, h)


IN_SLOTS = 4
OUT_SLOTS = 2


def _expert_kernel(blk0_ref, nblk_ref, wsel_ref, nused_ref, xs_hbm, wg_ref, wu_ref, wd_ref, ys_hbm,
                   xbuf0, xbuf1, xbuf2, xbuf3, ybuf0, ybuf1, wg_bf, wu_bf, wd_bf, sem_in, sem_out):
    del wsel_ref
    e = pl.program_id(0)
    blk_tok = xbuf0.shape[0]
    xbufs = (xbuf0, xbuf1, xbuf2, xbuf3)
    ybufs = (ybuf0, ybuf1)
    n_used = nused_ref[0]
    ahead = IN_SLOTS - 1

    def in_copy(g, slot):
        return pltpu.make_async_copy(
            _rows(xs_hbm, g * blk_tok, blk_tok), xbufs[slot], sem_in.at[slot])

    def out_copy(g, slot):
        return pltpu.make_async_copy(
            ybufs[slot], _rows(ys_hbm, g * blk_tok, blk_tok), sem_out.at[slot])

    @pl.when(e == 0)
    def _():
        for g in range(ahead):
            @pl.when(g < n_used)
            def _():
                in_copy(g, g).start()

    @pl.when(nblk_ref[e] > 0)
    def _():
        wg_bf[...] = wg_ref[0].astype(BF16)
        wu_bf[...] = wu_ref[0].astype(BF16)
        wd_bf[...] = wd_ref[0].astype(BF16)

    def block(j, carry):
        g = blk0_ref[e] + j
        for slot in range(IN_SLOTS):
            oslot = slot % OUT_SLOTS

            @pl.when(g % IN_SLOTS == slot)
            def _():
                in_copy(g, slot).wait()

                @pl.when(g + ahead < n_used)
                def _():
                    in_copy(g + ahead, (slot + ahead) % IN_SLOTS).start()

                @pl.when(g >= OUT_SLOTS)
                def _():
                    out_copy(g - OUT_SLOTS, oslot).wait()

                xb = _unpack_rows(xbufs[slot])
                gate = _dot(xb, wg_bf[...])
                up = _dot(xb, wu_bf[...])
                act = (gate * jax.nn.sigmoid(gate) * up).astype(BF16)
                _pack_rows(ybufs[oslot], _dot(act, wd_bf[...]))
                out_copy(g, oslot).start()
        return carry

    lax.fori_loop(0, nblk_ref[e], block, 0)

    @pl.when(e == pl.num_programs(0) - 1)
    def _():
        for oslot in range(OUT_SLOTS):
            @pl.when(n_used > oslot)
            def _():
                g_last = n_used - 1 - ((n_used - 1 - oslot) % OUT_SLOTS)
                out_copy(g_last, oslot).wait()


def _experts(blk0, nblk, wsel, n_used, xs, w_gate, w_up, w_down):
    bm = EXPERT_ROWS
    n_exp, d, de = w_gate.shape
    wblk = lambda e, b0, nb, ws, nu: (ws[e], 0, 0)
    grid_spec = pltpu.PrefetchScalarGridSpec(
        num_scalar_prefetch=4,
        grid=(n_exp,),
        in_specs=[
            pl.BlockSpec(memory_space=pl.ANY),
            pl.BlockSpec((1, d, de), wblk),
            pl.BlockSpec((1, d, de), wblk),
            pl.BlockSpec((1, de, d), wblk),
        ],
        out_specs=pl.BlockSpec(memory_space=pl.ANY),
        scratch_shapes=[pltpu.VMEM(_packed_shape(bm), I32)] * (IN_SLOTS + OUT_SLOTS) + [
            pltpu.VMEM((d, de), BF16),
            pltpu.VMEM((d, de), BF16),
            pltpu.VMEM((de, d), BF16),
            pltpu.SemaphoreType.DMA((IN_SLOTS,)),
            pltpu.SemaphoreType.DMA((OUT_SLOTS,)),
        ],
    )
    n_prefetch = 4
    return pl.pallas_call(
        _expert_kernel,
        grid_spec=grid_spec,
        out_shape=jax.ShapeDtypeStruct(xs.shape, xs.dtype),
        input_output_aliases={n_prefetch: 0},
        compiler_params=pltpu.CompilerParams(
            dimension_semantics=("arbitrary",), vmem_limit_bytes=VMEM_LIMIT_BYTES,
            has_side_effects=True),
        name="experts",
    )(blk0, nblk, wsel, n_used, xs, w_gate, w_up, w_down)


def _combine_kernel(pstart_ref, tab_ref, tab_next_ref, x1_ref, mod_ref, ln2g_ref, ln2b_ref,
                    ewcol_ref, ys_hbm, out_ref, ybuf0, ybuf1, sem):
    i = pl.program_id(0)
    t_rows = x1_ref.shape[0]
    n_loc = TOP_K * t_rows
    ybufs = (ybuf0, ybuf1)

    @pl.when(i == 0)
    def _():
        _strip_copies(tab_ref, pstart_ref, ybuf0, ys_hbm, sem.at[0], to_hbm=False)

    for slot in range(2):
        @pl.when(i % 2 == slot)
        def _():
            @pl.when(i + 1 < pl.num_programs(0))
            def _():
                _strip_copies(tab_next_ref, pstart_ref, ybufs[1 - slot], ys_hbm,
                              sem.at[1 - slot], to_hbm=False)

            pltpu.make_async_copy(_rows(ys_hbm, 0, n_loc), ybufs[slot], sem.at[slot]).wait()
            ew = ewcol_ref[...]
            lane = lax.broadcasted_iota(I32, (t_rows, n_loc), 1).astype(F32)
            weights = (jnp.where(lane == ew[:, 2:3], ew[:, 0:1], 0.0)
                       + jnp.where(lane == ew[:, 3:4], ew[:, 1:2], 0.0))
            y = _dot(weights.astype(BF16), _unpack_rows(ybufs[slot]))
            gate2 = mod_ref[0, 5:6, :]
            out_ref[...] = _layer_norm(DEEPNORM_ALPHA * x1_ref[...] + (1.0 + gate2) * y,
                                       ln2g_ref[...], ln2b_ref[...])


def _combine(pad_start, tab, x1, mod, ln2_g, ln2_b, ewcol, ys, *, seq):
    n, d = x1.shape
    t = MIX_ROWS
    per_seq = seq // t
    n_steps = n // t
    row_blk = lambda i, ps: (i, 0)
    grid_spec = pltpu.PrefetchScalarGridSpec(
        num_scalar_prefetch=1,
        grid=(n_steps,),
        in_specs=[
            pl.BlockSpec((SUBLANES, LANES), row_blk, memory_space=pltpu.SMEM),
            pl.BlockSpec((SUBLANES, LANES), lambda i, ps: (jnp.minimum(i + 1, n_steps - 1), 0),
                         memory_space=pltpu.SMEM),
            pl.BlockSpec((t, d), row_blk),
            pl.BlockSpec((1, N_MOD, d), lambda i, ps: (i // per_seq, 0, 0)),
            pl.BlockSpec((1, d), lambda i, ps: (0, 0)),
            pl.BlockSpec((1, d), lambda i, ps: (0, 0)),
            pl.BlockSpec((t, LANES), row_blk),
            pl.BlockSpec(memory_space=pl.ANY),
        ],
        out_specs=pl.BlockSpec((t, d), row_blk),
        scratch_shapes=[
            pltpu.VMEM(_packed_shape(TOP_K * t), I32),
            pltpu.VMEM(_packed_shape(TOP_K * t), I32),
            pltpu.SemaphoreType.DMA((2,)),
        ],
    )
    return pl.pallas_call(
        _combine_kernel,
        grid_spec=grid_spec,
        out_shape=jax.ShapeDtypeStruct((n, d), F32),
        compiler_params=pltpu.CompilerParams(
            dimension_semantics=("arbitrary",), vmem_limit_bytes=VMEM_LIMIT_BYTES),
        name="combine",
    )(pad_start, tab, tab, x1, mod, ln2_g, ln2_b, ewcol, ys)


def _block_layout(counts, n_assign):
    bm = EXPERT_ROWS
    n_blocks = -(-n_assign // bm) + N_EXPERTS
    padded = (counts + bm - 1) // bm * bm
    pad_end = jnp.cumsum(padded)
    pad_start = (pad_end - padded).astype(I32)
    n_used = (pad_end[-1] // bm).astype(I32)
    pad_row = (pad_start + counts).astype(I32)
    pad_len = (padded - counts).astype(I32)
    ids = jnp.arange(N_EXPERTS, dtype=I32)
    prev_used = lax.cummax(jnp.where(padded > 0, ids, -1))
    first_used = jnp.argmax(padded > 0).astype(I32)
    weight_sel = jnp.where(prev_used >= 0, prev_used, first_used).astype(I32)
    return (pad_start, pad_row, pad_len, pad_start // bm, (padded // bm).astype(I32), weight_sel,
            n_used.reshape(1), n_blocks)


def kernel(x, c, w_ada, b_ada, w_in, conv_w, w_out_conv, w_pool, pool_scale, w_o, ln1_g, ln1_b,
           w_group, b_group, w_router, b_router, w_gate, w_up, w_down, ln2_g, ln2_b):
    bsz, seq, d = x.shape
    n = bsz * seq
    assert d == 2 * PACK_SUBLANES * LANES and seq % MIX_ROWS == 0 and w_ada.shape[0] == DEPTH
    x2d = x.reshape(n, d)
    c_pad = jnp.pad(c, ((0, -bsz % SUBLANES), (0, 0)))
    for l in range(DEPTH):
        mod = _ada(c_pad, w_ada[l], b_ada[l][None, :])[:bsz].reshape(bsz, N_MOD, d)
        wr = jnp.concatenate(
            [w_router[l], w_group[l],
             jnp.zeros((d, LANES - N_EXPERTS - N_GROUPS), F32)], axis=1)
        rb = jnp.concatenate(
            [b_router[l].reshape(-1), b_group[l],
             jnp.zeros((LANES - N_EXPERTS - N_GROUPS,), F32)])[:, None]
        x1, h, loc, ewcol, tab, cnt = _mix(
            x2d, mod, w_in[l].astype(BF16), conv_w[l], w_out_conv[l].astype(BF16),
            w_pool[l].astype(BF16), pool_scale[l][None, :], w_o[l].astype(BF16),
            ln1_g[l][None, :], ln1_b[l][None, :], wr, rb, batch=bsz, seq=seq)
        pad_start, pad_row, pad_len, blk0, nblk, weight_sel, n_used, n_blocks = _block_layout(
            cnt[:, 0], n * TOP_K)
        xs = _dispatch(pad_start, pad_row, pad_len, n_used, tab, loc, h, n_blocks)
        ys = _experts(blk0, nblk, weight_sel, n_used, xs, w_gate[l], w_up[l], w_down[l])
        x2d = _combine(pad_start, tab, x1, mod, ln2_g[l][None, :], ln2_b[l][None, :],
                       ewcol, ys, seq=seq)
    return x2d.reshape(bsz, seq, d)
```

```python
import functools

import jax
import jax.numpy as jnp
from jax import lax
from jax.experimental import pallas as pl
from jax.experimental.pallas import tpu as pltpu

F32 = jnp.float32
BF16 = jnp.bfloat16
I32 = jnp.int32

LANES = 128
SUBLANES = 8
VMEM_LIMIT_BYTES = 56 * 1024 * 1024

N_POOL_GROUPS = 4
POOL_WINDOWS = (2, 4, 8, 16)
CONV_WIDTH = 3
N_GROUPS = 8
EXPERTS_PER_GROUP = 8
N_EXPERTS = N_GROUPS * EXPERTS_PER_GROUP
TOP_K = 2
N_MOD = 6
LN_EPS = 1e-5
DEPTH = 1
DEEPNORM_ALPHA = (2.0 * DEPTH) ** 0.25

MIX_ROWS = 512
EXPERT_ROWS = 256
HALO = SUBLANES
PACK_SUBLANES = 4


def _split_bf16(a):
    hi = a.astype(BF16)
    lo = (a - hi.astype(F32)).astype(BF16)
    return hi, lo


def _dot(a, b):
    return jnp.dot(a, b, preferred_element_type=F32)


def _dot3(a, b):
    a_hi, a_lo = _split_bf16(a)
    b_hi, b_lo = _split_bf16(b)
    return _dot(a_hi, b_hi) + _dot(a_lo, b_hi) + _dot(a_hi, b_lo)


def _dot3_wide(a, b):
    a_hi, a_lo = _split_bf16(a)
    n = b.shape[1]
    b_both = jnp.concatenate(_split_bf16(b), axis=1)
    hi = _dot(a_hi, b_both)
    lo = _dot(a_lo, b_both)
    return hi[:, :n] + hi[:, n:] + lo[:, :n] + lo[:, n:]


def _layer_norm(r, g, b):
    mu = jnp.mean(r, axis=-1, keepdims=True)
    d = r - mu
    var = jnp.mean(d * d, axis=-1, keepdims=True)
    return d * lax.rsqrt(var + LN_EPS) * g + b


def _packed_shape(rows):
    return (rows, PACK_SUBLANES, LANES)


def _plane(ref, s, row0, rows):
    flat = ref.reshape(ref.shape[0] * PACK_SUBLANES, LANES)
    return flat.at[pl.ds(row0 * PACK_SUBLANES + s, rows, stride=PACK_SUBLANES), :]


def _pack_rows(ref, val, row0=0):
    rows, d = val.shape
    bits = lax.bitcast_convert_type(val.astype(BF16).astype(F32), I32)
    for s in range(PACK_SUBLANES):
        hi = bits[:, s * LANES:(s + 1) * LANES]
        lo = bits[:, d // 2 + s * LANES:d // 2 + (s + 1) * LANES]
        _plane(ref, s, row0, rows)[...] = jnp.bitwise_or(hi, lax.shift_right_logical(lo, 16))


def _unpack_rows(ref, row0=0, rows=None):
    rows = ref.shape[0] - row0 if rows is None else rows
    his, los = [], []
    for s in range(PACK_SUBLANES):
        word = _plane(ref, s, row0, rows)[...]
        his.append(lax.bitcast_convert_type(jnp.bitwise_and(word, jnp.int32(-65536)), F32))
        los.append(lax.bitcast_convert_type(lax.shift_left(word, 16), F32))
    return jnp.concatenate(his + los, axis=1).astype(BF16)


def _rows(ref, row, n_rows):
    return ref.at[pl.ds(row, n_rows)]


def _ada_kernel(c_ref, w_ref, b_ref, o_ref):
    cc = c_ref[...]
    act = cc * jax.nn.sigmoid(cc)
    o_ref[...] = _dot3(act, w_ref[...]) + b_ref[...]


def _ada(c_pad, w_ada, b_ada):
    rows, d = c_pad.shape
    n_out = w_ada.shape[1]
    return pl.pallas_call(
        _ada_kernel,
        grid=(n_out // d,),
        in_specs=[
            pl.BlockSpec((rows, d), lambda j: (0, 0)),
            pl.BlockSpec((d, d), lambda j: (0, j)),
            pl.BlockSpec((1, d), lambda j: (0, j)),
        ],
        out_specs=pl.BlockSpec((rows, d), lambda j: (0, j)),
        out_shape=jax.ShapeDtypeStruct((rows, n_out), F32),
        compiler_params=pltpu.CompilerParams(
            dimension_semantics=("arbitrary",), vmem_limit_bytes=VMEM_LIMIT_BYTES),
        name="ada",
    )(c_pad, w_ada, b_ada)


def _mix_kernel(x_ref, mod_ref, win_ref, convw_ref, woc_ref, wpool_ref, pscale_ref, wo_ref,
                ln1g_ref, ln1b_ref, wr_ref, rb_ref, mod_prev_ref,
                x1_ref, h_ref, loc_ref, ewcol_ref, tab_ref, cnt_ref,
                cvbuf, zbuf, s2buf, s4buf, s8buf, base_ref, res_buf, *, tiles_per_seq):
    t_rows, d = x_ref.shape
    pg = d // N_POOL_GROUPS
    step = pl.program_id(0)
    n_tiles = pl.num_programs(0) - 1
    s = jnp.minimum(step, n_tiles - 1) % tiles_per_seq
    shift_bufs = (cvbuf, zbuf, s2buf, s4buf, s8buf)

    @pl.when(step == 0)
    def _():
        base_ref[...] = jnp.zeros_like(base_ref)
        res_buf[...] = jnp.zeros_like(res_buf)

    @pl.when(s == 0)
    def _():
        for buf in shift_bufs:
            buf[0:HALO, :] = jnp.zeros((HALO, buf.shape[1]), F32)

    @pl.when(s > 0)
    def _():
        for buf in shift_bufs:
            buf[0:HALO, :] = buf[t_rows:t_rows + HALO, :]

    def shifted(buf, k):
        return buf[HALO - k:HALO - k + t_rows, :]

    xt = x_ref[...]
    shift1 = mod_ref[0, 0:1, :]
    scale1 = mod_ref[0, 1:2, :]
    gate1 = mod_ref[0, 2:3, :]

    u = (xt * (1.0 + scale1) + shift1).astype(BF16)

    def proj(i):
        return _dot(u, win_ref[:, i * d:(i + 1) * d])

    cv = proj(2) * proj(0)

    x1 = _layer_norm(res_buf[...], ln1g_ref[...], ln1b_ref[...])
    x1_ref[...] = x1
    h = x1 * (1.0 + mod_prev_ref[0, 4:5, :]) + mod_prev_ref[0, 3:4, :]
    h_ref[...] = h.astype(BF16)
    lt = _dot3_wide(h, wr_ref[...]).T + rb_ref[...]
    _route_tile(lt, step > 0, base_ref, loc_ref, ewcol_ref, tab_ref, cnt_ref)

    cvbuf[HALO:HALO + t_rows, :] = cv
    conv = (convw_ref[2:3, :] * cv + convw_ref[1:2, :] * shifted(cvbuf, 1)
            + convw_ref[0:1, :] * shifted(cvbuf, 2))
    y_a = _dot((conv * proj(1)).astype(BF16), woc_ref[...])

    z = proj(3)
    zbuf[HALO:HALO + t_rows, :] = z
    s2 = z + shifted(zbuf, 1)
    s2buf[HALO:HALO + t_rows, :] = s2[:, pg:]
    s4 = s2[:, pg:] + shifted(s2buf, 2)
    s4buf[HALO:HALO + t_rows, :] = s4[:, pg:]
    s8 = s4[:, pg:] + shifted(s4buf, 4)
    s8buf[HALO:HALO + t_rows, :] = s8[:, pg:]
    s16 = s8[:, pg:] + shifted(s8buf, 8)
    win_sums = (s2[:, :pg], s4[:, :pg], s8[:, :pg], s16)
    t_pos = s * t_rows + lax.broadcasted_iota(I32, (t_rows, 1), 0) + 1
    y_b_parts = []
    for gi, w in enumerate(POOL_WINDOWS):
        count = jnp.minimum(t_pos, w).astype(F32)
        pooled = win_sums[gi] / count - z[:, gi * pg:(gi + 1) * pg]
        y_b_parts.append(_dot(pooled.astype(BF16), wpool_ref[gi]))
    y_b = jnp.concatenate(y_b_parts, axis=1) * pscale_ref[...]

    merged = jax.nn.sigmoid(proj(4)) * y_a + jax.nn.sigmoid(proj(5)) * y_b
    mix = _dot(merged.astype(BF16), wo_ref[...])

    res_buf[...] = DEEPNORM_ALPHA * xt + (1.0 + gate1) * mix


def _route_tile(lt, valid, base_ref, loc_ref, ewcol_ref, tab_ref, cnt_ref):
    t_rows = lt.shape[1]
    iota8 = lax.broadcasted_iota(I32, (SUBLANES, t_rows), 0)
    lg = lt[N_EXPERTS:N_EXPERTS + N_GROUPS, :]
    g_max = jnp.max(lg, axis=0, keepdims=True)
    g_top = jnp.min(jnp.where(lg == g_max, iota8, N_GROUPS), axis=0, keepdims=True)
    g_top_p = 1.0 / jnp.sum(jnp.exp(lg - g_max), axis=0, keepdims=True)
    sel = jnp.zeros((EXPERTS_PER_GROUP, t_rows), F32)
    for g in range(N_GROUPS):
        sel = jnp.where(g_top == g, lt[g * EXPERTS_PER_GROUP:(g + 1) * EXPERTS_PER_GROUP, :], sel)
    m1 = jnp.max(sel, axis=0, keepdims=True)
    i1 = jnp.min(jnp.where(sel == m1, iota8, EXPERTS_PER_GROUP), axis=0, keepdims=True)
    sel2 = jnp.where(iota8 == i1, -jnp.inf, sel)
    m2 = jnp.max(sel2, axis=0, keepdims=True)
    i2 = jnp.min(jnp.where(sel2 == m2, iota8, EXPERTS_PER_GROUP), axis=0, keepdims=True)
    ratio = jnp.exp(m2 - m1)
    p1 = 1.0 / (1.0 + ratio)
    w1 = g_top_p * p1
    w2 = g_top_p * (ratio * p1)
    e1 = g_top * EXPERTS_PER_GROUP + i1
    e2 = g_top * EXPERTS_PER_GROUP + i2

    iota_e = lax.broadcasted_iota(I32, (N_EXPERTS, t_rows), 0)
    hit1 = jnp.logical_and(iota_e == e1, valid)
    hit2 = jnp.logical_and(iota_e == e2, valid)
    onehot = jnp.where(jnp.logical_or(hit1, hit2), 1.0, 0.0)
    before = (lax.broadcasted_iota(I32, (t_rows, t_rows), 0)
              < lax.broadcasted_iota(I32, (t_rows, t_rows), 1))
    prefix = _dot(onehot.astype(BF16), jnp.where(before, 1.0, 0.0).astype(BF16))
    tile_cnt = jnp.sum(onehot, axis=1, keepdims=True)
    lower = (lax.broadcasted_iota(I32, (N_EXPERTS, N_EXPERTS), 1)
             < lax.broadcasted_iota(I32, (N_EXPERTS, N_EXPERTS), 0))
    lower = jnp.where(lower, 1.0, 0.0).astype(BF16)
    cnt_hi, cnt_lo = _split_bf16(jnp.broadcast_to(tile_cnt, (N_EXPERTS, LANES)))
    loc_start = (_dot(lower, cnt_hi) + _dot(lower, cnt_lo))[:, 0:1]
    pos = prefix + loc_start
    loc1 = jnp.sum(jnp.where(hit1, pos, 0.0), axis=0, keepdims=True)
    loc2 = jnp.sum(jnp.where(hit2, pos, 0.0), axis=0, keepdims=True)
    base = base_ref[...]
    base_ref[...] = base + tile_cnt
    cnt_ref[...] = jnp.broadcast_to(base + tile_cnt, cnt_ref.shape).astype(I32)

    diag = (lax.broadcasted_iota(I32, (N_EXPERTS, LANES), 0)
            == lax.broadcasted_iota(I32, (N_EXPERTS, LANES), 1))

    def to_lanes(col):
        return jnp.sum(jnp.where(diag, col, 0.0), axis=0, keepdims=True)

    iota8l = lax.broadcasted_iota(I32, (SUBLANES, LANES), 0)
    min_len = jnp.min(jnp.broadcast_to(tile_cnt, (N_EXPERTS, LANES)), axis=0, keepdims=True)
    tab_ref[...] = jnp.where(
        iota8l == TAB_LEN, to_lanes(tile_cnt),
        jnp.where(iota8l == TAB_RANK, to_lanes(base),
                  jnp.where(iota8l == TAB_LOC, to_lanes(loc_start),
                            jnp.where(iota8l == TAB_MIN_LEN, min_len, 0.0)))).astype(I32)

    loc_ref[...] = jnp.where(iota8 == 0, loc1, jnp.where(iota8 == 1, loc2, 0.0)).astype(I32)
    iota_l = lax.broadcasted_iota(I32, (LANES, t_rows), 0)
    ewcol_ref[...] = jnp.where(
        iota_l == 0, w1, jnp.where(iota_l == 1, w2,
                                   jnp.where(iota_l == 2, loc1, jnp.where(iota_l == 3, loc2, 0.0)))).T


def _mix(x2d, mod, w_in, conv_w, w_oc, w_pool, pool_scale, w_o, ln1_g, ln1_b, wr, rb,
         *, batch, seq):
    n, d = x2d.shape
    t = MIX_ROWS
    n_s = seq // t
    n_tiles = batch * n_s
    const = lambda shape: pl.BlockSpec(shape, lambda i: (0,) * len(shape),
                                       pipeline_mode=pl.Buffered(1))
    cur = lambda i: jnp.minimum(i, n_tiles - 1)
    prev = lambda i: jnp.maximum(i - 1, 0)
    pg = d // N_POOL_GROUPS
    return pl.pallas_call(
        functools.partial(_mix_kernel, tiles_per_seq=n_s),
        grid=(n_tiles + 1,),
        in_specs=[
            pl.BlockSpec((t, d), lambda i: (cur(i), 0)),
            pl.BlockSpec((1, N_MOD, d), lambda i: (cur(i) // n_s, 0, 0)),
            const(w_in.shape), const(conv_w.shape), const(w_oc.shape), const(w_pool.shape),
            const(pool_scale.shape), const(w_o.shape), const(ln1_g.shape), const(ln1_b.shape),
            const(wr.shape), const(rb.shape),
            pl.BlockSpec((1, N_MOD, d), lambda i: (prev(i) // n_s, 0, 0)),
        ],
        out_specs=[
            pl.BlockSpec((t, d), lambda i: (prev(i), 0)),
            pl.BlockSpec((t, d), lambda i: (prev(i), 0)),
            pl.BlockSpec((SUBLANES, t), lambda i: (0, prev(i))),
            pl.BlockSpec((t, LANES), lambda i: (prev(i), 0)),
            pl.BlockSpec((SUBLANES, LANES), lambda i: (prev(i), 0)),
            pl.BlockSpec((N_EXPERTS, LANES), lambda i: (0, 0)),
        ],
        out_shape=[
            jax.ShapeDtypeStruct((n, d), F32),
            jax.ShapeDtypeStruct((n, d), BF16),
            jax.ShapeDtypeStruct((SUBLANES, n), I32),
            jax.ShapeDtypeStruct((n, LANES), F32),
            jax.ShapeDtypeStruct((n // t * SUBLANES, LANES), I32),
            jax.ShapeDtypeStruct((N_EXPERTS, LANES), I32),
        ],
        scratch_shapes=[
            pltpu.VMEM((t + HALO, d), F32),
            pltpu.VMEM((t + HALO, d), F32),
            pltpu.VMEM((t + HALO, 3 * pg), F32),
            pltpu.VMEM((t + HALO, 2 * pg), F32),
            pltpu.VMEM((t + HALO, pg), F32),
            pltpu.VMEM((N_EXPERTS, 1), F32),
            pltpu.VMEM((t, d), F32),
        ],
        compiler_params=pltpu.CompilerParams(
            dimension_semantics=("arbitrary",), vmem_limit_bytes=VMEM_LIMIT_BYTES),
        name="mix",
    )(x2d, mod, w_in, conv_w, w_oc, w_pool, pool_scale, w_o, ln1_g, ln1_b, wr, rb, mod)


TAB_LEN, TAB_RANK, TAB_LOC, TAB_MIN_LEN = 0, 1, 2, 3
FILL_SEM = 2


def _strip_copies(tab_ref, pstart_ref, local_ref, sorted_hbm, sem, *, to_hbm):
    def start(e):
        n_rows = tab_ref[TAB_LEN, e]
        local = _rows(local_ref, tab_ref[TAB_LOC, e], n_rows)
        remote = _rows(sorted_hbm, pstart_ref[e] + tab_ref[TAB_RANK, e], n_rows)
        src, dst = (local, remote) if to_hbm else (remote, local)
        pltpu.make_async_copy(src, dst, sem).start()

    @pl.when(tab_ref[TAB_MIN_LEN, 0] > 0)
    def _():
        for e in range(N_EXPERTS):
            start(e)

    @pl.when(tab_ref[TAB_MIN_LEN, 0] <= 0)
    def _():
        def strip(e, carry):
            @pl.when(tab_ref[TAB_LEN, e] > 0)
            def _():
                start(e)
            return carry

        lax.fori_loop(0, N_EXPERTS, strip, 0)


def _dispatch_kernel(pstart_ref, padrow_ref, padlen_ref, nused_ref, tab_ref, loc_ref, h_ref, xs_hbm,
                     xloc0, xloc1, zero_blk, sem):
    i = pl.program_id(0)
    n_steps = pl.num_programs(0)
    t_rows = h_ref.shape[0]
    n_loc = TOP_K * t_rows
    blk_tok = zero_blk.shape[0]
    n_blocks = xs_hbm.shape[0] // blk_tok
    xlocs = (xloc0, xloc1)

    def zero_fills(act):
        def pad_rows(e, carry):
            @pl.when(padlen_ref[e] > 0)
            def _():
                act(pltpu.make_async_copy(
                    _rows(zero_blk, 0, padlen_ref[e]),
                    _rows(xs_hbm, padrow_ref[e], padlen_ref[e]), sem.at[FILL_SEM]))
            return carry

        def unused_block(b, carry):
            act(pltpu.make_async_copy(
                zero_blk, _rows(xs_hbm, b * blk_tok, blk_tok), sem.at[FILL_SEM]))
            return carry

        lax.fori_loop(0, N_EXPERTS, pad_rows, 0)
        lax.fori_loop(nused_ref[0], n_blocks, unused_block, 0)

    @pl.when(i == 0)
    def _():
        zero_blk[...] = jnp.zeros_like(zero_blk)
        zero_fills(lambda copy: copy.start())

    def strips_done(slot):
        pltpu.make_async_copy(xlocs[slot], _rows(xs_hbm, 0, n_loc), sem.at[slot]).wait()

    for slot in range(2):
        @pl.when(i % 2 == slot)
        def _():
            @pl.when(i >= 2)
            def _():
                strips_done(slot)

            iota_p = lax.broadcasted_iota(I32, (n_loc, t_rows), 0)
            pick = jnp.logical_or(iota_p == loc_ref[0:1, :], iota_p == loc_ref[1:2, :])
            xsorted = _dot(jnp.where(pick, 1.0, 0.0).astype(BF16), h_ref[...])
            _pack_rows(xlocs[slot], xsorted)
            _strip_copies(tab_ref, pstart_ref, xlocs[slot], xs_hbm, sem.at[slot], to_hbm=True)

    @pl.when(i == n_steps - 1)
    def _():
        for slot in range(2):
            @pl.when(n_steps > slot)
            def _():
                strips_done(slot)
        zero_fills(lambda copy: copy.wait())


def _dispatch(pad_start, pad_row, pad_len, n_used, tab, loc, h, n_blocks):
    n, d = h.shape
    t = MIX_ROWS
    grid_spec = pltpu.PrefetchScalarGridSpec(
        num_scalar_prefetch=4,
        grid=(n // t,),
        in_specs=[
            pl.BlockSpec((SUBLANES, LANES), lambda i, *_: (i, 0), memory_space=pltpu.SMEM),
            pl.BlockSpec((SUBLANES, t), lambda i, *_: (0, i)),
            pl.BlockSpec((t, d), lambda i, *_: (i, 0)),
        ],
        out_specs=pl.BlockSpec(memory_space=pl.ANY),
        scratch_shapes=[
            pltpu.VMEM(_packed_shape(TOP_K * t), I32),
            pltpu.VMEM(_packed_shape(TOP_K * t), I32),
            pltpu.VMEM(_packed_shape(EXPERT_ROWS), I32),
            pltpu.SemaphoreType.DMA((3,)),
        ],
    )
    return pl.pallas_call(
        _dispatch_kernel,
        grid_spec=grid_spec,
        out_shape=jax.ShapeDtypeStruct(_packed_shape(n_blocks * EXPERT_ROWS), I32),
        compiler_params=pltpu.CompilerParams(
            dimension_semantics=("arbitrary",), vmem_limit_bytes=VMEM_LIMIT_BYTES,
            has_side_effects=True),
        name="dispatch",
    )(pad_start, pad_row, pad_len, n_used, tab, loc, h)


IN_SLOTS = 4
OUT_SLOTS = 2


def _expert_kernel(blk0_ref, nblk_ref, wsel_ref, nused_ref, xs_hbm, wg_ref, wu_ref, wd_ref, ys_hbm,
                   xbuf0, xbuf1, xbuf2, xbuf3, ybuf0, ybuf1, wg_bf, wu_bf, wd_bf, sem_in, sem_out):
    del wsel_ref
    e = pl.program_id(0)
    blk_tok = xbuf0.shape[0]
    xbufs = (xbuf0, xbuf1, xbuf2, xbuf3)
    ybufs = (ybuf0, ybuf1)
    n_used = nused_ref[0]
    ahead = IN_SLOTS - 1

    def in_copy(g, slot):
        return pltpu.make_async_copy(
            _rows(xs_hbm, g * blk_tok, blk_tok), xbufs[slot], sem_in.at[slot])

    def out_copy(g, slot):
        return pltpu.make_async_copy(
            ybufs[slot], _rows(ys_hbm, g * blk_tok, blk_tok), sem_out.at[slot])

    @pl.when(e == 0)
    def _():
        for g in range(ahead):
            @pl.when(g < n_used)
            def _():
                in_copy(g, g).start()

    @pl.when(nblk_ref[e] > 0)
    def _():
        wg_bf[...] = wg_ref[0].astype(BF16)
        wu_bf[...] = wu_ref[0].astype(BF16)
        wd_bf[...] = wd_ref[0].astype(BF16)

    def block(j, carry):
        g = blk0_ref[e] + j
        for slot in range(IN_SLOTS):
            oslot = slot % OUT_SLOTS

            @pl.when(g % IN_SLOTS == slot)
            def _():
                in_copy(g, slot).wait()

                @pl.when(g + ahead < n_used)
                def _():
                    in_copy(g + ahead, (slot + ahead) % IN_SLOTS).start()

                @pl.when(g >= OUT_SLOTS)
                def _():
                    out_copy(g - OUT_SLOTS, oslot).wait()

                xb = _unpack_rows(xbufs[slot])
                gate = _dot(xb, wg_bf[...])
                up = _dot(xb, wu_bf[...])
                act = (gate * jax.nn.sigmoid(gate) * up).astype(BF16)
                _pack_rows(ybufs[oslot], _dot(act, wd_bf[...]))
                out_copy(g, oslot).start()
        return carry

    lax.fori_loop(0, nblk_ref[e], block, 0)

    @pl.when(e == pl.num_programs(0) - 1)
    def _():
        for oslot in range(OUT_SLOTS):
            @pl.when(n_used > oslot)
            def _():
                g_last = n_used - 1 - ((n_used - 1 - oslot) % OUT_SLOTS)
                out_copy(g_last, oslot).wait()


def _experts(blk0, nblk, wsel, n_used, xs, w_gate, w_up, w_down):
    bm = EXPERT_ROWS
    n_exp, d, de = w_gate.shape
    wblk = lambda e, b0, nb, ws, nu: (ws[e], 0, 0)
    grid_spec = pltpu.PrefetchScalarGridSpec(
        num_scalar_prefetch=4,
        grid=(n_exp,),
        in_specs=[
            pl.BlockSpec(memory_space=pl.ANY),
            pl.BlockSpec((1, d, de), wblk),
            pl.BlockSpec((1, d, de), wblk),
            pl.BlockSpec((1, de, d), wblk),
        ],
        out_specs=pl.BlockSpec(memory_space=pl.ANY),
        scratch_shapes=[pltpu.VMEM(_packed_shape(bm), I32)] * (IN_SLOTS + OUT_SLOTS) + [
            pltpu.VMEM((d, de), BF16),
            pltpu.VMEM((d, de), BF16),
            pltpu.VMEM((de, d), BF16),
            pltpu.SemaphoreType.DMA((IN_SLOTS,)),
            pltpu.SemaphoreType.DMA((OUT_SLOTS,)),
        ],
    )
    n_prefetch = 4
    return pl.pallas_call(
        _expert_kernel,
        grid_spec=grid_spec,
        out_shape=jax.ShapeDtypeStruct(xs.shape, xs.dtype),
        input_output_aliases={n_prefetch: 0},
        compiler_params=pltpu.CompilerParams(
            dimension_semantics=("arbitrary",), vmem_limit_bytes=VMEM_LIMIT_BYTES,
            has_side_effects=True),
        name="experts",
    )(blk0, nblk, wsel, n_used, xs, w_gate, w_up, w_down)


def _combine_kernel(pstart_ref, tab_ref, tab_next_ref, x1_ref, mod_ref, ln2g_ref, ln2b_ref,
                    ewcol_ref, ys_hbm, out_ref, ybuf0, ybuf1, sem):
    i = pl.program_id(0)
    t_rows = x1_ref.shape[0]
    n_loc = TOP_K * t_rows
    ybufs = (ybuf0, ybuf1)

    @pl.when(i == 0)
    def _():
        _strip_copies(tab_ref, pstart_ref, ybuf0, ys_hbm, sem.at[0], to_hbm=False)

    for slot in range(2):
        @pl.when(i % 2 == slot)
        def _():
            @pl.when(i + 1 < pl.num_programs(0))
            def _():
                _strip_copies(tab_next_ref, pstart_ref, ybufs[1 - slot], ys_hbm,
                              sem.at[1 - slot], to_hbm=False)

            pltpu.make_async_copy(_rows(ys_hbm, 0, n_loc), ybufs[slot], sem.at[slot]).wait()
            ew = ewcol_ref[...]
            lane = lax.broadcasted_iota(I32, (t_rows, n_loc), 1).astype(F32)
            weights = (jnp.where(lane == ew[:, 2:3], ew[:, 0:1], 0.0)
                       + jnp.where(lane == ew[:, 3:4], ew[:, 1:2], 0.0))
            y = _dot(weights.astype(BF16), _unpack_rows(ybufs[slot]))
            gate2 = mod_ref[0, 5:6, :]
            out_ref[...] = _layer_norm(DEEPNORM_ALPHA * x1_ref[...] + (1.0 + gate2) * y,
                                       ln2g_ref[...], ln2b_ref[...])


def _combine(pad_start, tab, x1, mod, ln2_g, ln2_b, ewcol, ys, *, seq):
    n, d = x1.shape
    t = MIX_ROWS
    per_seq = seq // t
    n_steps = n // t
    row_blk = lambda i, ps: (i, 0)
    grid_spec = pltpu.PrefetchScalarGridSpec(
        num_scalar_prefetch=1,
        grid=(n_steps,),
        in_specs=[
            pl.BlockSpec((SUBLANES, LANES), row_blk, memory_space=pltpu.SMEM),
            pl.BlockSpec((SUBLANES, LANES), lambda i, ps: (jnp.minimum(i + 1, n_steps - 1), 0),
                         memory_space=pltpu.SMEM),
            pl.BlockSpec((t, d), row_blk),
            pl.BlockSpec((1, N_MOD, d), lambda i, ps: (i // per_seq, 0, 0)),
            pl.BlockSpec((1, d), lambda i, ps: (0, 0)),
            pl.BlockSpec((1, d), lambda i, ps: (0, 0)),
            pl.BlockSpec((t, LANES), row_blk),
            pl.BlockSpec(memory_space=pl.ANY),
        ],
        out_specs=pl.BlockSpec((t, d), row_blk),
        scratch_shapes=[
            pltpu.VMEM(_packed_shape(TOP_K * t), I32),
            pltpu.VMEM(_packed_shape(TOP_K * t), I32),
            pltpu.SemaphoreType.DMA((2,)),
        ],
    )
    return pl.pallas_call(
        _combine_kernel,
        grid_spec=grid_spec,
        out_shape=jax.ShapeDtypeStruct((n, d), F32),
        compiler_params=pltpu.CompilerParams(
            dimension_semantics=("arbitrary",), vmem_limit_bytes=VMEM_LIMIT_BYTES),
        name="combine",
    )(pad_start, tab, tab, x1, mod, ln2_g, ln2_b, ewcol, ys)


def _block_layout(counts, n_assign):
    bm = EXPERT_ROWS
    n_blocks = -(-n_assign // bm) + N_EXPERTS
    padded = (counts + bm - 1) // bm * bm
    pad_end = jnp.cumsum(padded)
    pad_start = (pad_end - padded).astype(I32)
    n_used = (pad_end[-1] // bm).astype(I32)
    pad_row = (pad_start + counts).astype(I32)
    pad_len = (padded - counts).astype(I32)
    ids = jnp.arange(N_EXPERTS, dtype=I32)
    prev_used = lax.cummax(jnp.where(padded > 0, ids, -1))
    first_used = jnp.argmax(padded > 0).astype(I32)
    weight_sel = jnp.where(prev_used >= 0, prev_used, first_used).astype(I32)
    return (pad_start, pad_row, pad_len, pad_start // bm, (padded // bm).astype(I32), weight_sel,
            n_used.reshape(1), n_blocks)


def kernel(x, c, w_ada, b_ada, w_in, conv_w, w_out_conv, w_pool, pool_scale, w_o, ln1_g, ln1_b,
           w_group, b_group, w_router, b_router, w_gate, w_up, w_down, ln2_g, ln2_b):
    bsz, seq, d = x.shape
    n = bsz * seq
    assert d == 2 * PACK_SUBLANES * LANES and seq % MIX_ROWS == 0 and w_ada.shape[0] == DEPTH
    x2d = x.reshape(n, d)
    c_pad = jnp.pad(c, ((0, -bsz % SUBLANES), (0, 0)))
    for l in range(DEPTH):
        mod = _ada(c_pad, w_ada[l], b_ada[l][None, :])[:bsz].reshape(bsz, N_MOD, d)
        wr = jnp.concatenate(
            [w_router[l], w_group[l],
             jnp.zeros((d, LANES - N_EXPERTS - N_GROUPS), F32)], axis=1)
        rb = jnp.concatenate(
            [b_router[l].reshape(-1), b_group[l],
             jnp.zeros((LANES - N_EXPERTS - N_GROUPS,), F32)])[:, None]
        x1, h, loc, ewcol, tab, cnt = _mix(
            x2d, mod, w_in[l].astype(BF16), conv_w[l], w_out_conv[l].astype(BF16),
            w_pool[l].astype(BF16), pool_scale[l][None, :], w_o[l].astype(BF16),
            ln1_g[l][None, :], ln1_b[l][None, :], wr, rb, batch=bsz, seq=seq)
        pad_start, pad_row, pad_len, blk0, nblk, weight_sel, n_used, n_blocks = _block_layout(
            cnt[:, 0], n * TOP_K)
        xs = _dispatch(pad_start, pad_row, pad_len, n_used, tab, loc, h, n_blocks)
        ys = _experts(blk0, nblk, weight_sel, n_used, xs, w_gate[l], w_up[l], w_down[l])
        x2d = _combine(pad_start, tab, x1, mod, ln2_g[l][None, :], ln2_b[l][None, :],
                       ewcol, ys, seq=seq)
    return x2d.reshape(bsz, seq, d)
```

```python
import functools

import jax
import jax.numpy as jnp
from jax import lax
from jax.experimental import pallas as pl
from jax.experimental.pallas import tpu as pltpu

F32 = jnp.float32
BF16 = jnp.bfloat16
I32 = jnp.int32

LANES = 128
SUBLANES = 8
VMEM_LIMIT_BYTES = 56 * 1024 * 1024

N_POOL_GROUPS = 4
POOL_WINDOWS = (2, 4, 8, 16)
CONV_WIDTH = 3
N_GROUPS = 8
EXPERTS_PER_GROUP = 8
N_EXPERTS = N_GROUPS * EXPERTS_PER_GROUP
TOP_K = 2
N_MOD = 6
LN_EPS = 1e-5
DEPTH = 1
DEEPNORM_ALPHA = (2.0 * DEPTH) ** 0.25

MIX_ROWS = 512
EXPERT_ROWS = 256
HALO = SUBLANES
PACK_SUBLANES = 4


def _split_bf16(a):
    hi = a.astype(BF16)
    lo = (a - hi.astype(F32)).astype(BF16)
    return hi, lo


def _dot(a, b):
    return jnp.dot(a, b, preferred_element_type=F32)


def _dot3(a, b):
    a_hi, a_lo = _split_bf16(a)
    b_hi, b_lo = _split_bf16(b)
    return _dot(a_hi, b_hi) + _dot(a_lo, b_hi) + _dot(a_hi, b_lo)


def _dot3_wide(a, b):
    a_hi, a_lo = _split_bf16(a)
    n = b.shape[1]
    b_both = jnp.concatenate(_split_bf16(b), axis=1)
    hi = _dot(a_hi, b_both)
    lo = _dot(a_lo, b_both)
    return hi[:, :n] + hi[:, n:] + lo[:, :n] + lo[:, n:]


def _layer_norm(r, g, b):
    mu = jnp.mean(r, axis=-1, keepdims=True)
    d = r - mu
    var = jnp.mean(d * d, axis=-1, keepdims=True)
    return d * lax.rsqrt(var + LN_EPS) * g + b


def _packed_shape(rows):
    return (rows, PACK_SUBLANES, LANES)


def _plane(ref, s, row0, rows):
    flat = ref.reshape(ref.shape[0] * PACK_SUBLANES, LANES)
    return flat.at[pl.ds(row0 * PACK_SUBLANES + s, rows, stride=PACK_SUBLANES), :]


def _pack_rows(ref, val, row0=0):
    rows, d = val.shape
    bits = lax.bitcast_convert_type(val.astype(BF16).astype(F32), I32)
    for s in range(PACK_SUBLANES):
        hi = bits[:, s * LANES:(s + 1) * LANES]
        lo = bits[:, d // 2 + s * LANES:d // 2 + (s + 1) * LANES]
        _plane(ref, s, row0, rows)[...] = jnp.bitwise_or(hi, lax.shift_right_logical(lo, 16))


def _unpack_rows(ref, row0=0, rows=None):
    rows = ref.shape[0] - row0 if rows is None else rows
    his, los = [], []
    for s in range(PACK_SUBLANES):
        word = _plane(ref, s, row0, rows)[...]
        his.append(lax.bitcast_convert_type(jnp.bitwise_and(word, jnp.int32(-65536)), F32))
        los.append(lax.bitcast_convert_type(lax.shift_left(word, 16), F32))
    return jnp.concatenate(his + los, axis=1).astype(BF16)


def _rows(ref, row, n_rows):
    return ref.at[pl.ds(row, n_rows)]


def _ada_kernel(c_ref, w_ref, b_ref, o_ref):
    cc = c_ref[...]
    act = cc * jax.nn.sigmoid(cc)
    o_ref[...] = _dot3(act, w_ref[...]) + b_ref[...]


def _ada(c_pad, w_ada, b_ada):
    rows, d = c_pad.shape
    n_out = w_ada.shape[1]
    return pl.pallas_call(
        _ada_kernel,
        grid=(n_out // d,),
        in_specs=[
            pl.BlockSpec((rows, d), lambda j: (0, 0)),
            pl.BlockSpec((d, d), lambda j: (0, j)),
            pl.BlockSpec((1, d), lambda j: (0, j)),
        ],
        out_specs=pl.BlockSpec((rows, d), lambda j: (0, j)),
        out_shape=jax.ShapeDtypeStruct((rows, n_out), F32),
        compiler_params=pltpu.CompilerParams(
            dimension_semantics=("arbitrary",), vmem_limit_bytes=VMEM_LIMIT_BYTES),
        name="ada",
    )(c_pad, w_ada, b_ada)


def _mix_kernel(x_ref, mod_ref, win_ref, convw_ref, woc_ref, wpool_ref, pscale_ref, wo_ref,
                ln1g_ref, ln1b_ref, wr_ref, rb_ref, mod_prev_ref,
                x1_ref, h_ref, loc_ref, ewcol_ref, tab_ref, cnt_ref,
                cvbuf, zbuf, s2buf, s4buf, s8buf, base_ref, res_buf, *, tiles_per_seq):
    t_rows, d = x_ref.shape
    pg = d // N_POOL_GROUPS
    step = pl.program_id(0)
    n_tiles = pl.num_programs(0) - 1
    s = jnp.minimum(step, n_tiles - 1) % tiles_per_seq
    shift_bufs = (cvbuf, zbuf, s2buf, s4buf, s8buf)

    @pl.when(step == 0)
    def _():
        base_ref[...] = jnp.zeros_like(base_ref)
        res_buf[...] = jnp.zeros_like(res_buf)

    @pl.when(s == 0)
    def _():
        for buf in shift_bufs:
            buf[0:HALO, :] = jnp.zeros((HALO, buf.shape[1]), F32)

    @pl.when(s > 0)
    def _():
        for buf in shift_bufs:
            buf[0:HALO, :] = buf[t_rows:t_rows + HALO, :]

    def shifted(buf, k):
        return buf[HALO - k:HALO - k + t_rows, :]

    xt = x_ref[...]
    shift1 = mod_ref[0, 0:1, :]
    scale1 = mod_ref[0, 1:2, :]
    gate1 = mod_ref[0, 2:3, :]

    u = (xt * (1.0 + scale1) + shift1).astype(BF16)

    def proj(i):
        return _dot(u, win_ref[:, i * d:(i + 1) * d])

    cv = proj(2) * proj(0)

    x1 = _layer_norm(res_buf[...], ln1g_ref[...], ln1b_ref[...])
    x1_ref[...] = x1
    h = x1 * (1.0 + mod_prev_ref[0, 4:5, :]) + mod_prev_ref[0, 3:4, :]
    h_ref[...] = h.astype(BF16)
    lt = _dot3_wide(h, wr_ref[...]).T + rb_ref[...]
    _route_tile(lt, step > 0, base_ref, loc_ref, ewcol_ref, tab_ref, cnt_ref)

    cvbuf[HALO:HALO + t_rows, :] = cv
    conv = (convw_ref[2:3, :] * cv + convw_ref[1:2, :] * shifted(cvbuf, 1)
            + convw_ref[0:1, :] * shifted(cvbuf, 2))
    y_a = _dot((conv * proj(1)).astype(BF16), woc_ref[...])

    z = proj(3)
    zbuf[HALO:HALO + t_rows, :] = z
    s2 = z + shifted(zbuf, 1)
    s2buf[HALO:HALO + t_rows, :] = s2[:, pg:]
    s4 = s2[:, pg:] + shifted(s2buf, 2)
    s4buf[HALO:HALO + t_rows, :] = s4[:, pg:]
    s8 = s4[:, pg:] + shifted(s4buf, 4)
    s8buf[HALO:HALO + t_rows, :] = s8[:, pg:]
    s16 = s8[:, pg:] + shifted(s8buf, 8)
    win_sums = (s2[:, :pg], s4[:, :pg], s8[:, :pg], s16)
    t_pos = s * t_rows + lax.broadcasted_iota(I32, (t_rows, 1), 0) + 1
    y_b_parts = []
    for gi, w in enumerate(POOL_WINDOWS):
        count = jnp.minimum(t_pos, w).astype(F32)
        pooled = win_sums[gi] / count - z[:, gi * pg:(gi + 1) * pg]
        y_b_parts.append(_dot(pooled.astype(BF16), wpool_ref[gi]))
    y_b = jnp.concatenate(y_b_parts, axis=1) * pscale_ref[...]

    merged = jax.nn.sigmoid(proj(4)) * y_a + jax.nn.sigmoid(proj(5)) * y_b
    mix = _dot(merged.astype(BF16), wo_ref[...])

    res_buf[...] = DEEPNORM_ALPHA * xt + (1.0 + gate1) * mix


def _route_tile(lt, valid, base_ref, loc_ref, ewcol_ref, tab_ref, cnt_ref):
    t_rows = lt.shape[1]
    iota8 = lax.broadcasted_iota(I32, (SUBLANES, t_rows), 0)
    lg = lt[N_EXPERTS:N_EXPERTS + N_GROUPS, :]
    g_max = jnp.max(lg, axis=0, keepdims=True)
    g_top = jnp.min(jnp.where(lg == g_max, iota8, N_GROUPS), axis=0, keepdims=True)
    g_top_p = 1.0 / jnp.sum(jnp.exp(lg - g_max), axis=0, keepdims=True)
    sel = jnp.zeros((EXPERTS_PER_GROUP, t_rows), F32)
    for g in range(N_GROUPS):
        sel = jnp.where(g_top == g, lt[g * EXPERTS_PER_GROUP:(g + 1) * EXPERTS_PER_GROUP, :], sel)
    m1 = jnp.max(sel, axis=0, keepdims=True)
    i1 = jnp.min(jnp.where(sel == m1, iota8, EXPERTS_PER_GROUP), axis=0, keepdims=True)
    sel2 = jnp.where(iota8 == i1, -jnp.inf, sel)
    m2 = jnp.max(sel2, axis=0, keepdims=True)
    i2 = jnp.min(jnp.where(sel2 == m2, iota8, EXPERTS_PER_GROUP), axis=0, keepdims=True)
    ratio = jnp.exp(m2 - m1)
    p1 = 1.0 / (1.0 + ratio)
    w1 = g_top_p * p1
    w2 = g_top_p * (ratio * p1)
    e1 = g_top * EXPERTS_PER_GROUP + i1
    e2 = g_top * EXPERTS_PER_GROUP + i2

    iota_e = lax.broadcasted_iota(I32, (N_EXPERTS, t_rows), 0)
    hit1 = jnp.logical_and(iota_e == e1, valid)
    hit2 = jnp.logical_and(iota_e == e2, valid)
    onehot = jnp.where(jnp.logical_or(hit1, hit2), 1.0, 0.0)
    before = (lax.broadcasted_iota(I32, (t_rows, t_rows), 0)
              < lax.broadcasted_iota(I32, (t_rows, t_rows), 1))
    prefix = _dot(onehot.astype(BF16), jnp.where(before, 1.0, 0.0).astype(BF16))
    tile_cnt = jnp.sum(onehot, axis=1, keepdims=True)
    lower = (lax.broadcasted_iota(I32, (N_EXPERTS, N_EXPERTS), 1)
             < lax.broadcasted_iota(I32, (N_EXPERTS, N_EXPERTS), 0))
    lower = jnp.where(lower, 1.0, 0.0).astype(BF16)
    cnt_hi, cnt_lo = _split_bf16(jnp.broadcast_to(tile_cnt, (N_EXPERTS, LANES)))
    loc_start = (_dot(lower, cnt_hi) + _dot(lower, cnt_lo))[:, 0:1]
    pos = prefix + loc_start
    loc1 = jnp.sum(jnp.where(hit1, pos, 0.0), axis=0, keepdims=True)
    loc2 = jnp.sum(jnp.where(hit2, pos, 0.0), axis=0, keepdims=True)
    base = base_ref[...]
    base_ref[...] = base + tile_cnt
    cnt_ref[...] = jnp.broadcast_to(base + tile_cnt, cnt_ref.shape).astype(I32)

    diag = (lax.broadcasted_iota(I32, (N_EXPERTS, LANES), 0)
            == lax.broadcasted_iota(I32, (N_EXPERTS, LANES), 1))

    def to_lanes(col):
        return jnp.sum(jnp.where(diag, col, 0.0), axis=0, keepdims=True)

    iota8l = lax.broadcasted_iota(I32, (SUBLANES, LANES), 0)
    min_len = jnp.min(jnp.broadcast_to(tile_cnt, (N_EXPERTS, LANES)), axis=0, keepdims=True)
    tab_ref[...] = jnp.where(
        iota8l == TAB_LEN, to_lanes(tile_cnt),
        jnp.where(iota8l == TAB_RANK, to_lanes(base),
                  jnp.where(iota8l == TAB_LOC, to_lanes(loc_start),
                            jnp.where(iota8l == TAB_MIN_LEN, min_len, 0.0)))).astype(I32)

    loc_ref[...] = jnp.where(iota8 == 0, loc1, jnp.where(iota8 == 1, loc2, 0.0)).astype(I32)
    iota_l = lax.broadcasted_iota(I32, (LANES, t_rows), 0)
    ewcol_ref[...] = jnp.where(
        iota_l == 0, w1, jnp.where(iota_l == 1, w2,
                                   jnp.where(iota_l == 2, loc1, jnp.where(iota_l == 3, loc2, 0.0)))).T


def _mix(x2d, mod, w_in, conv_w, w_oc, w_pool, pool_scale, w_o, ln1_g, ln1_b, wr, rb,
         *, batch, seq):
    n, d = x2d.shape
    t = MIX_ROWS
    n_s = seq // t
    n_tiles = batch * n_s
    const = lambda shape: pl.BlockSpec(shape, lambda i: (0,) * len(shape),
                                       pipeline_mode=pl.Buffered(1))
    cur = lambda i: jnp.minimum(i, n_tiles - 1)
    prev = lambda i: jnp.maximum(i - 1, 0)
    pg = d // N_POOL_GROUPS
    return pl.pallas_call(
        functools.partial(_mix_kernel, tiles_per_seq=n_s),
        grid=(n_tiles + 1,),
        in_specs=[
            pl.BlockSpec((t, d), lambda i: (cur(i), 0)),
            pl.BlockSpec((1, N_MOD, d), lambda i: (cur(i) // n_s, 0, 0)),
            const(w_in.shape), const(conv_w.shape), const(w_oc.shape), const(w_pool.shape),
            const(pool_scale.shape), const(w_o.shape), const(ln1_g.shape), const(ln1_b.shape),
            const(wr.shape), const(rb.shape),
            pl.BlockSpec((1, N_MOD, d), lambda i: (prev(i) // n_s, 0, 0)),
        ],
        out_specs=[
            pl.BlockSpec((t, d), lambda i: (prev(i), 0)),
            pl.BlockSpec((t, d), lambda i: (prev(i), 0)),
            pl.BlockSpec((SUBLANES, t), lambda i: (0, prev(i))),
            pl.BlockSpec((t, LANES), lambda i: (prev(i), 0)),
            pl.BlockSpec((SUBLANES, LANES), lambda i: (prev(i), 0)),
            pl.BlockSpec((N_EXPERTS, LANES), lambda i: (0, 0)),
        ],
        out_shape=[
            jax.ShapeDtypeStruct((n, d), F32),
            jax.ShapeDtypeStruct((n, d), BF16),
            jax.ShapeDtypeStruct((SUBLANES, n), I32),
            jax.ShapeDtypeStruct((n, LANES), F32),
            jax.ShapeDtypeStruct((n // t * SUBLANES, LANES), I32),
            jax.ShapeDtypeStruct((N_EXPERTS, LANES), I32),
        ],
        scratch_shapes=[
            pltpu.VMEM((t + HALO, d), F32),
            pltpu.VMEM((t + HALO, d), F32),
            pltpu.VMEM((t + HALO, 3 * pg), F32),
            pltpu.VMEM((t + HALO, 2 * pg), F32),
            pltpu.VMEM((t + HALO, pg), F32),
            pltpu.VMEM((N_EXPERTS, 1), F32),
            pltpu.VMEM((t, d), F32),
        ],
        compiler_params=pltpu.CompilerParams(
            dimension_semantics=("arbitrary",), vmem_limit_bytes=VMEM_LIMIT_BYTES),
        name="mix",
    )(x2d, mod, w_in, conv_w, w_oc, w_pool, pool_scale, w_o, ln1_g, ln1_b, wr, rb, mod)


TAB_LEN, TAB_RANK, TAB_LOC, TAB_MIN_LEN = 0, 1, 2, 3
FILL_SEM = 2


def _strip_copies(tab_ref, pstart_ref, local_ref, sorted_hbm, sem, *, to_hbm):
    def start(e):
        n_rows = tab_ref[TAB_LEN, e]
        local = _rows(local_ref, tab_ref[TAB_LOC, e], n_rows)
        remote = _rows(sorted_hbm, pstart_ref[e] + tab_ref[TAB_RANK, e], n_rows)
        src, dst = (local, remote) if to_hbm else (remote, local)
        pltpu.make_async_copy(src, dst, sem).start()

    @pl.when(tab_ref[TAB_MIN_LEN, 0] > 0)
    def _():
        for e in range(N_EXPERTS):
            start(e)

    @pl.when(tab_ref[TAB_MIN_LEN, 0] <= 0)
    def _():
        def strip(e, carry):
            @pl.when(tab_ref[TAB_LEN, e] > 0)
            def _():
                start(e)
            return carry

        lax.fori_loop(0, N_EXPERTS, strip, 0)


def _dispatch_kernel(pstart_ref, padrow_ref, padlen_ref, nused_ref, tab_ref, loc_ref, h_ref, xs_hbm,
                     xloc0, xloc1, zero_blk, sem):
    i = pl.program_id(0)
    n_steps = pl.num_programs(0)
    t_rows = h_ref.shape[0]
    n_loc = TOP_K * t_rows
    blk_tok = zero_blk.shape[0]
    n_blocks = xs_hbm.shape[0] // blk_tok
    xlocs = (xloc0, xloc1)

    def zero_fills(act):
        def pad_rows(e, carry):
            @pl.when(padlen_ref[e] > 0)
            def _():
                act(pltpu.make_async_copy(
                    _rows(zero_blk, 0, padlen_ref[e]),
                    _rows(xs_hbm, padrow_ref[e], padlen_ref[e]), sem.at[FILL_SEM]))
            return carry

        def unused_block(b, carry):
            act(pltpu.make_async_copy(
                zero_blk, _rows(xs_hbm, b * blk_tok, blk_tok), sem.at[FILL_SEM]))
            return carry

        lax.fori_loop(0, N_EXPERTS, pad_rows, 0)
        lax.fori_loop(nused_ref[0], n_blocks, unused_block, 0)

    @pl.when(i == 0)
    def _():
        zero_blk[...] = jnp.zeros_like(zero_blk)
        zero_fills(lambda copy: copy.start())

    def strips_done(slot):
        pltpu.make_async_copy(xlocs[slot], _rows(xs_hbm, 0, n_loc), sem.at[slot]).wait()

    for slot in range(2):
        @pl.when(i % 2 == slot)
        def _():
            @pl.when(i >= 2)
            def _():
                strips_done(slot)

            iota_p = lax.broadcasted_iota(I32, (n_loc, t_rows), 0)
            pick = jnp.logical_or(iota_p == loc_ref[0:1, :], iota_p == loc_ref[1:2, :])
            xsorted = _dot(jnp.where(pick, 1.0, 0.0).astype(BF16), h_ref[...])
            _pack_rows(xlocs[slot], xsorted)
            _strip_copies(tab_ref, pstart_ref, xlocs[slot], xs_hbm, sem.at[slot], to_hbm=True)

    @pl.when(i == n_steps - 1)
    def _():
        for slot in range(2):
            @pl.when(n_steps > slot)
            def _():
                strips_done(slot)
        zero_fills(lambda copy: copy.wait())


def _dispatch(pad_start, pad_row, pad_len, n_used, tab, loc, h, n_blocks):
    n, d = h.shape
    t = MIX_ROWS
    grid_spec = pltpu.PrefetchScalarGridSpec(
        num_scalar_prefetch=4,
        grid=(n // t,),
        in_specs=[
            pl.BlockSpec((SUBLANES, LANES), lambda i, *_: (i, 0), memory_space=pltpu.SMEM),
            pl.BlockSpec((SUBLANES, t), lambda i, *_: (0, i)),
            pl.BlockSpec((t, d), lambda i, *_: (i, 0)),
        ],
        out_specs=pl.BlockSpec(memory_space=pl.ANY),
        scratch_shapes=[
            pltpu.VMEM(_packed_shape(TOP_K * t), I32),
            pltpu.VMEM(_packed_shape(TOP_K * t), I32),
            pltpu.VMEM(_packed_shape(EXPERT_ROWS), I32),
            pltpu.SemaphoreType.DMA((3,)),
        ],
    )
    return pl.pallas_call(
        _dispatch_kernel,
        grid_spec=grid_spec,
        out_shape=jax.ShapeDtypeStruct(_packed_shape(n_blocks * EXPERT_ROWS), I32),
        compiler_params=pltpu.CompilerParams(
            dimension_semantics=("arbitrary",), vmem_limit_bytes=VMEM_LIMIT_BYTES,
            has_side_effects=True),
        name="dispatch",
    )(pad_start, pad_row, pad_len, n_used, tab, loc, h)


IN_SLOTS = 6
AHEAD = 4
OUT_SLOTS = 4


def _expert_kernel(blk0_ref, nblk_ref, wsel_ref, nused_ref, xs_hbm, wg_ref, wu_ref, wd_ref, ys_hbm,
                   xring, yring, wg_bf, wu_bf, wd_bf, sem_in, sem_out):
    del wsel_ref
    e = pl.program_id(0)
    blk_tok = xring.shape[0] // IN_SLOTS
    n_used = nused_ref[0]
    n_blk = nblk_ref[e]

    def in_copy(g):
        slot = g % IN_SLOTS
        return pltpu.make_async_copy(
            _rows(xs_hbm, g * blk_tok, blk_tok), _rows(xring, slot * blk_tok, blk_tok),
            sem_in.at[slot])

    def out_copy(g):
        slot = g % OUT_SLOTS
        return pltpu.make_async_copy(
            _rows(yring, slot * blk_tok, blk_tok), _rows(ys_hbm, g * blk_tok, blk_tok),
            sem_out.at[slot])

    def request(g):
        @pl.when(g < n_used)
        def _():
            in_copy(g).start()

    def retire(g):
        @pl.when(g >= OUT_SLOTS)
        def _():
            out_copy(g - OUT_SLOTS).wait()

    def mlp(g):
        xb = _unpack_rows(xring, (g % IN_SLOTS) * blk_tok, blk_tok)
        gate = _dot(xb, wg_bf[...])
        up = _dot(xb, wu_bf[...])
        act = (gate * jax.nn.sigmoid(gate) * up).astype(BF16)
        _pack_rows(yring, _dot(act, wd_bf[...]), (g % OUT_SLOTS) * blk_tok)

    @pl.when(e == 0)
    def _():
        for g in range(AHEAD):
            request(g)

    @pl.when(n_blk > 0)
    def _():
        wg_bf[...] = wg_ref[0].astype(BF16)
        wu_bf[...] = wu_ref[0].astype(BF16)
        wd_bf[...] = wd_ref[0].astype(BF16)

    def pair(p, carry):
        g = blk0_ref[e] + 2 * p
        in_copy(g).wait()
        in_copy(g + 1).wait()
        request(g + AHEAD)
        request(g + AHEAD + 1)
        retire(g)
        retire(g + 1)
        mlp(g)
        mlp(g + 1)
        out_copy(g).start()
        out_copy(g + 1).start()
        return carry

    lax.fori_loop(0, n_blk // 2, pair, 0)

    @pl.when(n_blk % 2 == 1)
    def _():
        g = blk0_ref[e] + n_blk - 1
        in_copy(g).wait()
        request(g + AHEAD)
        retire(g)
        mlp(g)
        out_copy(g).start()

    @pl.when(e == pl.num_programs(0) - 1)
    def _():
        for k in range(1, OUT_SLOTS + 1):
            @pl.when(n_used >= k)
            def _():
                out_copy(n_used - k).wait()


def _experts(blk0, nblk, wsel, n_used, xs, w_gate, w_up, w_down):
    bm = EXPERT_ROWS
    n_exp, d, de = w_gate.shape
    wblk = lambda e, b0, nb, ws, nu: (ws[e], 0, 0)
    grid_spec = pltpu.PrefetchScalarGridSpec(
        num_scalar_prefetch=4,
        grid=(n_exp,),
        in_specs=[
            pl.BlockSpec(memory_space=pl.ANY),
            pl.BlockSpec((1, d, de), wblk),
            pl.BlockSpec((1, d, de), wblk),
            pl.BlockSpec((1, de, d), wblk),
        ],
        out_specs=pl.BlockSpec(memory_space=pl.ANY),
        scratch_shapes=[
            pltpu.VMEM(_packed_shape(IN_SLOTS * bm), I32),
            pltpu.VMEM(_packed_shape(OUT_SLOTS * bm), I32),
            pltpu.VMEM((d, de), BF16),
            pltpu.VMEM((d, de), BF16),
            pltpu.VMEM((de, d), BF16),
            pltpu.SemaphoreType.DMA((IN_SLOTS,)),
            pltpu.SemaphoreType.DMA((OUT_SLOTS,)),
        ],
    )
    n_prefetch = 4
    return pl.pallas_call(
        _expert_kernel,
        grid_spec=grid_spec,
        out_shape=jax.ShapeDtypeStruct(xs.shape, xs.dtype),
        input_output_aliases={n_prefetch: 0},
        compiler_params=pltpu.CompilerParams(
            dimension_semantics=("arbitrary",), vmem_limit_bytes=VMEM_LIMIT_BYTES,
            has_side_effects=True),
        name="experts",
    )(blk0, nblk, wsel, n_used, xs, w_gate, w_up, w_down)


def _combine_kernel(pstart_ref, tab_ref, tab_next_ref, x1_ref, mod_ref, ln2g_ref, ln2b_ref,
                    ewcol_ref, ys_hbm, out_ref, ybuf0, ybuf1, sem):
    i = pl.program_id(0)
    t_rows = x1_ref.shape[0]
    n_loc = TOP_K * t_rows
    ybufs = (ybuf0, ybuf1)

    @pl.when(i == 0)
    def _():
        _strip_copies(tab_ref, pstart_ref, ybuf0, ys_hbm, sem.at[0], to_hbm=False)

    for slot in range(2):
        @pl.when(i % 2 == slot)
        def _():
            @pl.when(i + 1 < pl.num_programs(0))
            def _():
                _strip_copies(tab_next_ref, pstart_ref, ybufs[1 - slot], ys_hbm,
                              sem.at[1 - slot], to_hbm=False)

            pltpu.make_async_copy(_rows(ys_hbm, 0, n_loc), ybufs[slot], sem.at[slot]).wait()
            ew = ewcol_ref[...]
            lane = lax.broadcasted_iota(I32, (t_rows, n_loc), 1).astype(F32)
            weights = (jnp.where(lane == ew[:, 2:3], ew[:, 0:1], 0.0)
                       + jnp.where(lane == ew[:, 3:4], ew[:, 1:2], 0.0))
            y = _dot(weights.astype(BF16), _unpack_rows(ybufs[slot]))
            gate2 = mod_ref[0, 5:6, :]
            out_ref[...] = _layer_norm(DEEPNORM_ALPHA * x1_ref[...] + (1.0 + gate2) * y,
                                       ln2g_ref[...], ln2b_ref[...])


def _combine(pad_start, tab, x1, mod, ln2_g, ln2_b, ewcol, ys, *, seq):
    n, d = x1.shape
    t = MIX_ROWS
    per_seq = seq // t
    n_steps = n // t
    row_blk = lambda i, ps: (i, 0)
    grid_spec = pltpu.PrefetchScalarGridSpec(
        num_scalar_prefetch=1,
        grid=(n_steps,),
        in_specs=[
            pl.BlockSpec((SUBLANES, LANES), row_blk, memory_space=pltpu.SMEM),
            pl.BlockSpec((SUBLANES, LANES), lambda i, ps: (jnp.minimum(i + 1, n_steps - 1), 0),
                         memory_space=pltpu.SMEM),
            pl.BlockSpec((t, d), row_blk),
            pl.BlockSpec((1, N_MOD, d), lambda i, ps: (i // per_seq, 0, 0)),
            pl.BlockSpec((1, d), lambda i, ps: (0, 0)),
            pl.BlockSpec((1, d), lambda i, ps: (0, 0)),
            pl.BlockSpec((t, LANES), row_blk),
            pl.BlockSpec(memory_space=pl.ANY),
        ],
        out_specs=pl.BlockSpec((t, d), row_blk),
        scratch_shapes=[
            pltpu.VMEM(_packed_shape(TOP_K * t), I32),
            pltpu.VMEM(_packed_shape(TOP_K * t), I32),
            pltpu.SemaphoreType.DMA((2,)),
        ],
    )
    return pl.pallas_call(
        _combine_kernel,
        grid_spec=grid_spec,
        out_shape=jax.ShapeDtypeStruct((n, d), F32),
        compiler_params=pltpu.CompilerParams(
            dimension_semantics=("arbitrary",), vmem_limit_bytes=VMEM_LIMIT_BYTES),
        name="combine",
    )(pad_start, tab, tab, x1, mod, ln2_g, ln2_b, ewcol, ys)


def _block_layout(counts, n_assign):
    bm = EXPERT_ROWS
    n_blocks = -(-n_assign // bm) + N_EXPERTS
    padded = (counts + bm - 1) // bm * bm
    pad_end = jnp.cumsum(padded)
    pad_start = (pad_end - padded).astype(I32)
    n_used = (pad_end[-1] // bm).astype(I32)
    pad_row = (pad_start + counts).astype(I32)
    pad_len = (padded - counts).astype(I32)
    ids = jnp.arange(N_EXPERTS, dtype=I32)
    prev_used = lax.cummax(jnp.where(padded > 0, ids, -1))
    first_used = jnp.argmax(padded > 0).astype(I32)
    weight_sel = jnp.where(prev_used >= 0, prev_used, first_used).astype(I32)
    return (pad_start, pad_row, pad_len, pad_start // bm, (padded // bm).astype(I32), weight_sel,
            n_used.reshape(1), n_blocks)


def kernel(x, c, w_ada, b_ada, w_in, conv_w, w_out_conv, w_pool, pool_scale, w_o, ln1_g, ln1_b,
           w_group, b_group, w_router, b_router, w_gate, w_up, w_down, ln2_g, ln2_b):
    bsz, seq, d = x.shape
    n = bsz * seq
    assert d == 2 * PACK_SUBLANES * LANES and seq % MIX_ROWS == 0 and w_ada.shape[0] == DEPTH
    x2d = x.reshape(n, d)
    c_pad = jnp.pad(c, ((0, -bsz % SUBLANES), (0, 0)))
    for l in range(DEPTH):
        mod = _ada(c_pad, w_ada[l], b_ada[l][None, :])[:bsz].reshape(bsz, N_MOD, d)
        wr = jnp.concatenate(
            [w_router[l], w_group[l],
             jnp.zeros((d, LANES - N_EXPERTS - N_GROUPS), F32)], axis=1)
        rb = jnp.concatenate(
            [b_router[l].reshape(-1), b_group[l],
             jnp.zeros((LANES - N_EXPERTS - N_GROUPS,), F32)])[:, None]
        x1, h, loc, ewcol, tab, cnt = _mix(
            x2d, mod, w_in[l].astype(BF16), conv_w[l], w_out_conv[l].astype(BF16),
            w_pool[l].astype(BF16), pool_scale[l][None, :], w_o[l].astype(BF16),
            ln1_g[l][None, :], ln1_b[l][None, :], wr, rb, batch=bsz, seq=seq)
        pad_start, pad_row, pad_len, blk0, nblk, weight_sel, n_used, n_blocks = _block_layout(
            cnt[:, 0], n * TOP_K)
        xs = _dispatch(pad_start, pad_row, pad_len, n_used, tab, loc, h, n_blocks)
        ys = _experts(blk0, nblk, weight_sel, n_used, xs, w_gate[l], w_up[l], w_down[l])
        x2d = _combine(pad_start, tab, x1, mod, ln2_g[l][None, :], ln2_b[l][None, :],
                       ewcol, ys, seq=seq)
    return x2d.reshape(bsz, seq, d)
```

```python
import functools

import jax
import jax.numpy as jnp
from jax import lax
from jax.experimental import pallas as pl
from jax.experimental.pallas import tpu as pltpu

F32 = jnp.float32
BF16 = jnp.bfloat16
I32 = jnp.int32

LANES = 128
SUBLANES = 8
VMEM_LIMIT_BYTES = 56 * 1024 * 1024

N_POOL_GROUPS = 4
POOL_WINDOWS = (2, 4, 8, 16)
CONV_WIDTH = 3
N_GROUPS = 8
EXPERTS_PER_GROUP = 8
N_EXPERTS = N_GROUPS * EXPERTS_PER_GROUP
TOP_K = 2
N_MOD = 6
LN_EPS = 1e-5
DEPTH = 1
DEEPNORM_ALPHA = (2.0 * DEPTH) ** 0.25

MIX_ROWS = 512
EXPERT_ROWS = 256
HALO = SUBLANES
PACK_SUBLANES = 4


def _split_bf16(a):
    hi = a.astype(BF16)
    lo = (a - hi.astype(F32)).astype(BF16)
    return hi, lo


def _dot(a, b):
    return jnp.dot(a, b, preferred_element_type=F32)


def _dot3(a, b):
    a_hi, a_lo = _split_bf16(a)
    b_hi, b_lo = _split_bf16(b)
    return _dot(a_hi, b_hi) + _dot(a_lo, b_hi) + _dot(a_hi, b_lo)


def _dot3_wide(a, b):
    a_hi, a_lo = _split_bf16(a)
    n = b.shape[1]
    b_both = jnp.concatenate(_split_bf16(b), axis=1)
    hi = _dot(a_hi, b_both)
    lo = _dot(a_lo, b_both)
    return hi[:, :n] + hi[:, n:] + lo[:, :n] + lo[:, n:]


def _layer_norm(r, g, b):
    mu = jnp.mean(r, axis=-1, keepdims=True)
    d = r - mu
    var = jnp.mean(d * d, axis=-1, keepdims=True)
    return d * lax.rsqrt(var + LN_EPS) * g + b


def _packed_shape(rows):
    return (rows, PACK_SUBLANES, LANES)


def _plane(ref, s, row0, rows):
    flat = ref.reshape(ref.shape[0] * PACK_SUBLANES, LANES)
    return flat.at[pl.ds(row0 * PACK_SUBLANES + s, rows, stride=PACK_SUBLANES), :]


def _pack_rows(ref, val, row0=0):
    rows, d = val.shape
    bits = lax.bitcast_convert_type(val.astype(BF16).astype(F32), I32)
    for s in range(PACK_SUBLANES):
        hi = bits[:, s * LANES:(s + 1) * LANES]
        lo = bits[:, d // 2 + s * LANES:d // 2 + (s + 1) * LANES]
        _plane(ref, s, row0, rows)[...] = jnp.bitwise_or(hi, lax.shift_right_logical(lo, 16))


def _unpack_rows(ref, row0=0, rows=None):
    rows = ref.shape[0] - row0 if rows is None else rows
    his, los = [], []
    for s in range(PACK_SUBLANES):
        word = _plane(ref, s, row0, rows)[...]
        his.append(lax.bitcast_convert_type(jnp.bitwise_and(word, jnp.int32(-65536)), F32))
        los.append(lax.bitcast_convert_type(lax.shift_left(word, 16), F32))
    return jnp.concatenate(his + los, axis=1).astype(BF16)


def _rows(ref, row, n_rows):
    return ref.at[pl.ds(row, n_rows)]


def _ada_kernel(c_ref, w_ref, b_ref, o_ref):
    cc = c_ref[...]
    act = cc * jax.nn.sigmoid(cc)
    o_ref[...] = _dot3(act, w_ref[...]) + b_ref[...]


def _ada(c_pad, w_ada, b_ada):
    rows, d = c_pad.shape
    n_out = w_ada.shape[1]
    return pl.pallas_call(
        _ada_kernel,
        grid=(n_out // d,),
        in_specs=[
            pl.BlockSpec((rows, d), lambda j: (0, 0)),
            pl.BlockSpec((d, d), lambda j: (0, j)),
            pl.BlockSpec((1, d), lambda j: (0, j)),
        ],
        out_specs=pl.BlockSpec((rows, d), lambda j: (0, j)),
        out_shape=jax.ShapeDtypeStruct((rows, n_out), F32),
        compiler_params=pltpu.CompilerParams(
            dimension_semantics=("arbitrary",), vmem_limit_bytes=VMEM_LIMIT_BYTES),
        name="ada",
    )(c_pad, w_ada, b_ada)


def _mix_kernel(x_ref, mod_ref, win_ref, convw_ref, woc_ref, wpool_ref, pscale_ref, wo_ref,
                ln1g_ref, ln1b_ref, wr_ref, rb_ref, mod_prev_ref,
                x1_ref, h_ref, loc_ref, ewcol_ref, tab_ref, cnt_ref,
                cvbuf, zbuf, s2buf, s4buf, s8buf, base_ref, res_buf, *, tiles_per_seq):
    t_rows, d = x_ref.shape
    pg = d // N_POOL_GROUPS
    step = pl.program_id(0)
    n_tiles = pl.num_programs(0) - 1
    s = jnp.minimum(step, n_tiles - 1) % tiles_per_seq
    shift_bufs = (cvbuf, zbuf, s2buf, s4buf, s8buf)

    @pl.when(step == 0)
    def _():
        base_ref[...] = jnp.zeros_like(base_ref)
        res_buf[...] = jnp.zeros_like(res_buf)

    @pl.when(s == 0)
    def _():
        for buf in shift_bufs:
            buf[0:HALO, :] = jnp.zeros((HALO, buf.shape[1]), F32)

    @pl.when(s > 0)
    def _():
        for buf in shift_bufs:
            buf[0:HALO, :] = buf[t_rows:t_rows + HALO, :]

    def shifted(buf, k):
        return buf[HALO - k:HALO - k + t_rows, :]

    xt = x_ref[...]
    shift1 = mod_ref[0, 0:1, :]
    scale1 = mod_ref[0, 1:2, :]
    gate1 = mod_ref[0, 2:3, :]

    u = (xt * (1.0 + scale1) + shift1).astype(BF16)

    def proj(i):
        return _dot(u, win_ref[:, i * d:(i + 1) * d])

    cv = proj(2) * proj(0)

    x1 = _layer_norm(res_buf[...], ln1g_ref[...], ln1b_ref[...])
    x1_ref[...] = x1
    h = x1 * (1.0 + mod_prev_ref[0, 4:5, :]) + mod_prev_ref[0, 3:4, :]
    h_ref[...] = h.astype(BF16)
    lt = _dot3_wide(h, wr_ref[...]).T + rb_ref[...]
    _route_tile(lt, step > 0, base_ref, loc_ref, ewcol_ref, tab_ref, cnt_ref)

    cvbuf[HALO:HALO + t_rows, :] = cv
    conv = (convw_ref[2:3, :] * cv + convw_ref[1:2, :] * shifted(cvbuf, 1)
            + convw_ref[0:1, :] * shifted(cvbuf, 2))
    y_a = _dot((conv * proj(1)).astype(BF16), woc_ref[...])

    z = proj(3)
    zbuf[HALO:HALO + t_rows, :] = z
    s2 = z + shifted(zbuf, 1)
    s2buf[HALO:HALO + t_rows, :] = s2[:, pg:]
    s4 = s2[:, pg:] + shifted(s2buf, 2)
    s4buf[HALO:HALO + t_rows, :] = s4[:, pg:]
    s8 = s4[:, pg:] + shifted(s4buf, 4)
    s8buf[HALO:HALO + t_rows, :] = s8[:, pg:]
    s16 = s8[:, pg:] + shifted(s8buf, 8)
    win_sums = (s2[:, :pg], s4[:, :pg], s8[:, :pg], s16)
    t_pos = s * t_rows + lax.broadcasted_iota(I32, (t_rows, 1), 0) + 1
    y_b_parts = []
    for gi, w in enumerate(POOL_WINDOWS):
        count = jnp.minimum(t_pos, w).astype(F32)
        pooled = win_sums[gi] / count - z[:, gi * pg:(gi + 1) * pg]
        y_b_parts.append(_dot(pooled.astype(BF16), wpool_ref[gi]))
    y_b = jnp.concatenate(y_b_parts, axis=1) * pscale_ref[...]

    merged = jax.nn.sigmoid(proj(4)) * y_a + jax.nn.sigmoid(proj(5)) * y_b
    mix = _dot(merged.astype(BF16), wo_ref[...])

    res_buf[...] = DEEPNORM_ALPHA * xt + (1.0 + gate1) * mix


def _route_tile(lt, valid, base_ref, loc_ref, ewcol_ref, tab_ref, cnt_ref):
    t_rows = lt.shape[1]
    iota8 = lax.broadcasted_iota(I32, (SUBLANES, t_rows), 0)
    lg = lt[N_EXPERTS:N_EXPERTS + N_GROUPS, :]
    g_max = jnp.max(lg, axis=0, keepdims=True)
    g_top = jnp.min(jnp.where(lg == g_max, iota8, N_GROUPS), axis=0, keepdims=True)
    g_top_p = 1.0 / jnp.sum(jnp.exp(lg - g_max), axis=0, keepdims=True)
    sel = jnp.zeros((EXPERTS_PER_GROUP, t_rows), F32)
    for g in range(N_GROUPS):
        sel = jnp.where(g_top == g, lt[g * EXPERTS_PER_GROUP:(g + 1) * EXPERTS_PER_GROUP, :], sel)
    m1 = jnp.max(sel, axis=0, keepdims=True)
    i1 = jnp.min(jnp.where(sel == m1, iota8, EXPERTS_PER_GROUP), axis=0, keepdims=True)
    sel2 = jnp.where(iota8 == i1, -jnp.inf, sel)
    m2 = jnp.max(sel2, axis=0, keepdims=True)
    i2 = jnp.min(jnp.where(sel2 == m2, iota8, EXPERTS_PER_GROUP), axis=0, keepdims=True)
    ratio = jnp.exp(m2 - m1)
    p1 = 1.0 / (1.0 + ratio)
    w1 = g_top_p * p1
    w2 = g_top_p * (ratio * p1)
    e1 = g_top * EXPERTS_PER_GROUP + i1
    e2 = g_top * EXPERTS_PER_GROUP + i2

    iota_e = lax.broadcasted_iota(I32, (N_EXPERTS, t_rows), 0)
    hit1 = jnp.logical_and(iota_e == e1, valid)
    hit2 = jnp.logical_and(iota_e == e2, valid)
    onehot = jnp.where(jnp.logical_or(hit1, hit2), 1.0, 0.0)
    before = (lax.broadcasted_iota(I32, (t_rows, t_rows), 0)
              < lax.broadcasted_iota(I32, (t_rows, t_rows), 1))
    prefix = _dot(onehot.astype(BF16), jnp.where(before, 1.0, 0.0).astype(BF16))
    tile_cnt = jnp.sum(onehot, axis=1, keepdims=True)
    lower = (lax.broadcasted_iota(I32, (N_EXPERTS, N_EXPERTS), 1)
             < lax.broadcasted_iota(I32, (N_EXPERTS, N_EXPERTS), 0))
    lower = jnp.where(lower, 1.0, 0.0).astype(BF16)
    cnt_hi, cnt_lo = _split_bf16(jnp.broadcast_to(tile_cnt, (N_EXPERTS, LANES)))
    loc_start = (_dot(lower, cnt_hi) + _dot(lower, cnt_lo))[:, 0:1]
    pos = prefix + loc_start
    loc1 = jnp.sum(jnp.where(hit1, pos, 0.0), axis=0, keepdims=True)
    loc2 = jnp.sum(jnp.where(hit2, pos, 0.0), axis=0, keepdims=True)
    base = base_ref[...]
    base_ref[...] = base + tile_cnt
    cnt_ref[...] = jnp.broadcast_to(base + tile_cnt, cnt_ref.shape).astype(I32)

    diag = (lax.broadcasted_iota(I32, (N_EXPERTS, LANES), 0)
            == lax.broadcasted_iota(I32, (N_EXPERTS, LANES), 1))

    def to_lanes(col):
        return jnp.sum(jnp.where(diag, col, 0.0), axis=0, keepdims=True)

    iota8l = lax.broadcasted_iota(I32, (SUBLANES, LANES), 0)
    min_len = jnp.min(jnp.broadcast_to(tile_cnt, (N_EXPERTS, LANES)), axis=0, keepdims=True)
    tab_ref[...] = jnp.where(
        iota8l == TAB_LEN, to_lanes(tile_cnt),
        jnp.where(iota8l == TAB_RANK, to_lanes(base),
                  jnp.where(iota8l == TAB_LOC, to_lanes(loc_start),
                            jnp.where(iota8l == TAB_MIN_LEN, min_len, 0.0)))).astype(I32)

    loc_ref[...] = jnp.where(iota8 == 0, loc1, jnp.where(iota8 == 1, loc2, 0.0)).astype(I32)
    iota_l = lax.broadcasted_iota(I32, (LANES, t_rows), 0)
    ewcol_ref[...] = jnp.where(
        iota_l == 0, w1, jnp.where(iota_l == 1, w2,
                                   jnp.where(iota_l == 2, loc1, jnp.where(iota_l == 3, loc2, 0.0)))).T


def _mix(x2d, mod, w_in, conv_w, w_oc, w_pool, pool_scale, w_o, ln1_g, ln1_b, wr, rb,
         *, batch, seq):
    n, d = x2d.shape
    t = MIX_ROWS
    n_s = seq // t
    n_tiles = batch * n_s
    const = lambda shape: pl.BlockSpec(shape, lambda i: (0,) * len(shape),
                                       pipeline_mode=pl.Buffered(1))
    cur = lambda i: jnp.minimum(i, n_tiles - 1)
    prev = lambda i: jnp.maximum(i - 1, 0)
    pg = d // N_POOL_GROUPS
    return pl.pallas_call(
        functools.partial(_mix_kernel, tiles_per_seq=n_s),
        grid=(n_tiles + 1,),
        in_specs=[
            pl.BlockSpec((t, d), lambda i: (cur(i), 0)),
            pl.BlockSpec((1, N_MOD, d), lambda i: (cur(i) // n_s, 0, 0)),
            const(w_in.shape), const(conv_w.shape), const(w_oc.shape), const(w_pool.shape),
            const(pool_scale.shape), const(w_o.shape), const(ln1_g.shape), const(ln1_b.shape),
            const(wr.shape), const(rb.shape),
            pl.BlockSpec((1, N_MOD, d), lambda i: (prev(i) // n_s, 0, 0)),
        ],
        out_specs=[
            pl.BlockSpec((t, d), lambda i: (prev(i), 0)),
            pl.BlockSpec((t, d), lambda i: (prev(i), 0)),
            pl.BlockSpec((SUBLANES, t), lambda i: (0, prev(i))),
            pl.BlockSpec((t, LANES), lambda i: (prev(i), 0)),
            pl.BlockSpec((SUBLANES, LANES), lambda i: (prev(i), 0)),
            pl.BlockSpec((N_EXPERTS, LANES), lambda i: (0, 0)),
        ],
        out_shape=[
            jax.ShapeDtypeStruct((n, d), F32),
            jax.ShapeDtypeStruct((n, d), BF16),
            jax.ShapeDtypeStruct((SUBLANES, n), I32),
            jax.ShapeDtypeStruct((n, LANES), F32),
            jax.ShapeDtypeStruct((n // t * SUBLANES, LANES), I32),
            jax.ShapeDtypeStruct((N_EXPERTS, LANES), I32),
        ],
        scratch_shapes=[
            pltpu.VMEM((t + HALO, d), F32),
            pltpu.VMEM((t + HALO, d), F32),
            pltpu.VMEM((t + HALO, 3 * pg), F32),
            pltpu.VMEM((t + HALO, 2 * pg), F32),
            pltpu.VMEM((t + HALO, pg), F32),
            pltpu.VMEM((N_EXPERTS, 1), F32),
            pltpu.VMEM((t, d), F32),
        ],
        compiler_params=pltpu.CompilerParams(
            dimension_semantics=("arbitrary",), vmem_limit_bytes=VMEM_LIMIT_BYTES),
        name="mix",
    )(x2d, mod, w_in, conv_w, w_oc, w_pool, pool_scale, w_o, ln1_g, ln1_b, wr, rb, mod)


TAB_LEN, TAB_RANK, TAB_LOC, TAB_MIN_LEN = 0, 1, 2, 3
FILL_SEM = 2


def _strip_copies(tab_ref, pstart_ref, local_ref, sorted_hbm, sem, *, to_hbm):
    def start(e):
        n_rows = tab_ref[TAB_LEN, e]
        local = _rows(local_ref, tab_ref[TAB_LOC, e], n_rows)
        remote = _rows(sorted_hbm, pstart_ref[e] + tab_ref[TAB_RANK, e], n_rows)
        src, dst = (local, remote) if to_hbm else (remote, local)
        pltpu.make_async_copy(src, dst, sem).start()

    @pl.when(tab_ref[TAB_MIN_LEN, 0] > 0)
    def _():
        for e in range(N_EXPERTS):
            start(e)

    @pl.when(tab_ref[TAB_MIN_LEN, 0] <= 0)
    def _():
        def strip(e, carry):
            @pl.when(tab_ref[TAB_LEN, e] > 0)
            def _():
                start(e)
            return carry

        lax.fori_loop(0, N_EXPERTS, strip, 0)


def _dispatch_kernel(pstart_ref, padrow_ref, padlen_ref, nused_ref, tab_ref, loc_ref, h_ref, xs_hbm,
                     xloc0, xloc1, zero_blk, sem):
    i = pl.program_id(0)
    n_steps = pl.num_programs(0)
    t_rows = h_ref.shape[0]
    n_loc = TOP_K * t_rows
    blk_tok = zero_blk.shape[0]
    n_blocks = xs_hbm.shape[0] // blk_tok
    xlocs = (xloc0, xloc1)

    def zero_fills(act):
        def pad_rows(e, carry):
            @pl.when(padlen_ref[e] > 0)
            def _():
                act(pltpu.make_async_copy(
                    _rows(zero_blk, 0, padlen_ref[e]),
                    _rows(xs_hbm, padrow_ref[e], padlen_ref[e]), sem.at[FILL_SEM]))
            return carry

        def unused_block(b, carry):
            act(pltpu.make_async_copy(
                zero_blk, _rows(xs_hbm, b * blk_tok, blk_tok), sem.at[FILL_SEM]))
            return carry

        lax.fori_loop(0, N_EXPERTS, pad_rows, 0)
        lax.fori_loop(nused_ref[0], n_blocks, unused_block, 0)

    @pl.when(i == 0)
    def _():
        zero_blk[...] = jnp.zeros_like(zero_blk)
        zero_fills(lambda copy: copy.start())

    def strips_done(slot):
        pltpu.make_async_copy(xlocs[slot], _rows(xs_hbm, 0, n_loc), sem.at[slot]).wait()

    for slot in range(2):
        @pl.when(i % 2 == slot)
        def _():
            @pl.when(i >= 2)
            def _():
                strips_done(slot)

            iota_p = lax.broadcasted_iota(I32, (n_loc, t_rows), 0)
            pick = jnp.logical_or(iota_p == loc_ref[0:1, :], iota_p == loc_ref[1:2, :])
            xsorted = _dot(jnp.where(pick, 1.0, 0.0).astype(BF16), h_ref[...])
            _pack_rows(xlocs[slot], xsorted)
            _strip_copies(tab_ref, pstart_ref, xlocs[slot], xs_hbm, sem.at[slot], to_hbm=True)

    @pl.when(i == n_steps - 1)
    def _():
        for slot in range(2):
            @pl.when(n_steps > slot)
            def _():
                strips_done(slot)
        zero_fills(lambda copy: copy.wait())


def _dispatch(pad_start, pad_row, pad_len, n_used, tab, loc, h, n_blocks):
    n, d = h.shape
    t = MIX_ROWS
    grid_spec = pltpu.PrefetchScalarGridSpec(
        num_scalar_prefetch=4,
        grid=(n // t,),
        in_specs=[
            pl.BlockSpec((SUBLANES, LANES), lambda i, *_: (i, 0), memory_space=pltpu.SMEM),
            pl.BlockSpec((SUBLANES, t), lambda i, *_: (0, i)),
            pl.BlockSpec((t, d), lambda i, *_: (i, 0)),
        ],
        out_specs=pl.BlockSpec(memory_space=pl.ANY),
        scratch_shapes=[
            pltpu.VMEM(_packed_shape(TOP_K * t), I32),
            pltpu.VMEM(_packed_shape(TOP_K * t), I32),
            pltpu.VMEM(_packed_shape(EXPERT_ROWS), I32),
            pltpu.SemaphoreType.DMA((3,)),
        ],
    )
    return pl.pallas_call(
        _dispatch_kernel,
        grid_spec=grid_spec,
        out_shape=jax.ShapeDtypeStruct(_packed_shape(n_blocks * EXPERT_ROWS), I32),
        compiler_params=pltpu.CompilerParams(
            dimension_semantics=("arbitrary",), vmem_limit_bytes=VMEM_LIMIT_BYTES,
            has_side_effects=True),
        name="dispatch",
    )(pad_start, pad_row, pad_len, n_used, tab, loc, h)


IN_SLOTS = 6
AHEAD = 4
OUT_SLOTS = 4


def _expert_kernel(blk0_ref, nblk_ref, wsel_ref, nused_ref, xs_hbm, wg_ref, wu_ref, wd_ref, ys_hbm,
                   xring, yring, wg_bf, wu_bf, wd_bf, sem_in, sem_out):
    del wsel_ref
    e = pl.program_id(0)
    blk_tok = xring.shape[0] // IN_SLOTS
    n_used = nused_ref[0]
    n_blk = nblk_ref[e]

    def in_copy(g):
        slot = g % IN_SLOTS
        return pltpu.make_async_copy(
            _rows(xs_hbm, g * blk_tok, blk_tok), _rows(xring, slot * blk_tok, blk_tok),
            sem_in.at[slot])

    def out_copy(g):
        slot = g % OUT_SLOTS
        return pltpu.make_async_copy(
            _rows(yring, slot * blk_tok, blk_tok), _rows(ys_hbm, g * blk_tok, blk_tok),
            sem_out.at[slot])

    def request(g):
        @pl.when(g < n_used)
        def _():
            in_copy(g).start()

    def retire(g):
        @pl.when(g >= OUT_SLOTS)
        def _():
            out_copy(g - OUT_SLOTS).wait()

    def mlp(*blocks):
        xbs = [_unpack_rows(xring, (g % IN_SLOTS) * blk_tok, blk_tok) for g in blocks]
        gates = [_dot(xb, wg_bf[...]) for xb in xbs]
        ups = [_dot(xb, wu_bf[...]) for xb in xbs]
        acts = [(gate * jax.nn.sigmoid(gate) * up).astype(BF16) for gate, up in zip(gates, ups)]
        for g, act in zip(blocks, acts):
            _pack_rows(yring, _dot(act, wd_bf[...]), (g % OUT_SLOTS) * blk_tok)

    @pl.when(e == 0)
    def _():
        for g in range(AHEAD):
            request(g)

    @pl.when(n_blk > 0)
    def _():
        wg_bf[...] = wg_ref[0].astype(BF16)
        wu_bf[...] = wu_ref[0].astype(BF16)
        wd_bf[...] = wd_ref[0].astype(BF16)

    def pair(p, carry):
        g = blk0_ref[e] + 2 * p
        in_copy(g).wait()
        in_copy(g + 1).wait()
        request(g + AHEAD)
        request(g + AHEAD + 1)
        retire(g)
        retire(g + 1)
        mlp(g, g + 1)
        out_copy(g).start()
        out_copy(g + 1).start()
        return carry

    lax.fori_loop(0, n_blk // 2, pair, 0)

    @pl.when(n_blk % 2 == 1)
    def _():
        g = blk0_ref[e] + n_blk - 1
        in_copy(g).wait()
        request(g + AHEAD)
        retire(g)
        mlp(g)
        out_copy(g).start()

    @pl.when(e == pl.num_programs(0) - 1)
    def _():
        for k in range(1, OUT_SLOTS + 1):
            @pl.when(n_used >= k)
            def _():
                out_copy(n_used - k).wait()


def _experts(blk0, nblk, wsel, n_used, xs, w_gate, w_up, w_down):
    bm = EXPERT_ROWS
    n_exp, d, de = w_gate.shape
    wblk = lambda e, b0, nb, ws, nu: (ws[e], 0, 0)
    grid_spec = pltpu.PrefetchScalarGridSpec(
        num_scalar_prefetch=4,
        grid=(n_exp,),
        in_specs=[
            pl.BlockSpec(memory_space=pl.ANY),
            pl.BlockSpec((1, d, de), wblk),
            pl.BlockSpec((1, d, de), wblk),
            pl.BlockSpec((1, de, d), wblk),
        ],
        out_specs=pl.BlockSpec(memory_space=pl.ANY),
        scratch_shapes=[
            pltpu.VMEM(_packed_shape(IN_SLOTS * bm), I32),
            pltpu.VMEM(_packed_shape(OUT_SLOTS * bm), I32),
            pltpu.VMEM((d, de), BF16),
            pltpu.VMEM((d, de), BF16),
            pltpu.VMEM((de, d), BF16),
            pltpu.SemaphoreType.DMA((IN_SLOTS,)),
            pltpu.SemaphoreType.DMA((OUT_SLOTS,)),
        ],
    )
    n_prefetch = 4
    return pl.pallas_call(
        _expert_kernel,
        grid_spec=grid_spec,
        out_shape=jax.ShapeDtypeStruct(xs.shape, xs.dtype),
        input_output_aliases={n_prefetch: 0},
        compiler_params=pltpu.CompilerParams(
            dimension_semantics=("arbitrary",), vmem_limit_bytes=VMEM_LIMIT_BYTES,
            has_side_effects=True),
        name="experts",
    )(blk0, nblk, wsel, n_used, xs, w_gate, w_up, w_down)


def _combine_kernel(pstart_ref, tab_ref, tab_next_ref, x1_ref, mod_ref, ln2g_ref, ln2b_ref,
                    ewcol_ref, ys_hbm, out_ref, ybuf0, ybuf1, sem):
    i = pl.program_id(0)
    t_rows = x1_ref.shape[0]
    n_loc = TOP_K * t_rows
    ybufs = (ybuf0, ybuf1)

    @pl.when(i == 0)
    def _():
        _strip_copies(tab_ref, pstart_ref, ybuf0, ys_hbm, sem.at[0], to_hbm=False)

    for slot in range(2):
        @pl.when(i % 2 == slot)
        def _():
            @pl.when(i + 1 < pl.num_programs(0))
            def _():
                _strip_copies(tab_next_ref, pstart_ref, ybufs[1 - slot], ys_hbm,
                              sem.at[1 - slot], to_hbm=False)

            pltpu.make_async_copy(_rows(ys_hbm, 0, n_loc), ybufs[slot], sem.at[slot]).wait()
            ew = ewcol_ref[...]
            lane = lax.broadcasted_iota(I32, (t_rows, n_loc), 1).astype(F32)
            weights = (jnp.where(lane == ew[:, 2:3], ew[:, 0:1], 0.0)
                       + jnp.where(lane == ew[:, 3:4], ew[:, 1:2], 0.0))
            y = _dot(weights.astype(BF16), _unpack_rows(ybufs[slot]))
            gate2 = mod_ref[0, 5:6, :]
            out_ref[...] = _layer_norm(DEEPNORM_ALPHA * x1_ref[...] + (1.0 + gate2) * y,
                                       ln2g_ref[...], ln2b_ref[...])


def _combine(pad_start, tab, x1, mod, ln2_g, ln2_b, ewcol, ys, *, seq):
    n, d = x1.shape
    t = MIX_ROWS
    per_seq = seq // t
    n_steps = n // t
    row_blk = lambda i, ps: (i, 0)
    grid_spec = pltpu.PrefetchScalarGridSpec(
        num_scalar_prefetch=1,
        grid=(n_steps,),
        in_specs=[
            pl.BlockSpec((SUBLANES, LANES), row_blk, memory_space=pltpu.SMEM),
            pl.BlockSpec((SUBLANES, LANES), lambda i, ps: (jnp.minimum(i + 1, n_steps - 1), 0),
                         memory_space=pltpu.SMEM),
            pl.BlockSpec((t, d), row_blk),
            pl.BlockSpec((1, N_MOD, d), lambda i, ps: (i // per_seq, 0, 0)),
            pl.BlockSpec((1, d), lambda i, ps: (0, 0)),
            pl.BlockSpec((1, d), lambda i, ps: (0, 0)),
            pl.BlockSpec((t, LANES), row_blk),
            pl.BlockSpec(memory_space=pl.ANY),
        ],
        out_specs=pl.BlockSpec((t, d), row_blk),
        scratch_shapes=[
            pltpu.VMEM(_packed_shape(TOP_K * t), I32),
            pltpu.VMEM(_packed_shape(TOP_K * t), I32),
            pltpu.SemaphoreType.DMA((2,)),
        ],
    )
    return pl.pallas_call(
        _combine_kernel,
        grid_spec=grid_spec,
        out_shape=jax.ShapeDtypeStruct((n, d), F32),
        compiler_params=pltpu.CompilerParams(
            dimension_semantics=("arbitrary",), vmem_limit_bytes=VMEM_LIMIT_BYTES),
        name="combine",
    )(pad_start, tab, tab, x1, mod, ln2_g, ln2_b, ewcol, ys)


def _block_layout(counts, n_assign):
    bm = EXPERT_ROWS
    n_blocks = -(-n_assign // bm) + N_EXPERTS
    padded = (counts + bm - 1) // bm * bm
    pad_end = jnp.cumsum(padded)
    pad_start = (pad_end - padded).astype(I32)
    n_used = (pad_end[-1] // bm).astype(I32)
    pad_row = (pad_start + counts).astype(I32)
    pad_len = (padded - counts).astype(I32)
    ids = jnp.arange(N_EXPERTS, dtype=I32)
    prev_used = lax.cummax(jnp.where(padded > 0, ids, -1))
    first_used = jnp.argmax(padded > 0).astype(I32)
    weight_sel = jnp.where(prev_used >= 0, prev_used, first_used).astype(I32)
    return (pad_start, pad_row, pad_len, pad_start // bm, (padded // bm).astype(I32), weight_sel,
            n_used.reshape(1), n_blocks)


def kernel(x, c, w_ada, b_ada, w_in, conv_w, w_out_conv, w_pool, pool_scale, w_o, ln1_g, ln1_b,
           w_group, b_group, w_router, b_router, w_gate, w_up, w_down, ln2_g, ln2_b):
    bsz, seq, d = x.shape
    n = bsz * seq
    assert d == 2 * PACK_SUBLANES * LANES and seq % MIX_ROWS == 0 and w_ada.shape[0] == DEPTH
    x2d = x.reshape(n, d)
    c_pad = jnp.pad(c, ((0, -bsz % SUBLANES), (0, 0)))
    for l in range(DEPTH):
        mod = _ada(c_pad, w_ada[l], b_ada[l][None, :])[:bsz].reshape(bsz, N_MOD, d)
        wr = jnp.concatenate(
            [w_router[l], w_group[l],
             jnp.zeros((d, LANES - N_EXPERTS - N_GROUPS), F32)], axis=1)
        rb = jnp.concatenate(
            [b_router[l].reshape(-1), b_group[l],
             jnp.zeros((LANES - N_EXPERTS - N_GROUPS,), F32)])[:, None]
        x1, h, loc, ewcol, tab, cnt = _mix(
            x2d, mod, w_in[l].astype(BF16), conv_w[l], w_out_conv[l].astype(BF16),
            w_pool[l].astype(BF16), pool_scale[l][None, :], w_o[l].astype(BF16),
            ln1_g[l][None, :], ln1_b[l][None, :], wr, rb, batch=bsz, seq=seq)
        pad_start, pad_row, pad_len, blk0, nblk, weight_sel, n_used, n_blocks = _block_layout(
            cnt[:, 0], n * TOP_K)
        xs = _dispatch(pad_start, pad_row, pad_len, n_used, tab, loc, h, n_blocks)
        ys = _experts(blk0, nblk, weight_sel, n_used, xs, w_gate[l], w_up[l], w_down[l])
        x2d = _combine(pad_start, tab, x1, mod, ln2_g[l][None, :], ln2_b[l][None, :],
                       ewcol, ys, seq=seq)
    return x2d.reshape(bsz, seq, d)
```

```python
import functools

import jax
import jax.numpy as jnp
from jax import lax
from jax.experimental import pallas as pl
from jax.experimental.pallas import tpu as pltpu

F32 = jnp.float32
BF16 = jnp.bfloat16
I32 = jnp.int32

LANES = 128
SUBLANES = 8
VMEM_LIMIT_BYTES = 56 * 1024 * 1024

N_POOL_GROUPS = 4
POOL_WINDOWS = (2, 4, 8, 16)
CONV_WIDTH = 3
N_GROUPS = 8
EXPERTS_PER_GROUP = 8
N_EXPERTS = N_GROUPS * EXPERTS_PER_GROUP
TOP_K = 2
N_MOD = 6
LN_EPS = 1e-5
DEPTH = 1
DEEPNORM_ALPHA = (2.0 * DEPTH) ** 0.25

MIX_ROWS = 512
EXPERT_ROWS = 256
HALO = SUBLANES
PACK_SUBLANES = 4


def _split_bf16(a):
    hi = a.astype(BF16)
    lo = (a - hi.astype(F32)).astype(BF16)
    return hi, lo


def _dot(a, b):
    return jnp.dot(a, b, preferred_element_type=F32)


def _dot3(a, b):
    a_hi, a_lo = _split_bf16(a)
    b_hi, b_lo = _split_bf16(b)
    return _dot(a_hi, b_hi) + _dot(a_lo, b_hi) + _dot(a_hi, b_lo)


def _dot3_wide(a, b):
    a_hi, a_lo = _split_bf16(a)
    n = b.shape[1]
    b_both = jnp.concatenate(_split_bf16(b), axis=1)
    hi = _dot(a_hi, b_both)
    lo = _dot(a_lo, b_both)
    return hi[:, :n] + hi[:, n:] + lo[:, :n] + lo[:, n:]


def _layer_norm(r, g, b):
    mu = jnp.mean(r, axis=-1, keepdims=True)
    d = r - mu
    var = jnp.mean(d * d, axis=-1, keepdims=True)
    return d * lax.rsqrt(var + LN_EPS) * g + b


def _packed_shape(rows):
    return (rows, PACK_SUBLANES, LANES)


def _plane(ref, s, row0, rows):
    flat = ref.reshape(ref.shape[0] * PACK_SUBLANES, LANES)
    return flat.at[pl.ds(row0 * PACK_SUBLANES + s, rows, stride=PACK_SUBLANES), :]


def _pack_rows(ref, val, row0=0):
    rows, d = val.shape
    bits = lax.bitcast_convert_type(val.astype(BF16).astype(F32), I32)
    for s in range(PACK_SUBLANES):
        hi = bits[:, s * LANES:(s + 1) * LANES]
        lo = bits[:, d // 2 + s * LANES:d // 2 + (s + 1) * LANES]
        _plane(ref, s, row0, rows)[...] = jnp.bitwise_or(hi, lax.shift_right_logical(lo, 16))


def _unpack_rows(ref, row0=0, rows=None):
    rows = ref.shape[0] - row0 if rows is None else rows
    his, los = [], []
    for s in range(PACK_SUBLANES):
        word = _plane(ref, s, row0, rows)[...]
        his.append(lax.bitcast_convert_type(jnp.bitwise_and(word, jnp.int32(-65536)), F32))
        los.append(lax.bitcast_convert_type(lax.shift_left(word, 16), F32))
    return jnp.concatenate(his + los, axis=1).astype(BF16)


def _rows(ref, row, n_rows):
    return ref.at[pl.ds(row, n_rows)]


def _ada_kernel(c_ref, w_ref, b_ref, o_ref):
    cc = c_ref[...]
    act = cc * jax.nn.sigmoid(cc)
    o_ref[...] = _dot3(act, w_ref[...]) + b_ref[...]


def _ada(c_pad, w_ada, b_ada):
    rows, d = c_pad.shape
    n_out = w_ada.shape[1]
    return pl.pallas_call(
        _ada_kernel,
        grid=(n_out // d,),
        in_specs=[
            pl.BlockSpec((rows, d), lambda j: (0, 0)),
            pl.BlockSpec((d, d), lambda j: (0, j)),
            pl.BlockSpec((1, d), lambda j: (0, j)),
        ],
        out_specs=pl.BlockSpec((rows, d), lambda j: (0, j)),
        out_shape=jax.ShapeDtypeStruct((rows, n_out), F32),
        compiler_params=pltpu.CompilerParams(
            dimension_semantics=("arbitrary",), vmem_limit_bytes=VMEM_LIMIT_BYTES),
        name="ada",
    )(c_pad, w_ada, b_ada)


def _mix_kernel(x_ref, mod_ref, win_ref, convw_ref, woc_ref, wpool_ref, pscale_ref, wo_ref,
                ln1g_ref, ln1b_ref, wr_ref, rb_ref, mod_prev_ref,
                x1_ref, h_ref, loc_ref, ewcol_ref, tab_ref, cnt_ref,
                cvbuf, zbuf, s2buf, s4buf, s8buf, base_ref, res_buf, *, tiles_per_seq):
    t_rows, d = x_ref.shape
    pg = d // N_POOL_GROUPS
    step = pl.program_id(0)
    n_tiles = pl.num_programs(0) - 1
    s = jnp.minimum(step, n_tiles - 1) % tiles_per_seq
    shift_bufs = (cvbuf, zbuf, s2buf, s4buf, s8buf)

    @pl.when(step == 0)
    def _():
        base_ref[...] = jnp.zeros_like(base_ref)
        res_buf[...] = jnp.zeros_like(res_buf)

    @pl.when(s == 0)
    def _():
        for buf in shift_bufs:
            buf[0:HALO, :] = jnp.zeros((HALO, buf.shape[1]), F32)

    @pl.when(s > 0)
    def _():
        for buf in shift_bufs:
            buf[0:HALO, :] = buf[t_rows:t_rows + HALO, :]

    def shifted(buf, k):
        return buf[HALO - k:HALO - k + t_rows, :]

    xt = x_ref[...]
    shift1 = mod_ref[0, 0:1, :]
    scale1 = mod_ref[0, 1:2, :]
    gate1 = mod_ref[0, 2:3, :]

    u = (xt * (1.0 + scale1) + shift1).astype(BF16)

    def proj(i):
        return _dot(u, win_ref[:, i * d:(i + 1) * d])

    cv = proj(2) * proj(0)

    x1 = _layer_norm(res_buf[...], ln1g_ref[...], ln1b_ref[...])
    x1_ref[...] = x1
    h = x1 * (1.0 + mod_prev_ref[0, 4:5, :]) + mod_prev_ref[0, 3:4, :]
    h_ref[...] = h.astype(BF16)
    lt = _dot3_wide(h, wr_ref[...]).T + rb_ref[...]
    _route_tile(lt, step > 0, base_ref, loc_ref, ewcol_ref, tab_ref, cnt_ref)

    cvbuf[HALO:HALO + t_rows, :] = cv
    conv = (convw_ref[2:3, :] * cv + convw_ref[1:2, :] * shifted(cvbuf, 1)
            + convw_ref[0:1, :] * shifted(cvbuf, 2))
    y_a = _dot((conv * proj(1)).astype(BF16), woc_ref[...])

    z = proj(3)
    zbuf[HALO:HALO + t_rows, :] = z
    s2 = z + shifted(zbuf, 1)
    s2buf[HALO:HALO + t_rows, :] = s2[:, pg:]
    s4 = s2[:, pg:] + shifted(s2buf, 2)
    s4buf[HALO:HALO + t_rows, :] = s4[:, pg:]
    s8 = s4[:, pg:] + shifted(s4buf, 4)
    s8buf[HALO:HALO + t_rows, :] = s8[:, pg:]
    s16 = s8[:, pg:] + shifted(s8buf, 8)
    win_sums = (s2[:, :pg], s4[:, :pg], s8[:, :pg], s16)
    t_pos = s * t_rows + lax.broadcasted_iota(I32, (t_rows, 1), 0) + 1
    y_b_parts = []
    for gi, w in enumerate(POOL_WINDOWS):
        count = jnp.minimum(t_pos, w).astype(F32)
        pooled = win_sums[gi] / count - z[:, gi * pg:(gi + 1) * pg]
        y_b_parts.append(_dot(pooled.astype(BF16), wpool_ref[gi]))
    y_b = jnp.concatenate(y_b_parts, axis=1) * pscale_ref[...]

    merged = jax.nn.sigmoid(proj(4)) * y_a + jax.nn.sigmoid(proj(5)) * y_b
    mix = _dot(merged.astype(BF16), wo_ref[...])

    res_buf[...] = DEEPNORM_ALPHA * xt + (1.0 + gate1) * mix


def _route_tile(lt, valid, base_ref, loc_ref, ewcol_ref, tab_ref, cnt_ref):
    t_rows = lt.shape[1]
    iota8 = lax.broadcasted_iota(I32, (SUBLANES, t_rows), 0)
    lg = lt[N_EXPERTS:N_EXPERTS + N_GROUPS, :]
    g_max = jnp.max(lg, axis=0, keepdims=True)
    g_top = jnp.min(jnp.where(lg == g_max, iota8, N_GROUPS), axis=0, keepdims=True)
    g_top_p = 1.0 / jnp.sum(jnp.exp(lg - g_max), axis=0, keepdims=True)
    sel = jnp.zeros((EXPERTS_PER_GROUP, t_rows), F32)
    for g in range(N_GROUPS):
        sel = jnp.where(g_top == g, lt[g * EXPERTS_PER_GROUP:(g + 1) * EXPERTS_PER_GROUP, :], sel)
    m1 = jnp.max(sel, axis=0, keepdims=True)
    i1 = jnp.min(jnp.where(sel == m1, iota8, EXPERTS_PER_GROUP), axis=0, keepdims=True)
    sel2 = jnp.where(iota8 == i1, -jnp.inf, sel)
    m2 = jnp.max(sel2, axis=0, keepdims=True)
    i2 = jnp.min(jnp.where(sel2 == m2, iota8, EXPERTS_PER_GROUP), axis=0, keepdims=True)
    ratio = jnp.exp(m2 - m1)
    p1 = 1.0 / (1.0 + ratio)
    w1 = g_top_p * p1
    w2 = g_top_p * (ratio * p1)
    e1 = g_top * EXPERTS_PER_GROUP + i1
    e2 = g_top * EXPERTS_PER_GROUP + i2

    iota_e = lax.broadcasted_iota(I32, (N_EXPERTS, t_rows), 0)
    hit1 = jnp.logical_and(iota_e == e1, valid)
    hit2 = jnp.logical_and(iota_e == e2, valid)
    onehot = jnp.where(jnp.logical_or(hit1, hit2), 1.0, 0.0)
    before = (lax.broadcasted_iota(I32, (t_rows, t_rows), 0)
              < lax.broadcasted_iota(I32, (t_rows, t_rows), 1))
    prefix = _dot(onehot.astype(BF16), jnp.where(before, 1.0, 0.0).astype(BF16))
    tile_cnt = jnp.sum(onehot, axis=1, keepdims=True)
    lower = (lax.broadcasted_iota(I32, (N_EXPERTS, N_EXPERTS), 1)
             < lax.broadcasted_iota(I32, (N_EXPERTS, N_EXPERTS), 0))
    lower = jnp.where(lower, 1.0, 0.0).astype(BF16)
    cnt_hi, cnt_lo = _split_bf16(jnp.broadcast_to(tile_cnt, (N_EXPERTS, LANES)))
    loc_start = (_dot(lower, cnt_hi) + _dot(lower, cnt_lo))[:, 0:1]
    pos = prefix + loc_start
    loc1 = jnp.sum(jnp.where(hit1, pos, 0.0), axis=0, keepdims=True)
    loc2 = jnp.sum(jnp.where(hit2, pos, 0.0), axis=0, keepdims=True)
    base = base_ref[...]
    base_ref[...] = base + tile_cnt
    cnt_ref[...] = jnp.broadcast_to(base + tile_cnt, cnt_ref.shape).astype(I32)

    diag = (lax.broadcasted_iota(I32, (N_EXPERTS, LANES), 0)
            == lax.broadcasted_iota(I32, (N_EXPERTS, LANES), 1))

    def to_lanes(col):
        return jnp.sum(jnp.where(diag, col, 0.0), axis=0, keepdims=True)

    iota8l = lax.broadcasted_iota(I32, (SUBLANES, LANES), 0)
    min_len = jnp.min(jnp.broadcast_to(tile_cnt, (N_EXPERTS, LANES)), axis=0, keepdims=True)
    tab_ref[...] = jnp.where(
        iota8l == TAB_LEN, to_lanes(tile_cnt),
        jnp.where(iota8l == TAB_RANK, to_lanes(base),
                  jnp.where(iota8l == TAB_LOC, to_lanes(loc_start),
                            jnp.where(iota8l == TAB_MIN_LEN, min_len, 0.0)))).astype(I32)

    loc_ref[...] = jnp.where(iota8 == 0, loc1, jnp.where(iota8 == 1, loc2, 0.0)).astype(I32)
    iota_l = lax.broadcasted_iota(I32, (LANES, t_rows), 0)
    ewcol_ref[...] = jnp.where(
        iota_l == 0, w1, jnp.where(iota_l == 1, w2,
                                   jnp.where(iota_l == 2, loc1, jnp.where(iota_l == 3, loc2, 0.0)))).T


def _mix(x2d, mod, w_in, conv_w, w_oc, w_pool, pool_scale, w_o, ln1_g, ln1_b, wr, rb,
         *, batch, seq):
    n, d = x2d.shape
    t = MIX_ROWS
    n_s = seq // t
    n_tiles = batch * n_s
    const = lambda shape: pl.BlockSpec(shape, lambda i: (0,) * len(shape),
                                       pipeline_mode=pl.Buffered(1))
    cur = lambda i: jnp.minimum(i, n_tiles - 1)
    prev = lambda i: jnp.maximum(i - 1, 0)
    pg = d // N_POOL_GROUPS
    return pl.pallas_call(
        functools.partial(_mix_kernel, tiles_per_seq=n_s),
        grid=(n_tiles + 1,),
        in_specs=[
            pl.BlockSpec((t, d), lambda i: (cur(i), 0)),
            pl.BlockSpec((1, N_MOD, d), lambda i: (cur(i) // n_s, 0, 0)),
            const(w_in.shape), const(conv_w.shape), const(w_oc.shape), const(w_pool.shape),
            const(pool_scale.shape), const(w_o.shape), const(ln1_g.shape), const(ln1_b.shape),
            const(wr.shape), const(rb.shape),
            pl.BlockSpec((1, N_MOD, d), lambda i: (prev(i) // n_s, 0, 0)),
        ],
        out_specs=[
            pl.BlockSpec((t, d), lambda i: (prev(i), 0)),
            pl.BlockSpec((t, d), lambda i: (prev(i), 0)),
            pl.BlockSpec((SUBLANES, t), lambda i: (0, prev(i))),
            pl.BlockSpec((t, LANES), lambda i: (prev(i), 0)),
            pl.BlockSpec((SUBLANES, LANES), lambda i: (prev(i), 0)),
            pl.BlockSpec((N_EXPERTS, LANES), lambda i: (0, 0)),
        ],
        out_shape=[
            jax.ShapeDtypeStruct((n, d), F32),
            jax.ShapeDtypeStruct((n, d), BF16),
            jax.ShapeDtypeStruct((SUBLANES, n), I32),
            jax.ShapeDtypeStruct((n, LANES), F32),
            jax.ShapeDtypeStruct((n // t * SUBLANES, LANES), I32),
            jax.ShapeDtypeStruct((N_EXPERTS, LANES), I32),
        ],
        scratch_shapes=[
            pltpu.VMEM((t + HALO, d), F32),
            pltpu.VMEM((t + HALO, d), F32),
            pltpu.VMEM((t + HALO, 3 * pg), F32),
            pltpu.VMEM((t + HALO, 2 * pg), F32),
            pltpu.VMEM((t + HALO, pg), F32),
            pltpu.VMEM((N_EXPERTS, 1), F32),
            pltpu.VMEM((t, d), F32),
        ],
        compiler_params=pltpu.CompilerParams(
            dimension_semantics=("arbitrary",), vmem_limit_bytes=VMEM_LIMIT_BYTES),
        name="mix",
    )(x2d, mod, w_in, conv_w, w_oc, w_pool, pool_scale, w_o, ln1_g, ln1_b, wr, rb, mod)


TAB_LEN, TAB_RANK, TAB_LOC, TAB_MIN_LEN = 0, 1, 2, 3
FILL_SEM = 2


def _strip_copies(tab_ref, pstart_ref, local_ref, sorted_hbm, sem, *, to_hbm, local_row0=0):
    def start(e):
        n_rows = tab_ref[TAB_LEN, e]
        local = _rows(local_ref, local_row0 + tab_ref[TAB_LOC, e], n_rows)
        remote = _rows(sorted_hbm, pstart_ref[e] + tab_ref[TAB_RANK, e], n_rows)
        src, dst = (local, remote) if to_hbm else (remote, local)
        pltpu.make_async_copy(src, dst, sem).start()

    @pl.when(tab_ref[TAB_MIN_LEN, 0] > 0)
    def _():
        for e in range(N_EXPERTS):
            start(e)

    @pl.when(tab_ref[TAB_MIN_LEN, 0] <= 0)
    def _():
        def strip(e, carry):
            @pl.when(tab_ref[TAB_LEN, e] > 0)
            def _():
                start(e)
            return carry

        lax.fori_loop(0, N_EXPERTS, strip, 0)


def _dispatch_kernel(pstart_ref, padrow_ref, padlen_ref, nused_ref, tab_ref, loc_ref, h_ref, xs_hbm,
                     xloc0, xloc1, zero_blk, sem):
    i = pl.program_id(0)
    n_steps = pl.num_programs(0)
    t_rows = h_ref.shape[0]
    n_loc = TOP_K * t_rows
    blk_tok = zero_blk.shape[0]
    n_blocks = xs_hbm.shape[0] // blk_tok
    xlocs = (xloc0, xloc1)

    def zero_fills(act):
        def pad_rows(e, carry):
            @pl.when(padlen_ref[e] > 0)
            def _():
                act(pltpu.make_async_copy(
                    _rows(zero_blk, 0, padlen_ref[e]),
                    _rows(xs_hbm, padrow_ref[e], padlen_ref[e]), sem.at[FILL_SEM]))
            return carry

        def unused_block(b, carry):
            act(pltpu.make_async_copy(
                zero_blk, _rows(xs_hbm, b * blk_tok, blk_tok), sem.at[FILL_SEM]))
            return carry

        lax.fori_loop(0, N_EXPERTS, pad_rows, 0)
        lax.fori_loop(nused_ref[0], n_blocks, unused_block, 0)

    @pl.when(i == 0)
    def _():
        zero_blk[...] = jnp.zeros_like(zero_blk)
        zero_fills(lambda copy: copy.start())

    def strips_done(slot):
        pltpu.make_async_copy(xlocs[slot], _rows(xs_hbm, 0, n_loc), sem.at[slot]).wait()

    for slot in range(2):
        @pl.when(i % 2 == slot)
        def _():
            @pl.when(i >= 2)
            def _():
                strips_done(slot)

            iota_p = lax.broadcasted_iota(I32, (n_loc, t_rows), 0)
            pick = jnp.logical_or(iota_p == loc_ref[0:1, :], iota_p == loc_ref[1:2, :])
            xsorted = _dot(jnp.where(pick, 1.0, 0.0).astype(BF16), h_ref[...])
            _pack_rows(xlocs[slot], xsorted)
            _strip_copies(tab_ref, pstart_ref, xlocs[slot], xs_hbm, sem.at[slot], to_hbm=True)

    @pl.when(i == n_steps - 1)
    def _():
        for slot in range(2):
            @pl.when(n_steps > slot)
            def _():
                strips_done(slot)
        zero_fills(lambda copy: copy.wait())


def _dispatch(pad_start, pad_row, pad_len, n_used, tab, loc, h, n_blocks):
    n, d = h.shape
    t = MIX_ROWS
    grid_spec = pltpu.PrefetchScalarGridSpec(
        num_scalar_prefetch=4,
        grid=(n // t,),
        in_specs=[
            pl.BlockSpec((SUBLANES, LANES), lambda i, *_: (i, 0), memory_space=pltpu.SMEM),
            pl.BlockSpec((SUBLANES, t), lambda i, *_: (0, i)),
            pl.BlockSpec((t, d), lambda i, *_: (i, 0)),
        ],
        out_specs=pl.BlockSpec(memory_space=pl.ANY),
        scratch_shapes=[
            pltpu.VMEM(_packed_shape(TOP_K * t), I32),
            pltpu.VMEM(_packed_shape(TOP_K * t), I32),
            pltpu.VMEM(_packed_shape(EXPERT_ROWS), I32),
            pltpu.SemaphoreType.DMA((3,)),
        ],
    )
    return pl.pallas_call(
        _dispatch_kernel,
        grid_spec=grid_spec,
        out_shape=jax.ShapeDtypeStruct(_packed_shape(n_blocks * EXPERT_ROWS), I32),
        compiler_params=pltpu.CompilerParams(
            dimension_semantics=("arbitrary",), vmem_limit_bytes=VMEM_LIMIT_BYTES,
            has_side_effects=True),
        name="dispatch",
    )(pad_start, pad_row, pad_len, n_used, tab, loc, h)


IN_SLOTS = 6
AHEAD = 4
OUT_SLOTS = 4


def _expert_kernel(blk0_ref, nblk_ref, wsel_ref, nused_ref, xs_hbm, wg_ref, wu_ref, wd_ref, ys_hbm,
                   xring, yring, wg_bf, wu_bf, wd_bf, sem_in, sem_out):
    del wsel_ref
    e = pl.program_id(0)
    blk_tok = xring.shape[0] // IN_SLOTS
    n_used = nused_ref[0]
    n_blk = nblk_ref[e]

    def in_copy(g):
        slot = g % IN_SLOTS
        return pltpu.make_async_copy(
            _rows(xs_hbm, g * blk_tok, blk_tok), _rows(xring, slot * blk_tok, blk_tok),
            sem_in.at[slot])

    def out_copy(g):
        slot = g % OUT_SLOTS
        return pltpu.make_async_copy(
            _rows(yring, slot * blk_tok, blk_tok), _rows(ys_hbm, g * blk_tok, blk_tok),
            sem_out.at[slot])

    def request(g):
        @pl.when(g < n_used)
        def _():
            in_copy(g).start()

    def retire(g):
        @pl.when(g >= OUT_SLOTS)
        def _():
            out_copy(g - OUT_SLOTS).wait()

    def mlp(*blocks):
        xbs = [_unpack_rows(xring, (g % IN_SLOTS) * blk_tok, blk_tok) for g in blocks]
        gates = [_dot(xb, wg_bf[...]) for xb in xbs]
        ups = [_dot(xb, wu_bf[...]) for xb in xbs]
        acts = [(gate * jax.nn.sigmoid(gate) * up).astype(BF16) for gate, up in zip(gates, ups)]
        for g, act in zip(blocks, acts):
            _pack_rows(yring, _dot(act, wd_bf[...]), (g % OUT_SLOTS) * blk_tok)

    @pl.when(e == 0)
    def _():
        for g in range(AHEAD):
            request(g)

    @pl.when(n_blk > 0)
    def _():
        wg_bf[...] = wg_ref[0].astype(BF16)
        wu_bf[...] = wu_ref[0].astype(BF16)
        wd_bf[...] = wd_ref[0].astype(BF16)

    def pair(p, carry):
        g = blk0_ref[e] + 2 * p
        in_copy(g).wait()
        in_copy(g + 1).wait()
        request(g + AHEAD)
        request(g + AHEAD + 1)
        retire(g)
        retire(g + 1)
        mlp(g, g + 1)
        out_copy(g).start()
        out_copy(g + 1).start()
        return carry

    lax.fori_loop(0, n_blk // 2, pair, 0)

    @pl.when(n_blk % 2 == 1)
    def _():
        g = blk0_ref[e] + n_blk - 1
        in_copy(g).wait()
        request(g + AHEAD)
        retire(g)
        mlp(g)
        out_copy(g).start()

    @pl.when(e == pl.num_programs(0) - 1)
    def _():
        for k in range(1, OUT_SLOTS + 1):
            @pl.when(n_used >= k)
            def _():
                out_copy(n_used - k).wait()


def _experts(blk0, nblk, wsel, n_used, xs, w_gate, w_up, w_down):
    bm = EXPERT_ROWS
    n_exp, d, de = w_gate.shape
    wblk = lambda e, b0, nb, ws, nu: (ws[e], 0, 0)
    grid_spec = pltpu.PrefetchScalarGridSpec(
        num_scalar_prefetch=4,
        grid=(n_exp,),
        in_specs=[
            pl.BlockSpec(memory_space=pl.ANY),
            pl.BlockSpec((1, d, de), wblk),
            pl.BlockSpec((1, d, de), wblk),
            pl.BlockSpec((1, de, d), wblk),
        ],
        out_specs=pl.BlockSpec(memory_space=pl.ANY),
        scratch_shapes=[
            pltpu.VMEM(_packed_shape(IN_SLOTS * bm), I32),
            pltpu.VMEM(_packed_shape(OUT_SLOTS * bm), I32),
            pltpu.VMEM((d, de), BF16),
            pltpu.VMEM((d, de), BF16),
            pltpu.VMEM((de, d), BF16),
            pltpu.SemaphoreType.DMA((IN_SLOTS,)),
            pltpu.SemaphoreType.DMA((OUT_SLOTS,)),
        ],
    )
    n_prefetch = 4
    return pl.pallas_call(
        _expert_kernel,
        grid_spec=grid_spec,
        out_shape=jax.ShapeDtypeStruct(xs.shape, xs.dtype),
        input_output_aliases={n_prefetch: 0},
        compiler_params=pltpu.CompilerParams(
            dimension_semantics=("arbitrary",), vmem_limit_bytes=VMEM_LIMIT_BYTES,
            has_side_effects=True),
        name="experts",
    )(blk0, nblk, wsel, n_used, xs, w_gate, w_up, w_down)


def _combine_kernel(pstart_ref, tab_ref, tab_next_ref, x1_ref, mod_ref, ln2g_ref, ln2b_ref,
                    ewcol_ref, ys_hbm, out_ref, yring, ycomb, sem):
    i = pl.program_id(0)
    n_tiles = pl.num_programs(0) - 1
    t_rows = x1_ref.shape[0]
    n_loc = TOP_K * t_rows

    def fetch(table_ref, tile):
        slot = tile % 2
        _strip_copies(table_ref, pstart_ref, yring, ys_hbm, sem.at[slot], to_hbm=False,
                      local_row0=slot * n_loc)

    @pl.when(i == 0)
    def _():
        yring[...] = jnp.zeros_like(yring)
        ycomb[...] = jnp.zeros_like(ycomb)
        fetch(tab_ref, 0)

    @pl.when(i + 1 < n_tiles)
    def _():
        fetch(tab_next_ref, i + 1)

    @pl.when(i < n_tiles)
    def _():
        slot = i % 2
        pltpu.make_async_copy(
            _rows(ys_hbm, 0, n_loc), _rows(yring, slot * n_loc, n_loc), sem.at[slot]).wait()

    y_prev = ycomb[...]
    ew = ewcol_ref[...]
    lane = lax.broadcasted_iota(I32, (t_rows, n_loc), 1).astype(F32)
    weights = (jnp.where(lane == ew[:, 2:3], ew[:, 0:1], 0.0)
               + jnp.where(lane == ew[:, 3:4], ew[:, 1:2], 0.0))
    ycomb[...] = _dot(weights.astype(BF16), _unpack_rows(yring, (i % 2) * n_loc, n_loc))

    gate2 = mod_ref[0, 5:6, :]
    out_ref[...] = _layer_norm(DEEPNORM_ALPHA * x1_ref[...] + (1.0 + gate2) * y_prev,
                               ln2g_ref[...], ln2b_ref[...])


def _combine(pad_start, tab, x1, mod, ln2_g, ln2_b, ewcol, ys, *, seq):
    n, d = x1.shape
    t = MIX_ROWS
    per_seq = seq // t
    n_tiles = n // t
    cur = lambda i, ps: (jnp.minimum(i, n_tiles - 1), 0)
    nxt = lambda i, ps: (jnp.minimum(i + 1, n_tiles - 1), 0)
    prev = lambda i, ps: (jnp.maximum(i - 1, 0), 0)
    grid_spec = pltpu.PrefetchScalarGridSpec(
        num_scalar_prefetch=1,
        grid=(n_tiles + 1,),
        in_specs=[
            pl.BlockSpec((SUBLANES, LANES), cur, memory_space=pltpu.SMEM),
            pl.BlockSpec((SUBLANES, LANES), nxt, memory_space=pltpu.SMEM),
            pl.BlockSpec((t, d), prev),
            pl.BlockSpec((1, N_MOD, d), lambda i, ps: (jnp.maximum(i - 1, 0) // per_seq, 0, 0)),
            pl.BlockSpec((1, d), lambda i, ps: (0, 0)),
            pl.BlockSpec((1, d), lambda i, ps: (0, 0)),
            pl.BlockSpec((t, LANES), cur),
            pl.BlockSpec(memory_space=pl.ANY),
        ],
        out_specs=pl.BlockSpec((t, d), prev),
        scratch_shapes=[
            pltpu.VMEM(_packed_shape(2 * TOP_K * t), I32),
            pltpu.VMEM((t, d), F32),
            pltpu.SemaphoreType.DMA((2,)),
        ],
    )
    return pl.pallas_call(
        _combine_kernel,
        grid_spec=grid_spec,
        out_shape=jax.ShapeDtypeStruct((n, d), F32),
        compiler_params=pltpu.CompilerParams(
            dimension_semantics=("arbitrary",), vmem_limit_bytes=VMEM_LIMIT_BYTES),
        name="combine",
    )(pad_start, tab, tab, x1, mod, ln2_g, ln2_b, ewcol, ys)


def _block_layout(counts, n_assign):
    bm = EXPERT_ROWS
    n_blocks = -(-n_assign // bm) + N_EXPERTS
    padded = (counts + bm - 1) // bm * bm
    pad_end = jnp.cumsum(padded)
    pad_start = (pad_end - padded).astype(I32)
    n_used = (pad_end[-1] // bm).astype(I32)
    pad_row = (pad_start + counts).astype(I32)
    pad_len = (padded - counts).astype(I32)
    ids = jnp.arange(N_EXPERTS, dtype=I32)
    prev_used = lax.cummax(jnp.where(padded > 0, ids, -1))
    first_used = jnp.argmax(padded > 0).astype(I32)
    weight_sel = jnp.where(prev_used >= 0, prev_used, first_used).astype(I32)
    return (pad_start, pad_row, pad_len, pad_start // bm, (padded // bm).astype(I32), weight_sel,
            n_used.reshape(1), n_blocks)


def kernel(x, c, w_ada, b_ada, w_in, conv_w, w_out_conv, w_pool, pool_scale, w_o, ln1_g, ln1_b,
           w_group, b_group, w_router, b_router, w_gate, w_up, w_down, ln2_g, ln2_b):
    bsz, seq, d = x.shape
    n = bsz * seq
    assert d == 2 * PACK_SUBLANES * LANES and seq % MIX_ROWS == 0 and w_ada.shape[0] == DEPTH
    x2d = x.reshape(n, d)
    c_pad = jnp.pad(c, ((0, -bsz % SUBLANES), (0, 0)))
    for l in range(DEPTH):
        mod = _ada(c_pad, w_ada[l], b_ada[l][None, :])[:bsz].reshape(bsz, N_MOD, d)
        wr = jnp.concatenate(
            [w_router[l], w_group[l],
             jnp.zeros((d, LANES - N_EXPERTS - N_GROUPS), F32)], axis=1)
        rb = jnp.concatenate(
            [b_router[l].reshape(-1), b_group[l],
             jnp.zeros((LANES - N_EXPERTS - N_GROUPS,), F32)])[:, None]
        x1, h, loc, ewcol, tab, cnt = _mix(
            x2d, mod, w_in[l].astype(BF16), conv_w[l], w_out_conv[l].astype(BF16),
            w_pool[l].astype(BF16), pool_scale[l][None, :], w_o[l].astype(BF16),
            ln1_g[l][None, :], ln1_b[l][None, :], wr, rb, batch=bsz, seq=seq)
        pad_start, pad_row, pad_len, blk0, nblk, weight_sel, n_used, n_blocks = _block_layout(
            cnt[:, 0], n * TOP_K)
        xs = _dispatch(pad_start, pad_row, pad_len, n_used, tab, loc, h, n_blocks)
        ys = _experts(blk0, nblk, weight_sel, n_used, xs, w_gate[l], w_up[l], w_down[l])
        x2d = _combine(pad_start, tab, x1, mod, ln2_g[l][None, :], ln2_b[l][None, :],
                       ewcol, ys, seq=seq)
    return x2d.reshape(bsz, seq, d)
```

```python
import functools

import jax
import jax.numpy as jnp
from jax import lax
from jax.experimental import pallas as pl
from jax.experimental.pallas import tpu as pltpu

F32 = jnp.float32
BF16 = jnp.bfloat16
I32 = jnp.int32

LANES = 128
SUBLANES = 8
VMEM_LIMIT_BYTES = 56 * 1024 * 1024

N_POOL_GROUPS = 4
POOL_WINDOWS = (2, 4, 8, 16)
CONV_WIDTH = 3
N_GROUPS = 8
EXPERTS_PER_GROUP = 8
N_EXPERTS = N_GROUPS * EXPERTS_PER_GROUP
TOP_K = 2
N_MOD = 6
LN_EPS = 1e-5
DEPTH = 1
DEEPNORM_ALPHA = (2.0 * DEPTH) ** 0.25

MIX_ROWS = 512
EXPERT_ROWS = 256
HALO = SUBLANES
PACK_SUBLANES = 4


def _split_bf16(a):
    hi = a.astype(BF16)
    lo = (a - hi.astype(F32)).astype(BF16)
    return hi, lo


def _dot(a, b):
    return jnp.dot(a, b, preferred_element_type=F32)


def _dot3(a, b):
    a_hi, a_lo = _split_bf16(a)
    b_hi, b_lo = _split_bf16(b)
    return _dot(a_hi, b_hi) + _dot(a_lo, b_hi) + _dot(a_hi, b_lo)


def _dot3_wide(a, b):
    a_hi, a_lo = _split_bf16(a)
    n = b.shape[1]
    b_both = jnp.concatenate(_split_bf16(b), axis=1)
    hi = _dot(a_hi, b_both)
    lo = _dot(a_lo, b_both)
    return hi[:, :n] + hi[:, n:] + lo[:, :n] + lo[:, n:]


def _layer_norm(r, g, b):
    mu = jnp.mean(r, axis=-1, keepdims=True)
    d = r - mu
    var = jnp.mean(d * d, axis=-1, keepdims=True)
    return d * lax.rsqrt(var + LN_EPS) * g + b


def _packed_shape(rows):
    return (rows, PACK_SUBLANES, LANES)


def _plane(ref, s, row0, rows):
    flat = ref.reshape(ref.shape[0] * PACK_SUBLANES, LANES)
    return flat.at[pl.ds(row0 * PACK_SUBLANES + s, rows, stride=PACK_SUBLANES), :]


def _pack_rows(ref, val, row0=0, *, is_bf16_exact=False):
    rows, d = val.shape
    if not is_bf16_exact:
        val = val.astype(BF16).astype(F32)
    bits = lax.bitcast_convert_type(val, I32)
    for s in range(PACK_SUBLANES):
        hi = bits[:, s * LANES:(s + 1) * LANES]
        lo = bits[:, d // 2 + s * LANES:d // 2 + (s + 1) * LANES]
        _plane(ref, s, row0, rows)[...] = jnp.bitwise_or(hi, lax.shift_right_logical(lo, 16))


def _unpack_rows(ref, row0=0, rows=None):
    rows = ref.shape[0] - row0 if rows is None else rows
    his, los = [], []
    for s in range(PACK_SUBLANES):
        word = _plane(ref, s, row0, rows)[...]
        his.append(lax.bitcast_convert_type(jnp.bitwise_and(word, jnp.int32(-65536)), F32))
        los.append(lax.bitcast_convert_type(lax.shift_left(word, 16), F32))
    return jnp.concatenate(his + los, axis=1).astype(BF16)


def _rows(ref, row, n_rows):
    return ref.at[pl.ds(row, n_rows)]


def _ada_kernel(c_ref, w_ref, b_ref, o_ref):
    cc = c_ref[...]
    act = cc * jax.nn.sigmoid(cc)
    o_ref[...] = _dot3(act, w_ref[...]) + b_ref[...]


def _ada(c_pad, w_ada, b_ada):
    rows, d = c_pad.shape
    n_out = w_ada.shape[1]
    return pl.pallas_call(
        _ada_kernel,
        grid=(n_out // d,),
        in_specs=[
            pl.BlockSpec((rows, d), lambda j: (0, 0)),
            pl.BlockSpec((d, d), lambda j: (0, j)),
            pl.BlockSpec((1, d), lambda j: (0, j)),
        ],
        out_specs=pl.BlockSpec((rows, d), lambda j: (0, j)),
        out_shape=jax.ShapeDtypeStruct((rows, n_out), F32),
        compiler_params=pltpu.CompilerParams(
            dimension_semantics=("arbitrary",), vmem_limit_bytes=VMEM_LIMIT_BYTES),
        name="ada",
    )(c_pad, w_ada, b_ada)


def _mix_kernel(x_ref, mod_ref, win_ref, convw_ref, woc_ref, wpool_ref, pscale_ref, wo_ref,
                ln1g_ref, ln1b_ref, wr_ref, rb_ref, mod_prev_ref,
                x1_ref, h_ref, loc_ref, ewcol_ref, tab_ref, cnt_ref,
                cvbuf, zbuf, s2buf, s4buf, s8buf, base_ref, res_buf, *, tiles_per_seq):
    t_rows, d = x_ref.shape
    pg = d // N_POOL_GROUPS
    step = pl.program_id(0)
    n_tiles = pl.num_programs(0) - 1
    s = jnp.minimum(step, n_tiles - 1) % tiles_per_seq
    shift_bufs = (cvbuf, zbuf, s2buf, s4buf, s8buf)

    @pl.when(step == 0)
    def _():
        base_ref[...] = jnp.zeros_like(base_ref)
        res_buf[...] = jnp.zeros_like(res_buf)

    @pl.when(s == 0)
    def _():
        for buf in shift_bufs:
            buf[0:HALO, :] = jnp.zeros((HALO, buf.shape[1]), F32)

    @pl.when(s > 0)
    def _():
        for buf in shift_bufs:
            buf[0:HALO, :] = buf[t_rows:t_rows + HALO, :]

    def shifted(buf, k):
        return buf[HALO - k:HALO - k + t_rows, :]

    xt = x_ref[...]
    shift1 = mod_ref[0, 0:1, :]
    scale1 = mod_ref[0, 1:2, :]
    gate1 = mod_ref[0, 2:3, :]

    u = (xt * (1.0 + scale1) + shift1).astype(BF16)

    def proj(i):
        return _dot(u, win_ref[:, i * d:(i + 1) * d])

    cv = proj(2) * proj(0)

    x1 = _layer_norm(res_buf[...], ln1g_ref[...], ln1b_ref[...])
    x1_ref[...] = x1
    h = x1 * (1.0 + mod_prev_ref[0, 4:5, :]) + mod_prev_ref[0, 3:4, :]
    h_ref[...] = h.astype(BF16)
    lt = _dot3_wide(h, wr_ref[...]).T + rb_ref[...]
    _route_tile(lt, step > 0, base_ref, loc_ref, ewcol_ref, tab_ref, cnt_ref)

    cvbuf[HALO:HALO + t_rows, :] = cv
    conv = (convw_ref[2:3, :] * cv + convw_ref[1:2, :] * shifted(cvbuf, 1)
            + convw_ref[0:1, :] * shifted(cvbuf, 2))
    y_a = _dot((conv * proj(1)).astype(BF16), woc_ref[...])

    z = proj(3)
    zbuf[HALO:HALO + t_rows, :] = z
    s2 = z + shifted(zbuf, 1)
    s2buf[HALO:HALO + t_rows, :] = s2[:, pg:]
    s4 = s2[:, pg:] + shifted(s2buf, 2)
    s4buf[HALO:HALO + t_rows, :] = s4[:, pg:]
    s8 = s4[:, pg:] + shifted(s4buf, 4)
    s8buf[HALO:HALO + t_rows, :] = s8[:, pg:]
    s16 = s8[:, pg:] + shifted(s8buf, 8)
    win_sums = (s2[:, :pg], s4[:, :pg], s8[:, :pg], s16)
    t_pos = s * t_rows + lax.broadcasted_iota(I32, (t_rows, 1), 0) + 1
    y_b_parts = []
    for gi, w in enumerate(POOL_WINDOWS):
        count = jnp.minimum(t_pos, w).astype(F32)
        pooled = win_sums[gi] / count - z[:, gi * pg:(gi + 1) * pg]
        y_b_parts.append(_dot(pooled.astype(BF16), wpool_ref[gi]))
    y_b = jnp.concatenate(y_b_parts, axis=1) * pscale_ref[...]

    merged = jax.nn.sigmoid(proj(4)) * y_a + jax.nn.sigmoid(proj(5)) * y_b
    mix = _dot(merged.astype(BF16), wo_ref[...])

    res_buf[...] = DEEPNORM_ALPHA * xt + (1.0 + gate1) * mix


def _route_tile(lt, valid, base_ref, loc_ref, ewcol_ref, tab_ref, cnt_ref):
    t_rows = lt.shape[1]
    iota8 = lax.broadcasted_iota(I32, (SUBLANES, t_rows), 0)
    lg = lt[N_EXPERTS:N_EXPERTS + N_GROUPS, :]
    g_max = jnp.max(lg, axis=0, keepdims=True)
    g_top = jnp.min(jnp.where(lg == g_max, iota8, N_GROUPS), axis=0, keepdims=True)
    g_top_p = 1.0 / jnp.sum(jnp.exp(lg - g_max), axis=0, keepdims=True)
    sel = jnp.zeros((EXPERTS_PER_GROUP, t_rows), F32)
    for g in range(N_GROUPS):
        sel = jnp.where(g_top == g, lt[g * EXPERTS_PER_GROUP:(g + 1) * EXPERTS_PER_GROUP, :], sel)
    m1 = jnp.max(sel, axis=0, keepdims=True)
    i1 = jnp.min(jnp.where(sel == m1, iota8, EXPERTS_PER_GROUP), axis=0, keepdims=True)
    sel2 = jnp.where(iota8 == i1, -jnp.inf, sel)
    m2 = jnp.max(sel2, axis=0, keepdims=True)
    i2 = jnp.min(jnp.where(sel2 == m2, iota8, EXPERTS_PER_GROUP), axis=0, keepdims=True)
    ratio = jnp.exp(m2 - m1)
    p1 = 1.0 / (1.0 + ratio)
    w1 = g_top_p * p1
    w2 = g_top_p * (ratio * p1)
    e1 = g_top * EXPERTS_PER_GROUP + i1
    e2 = g_top * EXPERTS_PER_GROUP + i2

    iota_e = lax.broadcasted_iota(I32, (N_EXPERTS, t_rows), 0)
    hit1 = jnp.logical_and(iota_e == e1, valid)
    hit2 = jnp.logical_and(iota_e == e2, valid)
    onehot = jnp.where(jnp.logical_or(hit1, hit2), 1.0, 0.0)
    before = (lax.broadcasted_iota(I32, (t_rows, t_rows), 0)
              < lax.broadcasted_iota(I32, (t_rows, t_rows), 1))
    prefix = _dot(onehot.astype(BF16), jnp.where(before, 1.0, 0.0).astype(BF16))
    tile_cnt = jnp.sum(onehot, axis=1, keepdims=True)
    lower = (lax.broadcasted_iota(I32, (N_EXPERTS, N_EXPERTS), 1)
             < lax.broadcasted_iota(I32, (N_EXPERTS, N_EXPERTS), 0))
    lower = jnp.where(lower, 1.0, 0.0).astype(BF16)
    cnt_hi, cnt_lo = _split_bf16(jnp.broadcast_to(tile_cnt, (N_EXPERTS, LANES)))
    loc_start = (_dot(lower, cnt_hi) + _dot(lower, cnt_lo))[:, 0:1]
    pos = prefix + loc_start
    loc1 = jnp.sum(jnp.where(hit1, pos, 0.0), axis=0, keepdims=True)
    loc2 = jnp.sum(jnp.where(hit2, pos, 0.0), axis=0, keepdims=True)
    base = base_ref[...]
    base_ref[...] = base + tile_cnt
    cnt_ref[...] = jnp.broadcast_to(base + tile_cnt, cnt_ref.shape).astype(I32)

    diag = (lax.broadcasted_iota(I32, (N_EXPERTS, LANES), 0)
            == lax.broadcasted_iota(I32, (N_EXPERTS, LANES), 1))

    def to_lanes(col):
        return jnp.sum(jnp.where(diag, col, 0.0), axis=0, keepdims=True)

    iota8l = lax.broadcasted_iota(I32, (SUBLANES, LANES), 0)
    min_len = jnp.min(jnp.broadcast_to(tile_cnt, (N_EXPERTS, LANES)), axis=0, keepdims=True)
    tab_ref[...] = jnp.where(
        iota8l == TAB_LEN, to_lanes(tile_cnt),
        jnp.where(iota8l == TAB_RANK, to_lanes(base),
                  jnp.where(iota8l == TAB_LOC, to_lanes(loc_start),
                            jnp.where(iota8l == TAB_MIN_LEN, min_len, 0.0)))).astype(I32)

    loc_ref[...] = jnp.where(iota8 == 0, loc1, jnp.where(iota8 == 1, loc2, 0.0)).astype(I32)
    iota_l = lax.broadcasted_iota(I32, (LANES, t_rows), 0)
    ewcol_ref[...] = jnp.where(
        iota_l == 0, w1, jnp.where(iota_l == 1, w2,
                                   jnp.where(iota_l == 2, loc1, jnp.where(iota_l == 3, loc2, 0.0)))).T


def _mix(x2d, mod, w_in, conv_w, w_oc, w_pool, pool_scale, w_o, ln1_g, ln1_b, wr, rb,
         *, batch, seq):
    n, d = x2d.shape
    t = MIX_ROWS
    n_s = seq // t
    n_tiles = batch * n_s
    const = lambda shape: pl.BlockSpec(shape, lambda i: (0,) * len(shape),
                                       pipeline_mode=pl.Buffered(1))
    cur = lambda i: jnp.minimum(i, n_tiles - 1)
    prev = lambda i: jnp.maximum(i - 1, 0)
    pg = d // N_POOL_GROUPS
    return pl.pallas_call(
        functools.partial(_mix_kernel, tiles_per_seq=n_s),
        grid=(n_tiles + 1,),
        in_specs=[
            pl.BlockSpec((t, d), lambda i: (cur(i), 0)),
            pl.BlockSpec((1, N_MOD, d), lambda i: (cur(i) // n_s, 0, 0)),
            const(w_in.shape), const(conv_w.shape), const(w_oc.shape), const(w_pool.shape),
            const(pool_scale.shape), const(w_o.shape), const(ln1_g.shape), const(ln1_b.shape),
            const(wr.shape), const(rb.shape),
            pl.BlockSpec((1, N_MOD, d), lambda i: (prev(i) // n_s, 0, 0)),
        ],
        out_specs=[
            pl.BlockSpec((t, d), lambda i: (prev(i), 0)),
            pl.BlockSpec((t, d), lambda i: (prev(i), 0)),
            pl.BlockSpec((SUBLANES, t), lambda i: (0, prev(i))),
            pl.BlockSpec((t, LANES), lambda i: (prev(i), 0)),
            pl.BlockSpec((SUBLANES, LANES), lambda i: (prev(i), 0)),
            pl.BlockSpec((N_EXPERTS, LANES), lambda i: (0, 0)),
        ],
        out_shape=[
            jax.ShapeDtypeStruct((n, d), F32),
            jax.ShapeDtypeStruct((n, d), BF16),
            jax.ShapeDtypeStruct((SUBLANES, n), I32),
            jax.ShapeDtypeStruct((n, LANES), F32),
            jax.ShapeDtypeStruct((n // t * SUBLANES, LANES), I32),
            jax.ShapeDtypeStruct((N_EXPERTS, LANES), I32),
        ],
        scratch_shapes=[
            pltpu.VMEM((t + HALO, d), F32),
            pltpu.VMEM((t + HALO, d), F32),
            pltpu.VMEM((t + HALO, 3 * pg), F32),
            pltpu.VMEM((t + HALO, 2 * pg), F32),
            pltpu.VMEM((t + HALO, pg), F32),
            pltpu.VMEM((N_EXPERTS, 1), F32),
            pltpu.VMEM((t, d), F32),
        ],
        compiler_params=pltpu.CompilerParams(
            dimension_semantics=("arbitrary",), vmem_limit_bytes=VMEM_LIMIT_BYTES),
        name="mix",
    )(x2d, mod, w_in, conv_w, w_oc, w_pool, pool_scale, w_o, ln1_g, ln1_b, wr, rb, mod)


TAB_LEN, TAB_RANK, TAB_LOC, TAB_MIN_LEN = 0, 1, 2, 3
FILL_SEM = 2


def _strip_copies(tab_ref, pstart_ref, local_ref, sorted_hbm, sem, *, to_hbm):
    def start(e):
        n_rows = tab_ref[TAB_LEN, e]
        local = _rows(local_ref, tab_ref[TAB_LOC, e], n_rows)
        remote = _rows(sorted_hbm, pstart_ref[e] + tab_ref[TAB_RANK, e], n_rows)
        src, dst = (local, remote) if to_hbm else (remote, local)
        pltpu.make_async_copy(src, dst, sem).start()

    @pl.when(tab_ref[TAB_MIN_LEN, 0] > 0)
    def _():
        for e in range(N_EXPERTS):
            start(e)

    @pl.when(tab_ref[TAB_MIN_LEN, 0] <= 0)
    def _():
        def strip(e, carry):
            @pl.when(tab_ref[TAB_LEN, e] > 0)
            def _():
                start(e)
            return carry

        lax.fori_loop(0, N_EXPERTS, strip, 0)


def _dispatch_kernel(pstart_ref, padrow_ref, padlen_ref, nused_ref, tab_ref, loc_ref, h_ref, xs_hbm,
                     xloc0, xloc1, zero_blk, sem):
    i = pl.program_id(0)
    n_steps = pl.num_programs(0)
    t_rows = h_ref.shape[0]
    n_loc = TOP_K * t_rows
    blk_tok = zero_blk.shape[0]
    n_blocks = xs_hbm.shape[0] // blk_tok
    xlocs = (xloc0, xloc1)

    def zero_fills(act):
        def pad_rows(e, carry):
            @pl.when(padlen_ref[e] > 0)
            def _():
                act(pltpu.make_async_copy(
                    _rows(zero_blk, 0, padlen_ref[e]),
                    _rows(xs_hbm, padrow_ref[e], padlen_ref[e]), sem.at[FILL_SEM]))
            return carry

        def unused_block(b, carry):
            act(pltpu.make_async_copy(
                zero_blk, _rows(xs_hbm, b * blk_tok, blk_tok), sem.at[FILL_SEM]))
            return carry

        lax.fori_loop(0, N_EXPERTS, pad_rows, 0)
        lax.fori_loop(nused_ref[0], n_blocks, unused_block, 0)

    @pl.when(i == 0)
    def _():
        zero_blk[...] = jnp.zeros_like(zero_blk)
        zero_fills(lambda copy: copy.start())

    def strips_done(slot):
        pltpu.make_async_copy(xlocs[slot], _rows(xs_hbm, 0, n_loc), sem.at[slot]).wait()

    for slot in range(2):
        @pl.when(i % 2 == slot)
        def _():
            @pl.when(i >= 2)
            def _():
                strips_done(slot)

            iota_p = lax.broadcasted_iota(I32, (n_loc, t_rows), 0)
            pick = jnp.logical_or(iota_p == loc_ref[0:1, :], iota_p == loc_ref[1:2, :])
            xsorted = _dot(jnp.where(pick, 1.0, 0.0).astype(BF16), h_ref[...])
            _pack_rows(xlocs[slot], xsorted, is_bf16_exact=True)
            _strip_copies(tab_ref, pstart_ref, xlocs[slot], xs_hbm, sem.at[slot], to_hbm=True)

    @pl.when(i == n_steps - 1)
    def _():
        for slot in range(2):
            @pl.when(n_steps > slot)
            def _():
                strips_done(slot)
        zero_fills(lambda copy: copy.wait())


def _dispatch(pad_start, pad_row, pad_len, n_used, tab, loc, h, n_blocks):
    n, d = h.shape
    t = MIX_ROWS
    grid_spec = pltpu.PrefetchScalarGridSpec(
        num_scalar_prefetch=4,
        grid=(n // t,),
        in_specs=[
            pl.BlockSpec((SUBLANES, LANES), lambda i, *_: (i, 0), memory_space=pltpu.SMEM),
            pl.BlockSpec((SUBLANES, t), lambda i, *_: (0, i)),
            pl.BlockSpec((t, d), lambda i, *_: (i, 0)),
        ],
        out_specs=pl.BlockSpec(memory_space=pl.ANY),
        scratch_shapes=[
            pltpu.VMEM(_packed_shape(TOP_K * t), I32),
            pltpu.VMEM(_packed_shape(TOP_K * t), I32),
            pltpu.VMEM(_packed_shape(EXPERT_ROWS), I32),
            pltpu.SemaphoreType.DMA((3,)),
        ],
    )
    return pl.pallas_call(
        _dispatch_kernel,
        grid_spec=grid_spec,
        out_shape=jax.ShapeDtypeStruct(_packed_shape(n_blocks * EXPERT_ROWS), I32),
        compiler_params=pltpu.CompilerParams(
            dimension_semantics=("arbitrary",), vmem_limit_bytes=VMEM_LIMIT_BYTES,
            has_side_effects=True),
        name="dispatch",
    )(pad_start, pad_row, pad_len, n_used, tab, loc, h)


IN_SLOTS = 6
AHEAD = 4
OUT_SLOTS = 4


def _expert_kernel(blk0_ref, nblk_ref, wsel_ref, nused_ref, xs_hbm, wg_ref, wu_ref, wd_ref, ys_hbm,
                   xring, yring, wg_bf, wu_bf, wd_bf, sem_in, sem_out):
    del wsel_ref
    e = pl.program_id(0)
    blk_tok = xring.shape[0] // IN_SLOTS
    n_used = nused_ref[0]
    n_blk = nblk_ref[e]

    def in_copy(g):
        slot = g % IN_SLOTS
        return pltpu.make_async_copy(
            _rows(xs_hbm, g * blk_tok, blk_tok), _rows(xring, slot * blk_tok, blk_tok),
            sem_in.at[slot])

    def out_copy(g):
        slot = g % OUT_SLOTS
        return pltpu.make_async_copy(
            _rows(yring, slot * blk_tok, blk_tok), _rows(ys_hbm, g * blk_tok, blk_tok),
            sem_out.at[slot])

    def request(g):
        @pl.when(g < n_used)
        def _():
            in_copy(g).start()

    def retire(g):
        @pl.when(g >= OUT_SLOTS)
        def _():
            out_copy(g - OUT_SLOTS).wait()

    def mlp(*blocks):
        xbs = [_unpack_rows(xring, (g % IN_SLOTS) * blk_tok, blk_tok) for g in blocks]
        gates = [_dot(xb, wg_bf[...]) for xb in xbs]
        ups = [_dot(xb, wu_bf[...]) for xb in xbs]
        acts = [(gate * jax.nn.sigmoid(gate) * up).astype(BF16) for gate, up in zip(gates, ups)]
        for g, act in zip(blocks, acts):
            _pack_rows(yring, _dot(act, wd_bf[...]), (g % OUT_SLOTS) * blk_tok)

    @pl.when(e == 0)
    def _():
        for g in range(AHEAD):
            request(g)

    @pl.when(n_blk > 0)
    def _():
        wg_bf[...] = wg_ref[0].astype(BF16)
        wu_bf[...] = wu_ref[0].astype(BF16)
        wd_bf[...] = wd_ref[0].astype(BF16)

    def pair(p, carry):
        g = blk0_ref[e] + 2 * p
        in_copy(g).wait()
        in_copy(g + 1).wait()
        request(g + AHEAD)
        request(g + AHEAD + 1)
        retire(g)
        retire(g + 1)
        mlp(g, g + 1)
        out_copy(g).start()
        out_copy(g + 1).start()
        return carry

    lax.fori_loop(0, n_blk // 2, pair, 0)

    @pl.when(n_blk % 2 == 1)
    def _():
        g = blk0_ref[e] + n_blk - 1
        in_copy(g).wait()
        request(g + AHEAD)
        retire(g)
        mlp(g)
        out_copy(g).start()

    @pl.when(e == pl.num_programs(0) - 1)
    def _():
        for k in range(1, OUT_SLOTS + 1):
            @pl.when(n_used >= k)
            def _():
                out_copy(n_used - k).wait()


def _experts(blk0, nblk, wsel, n_used, xs, w_gate, w_up, w_down):
    bm = EXPERT_ROWS
    n_exp, d, de = w_gate.shape
    wblk = lambda e, b0, nb, ws, nu: (ws[e], 0, 0)
    grid_spec = pltpu.PrefetchScalarGridSpec(
        num_scalar_prefetch=4,
        grid=(n_exp,),
        in_specs=[
            pl.BlockSpec(memory_space=pl.ANY),
            pl.BlockSpec((1, d, de), wblk),
            pl.BlockSpec((1, d, de), wblk),
            pl.BlockSpec((1, de, d), wblk),
        ],
        out_specs=pl.BlockSpec(memory_space=pl.ANY),
        scratch_shapes=[
            pltpu.VMEM(_packed_shape(IN_SLOTS * bm), I32),
            pltpu.VMEM(_packed_shape(OUT_SLOTS * bm), I32),
            pltpu.VMEM((d, de), BF16),
            pltpu.VMEM((d, de), BF16),
            pltpu.VMEM((de, d), BF16),
            pltpu.SemaphoreType.DMA((IN_SLOTS,)),
            pltpu.SemaphoreType.DMA((OUT_SLOTS,)),
        ],
    )
    n_prefetch = 4
    return pl.pallas_call(
        _expert_kernel,
        grid_spec=grid_spec,
        out_shape=jax.ShapeDtypeStruct(xs.shape, xs.dtype),
        input_output_aliases={n_prefetch: 0},
        compiler_params=pltpu.CompilerParams(
            dimension_semantics=("arbitrary",), vmem_limit_bytes=VMEM_LIMIT_BYTES,
            has_side_effects=True),
        name="experts",
    )(blk0, nblk, wsel, n_used, xs, w_gate, w_up, w_down)


def _combine_kernel(pstart_ref, tab_ref, tab_next_ref, x1_ref, mod_ref, ln2g_ref, ln2b_ref,
                    ewcol_ref, ys_hbm, out_ref, ybuf0, ybuf1, sem):
    i = pl.program_id(0)
    t_rows = x1_ref.shape[0]
    n_loc = TOP_K * t_rows
    ybufs = (ybuf0, ybuf1)

    @pl.when(i == 0)
    def _():
        _strip_copies(tab_ref, pstart_ref, ybuf0, ys_hbm, sem.at[0], to_hbm=False)

    for slot in range(2):
        @pl.when(i % 2 == slot)
        def _():
            @pl.when(i + 1 < pl.num_programs(0))
            def _():
                _strip_copies(tab_next_ref, pstart_ref, ybufs[1 - slot], ys_hbm,
                              sem.at[1 - slot], to_hbm=False)

            pltpu.make_async_copy(_rows(ys_hbm, 0, n_loc), ybufs[slot], sem.at[slot]).wait()
            ew = ewcol_ref[...]
            lane = lax.broadcasted_iota(I32, (t_rows, n_loc), 1)
            weights = jnp.where(lane == ew[:, 2:3].astype(I32), ew[:, 0:1],
                                jnp.where(lane == ew[:, 3:4].astype(I32), ew[:, 1:2], 0.0))
            y = _dot(weights.astype(BF16), _unpack_rows(ybufs[slot]))
            gate2 = mod_ref[0, 5:6, :]
            out_ref[...] = _layer_norm(DEEPNORM_ALPHA * x1_ref[...] + (1.0 + gate2) * y,
                                       ln2g_ref[...], ln2b_ref[...])


def _combine(pad_start, tab, x1, mod, ln2_g, ln2_b, ewcol, ys, *, seq):
    n, d = x1.shape
    t = MIX_ROWS
    per_seq = seq // t
    n_steps = n // t
    row_blk = lambda i, ps: (i, 0)
    grid_spec = pltpu.PrefetchScalarGridSpec(
        num_scalar_prefetch=1,
        grid=(n_steps,),
        in_specs=[
            pl.BlockSpec((SUBLANES, LANES), row_blk, memory_space=pltpu.SMEM),
            pl.BlockSpec((SUBLANES, LANES), lambda i, ps: (jnp.minimum(i + 1, n_steps - 1), 0),
                         memory_space=pltpu.SMEM),
            pl.BlockSpec((t, d), row_blk),
            pl.BlockSpec((1, N_MOD, d), lambda i, ps: (i // per_seq, 0, 0)),
            pl.BlockSpec((1, d), lambda i, ps: (0, 0)),
            pl.BlockSpec((1, d), lambda i, ps: (0, 0)),
            pl.BlockSpec((t, LANES), row_blk),
            pl.BlockSpec(memory_space=pl.ANY),
        ],
        out_specs=pl.BlockSpec((t, d), row_blk),
        scratch_shapes=[
            pltpu.VMEM(_packed_shape(TOP_K * t), I32),
            pltpu.VMEM(_packed_shape(TOP_K * t), I32),
            pltpu.SemaphoreType.DMA((2,)),
        ],
    )
    return pl.pallas_call(
        _combine_kernel,
        grid_spec=grid_spec,
        out_shape=jax.ShapeDtypeStruct((n, d), F32),
        compiler_params=pltpu.CompilerParams(
            dimension_semantics=("arbitrary",), vmem_limit_bytes=VMEM_LIMIT_BYTES),
        name="combine",
    )(pad_start, tab, tab, x1, mod, ln2_g, ln2_b, ewcol, ys)


def _block_layout(counts, n_assign):
    bm = EXPERT_ROWS
    n_blocks = -(-n_assign // bm) + N_EXPERTS
    padded = (counts + bm - 1) // bm * bm
    pad_end = jnp.cumsum(padded)
    pad_start = (pad_end - padded).astype(I32)
    n_used = (pad_end[-1] // bm).astype(I32)
    pad_row = (pad_start + counts).astype(I32)
    pad_len = (padded - counts).astype(I32)
    ids = jnp.arange(N_EXPERTS, dtype=I32)
    prev_used = lax.cummax(jnp.where(padded > 0, ids, -1))
    first_used = jnp.argmax(padded > 0).astype(I32)
    weight_sel = jnp.where(prev_used >= 0, prev_used, first_used).astype(I32)
    return (pad_start, pad_row, pad_len, pad_start // bm, (padded // bm).astype(I32), weight_sel,
            n_used.reshape(1), n_blocks)


def kernel(x, c, w_ada, b_ada, w_in, conv_w, w_out_conv, w_pool, pool_scale, w_o, ln1_g, ln1_b,
           w_group, b_group, w_router, b_router, w_gate, w_up, w_down, ln2_g, ln2_b):
    bsz, seq, d = x.shape
    n = bsz * seq
    assert d == 2 * PACK_SUBLANES * LANES and seq % MIX_ROWS == 0 and w_ada.shape[0] == DEPTH
    x2d = x.reshape(n, d)
    c_pad = jnp.pad(c, ((0, -bsz % SUBLANES), (0, 0)))
    for l in range(DEPTH):
        mod = _ada(c_pad, w_ada[l], b_ada[l][None, :])[:bsz].reshape(bsz, N_MOD, d)
        wr = jnp.concatenate(
            [w_router[l], w_group[l],
             jnp.zeros((d, LANES - N_EXPERTS - N_GROUPS), F32)], axis=1)
        rb = jnp.concatenate(
            [b_router[l].reshape(-1), b_group[l],
             jnp.zeros((LANES - N_EXPERTS - N_GROUPS,), F32)])[:, None]
        x1, h, loc, ewcol, tab, cnt = _mix(
            x2d, mod, w_in[l].astype(BF16), conv_w[l], w_out_conv[l].astype(BF16),
            w_pool[l].astype(BF16), pool_scale[l][None, :], w_o[l].astype(BF16),
            ln1_g[l][None, :], ln1_b[l][None, :], wr, rb, batch=bsz, seq=seq)
        pad_start, pad_row, pad_len, blk0, nblk, weight_sel, n_used, n_blocks = _block_layout(
            cnt[:, 0], n * TOP_K)
        xs = _dispatch(pad_start, pad_row, pad_len, n_used, tab, loc, h, n_blocks)
        ys = _experts(blk0, nblk, weight_sel, n_used, xs, w_gate[l], w_up[l], w_down[l])
        x2d = _combine(pad_start, tab, x1, mod, ln2_g[l][None, :], ln2_b[l][None, :],
                       ewcol, ys, seq=seq)
    return x2d.reshape(bsz, seq, d)
```

```python
import functools

import jax
import jax.numpy as jnp
from jax import lax
from jax.experimental import pallas as pl
from jax.experimental.pallas import tpu as pltpu

F32 = jnp.float32
BF16 = jnp.bfloat16
I32 = jnp.int32

LANES = 128
SUBLANES = 8
VMEM_LIMIT_BYTES = 56 * 1024 * 1024

N_POOL_GROUPS = 4
POOL_WINDOWS = (2, 4, 8, 16)
CONV_WIDTH = 3
N_GROUPS = 8
EXPERTS_PER_GROUP = 8
N_EXPERTS = N_GROUPS * EXPERTS_PER_GROUP
TOP_K = 2
N_MOD = 6
LN_EPS = 1e-5
DEPTH = 1
DEEPNORM_ALPHA = (2.0 * DEPTH) ** 0.25

MIX_ROWS = 512
EXPERT_ROWS = 256
HALO = SUBLANES
PACK_SUBLANES = 4


def _split_bf16(a):
    hi = a.astype(BF16)
    lo = (a - hi.astype(F32)).astype(BF16)
    return hi, lo


def _dot(a, b):
    return jnp.dot(a, b, preferred_element_type=F32)


def _dot3(a, b):
    a_hi, a_lo = _split_bf16(a)
    b_hi, b_lo = _split_bf16(b)
    return _dot(a_hi, b_hi) + _dot(a_lo, b_hi) + _dot(a_hi, b_lo)


def _dot3_wide(a, b):
    a_hi, a_lo = _split_bf16(a)
    n = b.shape[1]
    b_both = jnp.concatenate(_split_bf16(b), axis=1)
    hi = _dot(a_hi, b_both)
    lo = _dot(a_lo, b_both)
    return hi[:, :n] + hi[:, n:] + lo[:, :n] + lo[:, n:]


def _layer_norm(r, g, b):
    mu = jnp.mean(r, axis=-1, keepdims=True)
    d = r - mu
    var = jnp.mean(d * d, axis=-1, keepdims=True)
    return d * lax.rsqrt(var + LN_EPS) * g + b


def _packed_shape(rows):
    return (rows, PACK_SUBLANES, LANES)


def _plane(ref, s, row0, rows):
    flat = ref.reshape(ref.shape[0] * PACK_SUBLANES, LANES)
    return flat.at[pl.ds(row0 * PACK_SUBLANES + s, rows, stride=PACK_SUBLANES), :]


def _pack_rows(ref, val, row0=0, *, is_bf16_exact=False):
    rows, d = val.shape
    if not is_bf16_exact:
        val = val.astype(BF16).astype(F32)
    bits = lax.bitcast_convert_type(val, I32)
    for s in range(PACK_SUBLANES):
        hi = bits[:, s * LANES:(s + 1) * LANES]
        lo = bits[:, d // 2 + s * LANES:d // 2 + (s + 1) * LANES]
        _plane(ref, s, row0, rows)[...] = jnp.bitwise_or(hi, lax.shift_right_logical(lo, 16))


def _unpack_rows(ref, row0=0, rows=None):
    rows = ref.shape[0] - row0 if rows is None else rows
    his, los = [], []
    for s in range(PACK_SUBLANES):
        word = _plane(ref, s, row0, rows)[...]
        his.append(lax.bitcast_convert_type(jnp.bitwise_and(word, jnp.int32(-65536)), F32))
        los.append(lax.bitcast_convert_type(lax.shift_left(word, 16), F32))
    return jnp.concatenate(his + los, axis=1).astype(BF16)


def _rows(ref, row, n_rows):
    return ref.at[pl.ds(row, n_rows)]


def _ada_kernel(c_ref, w_ref, b_ref, o_ref):
    cc = c_ref[...]
    act = cc * jax.nn.sigmoid(cc)
    o_ref[...] = _dot3(act, w_ref[...]) + b_ref[...]


def _ada(c_pad, w_ada, b_ada):
    rows, d = c_pad.shape
    n_out = w_ada.shape[1]
    return pl.pallas_call(
        _ada_kernel,
        grid=(n_out // d,),
        in_specs=[
            pl.BlockSpec((rows, d), lambda j: (0, 0)),
            pl.BlockSpec((d, d), lambda j: (0, j)),
            pl.BlockSpec((1, d), lambda j: (0, j)),
        ],
        out_specs=pl.BlockSpec((rows, d), lambda j: (0, j)),
        out_shape=jax.ShapeDtypeStruct((rows, n_out), F32),
        compiler_params=pltpu.CompilerParams(
            dimension_semantics=("arbitrary",), vmem_limit_bytes=VMEM_LIMIT_BYTES),
        name="ada",
    )(c_pad, w_ada, b_ada)


def _mix_kernel(x_ref, mod_ref, win_ref, convw_ref, woc_ref, wpool_ref, pscale_ref, wo_ref,
                ln1g_ref, ln1b_ref, wr_ref, rb_ref, mod_prev_ref,
                x1_ref, h_ref, loc_ref, ewcol_ref, tab_ref, cnt_ref,
                cvbuf, zbuf, s2buf, s4buf, s8buf, base_ref, res_buf, *, tiles_per_seq):
    t_rows, d = x_ref.shape
    pg = d // N_POOL_GROUPS
    step = pl.program_id(0)
    n_tiles = pl.num_programs(0) - 1
    s = jnp.minimum(step, n_tiles - 1) % tiles_per_seq
    shift_bufs = (cvbuf, zbuf, s2buf, s4buf, s8buf)

    @pl.when(step == 0)
    def _():
        base_ref[...] = jnp.zeros_like(base_ref)
        res_buf[...] = jnp.zeros_like(res_buf)

    @pl.when(s == 0)
    def _():
        for buf in shift_bufs:
            buf[0:HALO, :] = jnp.zeros((HALO, buf.shape[1]), F32)

    @pl.when(s > 0)
    def _():
        for buf in shift_bufs:
            buf[0:HALO, :] = buf[t_rows:t_rows + HALO, :]

    def shifted(buf, k):
        return buf[HALO - k:HALO - k + t_rows, :]

    xt = x_ref[...]
    shift1 = mod_ref[0, 0:1, :]
    scale1 = mod_ref[0, 1:2, :]
    gate1 = mod_ref[0, 2:3, :]

    u = (xt * (1.0 + scale1) + shift1).astype(BF16)

    def proj(i):
        return _dot(u, win_ref[:, i * d:(i + 1) * d])


    a_c = proj(2)
    a_val = proj(0)
    a_b = proj(1)
    cv = a_c * a_val

    x1 = _layer_norm(res_buf[...], ln1g_ref[...], ln1b_ref[...])
    x1_ref[...] = x1
    h = x1 * (1.0 + mod_prev_ref[0, 4:5, :]) + mod_prev_ref[0, 3:4, :]
    h_ref[...] = h.astype(BF16)
    lt = _dot3_wide(h, wr_ref[...]).T + rb_ref[...]
    _route_tile(lt, step > 0, base_ref, loc_ref, ewcol_ref, tab_ref, cnt_ref)

    z = proj(3)
    cvbuf[HALO:HALO + t_rows, :] = cv
    conv = (convw_ref[2:3, :] * cv + convw_ref[1:2, :] * shifted(cvbuf, 1)
            + convw_ref[0:1, :] * shifted(cvbuf, 2))
    y_a = _dot((conv * a_b).astype(BF16), woc_ref[...])

    zbuf[HALO:HALO + t_rows, :] = z
    s2 = z + shifted(zbuf, 1)
    s2buf[HALO:HALO + t_rows, :] = s2[:, pg:]
    s4 = s2[:, pg:] + shifted(s2buf, 2)
    s4buf[HALO:HALO + t_rows, :] = s4[:, pg:]
    s8 = s4[:, pg:] + shifted(s4buf, 4)
    s8buf[HALO:HALO + t_rows, :] = s8[:, pg:]
    s16 = s8[:, pg:] + shifted(s8buf, 8)
    win_sums = (s2[:, :pg], s4[:, :pg], s8[:, :pg], s16)
    t_pos = s * t_rows + lax.broadcasted_iota(I32, (t_rows, 1), 0) + 1
    g_a = proj(4)
    y_b_parts = []
    for gi, w in enumerate(POOL_WINDOWS):
        count = jnp.minimum(t_pos, w).astype(F32)
        pooled = win_sums[gi] / count - z[:, gi * pg:(gi + 1) * pg]
        y_b_parts.append(_dot(pooled.astype(BF16), wpool_ref[gi]))
    g_b = proj(5)
    y_b = jnp.concatenate(y_b_parts, axis=1) * pscale_ref[...]

    merged = jax.nn.sigmoid(g_a) * y_a + jax.nn.sigmoid(g_b) * y_b
    mix = _dot(merged.astype(BF16), wo_ref[...])

    res_buf[...] = DEEPNORM_ALPHA * xt + (1.0 + gate1) * mix


def _route_tile(lt, valid, base_ref, loc_ref, ewcol_ref, tab_ref, cnt_ref):
    t_rows = lt.shape[1]
    iota8 = lax.broadcasted_iota(I32, (SUBLANES, t_rows), 0)
    lg = lt[N_EXPERTS:N_EXPERTS + N_GROUPS, :]
    g_max = jnp.max(lg, axis=0, keepdims=True)
    g_top = jnp.min(jnp.where(lg == g_max, iota8, N_GROUPS), axis=0, keepdims=True)
    g_top_p = 1.0 / jnp.sum(jnp.exp(lg - g_max), axis=0, keepdims=True)
    sel = jnp.zeros((EXPERTS_PER_GROUP, t_rows), F32)
    for g in range(N_GROUPS):
        sel = jnp.where(g_top == g, lt[g * EXPERTS_PER_GROUP:(g + 1) * EXPERTS_PER_GROUP, :], sel)
    m1 = jnp.max(sel, axis=0, keepdims=True)
    i1 = jnp.min(jnp.where(sel == m1, iota8, EXPERTS_PER_GROUP), axis=0, keepdims=True)
    sel2 = jnp.where(iota8 == i1, -jnp.inf, sel)
    m2 = jnp.max(sel2, axis=0, keepdims=True)
    i2 = jnp.min(jnp.where(sel2 == m2, iota8, EXPERTS_PER_GROUP), axis=0, keepdims=True)
    ratio = jnp.exp(m2 - m1)
    p1 = 1.0 / (1.0 + ratio)
    w1 = g_top_p * p1
    w2 = g_top_p * (ratio * p1)
    e1 = g_top * EXPERTS_PER_GROUP + i1
    e2 = g_top * EXPERTS_PER_GROUP + i2

    iota_e = lax.broadcasted_iota(I32, (N_EXPERTS, t_rows), 0)
    hit1 = jnp.logical_and(iota_e == e1, valid)
    hit2 = jnp.logical_and(iota_e == e2, valid)
    onehot = jnp.where(jnp.logical_or(hit1, hit2), 1.0, 0.0)
    before = (lax.broadcasted_iota(I32, (t_rows, t_rows), 0)
              < lax.broadcasted_iota(I32, (t_rows, t_rows), 1))
    prefix = _dot(onehot.astype(BF16), jnp.where(before, 1.0, 0.0).astype(BF16))
    tile_cnt = jnp.sum(onehot, axis=1, keepdims=True)
    lower = (lax.broadcasted_iota(I32, (N_EXPERTS, N_EXPERTS), 1)
             < lax.broadcasted_iota(I32, (N_EXPERTS, N_EXPERTS), 0))
    lower = jnp.where(lower, 1.0, 0.0).astype(BF16)
    cnt_hi, cnt_lo = _split_bf16(jnp.broadcast_to(tile_cnt, (N_EXPERTS, LANES)))
    loc_start = (_dot(lower, cnt_hi) + _dot(lower, cnt_lo))[:, 0:1]
    pos = prefix + loc_start
    loc1 = jnp.sum(jnp.where(hit1, pos, 0.0), axis=0, keepdims=True)
    loc2 = jnp.sum(jnp.where(hit2, pos, 0.0), axis=0, keepdims=True)
    base = base_ref[...]
    base_ref[...] = base + tile_cnt
    cnt_ref[...] = jnp.broadcast_to(base + tile_cnt, cnt_ref.shape).astype(I32)

    diag = (lax.broadcasted_iota(I32, (N_EXPERTS, LANES), 0)
            == lax.broadcasted_iota(I32, (N_EXPERTS, LANES), 1))

    def to_lanes(col):
        return jnp.sum(jnp.where(diag, col, 0.0), axis=0, keepdims=True)

    iota8l = lax.broadcasted_iota(I32, (SUBLANES, LANES), 0)
    min_len = jnp.min(jnp.broadcast_to(tile_cnt, (N_EXPERTS, LANES)), axis=0, keepdims=True)
    tab_ref[...] = jnp.where(
        iota8l == TAB_LEN, to_lanes(tile_cnt),
        jnp.where(iota8l == TAB_RANK, to_lanes(base),
                  jnp.where(iota8l == TAB_LOC, to_lanes(loc_start),
                            jnp.where(iota8l == TAB_MIN_LEN, min_len, 0.0)))).astype(I32)

    loc_ref[...] = jnp.where(iota8 == 0, loc1, jnp.where(iota8 == 1, loc2, 0.0)).astype(I32)
    iota_l = lax.broadcasted_iota(I32, (LANES, t_rows), 0)
    ewcol_ref[...] = jnp.where(
        iota_l == 0, w1, jnp.where(iota_l == 1, w2,
                                   jnp.where(iota_l == 2, loc1, jnp.where(iota_l == 3, loc2, 0.0)))).T


def _mix(x2d, mod, w_in, conv_w, w_oc, w_pool, pool_scale, w_o, ln1_g, ln1_b, wr, rb,
         *, batch, seq):
    n, d = x2d.shape
    t = MIX_ROWS
    n_s = seq // t
    n_tiles = batch * n_s
    const = lambda shape: pl.BlockSpec(shape, lambda i: (0,) * len(shape),
                                       pipeline_mode=pl.Buffered(1))
    cur = lambda i: jnp.minimum(i, n_tiles - 1)
    prev = lambda i: jnp.maximum(i - 1, 0)
    pg = d // N_POOL_GROUPS
    return pl.pallas_call(
        functools.partial(_mix_kernel, tiles_per_seq=n_s),
        grid=(n_tiles + 1,),
        in_specs=[
            pl.BlockSpec((t, d), lambda i: (cur(i), 0)),
            pl.BlockSpec((1, N_MOD, d), lambda i: (cur(i) // n_s, 0, 0)),
            const(w_in.shape), const(conv_w.shape), const(w_oc.shape), const(w_pool.shape),
            const(pool_scale.shape), const(w_o.shape), const(ln1_g.shape), const(ln1_b.shape),
            const(wr.shape), const(rb.shape),
            pl.BlockSpec((1, N_MOD, d), lambda i: (prev(i) // n_s, 0, 0)),
        ],
        out_specs=[
            pl.BlockSpec((t, d), lambda i: (prev(i), 0)),
            pl.BlockSpec((t, d), lambda i: (prev(i), 0)),
            pl.BlockSpec((SUBLANES, t), lambda i: (0, prev(i))),
            pl.BlockSpec((t, LANES), lambda i: (prev(i), 0)),
            pl.BlockSpec((SUBLANES, LANES), lambda i: (prev(i), 0)),
            pl.BlockSpec((N_EXPERTS, LANES), lambda i: (0, 0)),
        ],
        out_shape=[
            jax.ShapeDtypeStruct((n, d), F32),
            jax.ShapeDtypeStruct((n, d), BF16),
            jax.ShapeDtypeStruct((SUBLANES, n), I32),
            jax.ShapeDtypeStruct((n, LANES), F32),
            jax.ShapeDtypeStruct((n // t * SUBLANES, LANES), I32),
            jax.ShapeDtypeStruct((N_EXPERTS, LANES), I32),
        ],
        scratch_shapes=[
            pltpu.VMEM((t + HALO, d), F32),
            pltpu.VMEM((t + HALO, d), F32),
            pltpu.VMEM((t + HALO, 3 * pg), F32),
            pltpu.VMEM((t + HALO, 2 * pg), F32),
            pltpu.VMEM((t + HALO, pg), F32),
            pltpu.VMEM((N_EXPERTS, 1), F32),
            pltpu.VMEM((t, d), F32),
        ],
        compiler_params=pltpu.CompilerParams(
            dimension_semantics=("arbitrary",), vmem_limit_bytes=VMEM_LIMIT_BYTES),
        name="mix",
    )(x2d, mod, w_in, conv_w, w_oc, w_pool, pool_scale, w_o, ln1_g, ln1_b, wr, rb, mod)


TAB_LEN, TAB_RANK, TAB_LOC, TAB_MIN_LEN = 0, 1, 2, 3
FILL_SEM = 2


def _strip_copies(tab_ref, pstart_ref, local_ref, sorted_hbm, sem, *, to_hbm):
    def start(e):
        n_rows = tab_ref[TAB_LEN, e]
        local = _rows(local_ref, tab_ref[TAB_LOC, e], n_rows)
        remote = _rows(sorted_hbm, pstart_ref[e] + tab_ref[TAB_RANK, e], n_rows)
        src, dst = (local, remote) if to_hbm else (remote, local)
        pltpu.make_async_copy(src, dst, sem).start()

    @pl.when(tab_ref[TAB_MIN_LEN, 0] > 0)
    def _():
        for e in range(N_EXPERTS):
            start(e)

    @pl.when(tab_ref[TAB_MIN_LEN, 0] <= 0)
    def _():
        def strip(e, carry):
            @pl.when(tab_ref[TAB_LEN, e] > 0)
            def _():
                start(e)
            return carry

        lax.fori_loop(0, N_EXPERTS, strip, 0)


def _dispatch_kernel(pstart_ref, padrow_ref, padlen_ref, nused_ref, tab_ref, loc_ref, h_ref, xs_hbm,
                     xloc0, xloc1, zero_blk, sem):
    i = pl.program_id(0)
    n_steps = pl.num_programs(0)
    t_rows = h_ref.shape[0]
    n_loc = TOP_K * t_rows
    blk_tok = zero_blk.shape[0]
    n_blocks = xs_hbm.shape[0] // blk_tok
    xlocs = (xloc0, xloc1)

    def zero_fills(act):
        def pad_rows(e, carry):
            @pl.when(padlen_ref[e] > 0)
            def _():
                act(pltpu.make_async_copy(
                    _rows(zero_blk, 0, padlen_ref[e]),
                    _rows(xs_hbm, padrow_ref[e], padlen_ref[e]), sem.at[FILL_SEM]))
            return carry

        def unused_block(b, carry):
            act(pltpu.make_async_copy(
                zero_blk, _rows(xs_hbm, b * blk_tok, blk_tok), sem.at[FILL_SEM]))
            return carry

        lax.fori_loop(0, N_EXPERTS, pad_rows, 0)
        lax.fori_loop(nused_ref[0], n_blocks, unused_block, 0)

    @pl.when(i == 0)
    def _():
        zero_blk[...] = jnp.zeros_like(zero_blk)
        zero_fills(lambda copy: copy.start())

    def strips_done(slot):
        pltpu.make_async_copy(xlocs[slot], _rows(xs_hbm, 0, n_loc), sem.at[slot]).wait()

    for slot in range(2):
        @pl.when(i % 2 == slot)
        def _():
            @pl.when(i >= 2)
            def _():
                strips_done(slot)

            iota_p = lax.broadcasted_iota(I32, (n_loc, t_rows), 0)
            pick = jnp.logical_or(iota_p == loc_ref[0:1, :], iota_p == loc_ref[1:2, :])
            xsorted = _dot(jnp.where(pick, 1.0, 0.0).astype(BF16), h_ref[...])
            _pack_rows(xlocs[slot], xsorted, is_bf16_exact=True)
            _strip_copies(tab_ref, pstart_ref, xlocs[slot], xs_hbm, sem.at[slot], to_hbm=True)

    @pl.when(i == n_steps - 1)
    def _():
        for slot in range(2):
            @pl.when(n_steps > slot)
            def _():
                strips_done(slot)
        zero_fills(lambda copy: copy.wait())


def _dispatch(pad_start, pad_row, pad_len, n_used, tab, loc, h, n_blocks):
    n, d = h.shape
    t = MIX_ROWS
    grid_spec = pltpu.PrefetchScalarGridSpec(
        num_scalar_prefetch=4,
        grid=(n // t,),
        in_specs=[
            pl.BlockSpec((SUBLANES, LANES), lambda i, *_: (i, 0), memory_space=pltpu.SMEM),
            pl.BlockSpec((SUBLANES, t), lambda i, *_: (0, i)),
            pl.BlockSpec((t, d), lambda i, *_: (i, 0)),
        ],
        out_specs=pl.BlockSpec(memory_space=pl.ANY),
        scratch_shapes=[
            pltpu.VMEM(_packed_shape(TOP_K * t), I32),
            pltpu.VMEM(_packed_shape(TOP_K * t), I32),
            pltpu.VMEM(_packed_shape(EXPERT_ROWS), I32),
            pltpu.SemaphoreType.DMA((3,)),
        ],
    )
    return pl.pallas_call(
        _dispatch_kernel,
        grid_spec=grid_spec,
        out_shape=jax.ShapeDtypeStruct(_packed_shape(n_blocks * EXPERT_ROWS), I32),
        compiler_params=pltpu.CompilerParams(
            dimension_semantics=("arbitrary",), vmem_limit_bytes=VMEM_LIMIT_BYTES,
            has_side_effects=True),
        name="dispatch",
    )(pad_start, pad_row, pad_len, n_used, tab, loc, h)


IN_SLOTS = 6
AHEAD = 4
OUT_SLOTS = 4


def _expert_kernel(blk0_ref, nblk_ref, wsel_ref, nused_ref, xs_hbm, wg_ref, wu_ref, wd_ref, ys_hbm,
                   xring, yring, wg_bf, wu_bf, wd_bf, sem_in, sem_out):
    del wsel_ref
    e = pl.program_id(0)
    blk_tok = xring.shape[0] // IN_SLOTS
    n_used = nused_ref[0]
    n_blk = nblk_ref[e]

    def in_copy(g):
        slot = g % IN_SLOTS
        return pltpu.make_async_copy(
            _rows(xs_hbm, g * blk_tok, blk_tok), _rows(xring, slot * blk_tok, blk_tok),
            sem_in.at[slot])

    def out_copy(g):
        slot = g % OUT_SLOTS
        return pltpu.make_async_copy(
            _rows(yring, slot * blk_tok, blk_tok), _rows(ys_hbm, g * blk_tok, blk_tok),
            sem_out.at[slot])

    def request(g):
        @pl.when(g < n_used)
        def _():
            in_copy(g).start()

    def retire(g):
        @pl.when(g >= OUT_SLOTS)
        def _():
            out_copy(g - OUT_SLOTS).wait()

    def mlp(*blocks):
        xbs = [_unpack_rows(xring, (g % IN_SLOTS) * blk_tok, blk_tok) for g in blocks]
        gates = [_dot(xb, wg_bf[...]) for xb in xbs]
        ups = [_dot(xb, wu_bf[...]) for xb in xbs]
        acts = [(gate * jax.nn.sigmoid(gate) * up).astype(BF16) for gate, up in zip(gates, ups)]
        for g, act in zip(blocks, acts):
            _pack_rows(yring, _dot(act, wd_bf[...]), (g % OUT_SLOTS) * blk_tok)

    @pl.when(e == 0)
    def _():
        for g in range(AHEAD):
            request(g)

    @pl.when(n_blk > 0)
    def _():
        wg_bf[...] = wg_ref[0].astype(BF16)
        wu_bf[...] = wu_ref[0].astype(BF16)
        wd_bf[...] = wd_ref[0].astype(BF16)

    def pair(p, carry):
        g = blk0_ref[e] + 2 * p
        in_copy(g).wait()
        in_copy(g + 1).wait()
        request(g + AHEAD)
        request(g + AHEAD + 1)
        retire(g)
        retire(g + 1)
        mlp(g, g + 1)
        out_copy(g).start()
        out_copy(g + 1).start()
        return carry

    lax.fori_loop(0, n_blk // 2, pair, 0)

    @pl.when(n_blk % 2 == 1)
    def _():
        g = blk0_ref[e] + n_blk - 1
        in_copy(g).wait()
        request(g + AHEAD)
        retire(g)
        mlp(g)
        out_copy(g).start()

    @pl.when(e == pl.num_programs(0) - 1)
    def _():
        for k in range(1, OUT_SLOTS + 1):
            @pl.when(n_used >= k)
            def _():
                out_copy(n_used - k).wait()


def _experts(blk0, nblk, wsel, n_used, xs, w_gate, w_up, w_down):
    bm = EXPERT_ROWS
    n_exp, d, de = w_gate.shape
    wblk = lambda e, b0, nb, ws, nu: (ws[e], 0, 0)
    grid_spec = pltpu.PrefetchScalarGridSpec(
        num_scalar_prefetch=4,
        grid=(n_exp,),
        in_specs=[
            pl.BlockSpec(memory_space=pl.ANY),
            pl.BlockSpec((1, d, de), wblk),
            pl.BlockSpec((1, d, de), wblk),
            pl.BlockSpec((1, de, d), wblk),
        ],
        out_specs=pl.BlockSpec(memory_space=pl.ANY),
        scratch_shapes=[
            pltpu.VMEM(_packed_shape(IN_SLOTS * bm), I32),
            pltpu.VMEM(_packed_shape(OUT_SLOTS * bm), I32),
            pltpu.VMEM((d, de), BF16),
            pltpu.VMEM((d, de), BF16),
            pltpu.VMEM((de, d), BF16),
            pltpu.SemaphoreType.DMA((IN_SLOTS,)),
            pltpu.SemaphoreType.DMA((OUT_SLOTS,)),
        ],
    )
    n_prefetch = 4
    return pl.pallas_call(
        _expert_kernel,
        grid_spec=grid_spec,
        out_shape=jax.ShapeDtypeStruct(xs.shape, xs.dtype),
        input_output_aliases={n_prefetch: 0},
        compiler_params=pltpu.CompilerParams(
            dimension_semantics=("arbitrary",), vmem_limit_bytes=VMEM_LIMIT_BYTES,
            has_side_effects=True),
        name="experts",
    )(blk0, nblk, wsel, n_used, xs, w_gate, w_up, w_down)


def _combine_kernel(pstart_ref, tab_ref, tab_next_ref, x1_ref, mod_ref, ln2g_ref, ln2b_ref,
                    ewcol_ref, ys_hbm, out_ref, ybuf0, ybuf1, sem):
    i = pl.program_id(0)
    t_rows = x1_ref.shape[0]
    n_loc = TOP_K * t_rows
    ybufs = (ybuf0, ybuf1)

    @pl.when(i == 0)
    def _():
        _strip_copies(tab_ref, pstart_ref, ybuf0, ys_hbm, sem.at[0], to_hbm=False)

    for slot in range(2):
        @pl.when(i % 2 == slot)
        def _():
            @pl.when(i + 1 < pl.num_programs(0))
            def _():
                _strip_copies(tab_next_ref, pstart_ref, ybufs[1 - slot], ys_hbm,
                              sem.at[1 - slot], to_hbm=False)

            pltpu.make_async_copy(_rows(ys_hbm, 0, n_loc), ybufs[slot], sem.at[slot]).wait()
            ew = ewcol_ref[...]
            lane = lax.broadcasted_iota(I32, (t_rows, n_loc), 1)
            weights = jnp.where(lane == ew[:, 2:3].astype(I32), ew[:, 0:1],
                                jnp.where(lane == ew[:, 3:4].astype(I32), ew[:, 1:2], 0.0))
            y = _dot(weights.astype(BF16), _unpack_rows(ybufs[slot]))
            gate2 = mod_ref[0, 5:6, :]
            out_ref[...] = _layer_norm(DEEPNORM_ALPHA * x1_ref[...] + (1.0 + gate2) * y,
                                       ln2g_ref[...], ln2b_ref[...])


def _combine(pad_start, tab, x1, mod, ln2_g, ln2_b, ewcol, ys, *, seq):
    n, d = x1.shape
    t = MIX_ROWS
    per_seq = seq // t
    n_steps = n // t
    row_blk = lambda i, ps: (i, 0)
    grid_spec = pltpu.PrefetchScalarGridSpec(
        num_scalar_prefetch=1,
        grid=(n_steps,),
        in_specs=[
            pl.BlockSpec((SUBLANES, LANES), row_blk, memory_space=pltpu.SMEM),
            pl.BlockSpec((SUBLANES, LANES), lambda i, ps: (jnp.minimum(i + 1, n_steps - 1), 0),
                         memory_space=pltpu.SMEM),
            pl.BlockSpec((t, d), row_blk),
            pl.BlockSpec((1, N_MOD, d), lambda i, ps: (i // per_seq, 0, 0)),
            pl.BlockSpec((1, d), lambda i, ps: (0, 0)),
            pl.BlockSpec((1, d), lambda i, ps: (0, 0)),
            pl.BlockSpec((t, LANES), row_blk),
            pl.BlockSpec(memory_space=pl.ANY),
        ],
        out_specs=pl.BlockSpec((t, d), row_blk),
        scratch_shapes=[
            pltpu.VMEM(_packed_shape(TOP_K * t), I32),
            pltpu.VMEM(_packed_shape(TOP_K * t), I32),
            pltpu.SemaphoreType.DMA((2,)),
        ],
    )
    return pl.pallas_call(
        _combine_kernel,
        grid_spec=grid_spec,
        out_shape=jax.ShapeDtypeStruct((n, d), F32),
        compiler_params=pltpu.CompilerParams(
            dimension_semantics=("arbitrary",), vmem_limit_bytes=VMEM_LIMIT_BYTES),
        name="combine",
    )(pad_start, tab, tab, x1, mod, ln2_g, ln2_b, ewcol, ys)


def _block_layout(counts, n_assign):
    bm = EXPERT_ROWS
    n_blocks = -(-n_assign // bm) + N_EXPERTS
    padded = (counts + bm - 1) // bm * bm
    pad_end = jnp.cumsum(padded)
    pad_start = (pad_end - padded).astype(I32)
    n_used = (pad_end[-1] // bm).astype(I32)
    pad_row = (pad_start + counts).astype(I32)
    pad_len = (padded - counts).astype(I32)
    ids = jnp.arange(N_EXPERTS, dtype=I32)
    prev_used = lax.cummax(jnp.where(padded > 0, ids, -1))
    first_used = jnp.argmax(padded > 0).astype(I32)
    weight_sel = jnp.where(prev_used >= 0, prev_used, first_used).astype(I32)
    return (pad_start, pad_row, pad_len, pad_start // bm, (padded // bm).astype(I32), weight_sel,
            n_used.reshape(1), n_blocks)


def kernel(x, c, w_ada, b_ada, w_in, conv_w, w_out_conv, w_pool, pool_scale, w_o, ln1_g, ln1_b,
           w_group, b_group, w_router, b_router, w_gate, w_up, w_down, ln2_g, ln2_b):
    bsz, seq, d = x.shape
    n = bsz * seq
    assert d == 2 * PACK_SUBLANES * LANES and seq % MIX_ROWS == 0 and w_ada.shape[0] == DEPTH
    x2d = x.reshape(n, d)
    c_pad = jnp.pad(c, ((0, -bsz % SUBLANES), (0, 0)))
    for l in range(DEPTH):
        mod = _ada(c_pad, w_ada[l], b_ada[l][None, :])[:bsz].reshape(bsz, N_MOD, d)
        wr = jnp.concatenate(
            [w_router[l], w_group[l],
             jnp.zeros((d, LANES - N_EXPERTS - N_GROUPS), F32)], axis=1)
        rb = jnp.concatenate(
            [b_router[l].reshape(-1), b_group[l],
             jnp.zeros((LANES - N_EXPERTS - N_GROUPS,), F32)])[:, None]
        x1, h, loc, ewcol, tab, cnt = _mix(
            x2d, mod, w_in[l].astype(BF16), conv_w[l], w_out_conv[l].astype(BF16),
            w_pool[l].astype(BF16), pool_scale[l][None, :], w_o[l].astype(BF16),
            ln1_g[l][None, :], ln1_b[l][None, :], wr, rb, batch=bsz, seq=seq)
        pad_start, pad_row, pad_len, blk0, nblk, weight_sel, n_used, n_blocks = _block_layout(
            cnt[:, 0], n * TOP_K)
        xs = _dispatch(pad_start, pad_row, pad_len, n_used, tab, loc, h, n_blocks)
        ys = _experts(blk0, nblk, weight_sel, n_used, xs, w_gate[l], w_up[l], w_down[l])
        x2d = _combine(pad_start, tab, x1, mod, ln2_g[l][None, :], ln2_b[l][None, :],
                       ewcol, ys, seq=seq)
    return x2d.reshape(bsz, seq, d)
```

```python
import functools

import jax
import jax.numpy as jnp
from jax import lax
from jax.experimental import pallas as pl
from jax.experimental.pallas import tpu as pltpu

F32 = jnp.float32
BF16 = jnp.bfloat16
I32 = jnp.int32

LANES = 128
SUBLANES = 8
VMEM_LIMIT_BYTES = 56 * 1024 * 1024

N_POOL_GROUPS = 4
POOL_WINDOWS = (2, 4, 8, 16)
CONV_WIDTH = 3
N_GROUPS = 8
EXPERTS_PER_GROUP = 8
N_EXPERTS = N_GROUPS * EXPERTS_PER_GROUP
TOP_K = 2
N_MOD = 6
LN_EPS = 1e-5
DEPTH = 1
DEEPNORM_ALPHA = (2.0 * DEPTH) ** 0.25

MIX_ROWS = 512
EXPERT_ROWS = 256
HALO = SUBLANES
PACK_SUBLANES = 4


def _split_bf16(a):
    hi = a.astype(BF16)
    lo = (a - hi.astype(F32)).astype(BF16)
    return hi, lo


def _dot(a, b):
    return jnp.dot(a, b, preferred_element_type=F32)


def _dot3(a, b):
    a_hi, a_lo = _split_bf16(a)
    b_hi, b_lo = _split_bf16(b)
    return _dot(a_hi, b_hi) + _dot(a_lo, b_hi) + _dot(a_hi, b_lo)


def _dot3_wide(a, b):
    a_hi, a_lo = _split_bf16(a)
    n = b.shape[1]
    b_both = jnp.concatenate(_split_bf16(b), axis=1)
    hi = _dot(a_hi, b_both)
    lo = _dot(a_lo, b_both)
    return hi[:, :n] + hi[:, n:] + lo[:, :n] + lo[:, n:]


def _layer_norm(r, g, b):
    mu = jnp.mean(r, axis=-1, keepdims=True)
    d = r - mu
    var = jnp.mean(d * d, axis=-1, keepdims=True)
    return d * lax.rsqrt(var + LN_EPS) * g + b


def _packed_shape(rows):
    return (rows, PACK_SUBLANES, LANES)


def _plane(ref, s, row0, rows):
    flat = ref.reshape(ref.shape[0] * PACK_SUBLANES, LANES)
    return flat.at[pl.ds(row0 * PACK_SUBLANES + s, rows, stride=PACK_SUBLANES), :]


def _pack_rows(ref, val, row0=0, *, is_bf16_exact=False):
    rows, d = val.shape
    if not is_bf16_exact:
        val = val.astype(BF16).astype(F32)
    bits = lax.bitcast_convert_type(val, I32)
    for s in range(PACK_SUBLANES):
        hi = bits[:, s * LANES:(s + 1) * LANES]
        lo = bits[:, d // 2 + s * LANES:d // 2 + (s + 1) * LANES]
        _plane(ref, s, row0, rows)[...] = jnp.bitwise_or(hi, lax.shift_right_logical(lo, 16))


def _unpack_rows(ref, row0=0, rows=None):
    rows = ref.shape[0] - row0 if rows is None else rows
    his, los = [], []
    for s in range(PACK_SUBLANES):
        word = _plane(ref, s, row0, rows)[...]
        his.append(lax.bitcast_convert_type(jnp.bitwise_and(word, jnp.int32(-65536)), F32))
        los.append(lax.bitcast_convert_type(lax.shift_left(word, 16), F32))
    return jnp.concatenate(his + los, axis=1).astype(BF16)


def _rows(ref, row, n_rows):
    return ref.at[pl.ds(row, n_rows)]


def _ada_kernel(c_ref, w_ref, b_ref, o_ref):
    cc = c_ref[...]
    act = cc * jax.nn.sigmoid(cc)
    o_ref[...] = _dot3(act, w_ref[...]) + b_ref[...]


def _ada(c_pad, w_ada, b_ada):
    rows, d = c_pad.shape
    n_out = w_ada.shape[1]
    return pl.pallas_call(
        _ada_kernel,
        grid=(n_out // d,),
        in_specs=[
            pl.BlockSpec((rows, d), lambda j: (0, 0)),
            pl.BlockSpec((d, d), lambda j: (0, j)),
            pl.BlockSpec((1, d), lambda j: (0, j)),
        ],
        out_specs=pl.BlockSpec((rows, d), lambda j: (0, j)),
        out_shape=jax.ShapeDtypeStruct((rows, n_out), F32),
        compiler_params=pltpu.CompilerParams(
            dimension_semantics=("arbitrary",), vmem_limit_bytes=VMEM_LIMIT_BYTES),
        name="ada",
    )(c_pad, w_ada, b_ada)


def _mix_kernel(x_ref, mod_ref, win_ref, convw_ref, woc_ref, wpool_ref, pscale_ref, wo_ref,
                ln1g_ref, ln1b_ref, wr_ref, rb_ref, mod_prev_ref,
                x1_ref, h_ref, loc_ref, ewcol_ref, tab_ref, cnt_ref,
                cvbuf, zbuf, s2buf, s4buf, s8buf, base_ref, res_buf, *, tiles_per_seq):
    t_rows, d = x_ref.shape
    pg = d // N_POOL_GROUPS
    step = pl.program_id(0)
    n_tiles = pl.num_programs(0) - 1
    s = jnp.minimum(step, n_tiles - 1) % tiles_per_seq
    shift_bufs = (cvbuf, zbuf, s2buf, s4buf, s8buf)

    @pl.when(step == 0)
    def _():
        base_ref[...] = jnp.zeros_like(base_ref)
        res_buf[...] = jnp.zeros_like(res_buf)

    @pl.when(s == 0)
    def _():
        for buf in shift_bufs:
            buf[0:HALO, :] = jnp.zeros((HALO, buf.shape[1]), F32)

    @pl.when(s > 0)
    def _():
        for buf in shift_bufs:
            buf[0:HALO, :] = buf[t_rows:t_rows + HALO, :]

    def shifted(buf, k):
        return buf[HALO - k:HALO - k + t_rows, :]

    xt = x_ref[...]
    shift1 = mod_ref[0, 0:1, :]
    scale1 = mod_ref[0, 1:2, :]
    gate1 = mod_ref[0, 2:3, :]

    u = (xt * (1.0 + scale1) + shift1).astype(BF16)

    def proj(i):
        return _dot(u, win_ref[:, i * d:(i + 1) * d])


    a_c = proj(2)
    a_val = proj(0)
    a_b = proj(1)
    cv = a_c * a_val

    x1 = _layer_norm(res_buf[...], ln1g_ref[...], ln1b_ref[...])
    x1_ref[...] = x1
    h = x1 * (1.0 + mod_prev_ref[0, 4:5, :]) + mod_prev_ref[0, 3:4, :]
    h_ref[...] = h.astype(BF16)
    lt = _dot3_wide(h, wr_ref[...]).T + rb_ref[...]
    _route_tile(lt, step > 0, base_ref, loc_ref, ewcol_ref, tab_ref, cnt_ref)

    z = proj(3)
    cvbuf[HALO:HALO + t_rows, :] = cv
    conv = (convw_ref[2:3, :] * cv + convw_ref[1:2, :] * shifted(cvbuf, 1)
            + convw_ref[0:1, :] * shifted(cvbuf, 2))
    y_a = _dot((conv * a_b).astype(BF16), woc_ref[...])

    zbuf[HALO:HALO + t_rows, :] = z
    s2 = z + shifted(zbuf, 1)
    s2buf[HALO:HALO + t_rows, :] = s2[:, pg:]
    s4 = s2[:, pg:] + shifted(s2buf, 2)
    s4buf[HALO:HALO + t_rows, :] = s4[:, pg:]
    s8 = s4[:, pg:] + shifted(s4buf, 4)
    s8buf[HALO:HALO + t_rows, :] = s8[:, pg:]
    s16 = s8[:, pg:] + shifted(s8buf, 8)
    win_sums = (s2[:, :pg], s4[:, :pg], s8[:, :pg], s16)
    t_pos = s * t_rows + lax.broadcasted_iota(I32, (t_rows, 1), 0) + 1
    g_a = proj(4)
    y_b_parts = []
    for gi, w in enumerate(POOL_WINDOWS):
        count = jnp.minimum(t_pos, w).astype(F32)
        pooled = win_sums[gi] / count - z[:, gi * pg:(gi + 1) * pg]
        y_b_parts.append(_dot(pooled.astype(BF16), wpool_ref[gi]))
    g_b = proj(5)
    y_b = jnp.concatenate(y_b_parts, axis=1) * pscale_ref[...]

    merged = jax.nn.sigmoid(g_a) * y_a + jax.nn.sigmoid(g_b) * y_b
    mix = _dot(merged.astype(BF16), wo_ref[...])

    res_buf[...] = DEEPNORM_ALPHA * xt + (1.0 + gate1) * mix


def _route_tile(lt, valid, base_ref, loc_ref, ewcol_ref, tab_ref, cnt_ref):
    t_rows = lt.shape[1]
    iota8 = lax.broadcasted_iota(I32, (SUBLANES, t_rows), 0)
    lg = lt[N_EXPERTS:N_EXPERTS + N_GROUPS, :]
    g_max = jnp.max(lg, axis=0, keepdims=True)
    g_top = jnp.min(jnp.where(lg == g_max, iota8, N_GROUPS), axis=0, keepdims=True)
    g_top_p = 1.0 / jnp.sum(jnp.exp(lg - g_max), axis=0, keepdims=True)
    sel = jnp.zeros((EXPERTS_PER_GROUP, t_rows), F32)
    for g in range(N_GROUPS):
        sel = jnp.where(g_top == g, lt[g * EXPERTS_PER_GROUP:(g + 1) * EXPERTS_PER_GROUP, :], sel)
    m1 = jnp.max(sel, axis=0, keepdims=True)
    i1 = jnp.min(jnp.where(sel == m1, iota8, EXPERTS_PER_GROUP), axis=0, keepdims=True)
    sel2 = jnp.where(iota8 == i1, -jnp.inf, sel)
    m2 = jnp.max(sel2, axis=0, keepdims=True)
    i2 = jnp.min(jnp.where(sel2 == m2, iota8, EXPERTS_PER_GROUP), axis=0, keepdims=True)
    ratio = jnp.exp(m2 - m1)
    p1 = 1.0 / (1.0 + ratio)
    w1 = g_top_p * p1
    w2 = g_top_p * (ratio * p1)
    e1 = g_top * EXPERTS_PER_GROUP + i1
    e2 = g_top * EXPERTS_PER_GROUP + i2

    iota_e = lax.broadcasted_iota(I32, (N_EXPERTS, t_rows), 0)
    hit1 = jnp.logical_and(iota_e == e1, valid)
    hit2 = jnp.logical_and(iota_e == e2, valid)
    onehot = jnp.where(jnp.logical_or(hit1, hit2), 1.0, 0.0)
    before = (lax.broadcasted_iota(I32, (t_rows, t_rows), 0)
              < lax.broadcasted_iota(I32, (t_rows, t_rows), 1))
    prefix = _dot(onehot.astype(BF16), jnp.where(before, 1.0, 0.0).astype(BF16))
    tile_cnt = jnp.sum(onehot, axis=1, keepdims=True)
    lower = (lax.broadcasted_iota(I32, (N_EXPERTS, N_EXPERTS), 1)
             < lax.broadcasted_iota(I32, (N_EXPERTS, N_EXPERTS), 0))
    lower = jnp.where(lower, 1.0, 0.0).astype(BF16)
    cnt_hi, cnt_lo = _split_bf16(jnp.broadcast_to(tile_cnt, (N_EXPERTS, LANES)))
    loc_start = (_dot(lower, cnt_hi) + _dot(lower, cnt_lo))[:, 0:1]
    pos = prefix + loc_start
    loc1 = jnp.sum(jnp.where(hit1, pos, 0.0), axis=0, keepdims=True)
    loc2 = jnp.sum(jnp.where(hit2, pos, 0.0), axis=0, keepdims=True)
    base = base_ref[...]
    base_ref[...] = base + tile_cnt
    cnt_ref[...] = jnp.broadcast_to(base + tile_cnt, cnt_ref.shape).astype(I32)

    diag = (lax.broadcasted_iota(I32, (N_EXPERTS, LANES), 0)
            == lax.broadcasted_iota(I32, (N_EXPERTS, LANES), 1))

    def to_lanes(col):
        return jnp.sum(jnp.where(diag, col, 0.0), axis=0, keepdims=True)

    iota8l = lax.broadcasted_iota(I32, (SUBLANES, LANES), 0)
    min_len = jnp.min(jnp.broadcast_to(tile_cnt, (N_EXPERTS, LANES)), axis=0, keepdims=True)
    tab_ref[...] = jnp.where(
        iota8l == TAB_LEN, to_lanes(tile_cnt),
        jnp.where(iota8l == TAB_RANK, to_lanes(base),
                  jnp.where(iota8l == TAB_LOC, to_lanes(loc_start),
                            jnp.where(iota8l == TAB_MIN_LEN, min_len, 0.0)))).astype(I32)

    loc_ref[...] = jnp.where(iota8 == 0, loc1, jnp.where(iota8 == 1, loc2, 0.0)).astype(I32)
    iota_l = lax.broadcasted_iota(I32, (LANES, t_rows), 0)
    ewcol_ref[...] = jnp.where(
        iota_l == 0, w1, jnp.where(iota_l == 1, w2,
                                   jnp.where(iota_l == 2, loc1, jnp.where(iota_l == 3, loc2, 0.0)))).T


def _mix(x2d, mod, w_in, conv_w, w_oc, w_pool, pool_scale, w_o, ln1_g, ln1_b, wr, rb,
         *, batch, seq):
    n, d = x2d.shape
    t = MIX_ROWS
    n_s = seq // t
    n_tiles = batch * n_s
    const = lambda shape: pl.BlockSpec(shape, lambda i: (0,) * len(shape),
                                       pipeline_mode=pl.Buffered(1))
    cur = lambda i: jnp.minimum(i, n_tiles - 1)
    prev = lambda i: jnp.maximum(i - 1, 0)
    pg = d // N_POOL_GROUPS
    return pl.pallas_call(
        functools.partial(_mix_kernel, tiles_per_seq=n_s),
        grid=(n_tiles + 1,),
        in_specs=[
            pl.BlockSpec((t, d), lambda i: (cur(i), 0)),
            pl.BlockSpec((1, N_MOD, d), lambda i: (cur(i) // n_s, 0, 0)),
            const(w_in.shape), const(conv_w.shape), const(w_oc.shape), const(w_pool.shape),
            const(pool_scale.shape), const(w_o.shape), const(ln1_g.shape), const(ln1_b.shape),
            const(wr.shape), const(rb.shape),
            pl.BlockSpec((1, N_MOD, d), lambda i: (prev(i) // n_s, 0, 0)),
        ],
        out_specs=[
            pl.BlockSpec((t, d), lambda i: (prev(i), 0)),
            pl.BlockSpec((t, d), lambda i: (prev(i), 0)),
            pl.BlockSpec((SUBLANES, t), lambda i: (0, prev(i))),
            pl.BlockSpec((t, LANES), lambda i: (prev(i), 0)),
            pl.BlockSpec((SUBLANES, LANES), lambda i: (prev(i), 0)),
            pl.BlockSpec((N_EXPERTS, LANES), lambda i: (0, 0)),
        ],
        out_shape=[
            jax.ShapeDtypeStruct((n, d), F32),
            jax.ShapeDtypeStruct((n, d), BF16),
            jax.ShapeDtypeStruct((SUBLANES, n), I32),
            jax.ShapeDtypeStruct((n, LANES), F32),
            jax.ShapeDtypeStruct((n // t * SUBLANES, LANES), I32),
            jax.ShapeDtypeStruct((N_EXPERTS, LANES), I32),
        ],
        scratch_shapes=[
            pltpu.VMEM((t + HALO, d), F32),
            pltpu.VMEM((t + HALO, d), F32),
            pltpu.VMEM((t + HALO, 3 * pg), F32),
            pltpu.VMEM((t + HALO, 2 * pg), F32),
            pltpu.VMEM((t + HALO, pg), F32),
            pltpu.VMEM((N_EXPERTS, 1), F32),
            pltpu.VMEM((t, d), F32),
        ],
        compiler_params=pltpu.CompilerParams(
            dimension_semantics=("arbitrary",), vmem_limit_bytes=VMEM_LIMIT_BYTES),
        name="mix",
    )(x2d, mod, w_in, conv_w, w_oc, w_pool, pool_scale, w_o, ln1_g, ln1_b, wr, rb, mod)


TAB_LEN, TAB_RANK, TAB_LOC, TAB_MIN_LEN = 0, 1, 2, 3
FILL_SEM = 2


def _strip_copies(tab_ref, pstart_ref, local_ref, sorted_hbm, sem, *, to_hbm):
    def start(e):
        n_rows = tab_ref[TAB_LEN, e]
        local = _rows(local_ref, tab_ref[TAB_LOC, e], n_rows)
        remote = _rows(sorted_hbm, pstart_ref[e] + tab_ref[TAB_RANK, e], n_rows)
        src, dst = (local, remote) if to_hbm else (remote, local)
        pltpu.make_async_copy(src, dst, sem).start()

    @pl.when(tab_ref[TAB_MIN_LEN, 0] > 0)
    def _():
        for e in range(N_EXPERTS):
            start(e)

    @pl.when(tab_ref[TAB_MIN_LEN, 0] <= 0)
    def _():
        def strip(e, carry):
            @pl.when(tab_ref[TAB_LEN, e] > 0)
            def _():
                start(e)
            return carry

        lax.fori_loop(0, N_EXPERTS, strip, 0)


def _dispatch_kernel(pstart_ref, padrow_ref, padlen_ref, nused_ref, tab_ref, loc_ref, h_ref, xs_hbm,
                     xloc0, xloc1, zero_blk, sem):
    i = pl.program_id(0)
    n_steps = pl.num_programs(0)
    t_rows = h_ref.shape[0]
    n_loc = TOP_K * t_rows
    blk_tok = zero_blk.shape[0]
    n_blocks = xs_hbm.shape[0] // blk_tok
    xlocs = (xloc0, xloc1)

    def zero_fills(act):
        def pad_rows(e, carry):
            @pl.when(padlen_ref[e] > 0)
            def _():
                act(pltpu.make_async_copy(
                    _rows(zero_blk, 0, padlen_ref[e]),
                    _rows(xs_hbm, padrow_ref[e], padlen_ref[e]), sem.at[FILL_SEM]))
            return carry

        def unused_block(b, carry):
            act(pltpu.make_async_copy(
                zero_blk, _rows(xs_hbm, b * blk_tok, blk_tok), sem.at[FILL_SEM]))
            return carry

        lax.fori_loop(0, N_EXPERTS, pad_rows, 0)
        lax.fori_loop(nused_ref[0], n_blocks, unused_block, 0)

    @pl.when(i == 0)
    def _():
        zero_blk[...] = jnp.zeros_like(zero_blk)
        zero_fills(lambda copy: copy.start())

    def strips_done(slot):
        pltpu.make_async_copy(xlocs[slot], _rows(xs_hbm, 0, n_loc), sem.at[slot]).wait()

    for slot in range(2):
        @pl.when(i % 2 == slot)
        def _():
            @pl.when(i >= 2)
            def _():
                strips_done(slot)

            iota_p = lax.broadcasted_iota(I32, (n_loc, t_rows), 0)
            pick = jnp.logical_or(iota_p == loc_ref[0:1, :], iota_p == loc_ref[1:2, :])
            xsorted = _dot(jnp.where(pick, 1.0, 0.0).astype(BF16), h_ref[...])
            _pack_rows(xlocs[slot], xsorted, is_bf16_exact=True)
            _strip_copies(tab_ref, pstart_ref, xlocs[slot], xs_hbm, sem.at[slot], to_hbm=True)

    @pl.when(i == n_steps - 1)
    def _():
        for slot in range(2):
            @pl.when(n_steps > slot)
            def _():
                strips_done(slot)
        zero_fills(lambda copy: copy.wait())


def _dispatch(pad_start, pad_row, pad_len, n_used, tab, loc, h, n_blocks):
    n, d = h.shape
    t = MIX_ROWS
    grid_spec = pltpu.PrefetchScalarGridSpec(
        num_scalar_prefetch=4,
        grid=(n // t,),
        in_specs=[
            pl.BlockSpec((SUBLANES, LANES), lambda i, *_: (i, 0), memory_space=pltpu.SMEM),
            pl.BlockSpec((SUBLANES, t), lambda i, *_: (0, i)),
            pl.BlockSpec((t, d), lambda i, *_: (i, 0)),
        ],
        out_specs=pl.BlockSpec(memory_space=pl.ANY),
        scratch_shapes=[
            pltpu.VMEM(_packed_shape(TOP_K * t), I32),
            pltpu.VMEM(_packed_shape(TOP_K * t), I32),
            pltpu.VMEM(_packed_shape(EXPERT_ROWS), I32),
            pltpu.SemaphoreType.DMA((3,)),
        ],
    )
    return pl.pallas_call(
        _dispatch_kernel,
        grid_spec=grid_spec,
        out_shape=jax.ShapeDtypeStruct(_packed_shape(n_blocks * EXPERT_ROWS), I32),
        compiler_params=pltpu.CompilerParams(
            dimension_semantics=("arbitrary",), vmem_limit_bytes=VMEM_LIMIT_BYTES,
            has_side_effects=True),
        name="dispatch",
    )(pad_start, pad_row, pad_len, n_used, tab, loc, h)


IN_SLOTS = 6
AHEAD = 4
OUT_SLOTS = 4


def _expert_kernel(blk0_ref, nblk_ref, wsel_ref, nused_ref, xs_hbm, wg_ref, wu_ref, wd_ref, ys_hbm,
                   xring, yring, wg_bf, wu_bf, wd_bf, sem_in, sem_out):
    del wsel_ref
    e = pl.program_id(0)
    blk_tok = xring.shape[0] // IN_SLOTS
    n_used = nused_ref[0]
    n_blk = nblk_ref[e]

    def in_copy(g):
        slot = g % IN_SLOTS
        return pltpu.make_async_copy(
            _rows(xs_hbm, g * blk_tok, blk_tok), _rows(xring, slot * blk_tok, blk_tok),
            sem_in.at[slot])

    def out_copy(g):
        slot = g % OUT_SLOTS
        return pltpu.make_async_copy(
            _rows(yring, slot * blk_tok, blk_tok), _rows(ys_hbm, g * blk_tok, blk_tok),
            sem_out.at[slot])

    def request(g):
        @pl.when(g < n_used)
        def _():
            in_copy(g).start()

    def retire(g):
        @pl.when(g >= OUT_SLOTS)
        def _():
            out_copy(g - OUT_SLOTS).wait()

    def mlp(*blocks):
        xbs = [_unpack_rows(xring, (g % IN_SLOTS) * blk_tok, blk_tok) for g in blocks]
        gates = [_dot(xb, wg_bf[...]) for xb in xbs]
        ups = [_dot(xb, wu_bf[...]) for xb in xbs]
        acts = [(gate * jax.nn.sigmoid(gate) * up).astype(BF16) for gate, up in zip(gates, ups)]
        for g, act in zip(blocks, acts):
            _pack_rows(yring, _dot(act, wd_bf[...]), (g % OUT_SLOTS) * blk_tok)

    @pl.when(e == 0)
    def _():
        for g in range(AHEAD):
            request(g)

    @pl.when(n_blk > 0)
    def _():
        wg_bf[...] = wg_ref[0].astype(BF16)
        wu_bf[...] = wu_ref[0].astype(BF16)
        wd_bf[...] = wd_ref[0].astype(BF16)

    def pair(p, carry):
        g = blk0_ref[e] + 2 * p
        in_copy(g).wait()
        in_copy(g + 1).wait()
        request(g + AHEAD)
        request(g + AHEAD + 1)
        retire(g)
        retire(g + 1)
        mlp(g, g + 1)
        out_copy(g).start()
        out_copy(g + 1).start()
        return carry

    lax.fori_loop(0, n_blk // 2, pair, 0)

    @pl.when(n_blk % 2 == 1)
    def _():
        g = blk0_ref[e] + n_blk - 1
        in_copy(g).wait()
        request(g + AHEAD)
        retire(g)
        mlp(g)
        out_copy(g).start()

    @pl.when(e == pl.num_programs(0) - 1)
    def _():
        for k in range(1, OUT_SLOTS + 1):
            @pl.when(n_used >= k)
            def _():
                out_copy(n_used - k).wait()


def _experts(blk0, nblk, wsel, n_used, xs, w_gate, w_up, w_down):
    bm = EXPERT_ROWS
    n_exp, d, de = w_gate.shape
    wblk = lambda e, b0, nb, ws, nu: (ws[e], 0, 0)
    grid_spec = pltpu.PrefetchScalarGridSpec(
        num_scalar_prefetch=4,
        grid=(n_exp,),
        in_specs=[
            pl.BlockSpec(memory_space=pl.ANY),
            pl.BlockSpec((1, d, de), wblk),
            pl.BlockSpec((1, d, de), wblk),
            pl.BlockSpec((1, de, d), wblk),
        ],
        out_specs=pl.BlockSpec(memory_space=pl.ANY),
        scratch_shapes=[
            pltpu.VMEM(_packed_shape(IN_SLOTS * bm), I32),
            pltpu.VMEM(_packed_shape(OUT_SLOTS * bm), I32),
            pltpu.VMEM((d, de), BF16),
            pltpu.VMEM((d, de), BF16),
            pltpu.VMEM((de, d), BF16),
            pltpu.SemaphoreType.DMA((IN_SLOTS,)),
            pltpu.SemaphoreType.DMA((OUT_SLOTS,)),
        ],
    )
    n_prefetch = 4
    return pl.pallas_call(
        _expert_kernel,
        grid_spec=grid_spec,
        out_shape=jax.ShapeDtypeStruct(xs.shape, xs.dtype),
        input_output_aliases={n_prefetch: 0},
        compiler_params=pltpu.CompilerParams(
            dimension_semantics=("arbitrary",), vmem_limit_bytes=VMEM_LIMIT_BYTES,
            has_side_effects=True),
        name="experts",
    )(blk0, nblk, wsel, n_used, xs, w_gate, w_up, w_down)


def _combine_kernel(pstart_ref, tab_ref, tab_next_ref, x1_ref, mod_ref, ln2g_ref, ln2b_ref,
                    ewcol_ref, ys_hbm, out_ref, ybuf0, ybuf1, sem):
    i = pl.program_id(0)
    t_rows = x1_ref.shape[0]
    n_loc = TOP_K * t_rows
    ybufs = (ybuf0, ybuf1)

    @pl.when(i == 0)
    def _():
        _strip_copies(tab_ref, pstart_ref, ybuf0, ys_hbm, sem.at[0], to_hbm=False)

    for slot in range(2):
        @pl.when(i % 2 == slot)
        def _():
            @pl.when(i + 1 < pl.num_programs(0))
            def _():
                _strip_copies(tab_next_ref, pstart_ref, ybufs[1 - slot], ys_hbm,
                              sem.at[1 - slot], to_hbm=False)

            pltpu.make_async_copy(_rows(ys_hbm, 0, n_loc), ybufs[slot], sem.at[slot]).wait()
            y_sorted = _unpack_rows(ybufs[slot])
            gate2 = mod_ref[0, 5:6, :]
            half = t_rows // 2
            ys_tok = []
            for r0 in (0, half):
                ew = ewcol_ref[r0:r0 + half, :]
                lane = lax.broadcasted_iota(I32, (half, n_loc), 1)
                weights = jnp.where(lane == ew[:, 2:3].astype(I32), ew[:, 0:1],
                                    jnp.where(lane == ew[:, 3:4].astype(I32), ew[:, 1:2], 0.0))
                ys_tok.append(_dot(weights.astype(BF16), y_sorted))
            for r0, y in zip((0, half), ys_tok):
                out_ref[r0:r0 + half, :] = _layer_norm(
                    DEEPNORM_ALPHA * x1_ref[r0:r0 + half, :] + (1.0 + gate2) * y,
                    ln2g_ref[...], ln2b_ref[...])


def _combine(pad_start, tab, x1, mod, ln2_g, ln2_b, ewcol, ys, *, seq):
    n, d = x1.shape
    t = MIX_ROWS
    per_seq = seq // t
    n_steps = n // t
    row_blk = lambda i, ps: (i, 0)
    grid_spec = pltpu.PrefetchScalarGridSpec(
        num_scalar_prefetch=1,
        grid=(n_steps,),
        in_specs=[
            pl.BlockSpec((SUBLANES, LANES), row_blk, memory_space=pltpu.SMEM),
            pl.BlockSpec((SUBLANES, LANES), lambda i, ps: (jnp.minimum(i + 1, n_steps - 1), 0),
                         memory_space=pltpu.SMEM),
            pl.BlockSpec((t, d), row_blk),
            pl.BlockSpec((1, N_MOD, d), lambda i, ps: (i // per_seq, 0, 0)),
            pl.BlockSpec((1, d), lambda i, ps: (0, 0)),
            pl.BlockSpec((1, d), lambda i, ps: (0, 0)),
            pl.BlockSpec((t, LANES), row_blk),
            pl.BlockSpec(memory_space=pl.ANY),
        ],
        out_specs=pl.BlockSpec((t, d), row_blk),
        scratch_shapes=[
            pltpu.VMEM(_packed_shape(TOP_K * t), I32),
            pltpu.VMEM(_packed_shape(TOP_K * t), I32),
            pltpu.SemaphoreType.DMA((2,)),
        ],
    )
    return pl.pallas_call(
        _combine_kernel,
        grid_spec=grid_spec,
        out_shape=jax.ShapeDtypeStruct((n, d), F32),
        compiler_params=pltpu.CompilerParams(
            dimension_semantics=("arbitrary",), vmem_limit_bytes=VMEM_LIMIT_BYTES),
        name="combine",
    )(pad_start, tab, tab, x1, mod, ln2_g, ln2_b, ewcol, ys)


def _block_layout(counts, n_assign):
    bm = EXPERT_ROWS
    n_blocks = -(-n_assign // bm) + N_EXPERTS
    padded = (counts + bm - 1) // bm * bm
    pad_end = jnp.cumsum(padded)
    pad_start = (pad_end - padded).astype(I32)
    n_used = (pad_end[-1] // bm).astype(I32)
    pad_row = (pad_start + counts).astype(I32)
    pad_len = (padded - counts).astype(I32)
    ids = jnp.arange(N_EXPERTS, dtype=I32)
    prev_used = lax.cummax(jnp.where(padded > 0, ids, -1))
    first_used = jnp.argmax(padded > 0).astype(I32)
    weight_sel = jnp.where(prev_used >= 0, prev_used, first_used).astype(I32)
    return (pad_start, pad_row, pad_len, pad_start // bm, (padded // bm).astype(I32), weight_sel,
            n_used.reshape(1), n_blocks)


def kernel(x, c, w_ada, b_ada, w_in, conv_w, w_out_conv, w_pool, pool_scale, w_o, ln1_g, ln1_b,
           w_group, b_group, w_router, b_router, w_gate, w_up, w_down, ln2_g, ln2_b):
    bsz, seq, d = x.shape
    n = bsz * seq
    assert d == 2 * PACK_SUBLANES * LANES and seq % MIX_ROWS == 0 and w_ada.shape[0] == DEPTH
    x2d = x.reshape(n, d)
    c_pad = jnp.pad(c, ((0, -bsz % SUBLANES), (0, 0)))
    for l in range(DEPTH):
        mod = _ada(c_pad, w_ada[l], b_ada[l][None, :])[:bsz].reshape(bsz, N_MOD, d)
        wr = jnp.concatenate(
            [w_router[l], w_group[l],
             jnp.zeros((d, LANES - N_EXPERTS - N_GROUPS), F32)], axis=1)
        rb = jnp.concatenate(
            [b_router[l].reshape(-1), b_group[l],
             jnp.zeros((LANES - N_EXPERTS - N_GROUPS,), F32)])[:, None]
        x1, h, loc, ewcol, tab, cnt = _mix(
            x2d, mod, w_in[l].astype(BF16), conv_w[l], w_out_conv[l].astype(BF16),
            w_pool[l].astype(BF16), pool_scale[l][None, :], w_o[l].astype(BF16),
            ln1_g[l][None, :], ln1_b[l][None, :], wr, rb, batch=bsz, seq=seq)
        pad_start, pad_row, pad_len, blk0, nblk, weight_sel, n_used, n_blocks = _block_layout(
            cnt[:, 0], n * TOP_K)
        xs = _dispatch(pad_start, pad_row, pad_len, n_used, tab, loc, h, n_blocks)
        ys = _experts(blk0, nblk, weight_sel, n_used, xs, w_gate[l], w_up[l], w_down[l])
        x2d = _combine(pad_start, tab, x1, mod, ln2_g[l][None, :], ln2_b[l][None, :],
                       ewcol, ys, seq=seq)
    return x2d.reshape(bsz, seq, d)
```

```python
import functools

import jax
import jax.numpy as jnp
from jax import lax
from jax.experimental import pallas as pl
from jax.experimental.pallas import tpu as pltpu

F32 = jnp.float32
BF16 = jnp.bfloat16
I32 = jnp.int32

LANES = 128
SUBLANES = 8
VMEM_LIMIT_BYTES = 56 * 1024 * 1024

N_POOL_GROUPS = 4
POOL_WINDOWS = (2, 4, 8, 16)
CONV_WIDTH = 3
N_GROUPS = 8
EXPERTS_PER_GROUP = 8
N_EXPERTS = N_GROUPS * EXPERTS_PER_GROUP
TOP_K = 2
N_MOD = 6
LN_EPS = 1e-5
DEPTH = 1
DEEPNORM_ALPHA = (2.0 * DEPTH) ** 0.25

MIX_ROWS = 512
EXPERT_ROWS = 256
HALO = SUBLANES
PACK_SUBLANES = 4


def _split_bf16(a):
    hi = a.astype(BF16)
    lo = (a - hi.astype(F32)).astype(BF16)
    return hi, lo


def _dot(a, b):
    return jnp.dot(a, b, preferred_element_type=F32)


def _dot3(a, b):
    a_hi, a_lo = _split_bf16(a)
    b_hi, b_lo = _split_bf16(b)
    return _dot(a_hi, b_hi) + _dot(a_lo, b_hi) + _dot(a_hi, b_lo)


def _dot3_wide(a, b):
    a_hi, a_lo = _split_bf16(a)
    n = b.shape[1]
    b_both = jnp.concatenate(_split_bf16(b), axis=1)
    hi = _dot(a_hi, b_both)
    lo = _dot(a_lo, b_both)
    return hi[:, :n] + hi[:, n:] + lo[:, :n] + lo[:, n:]


def _layer_norm(r, g, b):
    mu = jnp.mean(r, axis=-1, keepdims=True)
    d = r - mu
    var = jnp.mean(d * d, axis=-1, keepdims=True)
    return d * lax.rsqrt(var + LN_EPS) * g + b


def _packed_shape(rows):
    return (rows, PACK_SUBLANES, LANES)


def _plane(ref, s, row0, rows):
    flat = ref.reshape(ref.shape[0] * PACK_SUBLANES, LANES)
    return flat.at[pl.ds(row0 * PACK_SUBLANES + s, rows, stride=PACK_SUBLANES), :]


def _pack_rows(ref, val, row0=0, *, is_bf16_exact=False):
    rows, d = val.shape
    if not is_bf16_exact:
        val = val.astype(BF16).astype(F32)
    bits = lax.bitcast_convert_type(val, I32)
    for s in range(PACK_SUBLANES):
        hi = bits[:, s * LANES:(s + 1) * LANES]
        lo = bits[:, d // 2 + s * LANES:d // 2 + (s + 1) * LANES]
        _plane(ref, s, row0, rows)[...] = jnp.bitwise_or(hi, lax.shift_right_logical(lo, 16))


def _unpack_rows(ref, row0=0, rows=None):
    rows = ref.shape[0] - row0 if rows is None else rows
    his, los = [], []
    for s in range(PACK_SUBLANES):
        word = _plane(ref, s, row0, rows)[...]
        his.append(lax.bitcast_convert_type(jnp.bitwise_and(word, jnp.int32(-65536)), F32))
        los.append(lax.bitcast_convert_type(lax.shift_left(word, 16), F32))
    return jnp.concatenate(his + los, axis=1).astype(BF16)


def _rows(ref, row, n_rows):
    return ref.at[pl.ds(row, n_rows)]


def _ada_kernel(c_ref, w_ref, b_ref, o_ref):
    cc = c_ref[...]
    act = cc * jax.nn.sigmoid(cc)
    o_ref[...] = _dot3(act, w_ref[...]) + b_ref[...]


def _ada(c_pad, w_ada, b_ada):
    rows, d = c_pad.shape
    n_out = w_ada.shape[1]
    return pl.pallas_call(
        _ada_kernel,
        grid=(n_out // d,),
        in_specs=[
            pl.BlockSpec((rows, d), lambda j: (0, 0)),
            pl.BlockSpec((d, d), lambda j: (0, j)),
            pl.BlockSpec((1, d), lambda j: (0, j)),
        ],
        out_specs=pl.BlockSpec((rows, d), lambda j: (0, j)),
        out_shape=jax.ShapeDtypeStruct((rows, n_out), F32),
        compiler_params=pltpu.CompilerParams(
            dimension_semantics=("arbitrary",), vmem_limit_bytes=VMEM_LIMIT_BYTES),
        name="ada",
    )(c_pad, w_ada, b_ada)


def _mix_kernel(x_ref, mod_ref, win_ref, convw_ref, woc_ref, wpool_ref, pscale_ref, wo_ref,
                ln1g_ref, ln1b_ref, wr_ref, rb_ref, mod_prev_ref,
                x1_ref, h_ref, loc_ref, ewcol_ref, tab_ref, cnt_ref,
                cvbuf, zbuf, s2buf, s4buf, s8buf, base_ref, res_buf, *, tiles_per_seq):
    t_rows, d = x_ref.shape
    pg = d // N_POOL_GROUPS
    step = pl.program_id(0)
    n_tiles = pl.num_programs(0) - 1
    s = jnp.minimum(step, n_tiles - 1) % tiles_per_seq
    shift_bufs = (cvbuf, zbuf, s2buf, s4buf, s8buf)

    @pl.when(step == 0)
    def _():
        base_ref[...] = jnp.zeros_like(base_ref)
        res_buf[...] = jnp.zeros_like(res_buf)

    @pl.when(s == 0)
    def _():
        for buf in shift_bufs:
            buf[0:HALO, :] = jnp.zeros((HALO, buf.shape[1]), F32)

    @pl.when(s > 0)
    def _():
        for buf in shift_bufs:
            buf[0:HALO, :] = buf[t_rows:t_rows + HALO, :]

    def shifted(buf, k):
        return buf[HALO - k:HALO - k + t_rows, :]

    xt = x_ref[...]
    shift1 = mod_ref[0, 0:1, :]
    scale1 = mod_ref[0, 1:2, :]
    gate1 = mod_ref[0, 2:3, :]

    u = (xt * (1.0 + scale1) + shift1).astype(BF16)

    def proj(i):
        return _dot(u, win_ref[:, i * d:(i + 1) * d])


    a_c = proj(2)
    a_val = proj(0)
    a_b = proj(1)
    cv = a_c * a_val

    x1 = _layer_norm(res_buf[...], ln1g_ref[...], ln1b_ref[...])
    x1_ref[...] = x1.astype(BF16)
    h = x1 * (1.0 + mod_prev_ref[0, 4:5, :]) + mod_prev_ref[0, 3:4, :]
    h_ref[...] = h.astype(BF16)
    lt = _dot3_wide(h, wr_ref[...]).T + rb_ref[...]
    _route_tile(lt, step > 0, base_ref, loc_ref, ewcol_ref, tab_ref, cnt_ref)

    z = proj(3)
    cvbuf[HALO:HALO + t_rows, :] = cv
    conv = (convw_ref[2:3, :] * cv + convw_ref[1:2, :] * shifted(cvbuf, 1)
            + convw_ref[0:1, :] * shifted(cvbuf, 2))
    y_a = _dot((conv * a_b).astype(BF16), woc_ref[...])

    zbuf[HALO:HALO + t_rows, :] = z
    s2 = z + shifted(zbuf, 1)
    s2buf[HALO:HALO + t_rows, :] = s2[:, pg:]
    s4 = s2[:, pg:] + shifted(s2buf, 2)
    s4buf[HALO:HALO + t_rows, :] = s4[:, pg:]
    s8 = s4[:, pg:] + shifted(s4buf, 4)
    s8buf[HALO:HALO + t_rows, :] = s8[:, pg:]
    s16 = s8[:, pg:] + shifted(s8buf, 8)
    win_sums = (s2[:, :pg], s4[:, :pg], s8[:, :pg], s16)
    t_pos = s * t_rows + lax.broadcasted_iota(I32, (t_rows, 1), 0) + 1
    g_a = proj(4)
    y_b_parts = []
    for gi, w in enumerate(POOL_WINDOWS):
        count = jnp.minimum(t_pos, w).astype(F32)
        pooled = win_sums[gi] / count - z[:, gi * pg:(gi + 1) * pg]
        y_b_parts.append(_dot(pooled.astype(BF16), wpool_ref[gi]))
    g_b = proj(5)
    y_b = jnp.concatenate(y_b_parts, axis=1) * pscale_ref[...]

    merged = jax.nn.sigmoid(g_a) * y_a + jax.nn.sigmoid(g_b) * y_b
    mix = _dot(merged.astype(BF16), wo_ref[...])

    res_buf[...] = DEEPNORM_ALPHA * xt + (1.0 + gate1) * mix


def _route_tile(lt, valid, base_ref, loc_ref, ewcol_ref, tab_ref, cnt_ref):
    t_rows = lt.shape[1]
    iota8 = lax.broadcasted_iota(I32, (SUBLANES, t_rows), 0)
    lg = lt[N_EXPERTS:N_EXPERTS + N_GROUPS, :]
    g_max = jnp.max(lg, axis=0, keepdims=True)
    g_top = jnp.min(jnp.where(lg == g_max, iota8, N_GROUPS), axis=0, keepdims=True)
    g_top_p = 1.0 / jnp.sum(jnp.exp(lg - g_max), axis=0, keepdims=True)
    sel = jnp.zeros((EXPERTS_PER_GROUP, t_rows), F32)
    for g in range(N_GROUPS):
        sel = jnp.where(g_top == g, lt[g * EXPERTS_PER_GROUP:(g + 1) * EXPERTS_PER_GROUP, :], sel)
    m1 = jnp.max(sel, axis=0, keepdims=True)
    i1 = jnp.min(jnp.where(sel == m1, iota8, EXPERTS_PER_GROUP), axis=0, keepdims=True)
    sel2 = jnp.where(iota8 == i1, -jnp.inf, sel)
    m2 = jnp.max(sel2, axis=0, keepdims=True)
    i2 = jnp.min(jnp.where(sel2 == m2, iota8, EXPERTS_PER_GROUP), axis=0, keepdims=True)
    ratio = jnp.exp(m2 - m1)
    p1 = 1.0 / (1.0 + ratio)
    w1 = g_top_p * p1
    w2 = g_top_p * (ratio * p1)
    e1 = g_top * EXPERTS_PER_GROUP + i1
    e2 = g_top * EXPERTS_PER_GROUP + i2

    iota_e = lax.broadcasted_iota(I32, (N_EXPERTS, t_rows), 0)
    hit1 = jnp.logical_and(iota_e == e1, valid)
    hit2 = jnp.logical_and(iota_e == e2, valid)
    onehot = jnp.where(jnp.logical_or(hit1, hit2), 1.0, 0.0)
    before = (lax.broadcasted_iota(I32, (t_rows, t_rows), 0)
              < lax.broadcasted_iota(I32, (t_rows, t_rows), 1))
    prefix = _dot(onehot.astype(BF16), jnp.where(before, 1.0, 0.0).astype(BF16))
    tile_cnt = jnp.sum(onehot, axis=1, keepdims=True)
    lower = (lax.broadcasted_iota(I32, (N_EXPERTS, N_EXPERTS), 1)
             < lax.broadcasted_iota(I32, (N_EXPERTS, N_EXPERTS), 0))
    lower = jnp.where(lower, 1.0, 0.0).astype(BF16)
    cnt_hi, cnt_lo = _split_bf16(jnp.broadcast_to(tile_cnt, (N_EXPERTS, LANES)))
    loc_start = (_dot(lower, cnt_hi) + _dot(lower, cnt_lo))[:, 0:1]
    pos = prefix + loc_start
    loc1 = jnp.sum(jnp.where(hit1, pos, 0.0), axis=0, keepdims=True)
    loc2 = jnp.sum(jnp.where(hit2, pos, 0.0), axis=0, keepdims=True)
    base = base_ref[...]
    base_ref[...] = base + tile_cnt
    cnt_ref[...] = jnp.broadcast_to(base + tile_cnt, cnt_ref.shape).astype(I32)

    diag = (lax.broadcasted_iota(I32, (N_EXPERTS, LANES), 0)
            == lax.broadcasted_iota(I32, (N_EXPERTS, LANES), 1))

    def to_lanes(col):
        return jnp.sum(jnp.where(diag, col, 0.0), axis=0, keepdims=True)

    iota8l = lax.broadcasted_iota(I32, (SUBLANES, LANES), 0)
    min_len = jnp.min(jnp.broadcast_to(tile_cnt, (N_EXPERTS, LANES)), axis=0, keepdims=True)
    tab_ref[...] = jnp.where(
        iota8l == TAB_LEN, to_lanes(tile_cnt),
        jnp.where(iota8l == TAB_RANK, to_lanes(base),
                  jnp.where(iota8l == TAB_LOC, to_lanes(loc_start),
                            jnp.where(iota8l == TAB_MIN_LEN, min_len, 0.0)))).astype(I32)

    loc_ref[...] = jnp.where(iota8 == 0, loc1, jnp.where(iota8 == 1, loc2, 0.0)).astype(I32)
    iota_l = lax.broadcasted_iota(I32, (LANES, t_rows), 0)
    ewcol_ref[...] = jnp.where(
        iota_l == 0, w1, jnp.where(iota_l == 1, w2,
                                   jnp.where(iota_l == 2, loc1, jnp.where(iota_l == 3, loc2, 0.0)))).T


def _mix(x2d, mod, w_in, conv_w, w_oc, w_pool, pool_scale, w_o, ln1_g, ln1_b, wr, rb,
         *, batch, seq):
    n, d = x2d.shape
    t = MIX_ROWS
    n_s = seq // t
    n_tiles = batch * n_s
    const = lambda shape: pl.BlockSpec(shape, lambda i: (0,) * len(shape),
                                       pipeline_mode=pl.Buffered(1))
    cur = lambda i: jnp.minimum(i, n_tiles - 1)
    prev = lambda i: jnp.maximum(i - 1, 0)
    pg = d // N_POOL_GROUPS
    return pl.pallas_call(
        functools.partial(_mix_kernel, tiles_per_seq=n_s),
        grid=(n_tiles + 1,),
        in_specs=[
            pl.BlockSpec((t, d), lambda i: (cur(i), 0)),
            pl.BlockSpec((1, N_MOD, d), lambda i: (cur(i) // n_s, 0, 0)),
            const(w_in.shape), const(conv_w.shape), const(w_oc.shape), const(w_pool.shape),
            const(pool_scale.shape), const(w_o.shape), const(ln1_g.shape), const(ln1_b.shape),
            const(wr.shape), const(rb.shape),
            pl.BlockSpec((1, N_MOD, d), lambda i: (prev(i) // n_s, 0, 0)),
        ],
        out_specs=[
            pl.BlockSpec((t, d), lambda i: (prev(i), 0)),
            pl.BlockSpec((t, d), lambda i: (prev(i), 0)),
            pl.BlockSpec((SUBLANES, t), lambda i: (0, prev(i))),
            pl.BlockSpec((t, LANES), lambda i: (prev(i), 0)),
            pl.BlockSpec((SUBLANES, LANES), lambda i: (prev(i), 0)),
            pl.BlockSpec((N_EXPERTS, LANES), lambda i: (0, 0)),
        ],
        out_shape=[
            jax.ShapeDtypeStruct((n, d), BF16),
            jax.ShapeDtypeStruct((n, d), BF16),
            jax.ShapeDtypeStruct((SUBLANES, n), I32),
            jax.ShapeDtypeStruct((n, LANES), F32),
            jax.ShapeDtypeStruct((n // t * SUBLANES, LANES), I32),
            jax.ShapeDtypeStruct((N_EXPERTS, LANES), I32),
        ],
        scratch_shapes=[
            pltpu.VMEM((t + HALO, d), F32),
            pltpu.VMEM((t + HALO, d), F32),
            pltpu.VMEM((t + HALO, 3 * pg), F32),
            pltpu.VMEM((t + HALO, 2 * pg), F32),
            pltpu.VMEM((t + HALO, pg), F32),
            pltpu.VMEM((N_EXPERTS, 1), F32),
            pltpu.VMEM((t, d), F32),
        ],
        compiler_params=pltpu.CompilerParams(
            dimension_semantics=("arbitrary",), vmem_limit_bytes=VMEM_LIMIT_BYTES),
        name="mix",
    )(x2d, mod, w_in, conv_w, w_oc, w_pool, pool_scale, w_o, ln1_g, ln1_b, wr, rb, mod)


TAB_LEN, TAB_RANK, TAB_LOC, TAB_MIN_LEN = 0, 1, 2, 3
FILL_SEM = 2


def _strip_copies(tab_ref, pstart_ref, local_ref, sorted_hbm, sem, *, to_hbm):
    def start(e):
        n_rows = tab_ref[TAB_LEN, e]
        local = _rows(local_ref, tab_ref[TAB_LOC, e], n_rows)
        remote = _rows(sorted_hbm, pstart_ref[e] + tab_ref[TAB_RANK, e], n_rows)
        src, dst = (local, remote) if to_hbm else (remote, local)
        pltpu.make_async_copy(src, dst, sem).start()

    @pl.when(tab_ref[TAB_MIN_LEN, 0] > 0)
    def _():
        for e in range(N_EXPERTS):
            start(e)

    @pl.when(tab_ref[TAB_MIN_LEN, 0] <= 0)
    def _():
        def strip(e, carry):
            @pl.when(tab_ref[TAB_LEN, e] > 0)
            def _():
                start(e)
            return carry

        lax.fori_loop(0, N_EXPERTS, strip, 0)


def _dispatch_kernel(pstart_ref, padrow_ref, padlen_ref, nused_ref, tab_ref, loc_ref, h_ref, xs_hbm,
                     xloc0, xloc1, zero_blk, sem):
    i = pl.program_id(0)
    n_steps = pl.num_programs(0)
    t_rows = h_ref.shape[0]
    n_loc = TOP_K * t_rows
    blk_tok = zero_blk.shape[0]
    n_blocks = xs_hbm.shape[0] // blk_tok
    xlocs = (xloc0, xloc1)

    def zero_fills(act):
        def pad_rows(e, carry):
            @pl.when(padlen_ref[e] > 0)
            def _():
                act(pltpu.make_async_copy(
                    _rows(zero_blk, 0, padlen_ref[e]),
                    _rows(xs_hbm, padrow_ref[e], padlen_ref[e]), sem.at[FILL_SEM]))
            return carry

        def unused_block(b, carry):
            act(pltpu.make_async_copy(
                zero_blk, _rows(xs_hbm, b * blk_tok, blk_tok), sem.at[FILL_SEM]))
            return carry

        lax.fori_loop(0, N_EXPERTS, pad_rows, 0)
        lax.fori_loop(nused_ref[0], n_blocks, unused_block, 0)

    @pl.when(i == 0)
    def _():
        zero_blk[...] = jnp.zeros_like(zero_blk)
        zero_fills(lambda copy: copy.start())

    def strips_done(slot):
        pltpu.make_async_copy(xlocs[slot], _rows(xs_hbm, 0, n_loc), sem.at[slot]).wait()

    for slot in range(2):
        @pl.when(i % 2 == slot)
        def _():
            @pl.when(i >= 2)
            def _():
                strips_done(slot)

            iota_p = lax.broadcasted_iota(I32, (n_loc, t_rows), 0)
            pick = jnp.logical_or(iota_p == loc_ref[0:1, :], iota_p == loc_ref[1:2, :])
            xsorted = _dot(jnp.where(pick, 1.0, 0.0).astype(BF16), h_ref[...])
            _pack_rows(xlocs[slot], xsorted, is_bf16_exact=True)
            _strip_copies(tab_ref, pstart_ref, xlocs[slot], xs_hbm, sem.at[slot], to_hbm=True)

    @pl.when(i == n_steps - 1)
    def _():
        for slot in range(2):
            @pl.when(n_steps > slot)
            def _():
                strips_done(slot)
        zero_fills(lambda copy: copy.wait())


def _dispatch(pad_start, pad_row, pad_len, n_used, tab, loc, h, n_blocks):
    n, d = h.shape
    t = MIX_ROWS
    grid_spec = pltpu.PrefetchScalarGridSpec(
        num_scalar_prefetch=4,
        grid=(n // t,),
        in_specs=[
            pl.BlockSpec((SUBLANES, LANES), lambda i, *_: (i, 0), memory_space=pltpu.SMEM),
            pl.BlockSpec((SUBLANES, t), lambda i, *_: (0, i)),
            pl.BlockSpec((t, d), lambda i, *_: (i, 0)),
        ],
        out_specs=pl.BlockSpec(memory_space=pl.ANY),
        scratch_shapes=[
            pltpu.VMEM(_packed_shape(TOP_K * t), I32),
            pltpu.VMEM(_packed_shape(TOP_K * t), I32),
            pltpu.VMEM(_packed_shape(EXPERT_ROWS), I32),
            pltpu.SemaphoreType.DMA((3,)),
        ],
    )
    return pl.pallas_call(
        _dispatch_kernel,
        grid_spec=grid_spec,
        out_shape=jax.ShapeDtypeStruct(_packed_shape(n_blocks * EXPERT_ROWS), I32),
        compiler_params=pltpu.CompilerParams(
            dimension_semantics=("arbitrary",), vmem_limit_bytes=VMEM_LIMIT_BYTES,
            has_side_effects=True),
        name="dispatch",
    )(pad_start, pad_row, pad_len, n_used, tab, loc, h)


IN_SLOTS = 6
AHEAD = 4
OUT_SLOTS = 4


def _expert_kernel(blk0_ref, nblk_ref, wsel_ref, nused_ref, xs_hbm, wg_ref, wu_ref, wd_ref, ys_hbm,
                   xring, yring, wg_bf, wu_bf, wd_bf, sem_in, sem_out):
    del wsel_ref
    e = pl.program_id(0)
    blk_tok = xring.shape[0] // IN_SLOTS
    n_used = nused_ref[0]
    n_blk = nblk_ref[e]

    def in_copy(g):
        slot = g % IN_SLOTS
        return pltpu.make_async_copy(
            _rows(xs_hbm, g * blk_tok, blk_tok), _rows(xring, slot * blk_tok, blk_tok),
            sem_in.at[slot])

    def out_copy(g):
        slot = g % OUT_SLOTS
        return pltpu.make_async_copy(
            _rows(yring, slot * blk_tok, blk_tok), _rows(ys_hbm, g * blk_tok, blk_tok),
            sem_out.at[slot])

    def request(g):
        @pl.when(g < n_used)
        def _():
            in_copy(g).start()

    def retire(g):
        @pl.when(g >= OUT_SLOTS)
        def _():
            out_copy(g - OUT_SLOTS).wait()

    def mlp(*blocks):
        xbs = [_unpack_rows(xring, (g % IN_SLOTS) * blk_tok, blk_tok) for g in blocks]
        gates = [_dot(xb, wg_bf[...]) for xb in xbs]
        ups = [_dot(xb, wu_bf[...]) for xb in xbs]
        acts = [(gate * jax.nn.sigmoid(gate) * up).astype(BF16) for gate, up in zip(gates, ups)]
        for g, act in zip(blocks, acts):
            _pack_rows(yring, _dot(act, wd_bf[...]), (g % OUT_SLOTS) * blk_tok)

    @pl.when(e == 0)
    def _():
        for g in range(AHEAD):
            request(g)

    @pl.when(n_blk > 0)
    def _():
        wg_bf[...] = wg_ref[0].astype(BF16)
        wu_bf[...] = wu_ref[0].astype(BF16)
        wd_bf[...] = wd_ref[0].astype(BF16)

    def pair(p, carry):
        g = blk0_ref[e] + 2 * p
        in_copy(g).wait()
        in_copy(g + 1).wait()
        request(g + AHEAD)
        request(g + AHEAD + 1)
        retire(g)
        retire(g + 1)
        mlp(g, g + 1)
        out_copy(g).start()
        out_copy(g + 1).start()
        return carry

    lax.fori_loop(0, n_blk // 2, pair, 0)

    @pl.when(n_blk % 2 == 1)
    def _():
        g = blk0_ref[e] + n_blk - 1
        in_copy(g).wait()
        request(g + AHEAD)
        retire(g)
        mlp(g)
        out_copy(g).start()

    @pl.when(e == pl.num_programs(0) - 1)
    def _():
        for k in range(1, OUT_SLOTS + 1):
            @pl.when(n_used >= k)
            def _():
                out_copy(n_used - k).wait()


def _experts(blk0, nblk, wsel, n_used, xs, w_gate, w_up, w_down):
    bm = EXPERT_ROWS
    n_exp, d, de = w_gate.shape
    wblk = lambda e, b0, nb, ws, nu: (ws[e], 0, 0)
    grid_spec = pltpu.PrefetchScalarGridSpec(
        num_scalar_prefetch=4,
        grid=(n_exp,),
        in_specs=[
            pl.BlockSpec(memory_space=pl.ANY),
            pl.BlockSpec((1, d, de), wblk),
            pl.BlockSpec((1, d, de), wblk),
            pl.BlockSpec((1, de, d), wblk),
        ],
        out_specs=pl.BlockSpec(memory_space=pl.ANY),
        scratch_shapes=[
            pltpu.VMEM(_packed_shape(IN_SLOTS * bm), I32),
            pltpu.VMEM(_packed_shape(OUT_SLOTS * bm), I32),
            pltpu.VMEM((d, de), BF16),
            pltpu.VMEM((d, de), BF16),
            pltpu.VMEM((de, d), BF16),
            pltpu.SemaphoreType.DMA((IN_SLOTS,)),
            pltpu.SemaphoreType.DMA((OUT_SLOTS,)),
        ],
    )
    n_prefetch = 4
    return pl.pallas_call(
        _expert_kernel,
        grid_spec=grid_spec,
        out_shape=jax.ShapeDtypeStruct(xs.shape, xs.dtype),
        input_output_aliases={n_prefetch: 0},
        compiler_params=pltpu.CompilerParams(
            dimension_semantics=("arbitrary",), vmem_limit_bytes=VMEM_LIMIT_BYTES,
            has_side_effects=True),
        name="experts",
    )(blk0, nblk, wsel, n_used, xs, w_gate, w_up, w_down)


def _combine_kernel(pstart_ref, tab_ref, tab_next_ref, x1_ref, mod_ref, ln2g_ref, ln2b_ref,
                    ewcol_ref, ys_hbm, out_ref, ybuf0, ybuf1, sem):
    i = pl.program_id(0)
    t_rows = x1_ref.shape[0]
    n_loc = TOP_K * t_rows
    ybufs = (ybuf0, ybuf1)

    @pl.when(i == 0)
    def _():
        _strip_copies(tab_ref, pstart_ref, ybuf0, ys_hbm, sem.at[0], to_hbm=False)

    for slot in range(2):
        @pl.when(i % 2 == slot)
        def _():
            @pl.when(i + 1 < pl.num_programs(0))
            def _():
                _strip_copies(tab_next_ref, pstart_ref, ybufs[1 - slot], ys_hbm,
                              sem.at[1 - slot], to_hbm=False)

            pltpu.make_async_copy(_rows(ys_hbm, 0, n_loc), ybufs[slot], sem.at[slot]).wait()
            ew = ewcol_ref[...]
            lane = lax.broadcasted_iota(I32, (t_rows, n_loc), 1)
            weights = jnp.where(lane == ew[:, 2:3].astype(I32), ew[:, 0:1],
                                jnp.where(lane == ew[:, 3:4].astype(I32), ew[:, 1:2], 0.0))
            y = _dot(weights.astype(BF16), _unpack_rows(ybufs[slot]))
            gate2 = mod_ref[0, 5:6, :]
            x1 = x1_ref[...].astype(F32)
            out_ref[...] = _layer_norm(DEEPNORM_ALPHA * x1 + (1.0 + gate2) * y,
                                       ln2g_ref[...], ln2b_ref[...])


def _combine(pad_start, tab, x1, mod, ln2_g, ln2_b, ewcol, ys, *, seq):
    n, d = x1.shape
    t = MIX_ROWS
    per_seq = seq // t
    n_steps = n // t
    row_blk = lambda i, ps: (i, 0)
    grid_spec = pltpu.PrefetchScalarGridSpec(
        num_scalar_prefetch=1,
        grid=(n_steps,),
        in_specs=[
            pl.BlockSpec((SUBLANES, LANES), row_blk, memory_space=pltpu.SMEM),
            pl.BlockSpec((SUBLANES, LANES), lambda i, ps: (jnp.minimum(i + 1, n_steps - 1), 0),
                         memory_space=pltpu.SMEM),
            pl.BlockSpec((t, d), row_blk),
            pl.BlockSpec((1, N_MOD, d), lambda i, ps: (i // per_seq, 0, 0)),
            pl.BlockSpec((1, d), lambda i, ps: (0, 0)),
            pl.BlockSpec((1, d), lambda i, ps: (0, 0)),
            pl.BlockSpec((t, LANES), row_blk),
            pl.BlockSpec(memory_space=pl.ANY),
        ],
        out_specs=pl.BlockSpec((t, d), row_blk),
        scratch_shapes=[
            pltpu.VMEM(_packed_shape(TOP_K * t), I32),
            pltpu.VMEM(_packed_shape(TOP_K * t), I32),
            pltpu.SemaphoreType.DMA((2,)),
        ],
    )
    return pl.pallas_call(
        _combine_kernel,
        grid_spec=grid_spec,
        out_shape=jax.ShapeDtypeStruct((n, d), F32),
        compiler_params=pltpu.CompilerParams(
            dimension_semantics=("arbitrary",), vmem_limit_bytes=VMEM_LIMIT_BYTES),
        name="combine",
    )(pad_start, tab, tab, x1, mod, ln2_g, ln2_b, ewcol, ys)


def _block_layout(counts, n_assign):
    bm = EXPERT_ROWS
    n_blocks = -(-n_assign // bm) + N_EXPERTS
    padded = (counts + bm - 1) // bm * bm
    pad_end = jnp.cumsum(padded)
    pad_start = (pad_end - padded).astype(I32)
    n_used = (pad_end[-1] // bm).astype(I32)
    pad_row = (pad_start + counts).astype(I32)
    pad_len = (padded - counts).astype(I32)
    ids = jnp.arange(N_EXPERTS, dtype=I32)
    prev_used = lax.cummax(jnp.where(padded > 0, ids, -1))
    first_used = jnp.argmax(padded > 0).astype(I32)
    weight_sel = jnp.where(prev_used >= 0, prev_used, first_used).astype(I32)
    return (pad_start, pad_row, pad_len, pad_start // bm, (padded // bm).astype(I32), weight_sel,
            n_used.reshape(1), n_blocks)


def kernel(x, c, w_ada, b_ada, w_in, conv_w, w_out_conv, w_pool, pool_scale, w_o, ln1_g, ln1_b,
           w_group, b_group, w_router, b_router, w_gate, w_up, w_down, ln2_g, ln2_b):
    bsz, seq, d = x.shape
    n = bsz * seq
    assert d == 2 * PACK_SUBLANES * LANES and seq % MIX_ROWS == 0 and w_ada.shape[0] == DEPTH
    x2d = x.reshape(n, d)
    c_pad = jnp.pad(c, ((0, -bsz % SUBLANES), (0, 0)))
    for l in range(DEPTH):
        mod = _ada(c_pad, w_ada[l], b_ada[l][None, :])[:bsz].reshape(bsz, N_MOD, d)
        wr = jnp.concatenate(
            [w_router[l], w_group[l],
             jnp.zeros((d, LANES - N_EXPERTS - N_GROUPS), F32)], axis=1)
        rb = jnp.concatenate(
            [b_router[l].reshape(-1), b_group[l],
             jnp.zeros((LANES - N_EXPERTS - N_GROUPS,), F32)])[:, None]
        x1, h, loc, ewcol, tab, cnt = _mix(
            x2d, mod, w_in[l].astype(BF16), conv_w[l], w_out_conv[l].astype(BF16),
            w_pool[l].astype(BF16), pool_scale[l][None, :], w_o[l].astype(BF16),
            ln1_g[l][None, :], ln1_b[l][None, :], wr, rb, batch=bsz, seq=seq)
        pad_start, pad_row, pad_len, blk0, nblk, weight_sel, n_used, n_blocks = _block_layout(
            cnt[:, 0], n * TOP_K)
        xs = _dispatch(pad_start, pad_row, pad_len, n_used, tab, loc, h, n_blocks)
        ys = _experts(blk0, nblk, weight_sel, n_used, xs, w_gate[l], w_up[l], w_down[l])
        x2d = _combine(pad_start, tab, x1, mod, ln2_g[l][None, :], ln2_b[l][None, :],
                       ewcol, ys, seq=seq)
    return x2d.reshape(bsz, seq, d)
```

```python
import functools

import jax
import jax.numpy as jnp
from jax import lax
from jax.experimental import pallas as pl
from jax.experimental.pallas import tpu as pltpu

F32 = jnp.float32
BF16 = jnp.bfloat16
I32 = jnp.int32

LANES = 128
SUBLANES = 8
VMEM_LIMIT_BYTES = 56 * 1024 * 1024

N_POOL_GROUPS = 4
POOL_WINDOWS = (2, 4, 8, 16)
CONV_WIDTH = 3
N_GROUPS = 8
EXPERTS_PER_GROUP = 8
N_EXPERTS = N_GROUPS * EXPERTS_PER_GROUP
TOP_K = 2
N_MOD = 6
LN_EPS = 1e-5
DEPTH = 1
DEEPNORM_ALPHA = (2.0 * DEPTH) ** 0.25

MIX_ROWS = 512
EXPERT_ROWS = 256
HALO = SUBLANES
PACK_SUBLANES = 4


def _split_bf16(a):
    hi = a.astype(BF16)
    lo = (a - hi.astype(F32)).astype(BF16)
    return hi, lo


def _dot(a, b):
    return jnp.dot(a, b, preferred_element_type=F32)


def _dot3(a, b):
    a_hi, a_lo = _split_bf16(a)
    b_hi, b_lo = _split_bf16(b)
    return _dot(a_hi, b_hi) + _dot(a_lo, b_hi) + _dot(a_hi, b_lo)


def _dot3_wide(a, b):
    a_hi, a_lo = _split_bf16(a)
    n = b.shape[1]
    b_both = jnp.concatenate(_split_bf16(b), axis=1)
    hi = _dot(a_hi, b_both)
    lo = _dot(a_lo, b_both)
    return hi[:, :n] + hi[:, n:] + lo[:, :n] + lo[:, n:]


def _layer_norm(r, g, b):
    mu = jnp.mean(r, axis=-1, keepdims=True)
    d = r - mu
    var = jnp.mean(d * d, axis=-1, keepdims=True)
    return d * lax.rsqrt(var + LN_EPS) * g + b


def _packed_shape(rows):
    return (rows, PACK_SUBLANES, LANES)


def _plane(ref, s, row0, rows):
    flat = ref.reshape(ref.shape[0] * PACK_SUBLANES, LANES)
    return flat.at[pl.ds(row0 * PACK_SUBLANES + s, rows, stride=PACK_SUBLANES), :]


def _pack_rows(ref, val, row0=0, *, is_bf16_exact=False):
    rows, d = val.shape
    if not is_bf16_exact:
        val = val.astype(BF16).astype(F32)
    bits = lax.bitcast_convert_type(val, I32)
    for s in range(PACK_SUBLANES):
        hi = bits[:, s * LANES:(s + 1) * LANES]
        lo = bits[:, d // 2 + s * LANES:d // 2 + (s + 1) * LANES]
        _plane(ref, s, row0, rows)[...] = jnp.bitwise_or(hi, lax.shift_right_logical(lo, 16))


def _unpack_rows(ref, row0=0, rows=None):
    rows = ref.shape[0] - row0 if rows is None else rows
    his, los = [], []
    for s in range(PACK_SUBLANES):
        word = _plane(ref, s, row0, rows)[...]
        his.append(lax.bitcast_convert_type(jnp.bitwise_and(word, jnp.int32(-65536)), F32))
        los.append(lax.bitcast_convert_type(lax.shift_left(word, 16), F32))
    return jnp.concatenate(his + los, axis=1).astype(BF16)


def _rows(ref, row, n_rows):
    return ref.at[pl.ds(row, n_rows)]


def _ada_kernel(c_ref, w_ref, b_ref, o_ref):
    cc = c_ref[...]
    act = cc * jax.nn.sigmoid(cc)
    o_ref[...] = _dot3(act, w_ref[...]) + b_ref[...]


def _ada(c_pad, w_ada, b_ada):
    rows, d = c_pad.shape
    n_out = w_ada.shape[1]
    return pl.pallas_call(
        _ada_kernel,
        grid=(n_out // d,),
        in_specs=[
            pl.BlockSpec((rows, d), lambda j: (0, 0)),
            pl.BlockSpec((d, d), lambda j: (0, j)),
            pl.BlockSpec((1, d), lambda j: (0, j)),
        ],
        out_specs=pl.BlockSpec((rows, d), lambda j: (0, j)),
        out_shape=jax.ShapeDtypeStruct((rows, n_out), F32),
        compiler_params=pltpu.CompilerParams(
            dimension_semantics=("arbitrary",), vmem_limit_bytes=VMEM_LIMIT_BYTES),
        name="ada",
    )(c_pad, w_ada, b_ada)


def _mix_kernel(x_ref, mod_ref, win_ref, convw_ref, woc_ref, wpool_ref, pscale_ref, wo_ref,
                ln1g_ref, ln1b_ref, wr_ref, rb_ref, mod_prev_ref,
                x1_ref, h_ref, loc_ref, ewcol_ref, tab_ref, cnt_ref,
                cvbuf, zbuf, s2buf, s4buf, s8buf, base_ref, res_buf, *, tiles_per_seq):
    t_rows, d = x_ref.shape
    pg = d // N_POOL_GROUPS
    step = pl.program_id(0)
    n_tiles = pl.num_programs(0) - 1
    s = jnp.minimum(step, n_tiles - 1) % tiles_per_seq
    shift_bufs = (cvbuf, zbuf, s2buf, s4buf, s8buf)

    @pl.when(step == 0)
    def _():
        base_ref[...] = jnp.zeros_like(base_ref)
        res_buf[...] = jnp.zeros_like(res_buf)

    @pl.when(s == 0)
    def _():
        for buf in shift_bufs:
            buf[0:HALO, :] = jnp.zeros((HALO, buf.shape[1]), F32)

    @pl.when(s > 0)
    def _():
        for buf in shift_bufs:
            buf[0:HALO, :] = buf[t_rows:t_rows + HALO, :]

    def lagged_tile():
        x1 = _layer_norm(res_buf[...], ln1g_ref[...], ln1b_ref[...])
        x1_ref[...] = x1
        h = x1 * (1.0 + mod_prev_ref[0, 4:5, :]) + mod_prev_ref[0, 3:4, :]
        h_ref[...] = h.astype(BF16)
        lt = _dot3_wide(h, wr_ref[...]).T + rb_ref[...]
        _route_tile(lt, step > 0, base_ref, loc_ref, ewcol_ref, tab_ref, cnt_ref)

    @pl.when(step == n_tiles)
    def _():
        lagged_tile()

    @pl.when(step < n_tiles)
    def _():
        def shifted(buf, k):
            return buf[HALO - k:HALO - k + t_rows, :]

        xt = x_ref[...]
        shift1 = mod_ref[0, 0:1, :]
        scale1 = mod_ref[0, 1:2, :]
        gate1 = mod_ref[0, 2:3, :]

        u = (xt * (1.0 + scale1) + shift1).astype(BF16)

        def proj(i):
            return _dot(u, win_ref[:, i * d:(i + 1) * d])


        a_c = proj(2)
        a_val = proj(0)
        a_b = proj(1)
        cv = a_c * a_val

        lagged_tile()

        z = proj(3)
        cvbuf[HALO:HALO + t_rows, :] = cv
        conv = (convw_ref[2:3, :] * cv + convw_ref[1:2, :] * shifted(cvbuf, 1)
                + convw_ref[0:1, :] * shifted(cvbuf, 2))
        y_a = _dot((conv * a_b).astype(BF16), woc_ref[...])

        zbuf[HALO:HALO + t_rows, :] = z
        s2 = z + shifted(zbuf, 1)
        s2buf[HALO:HALO + t_rows, :] = s2[:, pg:]
        s4 = s2[:, pg:] + shifted(s2buf, 2)
        s4buf[HALO:HALO + t_rows, :] = s4[:, pg:]
        s8 = s4[:, pg:] + shifted(s4buf, 4)
        s8buf[HALO:HALO + t_rows, :] = s8[:, pg:]
        s16 = s8[:, pg:] + shifted(s8buf, 8)
        win_sums = (s2[:, :pg], s4[:, :pg], s8[:, :pg], s16)
        t_pos = s * t_rows + lax.broadcasted_iota(I32, (t_rows, 1), 0) + 1
        g_a = proj(4)
        y_b_parts = []
        for gi, w in enumerate(POOL_WINDOWS):
            count = jnp.minimum(t_pos, w).astype(F32)
            pooled = win_sums[gi] / count - z[:, gi * pg:(gi + 1) * pg]
            y_b_parts.append(_dot(pooled.astype(BF16), wpool_ref[gi]))
        g_b = proj(5)
        y_b = jnp.concatenate(y_b_parts, axis=1) * pscale_ref[...]

        merged = jax.nn.sigmoid(g_a) * y_a + jax.nn.sigmoid(g_b) * y_b
        mix = _dot(merged.astype(BF16), wo_ref[...])

        res_buf[...] = DEEPNORM_ALPHA * xt + (1.0 + gate1) * mix


def _route_tile(lt, valid, base_ref, loc_ref, ewcol_ref, tab_ref, cnt_ref):
    t_rows = lt.shape[1]
    iota8 = lax.broadcasted_iota(I32, (SUBLANES, t_rows), 0)
    lg = lt[N_EXPERTS:N_EXPERTS + N_GROUPS, :]
    g_max = jnp.max(lg, axis=0, keepdims=True)
    g_top = jnp.min(jnp.where(lg == g_max, iota8, N_GROUPS), axis=0, keepdims=True)
    g_top_p = 1.0 / jnp.sum(jnp.exp(lg - g_max), axis=0, keepdims=True)
    sel = jnp.zeros((EXPERTS_PER_GROUP, t_rows), F32)
    for g in range(N_GROUPS):
        sel = jnp.where(g_top == g, lt[g * EXPERTS_PER_GROUP:(g + 1) * EXPERTS_PER_GROUP, :], sel)
    m1 = jnp.max(sel, axis=0, keepdims=True)
    i1 = jnp.min(jnp.where(sel == m1, iota8, EXPERTS_PER_GROUP), axis=0, keepdims=True)
    sel2 = jnp.where(iota8 == i1, -jnp.inf, sel)
    m2 = jnp.max(sel2, axis=0, keepdims=True)
    i2 = jnp.min(jnp.where(sel2 == m2, iota8, EXPERTS_PER_GROUP), axis=0, keepdims=True)
    ratio = jnp.exp(m2 - m1)
    p1 = 1.0 / (1.0 + ratio)
    w1 = g_top_p * p1
    w2 = g_top_p * (ratio * p1)
    e1 = g_top * EXPERTS_PER_GROUP + i1
    e2 = g_top * EXPERTS_PER_GROUP + i2

    iota_e = lax.broadcasted_iota(I32, (N_EXPERTS, t_rows), 0)
    hit1 = jnp.logical_and(iota_e == e1, valid)
    hit2 = jnp.logical_and(iota_e == e2, valid)
    onehot = jnp.where(jnp.logical_or(hit1, hit2), 1.0, 0.0)
    before = (lax.broadcasted_iota(I32, (t_rows, t_rows), 0)
              < lax.broadcasted_iota(I32, (t_rows, t_rows), 1))
    prefix = _dot(onehot.astype(BF16), jnp.where(before, 1.0, 0.0).astype(BF16))
    tile_cnt = jnp.sum(onehot, axis=1, keepdims=True)
    lower = (lax.broadcasted_iota(I32, (N_EXPERTS, N_EXPERTS), 1)
             < lax.broadcasted_iota(I32, (N_EXPERTS, N_EXPERTS), 0))
    lower = jnp.where(lower, 1.0, 0.0).astype(BF16)
    cnt_hi, cnt_lo = _split_bf16(jnp.broadcast_to(tile_cnt, (N_EXPERTS, LANES)))
    loc_start = (_dot(lower, cnt_hi) + _dot(lower, cnt_lo))[:, 0:1]
    pos = prefix + loc_start
    loc1 = jnp.sum(jnp.where(hit1, pos, 0.0), axis=0, keepdims=True)
    loc2 = jnp.sum(jnp.where(hit2, pos, 0.0), axis=0, keepdims=True)
    base = base_ref[...]
    base_ref[...] = base + tile_cnt
    cnt_ref[...] = jnp.broadcast_to(base + tile_cnt, cnt_ref.shape).astype(I32)

    diag = (lax.broadcasted_iota(I32, (N_EXPERTS, LANES), 0)
            == lax.broadcasted_iota(I32, (N_EXPERTS, LANES), 1))

    def to_lanes(col):
        return jnp.sum(jnp.where(diag, col, 0.0), axis=0, keepdims=True)

    iota8l = lax.broadcasted_iota(I32, (SUBLANES, LANES), 0)
    min_len = jnp.min(jnp.broadcast_to(tile_cnt, (N_EXPERTS, LANES)), axis=0, keepdims=True)
    tab_ref[...] = jnp.where(
        iota8l == TAB_LEN, to_lanes(tile_cnt),
        jnp.where(iota8l == TAB_RANK, to_lanes(base),
                  jnp.where(iota8l == TAB_LOC, to_lanes(loc_start),
                            jnp.where(iota8l == TAB_MIN_LEN, min_len, 0.0)))).astype(I32)

    loc_ref[...] = jnp.where(iota8 == 0, loc1, jnp.where(iota8 == 1, loc2, 0.0)).astype(I32)
    iota_l = lax.broadcasted_iota(I32, (LANES, t_rows), 0)
    ewcol_ref[...] = jnp.where(
        iota_l == 0, w1, jnp.where(iota_l == 1, w2,
                                   jnp.where(iota_l == 2, loc1, jnp.where(iota_l == 3, loc2, 0.0)))).T


def _mix(x2d, mod, w_in, conv_w, w_oc, w_pool, pool_scale, w_o, ln1_g, ln1_b, wr, rb,
         *, batch, seq):
    n, d = x2d.shape
    t = MIX_ROWS
    n_s = seq // t
    n_tiles = batch * n_s
    const = lambda shape: pl.BlockSpec(shape, lambda i: (0,) * len(shape),
                                       pipeline_mode=pl.Buffered(1))
    cur = lambda i: jnp.minimum(i, n_tiles - 1)
    prev = lambda i: jnp.maximum(i - 1, 0)
    pg = d // N_POOL_GROUPS
    return pl.pallas_call(
        functools.partial(_mix_kernel, tiles_per_seq=n_s),
        grid=(n_tiles + 1,),
        in_specs=[
            pl.BlockSpec((t, d), lambda i: (cur(i), 0)),
            pl.BlockSpec((1, N_MOD, d), lambda i: (cur(i) // n_s, 0, 0)),
            const(w_in.shape), const(conv_w.shape), const(w_oc.shape), const(w_pool.shape),
            const(pool_scale.shape), const(w_o.shape), const(ln1_g.shape), const(ln1_b.shape),
            const(wr.shape), const(rb.shape),
            pl.BlockSpec((1, N_MOD, d), lambda i: (prev(i) // n_s, 0, 0)),
        ],
        out_specs=[
            pl.BlockSpec((t, d), lambda i: (prev(i), 0)),
            pl.BlockSpec((t, d), lambda i: (prev(i), 0)),
            pl.BlockSpec((SUBLANES, t), lambda i: (0, prev(i))),
            pl.BlockSpec((t, LANES), lambda i: (prev(i), 0)),
            pl.BlockSpec((SUBLANES, LANES), lambda i: (prev(i), 0)),
            pl.BlockSpec((N_EXPERTS, LANES), lambda i: (0, 0)),
        ],
        out_shape=[
            jax.ShapeDtypeStruct((n, d), F32),
            jax.ShapeDtypeStruct((n, d), BF16),
            jax.ShapeDtypeStruct((SUBLANES, n), I32),
            jax.ShapeDtypeStruct((n, LANES), F32),
            jax.ShapeDtypeStruct((n // t * SUBLANES, LANES), I32),
            jax.ShapeDtypeStruct((N_EXPERTS, LANES), I32),
        ],
        scratch_shapes=[
            pltpu.VMEM((t + HALO, d), F32),
            pltpu.VMEM((t + HALO, d), F32),
            pltpu.VMEM((t + HALO, 3 * pg), F32),
            pltpu.VMEM((t + HALO, 2 * pg), F32),
            pltpu.VMEM((t + HALO, pg), F32),
            pltpu.VMEM((N_EXPERTS, 1), F32),
            pltpu.VMEM((t, d), F32),
        ],
        compiler_params=pltpu.CompilerParams(
            dimension_semantics=("arbitrary",), vmem_limit_bytes=VMEM_LIMIT_BYTES),
        name="mix",
    )(x2d, mod, w_in, conv_w, w_oc, w_pool, pool_scale, w_o, ln1_g, ln1_b, wr, rb, mod)


TAB_LEN, TAB_RANK, TAB_LOC, TAB_MIN_LEN = 0, 1, 2, 3
FILL_SEM = 2


def _strip_copies(tab_ref, pstart_ref, local_ref, sorted_hbm, sem, *, to_hbm):
    def start(e):
        n_rows = tab_ref[TAB_LEN, e]
        local = _rows(local_ref, tab_ref[TAB_LOC, e], n_rows)
        remote = _rows(sorted_hbm, pstart_ref[e] + tab_ref[TAB_RANK, e], n_rows)
        src, dst = (local, remote) if to_hbm else (remote, local)
        pltpu.make_async_copy(src, dst, sem).start()

    @pl.when(tab_ref[TAB_MIN_LEN, 0] > 0)
    def _():
        for e in range(N_EXPERTS):
            start(e)

    @pl.when(tab_ref[TAB_MIN_LEN, 0] <= 0)
    def _():
        def strip(e, carry):
            @pl.when(tab_ref[TAB_LEN, e] > 0)
            def _():
                start(e)
            return carry

        lax.fori_loop(0, N_EXPERTS, strip, 0)


def _dispatch_kernel(pstart_ref, padrow_ref, padlen_ref, nused_ref, tab_ref, loc_ref, h_ref, xs_hbm,
                     xloc0, xloc1, zero_blk, sem):
    i = pl.program_id(0)
    n_steps = pl.num_programs(0)
    t_rows = h_ref.shape[0]
    n_loc = TOP_K * t_rows
    blk_tok = zero_blk.shape[0]
    n_blocks = xs_hbm.shape[0] // blk_tok
    xlocs = (xloc0, xloc1)

    def zero_fills(act):
        def pad_rows(e, carry):
            @pl.when(padlen_ref[e] > 0)
            def _():
                act(pltpu.make_async_copy(
                    _rows(zero_blk, 0, padlen_ref[e]),
                    _rows(xs_hbm, padrow_ref[e], padlen_ref[e]), sem.at[FILL_SEM]))
            return carry

        def unused_block(b, carry):
            act(pltpu.make_async_copy(
                zero_blk, _rows(xs_hbm, b * blk_tok, blk_tok), sem.at[FILL_SEM]))
            return carry

        lax.fori_loop(0, N_EXPERTS, pad_rows, 0)
        lax.fori_loop(nused_ref[0], n_blocks, unused_block, 0)

    @pl.when(i == 0)
    def _():
        zero_blk[...] = jnp.zeros_like(zero_blk)
        zero_fills(lambda copy: copy.start())

    def strips_done(slot):
        pltpu.make_async_copy(xlocs[slot], _rows(xs_hbm, 0, n_loc), sem.at[slot]).wait()

    for slot in range(2):
        @pl.when(i % 2 == slot)
        def _():
            @pl.when(i >= 2)
            def _():
                strips_done(slot)

            iota_p = lax.broadcasted_iota(I32, (n_loc, t_rows), 0)
            pick = jnp.logical_or(iota_p == loc_ref[0:1, :], iota_p == loc_ref[1:2, :])
            xsorted = _dot(jnp.where(pick, 1.0, 0.0).astype(BF16), h_ref[...])
            _pack_rows(xlocs[slot], xsorted, is_bf16_exact=True)
            _strip_copies(tab_ref, pstart_ref, xlocs[slot], xs_hbm, sem.at[slot], to_hbm=True)

    @pl.when(i == n_steps - 1)
    def _():
        for slot in range(2):
            @pl.when(n_steps > slot)
            def _():
                strips_done(slot)
        zero_fills(lambda copy: copy.wait())


def _dispatch(pad_start, pad_row, pad_len, n_used, tab, loc, h, n_blocks):
    n, d = h.shape
    t = MIX_ROWS
    grid_spec = pltpu.PrefetchScalarGridSpec(
        num_scalar_prefetch=4,
        grid=(n // t,),
        in_specs=[
            pl.BlockSpec((SUBLANES, LANES), lambda i, *_: (i, 0), memory_space=pltpu.SMEM),
            pl.BlockSpec((SUBLANES, t), lambda i, *_: (0, i)),
            pl.BlockSpec((t, d), lambda i, *_: (i, 0)),
        ],
        out_specs=pl.BlockSpec(memory_space=pl.ANY),
        scratch_shapes=[
            pltpu.VMEM(_packed_shape(TOP_K * t), I32),
            pltpu.VMEM(_packed_shape(TOP_K * t), I32),
            pltpu.VMEM(_packed_shape(EXPERT_ROWS), I32),
            pltpu.SemaphoreType.DMA((3,)),
        ],
    )
    return pl.pallas_call(
        _dispatch_kernel,
        grid_spec=grid_spec,
        out_shape=jax.ShapeDtypeStruct(_packed_shape(n_blocks * EXPERT_ROWS), I32),
        compiler_params=pltpu.CompilerParams(
            dimension_semantics=("arbitrary",), vmem_limit_bytes=VMEM_LIMIT_BYTES,
            has_side_effects=True),
        name="dispatch",
    )(pad_start, pad_row, pad_len, n_used, tab, loc, h)


IN_SLOTS = 6
AHEAD = 4
OUT_SLOTS = 4


def _expert_kernel(blk0_ref, nblk_ref, wsel_ref, nused_ref, xs_hbm, wg_ref, wu_ref, wd_ref, ys_hbm,
                   xring, yring, wg_bf, wu_bf, wd_bf, sem_in, sem_out):
    del wsel_ref
    e = pl.program_id(0)
    blk_tok = xring.shape[0] // IN_SLOTS
    n_used = nused_ref[0]
    n_blk = nblk_ref[e]

    def in_copy(g):
        slot = g % IN_SLOTS
        return pltpu.make_async_copy(
            _rows(xs_hbm, g * blk_tok, blk_tok), _rows(xring, slot * blk_tok, blk_tok),
            sem_in.at[slot])

    def out_copy(g):
        slot = g % OUT_SLOTS
        return pltpu.make_async_copy(
            _rows(yring, slot * blk_tok, blk_tok), _rows(ys_hbm, g * blk_tok, blk_tok),
            sem_out.at[slot])

    def request(g):
        @pl.when(g < n_used)
        def _():
            in_copy(g).start()

    def retire(g):
        @pl.when(g >= OUT_SLOTS)
        def _():
            out_copy(g - OUT_SLOTS).wait()

    def mlp(*blocks):
        xbs = [_unpack_rows(xring, (g % IN_SLOTS) * blk_tok, blk_tok) for g in blocks]
        gates = [_dot(xb, wg_bf[...]) for xb in xbs]
        ups = [_dot(xb, wu_bf[...]) for xb in xbs]
        acts = [(gate * jax.nn.sigmoid(gate) * up).astype(BF16) for gate, up in zip(gates, ups)]
        for g, act in zip(blocks, acts):
            _pack_rows(yring, _dot(act, wd_bf[...]), (g % OUT_SLOTS) * blk_tok)

    @pl.when(e == 0)
    def _():
        for g in range(AHEAD):
            request(g)

    @pl.when(n_blk > 0)
    def _():
        wg_bf[...] = wg_ref[0].astype(BF16)
        wu_bf[...] = wu_ref[0].astype(BF16)
        wd_bf[...] = wd_ref[0].astype(BF16)

    def pair(p, carry):
        g = blk0_ref[e] + 2 * p
        in_copy(g).wait()
        in_copy(g + 1).wait()
        request(g + AHEAD)
        request(g + AHEAD + 1)
        retire(g)
        retire(g + 1)
        mlp(g, g + 1)
        out_copy(g).start()
        out_copy(g + 1).start()
        return carry

    lax.fori_loop(0, n_blk // 2, pair, 0)

    @pl.when(n_blk % 2 == 1)
    def _():
        g = blk0_ref[e] + n_blk - 1
        in_copy(g).wait()
        request(g + AHEAD)
        retire(g)
        mlp(g)
        out_copy(g).start()

    @pl.when(e == pl.num_programs(0) - 1)
    def _():
        for k in range(1, OUT_SLOTS + 1):
            @pl.when(n_used >= k)
            def _():
                out_copy(n_used - k).wait()


def _experts(blk0, nblk, wsel, n_used, xs, w_gate, w_up, w_down):
    bm = EXPERT_ROWS
    n_exp, d, de = w_gate.shape
    wblk = lambda e, b0, nb, ws, nu: (ws[e], 0, 0)
    grid_spec = pltpu.PrefetchScalarGridSpec(
        num_scalar_prefetch=4,
        grid=(n_exp,),
        in_specs=[
            pl.BlockSpec(memory_space=pl.ANY),
            pl.BlockSpec((1, d, de), wblk),
            pl.BlockSpec((1, d, de), wblk),
            pl.BlockSpec((1, de, d), wblk),
        ],
        out_specs=pl.BlockSpec(memory_space=pl.ANY),
        scratch_shapes=[
            pltpu.VMEM(_packed_shape(IN_SLOTS * bm), I32),
            pltpu.VMEM(_packed_shape(OUT_SLOTS * bm), I32),
            pltpu.VMEM((d, de), BF16),
            pltpu.VMEM((d, de), BF16),
            pltpu.VMEM((de, d), BF16),
            pltpu.SemaphoreType.DMA((IN_SLOTS,)),
            pltpu.SemaphoreType.DMA((OUT_SLOTS,)),
        ],
    )
    n_prefetch = 4
    return pl.pallas_call(
        _expert_kernel,
        grid_spec=grid_spec,
        out_shape=jax.ShapeDtypeStruct(xs.shape, xs.dtype),
        input_output_aliases={n_prefetch: 0},
        compiler_params=pltpu.CompilerParams(
            dimension_semantics=("arbitrary",), vmem_limit_bytes=VMEM_LIMIT_BYTES,
            has_side_effects=True),
        name="experts",
    )(blk0, nblk, wsel, n_used, xs, w_gate, w_up, w_down)


def _combine_kernel(pstart_ref, tab_ref, tab_next_ref, x1_ref, mod_ref, ln2g_ref, ln2b_ref,
                    ewcol_ref, ys_hbm, out_ref, ybuf0, ybuf1, sem):
    i = pl.program_id(0)
    t_rows = x1_ref.shape[0]
    n_loc = TOP_K * t_rows
    ybufs = (ybuf0, ybuf1)

    @pl.when(i == 0)
    def _():
        _strip_copies(tab_ref, pstart_ref, ybuf0, ys_hbm, sem.at[0], to_hbm=False)

    for slot in range(2):
        @pl.when(i % 2 == slot)
        def _():
            @pl.when(i + 1 < pl.num_programs(0))
            def _():
                _strip_copies(tab_next_ref, pstart_ref, ybufs[1 - slot], ys_hbm,
                              sem.at[1 - slot], to_hbm=False)

            pltpu.make_async_copy(_rows(ys_hbm, 0, n_loc), ybufs[slot], sem.at[slot]).wait()
            ew = ewcol_ref[...]
            lane = lax.broadcasted_iota(I32, (t_rows, n_loc), 1)
            weights = jnp.where(lane == ew[:, 2:3].astype(I32), ew[:, 0:1],
                                jnp.where(lane == ew[:, 3:4].astype(I32), ew[:, 1:2], 0.0))
            y = _dot(weights.astype(BF16), _unpack_rows(ybufs[slot]))
            gate2 = mod_ref[0, 5:6, :]
            out_ref[...] = _layer_norm(DEEPNORM_ALPHA * x1_ref[...] + (1.0 + gate2) * y,
                                       ln2g_ref[...], ln2b_ref[...])


def _combine(pad_start, tab, x1, mod, ln2_g, ln2_b, ewcol, ys, *, seq):
    n, d = x1.shape
    t = MIX_ROWS
    per_seq = seq // t
    n_steps = n // t
    row_blk = lambda i, ps: (i, 0)
    grid_spec = pltpu.PrefetchScalarGridSpec(
        num_scalar_prefetch=1,
        grid=(n_steps,),
        in_specs=[
            pl.BlockSpec((SUBLANES, LANES), row_blk, memory_space=pltpu.SMEM),
            pl.BlockSpec((SUBLANES, LANES), lambda i, ps: (jnp.minimum(i + 1, n_steps - 1), 0),
                         memory_space=pltpu.SMEM),
            pl.BlockSpec((t, d), row_blk),
            pl.BlockSpec((1, N_MOD, d), lambda i, ps: (i // per_seq, 0, 0)),
            pl.BlockSpec((1, d), lambda i, ps: (0, 0)),
            pl.BlockSpec((1, d), lambda i, ps: (0, 0)),
            pl.BlockSpec((t, LANES), row_blk),
            pl.BlockSpec(memory_space=pl.ANY),
        ],
        out_specs=pl.BlockSpec((t, d), row_blk),
        scratch_shapes=[
            pltpu.VMEM(_packed_shape(TOP_K * t), I32),
            pltpu.VMEM(_packed_shape(TOP_K * t), I32),
            pltpu.SemaphoreType.DMA((2,)),
        ],
    )
    return pl.pallas_call(
        _combine_kernel,
        grid_spec=grid_spec,
        out_shape=jax.ShapeDtypeStruct((n, d), F32),
        compiler_params=pltpu.CompilerParams(
            dimension_semantics=("arbitrary",), vmem_limit_bytes=VMEM_LIMIT_BYTES),
        name="combine",
    )(pad_start, tab, tab, x1, mod, ln2_g, ln2_b, ewcol, ys)


def _block_layout(counts, n_assign):
    bm = EXPERT_ROWS
    n_blocks = -(-n_assign // bm) + N_EXPERTS
    padded = (counts + bm - 1) // bm * bm
    pad_end = jnp.cumsum(padded)
    pad_start = (pad_end - padded).astype(I32)
    n_used = (pad_end[-1] // bm).astype(I32)
    pad_row = (pad_start + counts).astype(I32)
    pad_len = (padded - counts).astype(I32)
    ids = jnp.arange(N_EXPERTS, dtype=I32)
    prev_used = lax.cummax(jnp.where(padded > 0, ids, -1))
    first_used = jnp.argmax(padded > 0).astype(I32)
    weight_sel = jnp.where(prev_used >= 0, prev_used, first_used).astype(I32)
    return (pad_start, pad_row, pad_len, pad_start // bm, (padded // bm).astype(I32), weight_sel,
            n_used.reshape(1), n_blocks)


def kernel(x, c, w_ada, b_ada, w_in, conv_w, w_out_conv, w_pool, pool_scale, w_o, ln1_g, ln1_b,
           w_group, b_group, w_router, b_router, w_gate, w_up, w_down, ln2_g, ln2_b):
    bsz, seq, d = x.shape
    n = bsz * seq
    assert d == 2 * PACK_SUBLANES * LANES and seq % MIX_ROWS == 0 and w_ada.shape[0] == DEPTH
    x2d = x.reshape(n, d)
    c_pad = jnp.pad(c, ((0, -bsz % SUBLANES), (0, 0)))
    for l in range(DEPTH):
        mod = _ada(c_pad, w_ada[l], b_ada[l][None, :])[:bsz].reshape(bsz, N_MOD, d)
        wr = jnp.concatenate(
            [w_router[l], w_group[l],
             jnp.zeros((d, LANES - N_EXPERTS - N_GROUPS), F32)], axis=1)
        rb = jnp.concatenate(
            [b_router[l].reshape(-1), b_group[l],
             jnp.zeros((LANES - N_EXPERTS - N_GROUPS,), F32)])[:, None]
        x1, h, loc, ewcol, tab, cnt = _mix(
            x2d, mod, w_in[l].astype(BF16), conv_w[l], w_out_conv[l].astype(BF16),
            w_pool[l].astype(BF16), pool_scale[l][None, :], w_o[l].astype(BF16),
            ln1_g[l][None, :], ln1_b[l][None, :], wr, rb, batch=bsz, seq=seq)
        pad_start, pad_row, pad_len, blk0, nblk, weight_sel, n_used, n_blocks = _block_layout(
            cnt[:, 0], n * TOP_K)
        xs = _dispatch(pad_start, pad_row, pad_len, n_used, tab, loc, h, n_blocks)
        ys = _experts(blk0, nblk, weight_sel, n_used, xs, w_gate[l], w_up[l], w_down[l])
        x2d = _combine(pad_start, tab, x1, mod, ln2_g[l][None, :], ln2_b[l][None, :],
                       ewcol, ys, seq=seq)
    return x2d.reshape(bsz, seq, d)
```

```python
import functools

import jax
import jax.numpy as jnp
from jax import lax
from jax.experimental import pallas as pl
from jax.experimental.pallas import tpu as pltpu

F32 = jnp.float32
BF16 = jnp.bfloat16
I32 = jnp.int32

LANES = 128
SUBLANES = 8
VMEM_LIMIT_BYTES = 56 * 1024 * 1024

N_POOL_GROUPS = 4
POOL_WINDOWS = (2, 4, 8, 16)
CONV_WIDTH = 3
N_GROUPS = 8
EXPERTS_PER_GROUP = 8
N_EXPERTS = N_GROUPS * EXPERTS_PER_GROUP
TOP_K = 2
N_MOD = 6
LN_EPS = 1e-5
DEPTH = 1
DEEPNORM_ALPHA = (2.0 * DEPTH) ** 0.25

MIX_ROWS = 512
EXPERT_ROWS = 256
HALO = SUBLANES
PACK_SUBLANES = 4


def _split_bf16(a):
    hi = a.astype(BF16)
    lo = (a - hi.astype(F32)).astype(BF16)
    return hi, lo


def _dot(a, b):
    return jnp.dot(a, b, preferred_element_type=F32)


def _dot3(a, b):
    a_hi, a_lo = _split_bf16(a)
    b_hi, b_lo = _split_bf16(b)
    return _dot(a_hi, b_hi) + _dot(a_lo, b_hi) + _dot(a_hi, b_lo)


def _dot3_wide(a, b):
    a_hi, a_lo = _split_bf16(a)
    n = b.shape[1]
    b_both = jnp.concatenate(_split_bf16(b), axis=1)
    hi = _dot(a_hi, b_both)
    lo = _dot(a_lo, b_both)
    return hi[:, :n] + hi[:, n:] + lo[:, :n] + lo[:, n:]


def _layer_norm(r, g, b):
    mu = jnp.mean(r, axis=-1, keepdims=True)
    d = r - mu
    var = jnp.mean(d * d, axis=-1, keepdims=True)
    return d * lax.rsqrt(var + LN_EPS) * g + b


def _packed_shape(rows):
    return (rows, PACK_SUBLANES, LANES)


def _plane(ref, s, row0, rows):
    flat = ref.reshape(ref.shape[0] * PACK_SUBLANES, LANES)
    return flat.at[pl.ds(row0 * PACK_SUBLANES + s, rows, stride=PACK_SUBLANES), :]


def _pack_rows(ref, val, row0=0, *, is_bf16_exact=False):
    rows, d = val.shape
    if not is_bf16_exact:
        val = val.astype(BF16).astype(F32)
    bits = lax.bitcast_convert_type(val, I32)
    for s in range(PACK_SUBLANES):
        hi = bits[:, s * LANES:(s + 1) * LANES]
        lo = bits[:, d // 2 + s * LANES:d // 2 + (s + 1) * LANES]
        _plane(ref, s, row0, rows)[...] = jnp.bitwise_or(hi, lax.shift_right_logical(lo, 16))


def _unpack_rows(ref, row0=0, rows=None):
    rows = ref.shape[0] - row0 if rows is None else rows
    his, los = [], []
    for s in range(PACK_SUBLANES):
        word = _plane(ref, s, row0, rows)[...]
        his.append(lax.bitcast_convert_type(jnp.bitwise_and(word, jnp.int32(-65536)), F32))
        los.append(lax.bitcast_convert_type(lax.shift_left(word, 16), F32))
    return jnp.concatenate(his + los, axis=1).astype(BF16)


def _rows(ref, row, n_rows):
    return ref.at[pl.ds(row, n_rows)]


def _ada_kernel(c_ref, w_ref, b_ref, o_ref):
    cc = c_ref[...]
    act = cc * jax.nn.sigmoid(cc)
    o_ref[...] = _dot3(act, w_ref[...]) + b_ref[...]


def _ada(c_pad, w_ada, b_ada):
    rows, d = c_pad.shape
    n_out = w_ada.shape[1]
    return pl.pallas_call(
        _ada_kernel,
        grid=(n_out // d,),
        in_specs=[
            pl.BlockSpec((rows, d), lambda j: (0, 0)),
            pl.BlockSpec((d, d), lambda j: (0, j)),
            pl.BlockSpec((1, d), lambda j: (0, j)),
        ],
        out_specs=pl.BlockSpec((rows, d), lambda j: (0, j)),
        out_shape=jax.ShapeDtypeStruct((rows, n_out), F32),
        compiler_params=pltpu.CompilerParams(
            dimension_semantics=("arbitrary",), vmem_limit_bytes=VMEM_LIMIT_BYTES),
        name="ada",
    )(c_pad, w_ada, b_ada)


def _mix_kernel(x_ref, mod_ref, win_ref, convw_ref, woc_ref, wpool_ref, pscale_ref, wo_ref,
                ln1g_ref, ln1b_ref, wr_ref, rb_ref, mod_prev_ref,
                x1_ref, h_ref, loc_ref, ewcol_ref, tab_ref, cnt_ref,
                cvbuf, zbuf, s2buf, s4buf, s8buf, base_ref, res_buf, *, tiles_per_seq):
    t_rows, d = x_ref.shape
    pg = d // N_POOL_GROUPS
    step = pl.program_id(0)
    n_tiles = pl.num_programs(0) - 1
    s = jnp.minimum(step, n_tiles - 1) % tiles_per_seq
    shift_bufs = (cvbuf, zbuf, s2buf, s4buf, s8buf)

    @pl.when(step == 0)
    def _():
        base_ref[...] = jnp.zeros_like(base_ref)
        res_buf[...] = jnp.zeros_like(res_buf)

    @pl.when(s == 0)
    def _():
        for buf in shift_bufs:
            buf[0:HALO, :] = jnp.zeros((HALO, buf.shape[1]), F32)

    @pl.when(s > 0)
    def _():
        for buf in shift_bufs:
            buf[0:HALO, :] = buf[t_rows:t_rows + HALO, :]

    def lagged_tile():
        x1 = _layer_norm(res_buf[...], ln1g_ref[...], ln1b_ref[...])
        x1_ref[...] = x1
        h = x1 * (1.0 + mod_prev_ref[0, 4:5, :]) + mod_prev_ref[0, 3:4, :]
        h_ref[...] = h.astype(BF16)
        lt = _dot3_wide(h, wr_ref[...]).T + rb_ref[...]
        _route_tile(lt, step > 0, base_ref, loc_ref, ewcol_ref, tab_ref, cnt_ref)

    @pl.when(step == n_tiles)
    def _():
        lagged_tile()

    @pl.when(step < n_tiles)
    def _():
        def shifted(buf, k):
            return buf[HALO - k:HALO - k + t_rows, :]

        xt = x_ref[...]
        shift1 = mod_ref[0, 0:1, :]
        scale1 = mod_ref[0, 1:2, :]
        gate1 = mod_ref[0, 2:3, :]

        u = (xt * (1.0 + scale1) + shift1).astype(BF16)

        def proj(i):
            return _dot(u, win_ref[:, i * d:(i + 1) * d])


        a_c = proj(2)
        a_val = proj(0)
        a_b = proj(1)
        cv = a_c * a_val

        lagged_tile()

        z = proj(3)
        cvbuf[HALO:HALO + t_rows, :] = cv
        conv = (convw_ref[2:3, :] * cv + convw_ref[1:2, :] * shifted(cvbuf, 1)
                + convw_ref[0:1, :] * shifted(cvbuf, 2))
        y_a = _dot((conv * a_b).astype(BF16), woc_ref[...])

        zbuf[HALO:HALO + t_rows, :] = z
        s2 = z + shifted(zbuf, 1)
        s2buf[HALO:HALO + t_rows, :] = s2[:, pg:]
        s4 = s2[:, pg:] + shifted(s2buf, 2)
        s4buf[HALO:HALO + t_rows, :] = s4[:, pg:]
        s8 = s4[:, pg:] + shifted(s4buf, 4)
        s8buf[HALO:HALO + t_rows, :] = s8[:, pg:]
        s16 = s8[:, pg:] + shifted(s8buf, 8)
        win_sums = (s2[:, :pg], s4[:, :pg], s8[:, :pg], s16)
        t_pos = s * t_rows + lax.broadcasted_iota(I32, (t_rows, 1), 0) + 1
        g_a = proj(4)
        y_b_parts = []
        for gi, w in enumerate(POOL_WINDOWS):
            count = jnp.minimum(t_pos, w).astype(F32)
            pooled = win_sums[gi] / count - z[:, gi * pg:(gi + 1) * pg]
            y_b_parts.append(_dot(pooled.astype(BF16), wpool_ref[gi]))
        g_b = proj(5)
        y_b = jnp.concatenate(y_b_parts, axis=1) * pscale_ref[...]

        merged = jax.nn.sigmoid(g_a) * y_a + jax.nn.sigmoid(g_b) * y_b
        mix = _dot(merged.astype(BF16), wo_ref[...])

        res_buf[...] = DEEPNORM_ALPHA * xt + (1.0 + gate1) * mix


def _route_tile(lt, valid, base_ref, loc_ref, ewcol_ref, tab_ref, cnt_ref):
    t_rows = lt.shape[1]
    iota8 = lax.broadcasted_iota(I32, (SUBLANES, t_rows), 0)
    lg = lt[N_EXPERTS:N_EXPERTS + N_GROUPS, :]
    g_max = jnp.max(lg, axis=0, keepdims=True)
    g_top = jnp.min(jnp.where(lg == g_max, iota8, N_GROUPS), axis=0, keepdims=True)
    g_top_p = 1.0 / jnp.sum(jnp.exp(lg - g_max), axis=0, keepdims=True)
    sel = jnp.zeros((EXPERTS_PER_GROUP, t_rows), F32)
    for g in range(N_GROUPS):
        sel = jnp.where(g_top == g, lt[g * EXPERTS_PER_GROUP:(g + 1) * EXPERTS_PER_GROUP, :], sel)
    m1 = jnp.max(sel, axis=0, keepdims=True)
    i1 = jnp.min(jnp.where(sel == m1, iota8, EXPERTS_PER_GROUP), axis=0, keepdims=True)
    sel2 = jnp.where(iota8 == i1, -jnp.inf, sel)
    m2 = jnp.max(sel2, axis=0, keepdims=True)
    i2 = jnp.min(jnp.where(sel2 == m2, iota8, EXPERTS_PER_GROUP), axis=0, keepdims=True)
    ratio = jnp.exp(m2 - m1)
    p1 = 1.0 / (1.0 + ratio)
    w1 = g_top_p * p1
    w2 = g_top_p * (ratio * p1)
    e1 = g_top * EXPERTS_PER_GROUP + i1
    e2 = g_top * EXPERTS_PER_GROUP + i2

    iota_e = lax.broadcasted_iota(I32, (N_EXPERTS, t_rows), 0)
    hit1 = jnp.logical_and(iota_e == e1, valid)
    hit2 = jnp.logical_and(iota_e == e2, valid)
    onehot = jnp.where(jnp.logical_or(hit1, hit2), 1.0, 0.0)
    before = (lax.broadcasted_iota(I32, (t_rows, t_rows), 0)
              < lax.broadcasted_iota(I32, (t_rows, t_rows), 1))
    prefix = _dot(onehot.astype(BF16), jnp.where(before, 1.0, 0.0).astype(BF16))
    tile_cnt = jnp.sum(onehot, axis=1, keepdims=True)
    lower = (lax.broadcasted_iota(I32, (N_EXPERTS, N_EXPERTS), 1)
             < lax.broadcasted_iota(I32, (N_EXPERTS, N_EXPERTS), 0))
    lower = jnp.where(lower, 1.0, 0.0).astype(BF16)
    cnt_hi, cnt_lo = _split_bf16(jnp.broadcast_to(tile_cnt, (N_EXPERTS, LANES)))
    loc_start = (_dot(lower, cnt_hi) + _dot(lower, cnt_lo))[:, 0:1]
    pos = prefix + loc_start
    loc1 = jnp.sum(jnp.where(hit1, pos, 0.0), axis=0, keepdims=True)
    loc2 = jnp.sum(jnp.where(hit2, pos, 0.0), axis=0, keepdims=True)
    base = base_ref[...]
    base_ref[...] = base + tile_cnt
    cnt_ref[...] = jnp.broadcast_to(base + tile_cnt, cnt_ref.shape).astype(I32)

    diag = (lax.broadcasted_iota(I32, (N_EXPERTS, LANES), 0)
            == lax.broadcasted_iota(I32, (N_EXPERTS, LANES), 1))

    def to_lanes(col):
        return jnp.sum(jnp.where(diag, col, 0.0), axis=0, keepdims=True)

    iota8l = lax.broadcasted_iota(I32, (SUBLANES, LANES), 0)
    min_len = jnp.min(jnp.broadcast_to(tile_cnt, (N_EXPERTS, LANES)), axis=0, keepdims=True)
    tab_ref[...] = jnp.where(
        iota8l == TAB_LEN, to_lanes(tile_cnt),
        jnp.where(iota8l == TAB_RANK, to_lanes(base),
                  jnp.where(iota8l == TAB_LOC, to_lanes(loc_start),
                            jnp.where(iota8l == TAB_MIN_LEN, min_len, 0.0)))).astype(I32)

    loc_ref[...] = jnp.where(iota8 == 0, loc1, jnp.where(iota8 == 1, loc2, 0.0)).astype(I32)
    iota_l = lax.broadcasted_iota(I32, (LANES, t_rows), 0)
    ewcol_ref[...] = jnp.where(
        iota_l == 0, w1, jnp.where(iota_l == 1, w2,
                                   jnp.where(iota_l == 2, loc1, jnp.where(iota_l == 3, loc2, 0.0)))).T


def _mix(x2d, mod, w_in, conv_w, w_oc, w_pool, pool_scale, w_o, ln1_g, ln1_b, wr, rb,
         *, batch, seq):
    n, d = x2d.shape
    t = MIX_ROWS
    n_s = seq // t
    n_tiles = batch * n_s
    const = lambda shape: pl.BlockSpec(shape, lambda i: (0,) * len(shape),
                                       pipeline_mode=pl.Buffered(1))
    cur = lambda i: jnp.minimum(i, n_tiles - 1)
    prev = lambda i: jnp.maximum(i - 1, 0)
    pg = d // N_POOL_GROUPS
    return pl.pallas_call(
        functools.partial(_mix_kernel, tiles_per_seq=n_s),
        grid=(n_tiles + 1,),
        in_specs=[
            pl.BlockSpec((t, d), lambda i: (cur(i), 0)),
            pl.BlockSpec((1, N_MOD, d), lambda i: (cur(i) // n_s, 0, 0)),
            const(w_in.shape), const(conv_w.shape), const(w_oc.shape), const(w_pool.shape),
            const(pool_scale.shape), const(w_o.shape), const(ln1_g.shape), const(ln1_b.shape),
            const(wr.shape), const(rb.shape),
            pl.BlockSpec((1, N_MOD, d), lambda i: (prev(i) // n_s, 0, 0)),
        ],
        out_specs=[
            pl.BlockSpec((t, d), lambda i: (prev(i), 0)),
            pl.BlockSpec((t, d), lambda i: (prev(i), 0)),
            pl.BlockSpec((SUBLANES, t), lambda i: (0, prev(i))),
            pl.BlockSpec((t, LANES), lambda i: (prev(i), 0)),
            pl.BlockSpec((SUBLANES, LANES), lambda i: (prev(i), 0)),
            pl.BlockSpec((N_EXPERTS, LANES), lambda i: (0, 0)),
        ],
        out_shape=[
            jax.ShapeDtypeStruct((n, d), F32),
            jax.ShapeDtypeStruct((n, d), BF16),
            jax.ShapeDtypeStruct((SUBLANES, n), I32),
            jax.ShapeDtypeStruct((n, LANES), F32),
            jax.ShapeDtypeStruct((n // t * SUBLANES, LANES), I32),
            jax.ShapeDtypeStruct((N_EXPERTS, LANES), I32),
        ],
        scratch_shapes=[
            pltpu.VMEM((t + HALO, d), F32),
            pltpu.VMEM((t + HALO, d), F32),
            pltpu.VMEM((t + HALO, 3 * pg), F32),
            pltpu.VMEM((t + HALO, 2 * pg), F32),
            pltpu.VMEM((t + HALO, pg), F32),
            pltpu.VMEM((N_EXPERTS, 1), F32),
            pltpu.VMEM((t, d), F32),
        ],
        compiler_params=pltpu.CompilerParams(
            dimension_semantics=("arbitrary",), vmem_limit_bytes=VMEM_LIMIT_BYTES),
        name="mix",
    )(x2d, mod, w_in, conv_w, w_oc, w_pool, pool_scale, w_o, ln1_g, ln1_b, wr, rb, mod)


TAB_LEN, TAB_RANK, TAB_LOC, TAB_MIN_LEN = 0, 1, 2, 3
FILL_SEM = 2


def _strip_copies(tab_ref, pstart_ref, local_ref, sorted_hbm, sem, *, to_hbm, local_row0=0):
    def start(e):
        n_rows = tab_ref[TAB_LEN, e]
        local = _rows(local_ref, local_row0 + tab_ref[TAB_LOC, e], n_rows)
        remote = _rows(sorted_hbm, pstart_ref[e] + tab_ref[TAB_RANK, e], n_rows)
        src, dst = (local, remote) if to_hbm else (remote, local)
        pltpu.make_async_copy(src, dst, sem).start()

    @pl.when(tab_ref[TAB_MIN_LEN, 0] > 0)
    def _():
        for e in range(N_EXPERTS):
            start(e)

    @pl.when(tab_ref[TAB_MIN_LEN, 0] <= 0)
    def _():
        def strip(e, carry):
            @pl.when(tab_ref[TAB_LEN, e] > 0)
            def _():
                start(e)
            return carry

        lax.fori_loop(0, N_EXPERTS, strip, 0)


def _dispatch_kernel(pstart_ref, padrow_ref, padlen_ref, nused_ref, tab_ref, loc_ref, h_ref, xs_hbm,
                     xloc0, xloc1, zero_blk, sem):
    i = pl.program_id(0)
    n_steps = pl.num_programs(0)
    t_rows = h_ref.shape[0]
    n_loc = TOP_K * t_rows
    blk_tok = zero_blk.shape[0]
    n_blocks = xs_hbm.shape[0] // blk_tok
    xlocs = (xloc0, xloc1)

    def zero_fills(act):
        def pad_rows(e, carry):
            @pl.when(padlen_ref[e] > 0)
            def _():
                act(pltpu.make_async_copy(
                    _rows(zero_blk, 0, padlen_ref[e]),
                    _rows(xs_hbm, padrow_ref[e], padlen_ref[e]), sem.at[FILL_SEM]))
            return carry

        def unused_block(b, carry):
            act(pltpu.make_async_copy(
                zero_blk, _rows(xs_hbm, b * blk_tok, blk_tok), sem.at[FILL_SEM]))
            return carry

        lax.fori_loop(0, N_EXPERTS, pad_rows, 0)
        lax.fori_loop(nused_ref[0], n_blocks, unused_block, 0)

    @pl.when(i == 0)
    def _():
        zero_blk[...] = jnp.zeros_like(zero_blk)
        zero_fills(lambda copy: copy.start())

    def strips_done(slot):
        pltpu.make_async_copy(xlocs[slot], _rows(xs_hbm, 0, n_loc), sem.at[slot]).wait()

    for slot in range(2):
        @pl.when(i % 2 == slot)
        def _():
            @pl.when(i >= 2)
            def _():
                strips_done(slot)

            iota_p = lax.broadcasted_iota(I32, (n_loc, t_rows), 0)
            pick = jnp.logical_or(iota_p == loc_ref[0:1, :], iota_p == loc_ref[1:2, :])
            xsorted = _dot(jnp.where(pick, 1.0, 0.0).astype(BF16), h_ref[...])
            _pack_rows(xlocs[slot], xsorted, is_bf16_exact=True)
            _strip_copies(tab_ref, pstart_ref, xlocs[slot], xs_hbm, sem.at[slot], to_hbm=True)

    @pl.when(i == n_steps - 1)
    def _():
        for slot in range(2):
            @pl.when(n_steps > slot)
            def _():
                strips_done(slot)
        zero_fills(lambda copy: copy.wait())


def _dispatch(pad_start, pad_row, pad_len, n_used, tab, loc, h, n_blocks):
    n, d = h.shape
    t = MIX_ROWS
    grid_spec = pltpu.PrefetchScalarGridSpec(
        num_scalar_prefetch=4,
        grid=(n // t,),
        in_specs=[
            pl.BlockSpec((SUBLANES, LANES), lambda i, *_: (i, 0), memory_space=pltpu.SMEM),
            pl.BlockSpec((SUBLANES, t), lambda i, *_: (0, i)),
            pl.BlockSpec((t, d), lambda i, *_: (i, 0)),
        ],
        out_specs=pl.BlockSpec(memory_space=pl.ANY),
        scratch_shapes=[
            pltpu.VMEM(_packed_shape(TOP_K * t), I32),
            pltpu.VMEM(_packed_shape(TOP_K * t), I32),
            pltpu.VMEM(_packed_shape(EXPERT_ROWS), I32),
            pltpu.SemaphoreType.DMA((3,)),
        ],
    )
    return pl.pallas_call(
        _dispatch_kernel,
        grid_spec=grid_spec,
        out_shape=jax.ShapeDtypeStruct(_packed_shape(n_blocks * EXPERT_ROWS), I32),
        compiler_params=pltpu.CompilerParams(
            dimension_semantics=("arbitrary",), vmem_limit_bytes=VMEM_LIMIT_BYTES,
            has_side_effects=True),
        name="dispatch",
    )(pad_start, pad_row, pad_len, n_used, tab, loc, h)


IN_SLOTS = 6
AHEAD = 4
OUT_SLOTS = 4


def _expert_kernel(blk0_ref, nblk_ref, wsel_ref, nused_ref, xs_hbm, wg_ref, wu_ref, wd_ref, ys_hbm,
                   xring, yring, wg_bf, wu_bf, wd_bf, sem_in, sem_out):
    del wsel_ref
    e = pl.program_id(0)
    blk_tok = xring.shape[0] // IN_SLOTS
    n_used = nused_ref[0]
    n_blk = nblk_ref[e]

    def in_copy(g):
        slot = g % IN_SLOTS
        return pltpu.make_async_copy(
            _rows(xs_hbm, g * blk_tok, blk_tok), _rows(xring, slot * blk_tok, blk_tok),
            sem_in.at[slot])

    def out_copy(g):
        slot = g % OUT_SLOTS
        return pltpu.make_async_copy(
            _rows(yring, slot * blk_tok, blk_tok), _rows(ys_hbm, g * blk_tok, blk_tok),
            sem_out.at[slot])

    def request(g):
        @pl.when(g < n_used)
        def _():
            in_copy(g).start()

    def retire(g):
        @pl.when(g >= OUT_SLOTS)
        def _():
            out_copy(g - OUT_SLOTS).wait()

    def mlp(*blocks):
        xbs = [_unpack_rows(xring, (g % IN_SLOTS) * blk_tok, blk_tok) for g in blocks]
        gates = [_dot(xb, wg_bf[...]) for xb in xbs]
        ups = [_dot(xb, wu_bf[...]) for xb in xbs]
        acts = [(gate * jax.nn.sigmoid(gate) * up).astype(BF16) for gate, up in zip(gates, ups)]
        for g, act in zip(blocks, acts):
            _pack_rows(yring, _dot(act, wd_bf[...]), (g % OUT_SLOTS) * blk_tok)

    @pl.when(e == 0)
    def _():
        for g in range(AHEAD):
            request(g)

    @pl.when(n_blk > 0)
    def _():
        wg_bf[...] = wg_ref[0].astype(BF16)
        wu_bf[...] = wu_ref[0].astype(BF16)
        wd_bf[...] = wd_ref[0].astype(BF16)

    def pair(p, carry):
        g = blk0_ref[e] + 2 * p
        in_copy(g).wait()
        in_copy(g + 1).wait()
        request(g + AHEAD)
        request(g + AHEAD + 1)
        retire(g)
        retire(g + 1)
        mlp(g, g + 1)
        out_copy(g).start()
        out_copy(g + 1).start()
        return carry

    lax.fori_loop(0, n_blk // 2, pair, 0)

    @pl.when(n_blk % 2 == 1)
    def _():
        g = blk0_ref[e] + n_blk - 1
        in_copy(g).wait()
        request(g + AHEAD)
        retire(g)
        mlp(g)
        out_copy(g).start()

    @pl.when(e == pl.num_programs(0) - 1)
    def _():
        for k in range(1, OUT_SLOTS + 1):
            @pl.when(n_used >= k)
            def _():
                out_copy(n_used - k).wait()


def _experts(blk0, nblk, wsel, n_used, xs, w_gate, w_up, w_down):
    bm = EXPERT_ROWS
    n_exp, d, de = w_gate.shape
    wblk = lambda e, b0, nb, ws, nu: (ws[e], 0, 0)
    grid_spec = pltpu.PrefetchScalarGridSpec(
        num_scalar_prefetch=4,
        grid=(n_exp,),
        in_specs=[
            pl.BlockSpec(memory_space=pl.ANY),
            pl.BlockSpec((1, d, de), wblk),
            pl.BlockSpec((1, d, de), wblk),
            pl.BlockSpec((1, de, d), wblk),
        ],
        out_specs=pl.BlockSpec(memory_space=pl.ANY),
        scratch_shapes=[
            pltpu.VMEM(_packed_shape(IN_SLOTS * bm), I32),
            pltpu.VMEM(_packed_shape(OUT_SLOTS * bm), I32),
            pltpu.VMEM((d, de), BF16),
            pltpu.VMEM((d, de), BF16),
            pltpu.VMEM((de, d), BF16),
            pltpu.SemaphoreType.DMA((IN_SLOTS,)),
            pltpu.SemaphoreType.DMA((OUT_SLOTS,)),
        ],
    )
    n_prefetch = 4
    return pl.pallas_call(
        _expert_kernel,
        grid_spec=grid_spec,
        out_shape=jax.ShapeDtypeStruct(xs.shape, xs.dtype),
        input_output_aliases={n_prefetch: 0},
        compiler_params=pltpu.CompilerParams(
            dimension_semantics=("arbitrary",), vmem_limit_bytes=VMEM_LIMIT_BYTES,
            has_side_effects=True),
        name="experts",
    )(blk0, nblk, wsel, n_used, xs, w_gate, w_up, w_down)


COMBINE_SLOTS = 3


def _combine_kernel(pstart_ref, tab_ref, tab1_ref, tab2_ref, x1_ref, mod_ref, ln2g_ref, ln2b_ref,
                    ewcol_ref, ys_hbm, out_ref, yring, sem):
    i = pl.program_id(0)
    n_tiles = pl.num_programs(0)
    t_rows = x1_ref.shape[0]
    n_loc = TOP_K * t_rows
    ahead = COMBINE_SLOTS - 1

    def fetch(table_ref, tile):
        slot = tile % COMBINE_SLOTS
        _strip_copies(table_ref, pstart_ref, yring, ys_hbm, sem.at[slot], to_hbm=False,
                      local_row0=slot * n_loc)

    @pl.when(i == 0)
    def _():
        fetch(tab_ref, 0)

        @pl.when(n_tiles > 1)
        def _():
            fetch(tab1_ref, 1)

    @pl.when(i + ahead < n_tiles)
    def _():
        fetch(tab2_ref, i + ahead)

    slot = i % COMBINE_SLOTS
    pltpu.make_async_copy(
        _rows(ys_hbm, 0, n_loc), _rows(yring, slot * n_loc, n_loc), sem.at[slot]).wait()
    ew = ewcol_ref[...]
    lane = lax.broadcasted_iota(I32, (t_rows, n_loc), 1)
    weights = jnp.where(lane == ew[:, 2:3].astype(I32), ew[:, 0:1],
                        jnp.where(lane == ew[:, 3:4].astype(I32), ew[:, 1:2], 0.0))
    y = _dot(weights.astype(BF16), _unpack_rows(yring, slot * n_loc, n_loc))
    gate2 = mod_ref[0, 5:6, :]
    out_ref[...] = _layer_norm(DEEPNORM_ALPHA * x1_ref[...] + (1.0 + gate2) * y,
                               ln2g_ref[...], ln2b_ref[...])


def _combine(pad_start, tab, x1, mod, ln2_g, ln2_b, ewcol, ys, *, seq):
    n, d = x1.shape
    t = MIX_ROWS
    per_seq = seq // t
    n_steps = n // t
    row_blk = lambda i, ps: (i, 0)
    later = lambda k: (lambda i, ps: (jnp.minimum(i + k, n_steps - 1), 0))
    grid_spec = pltpu.PrefetchScalarGridSpec(
        num_scalar_prefetch=1,
        grid=(n_steps,),
        in_specs=[
            pl.BlockSpec((SUBLANES, LANES), row_blk, memory_space=pltpu.SMEM),
            pl.BlockSpec((SUBLANES, LANES), later(1), memory_space=pltpu.SMEM),
            pl.BlockSpec((SUBLANES, LANES), later(COMBINE_SLOTS - 1), memory_space=pltpu.SMEM),
            pl.BlockSpec((t, d), row_blk),
            pl.BlockSpec((1, N_MOD, d), lambda i, ps: (i // per_seq, 0, 0)),
            pl.BlockSpec((1, d), lambda i, ps: (0, 0)),
            pl.BlockSpec((1, d), lambda i, ps: (0, 0)),
            pl.BlockSpec((t, LANES), row_blk),
            pl.BlockSpec(memory_space=pl.ANY),
        ],
        out_specs=pl.BlockSpec((t, d), row_blk),
        scratch_shapes=[
            pltpu.VMEM(_packed_shape(COMBINE_SLOTS * TOP_K * t), I32),
            pltpu.SemaphoreType.DMA((COMBINE_SLOTS,)),
        ],
    )
    return pl.pallas_call(
        _combine_kernel,
        grid_spec=grid_spec,
        out_shape=jax.ShapeDtypeStruct((n, d), F32),
        compiler_params=pltpu.CompilerParams(
            dimension_semantics=("arbitrary",), vmem_limit_bytes=VMEM_LIMIT_BYTES),
        name="combine",
    )(pad_start, tab, tab, tab, x1, mod, ln2_g, ln2_b, ewcol, ys)


def _block_layout(counts, n_assign):
    bm = EXPERT_ROWS
    n_blocks = -(-n_assign // bm) + N_EXPERTS
    padded = (counts + bm - 1) // bm * bm
    pad_end = jnp.cumsum(padded)
    pad_start = (pad_end - padded).astype(I32)
    n_used = (pad_end[-1] // bm).astype(I32)
    pad_row = (pad_start + counts).astype(I32)
    pad_len = (padded - counts).astype(I32)
    ids = jnp.arange(N_EXPERTS, dtype=I32)
    prev_used = lax.cummax(jnp.where(padded > 0, ids, -1))
    first_used = jnp.argmax(padded > 0).astype(I32)
    weight_sel = jnp.where(prev_used >= 0, prev_used, first_used).astype(I32)
    return (pad_start, pad_row, pad_len, pad_start // bm, (padded // bm).astype(I32), weight_sel,
            n_used.reshape(1), n_blocks)


def kernel(x, c, w_ada, b_ada, w_in, conv_w, w_out_conv, w_pool, pool_scale, w_o, ln1_g, ln1_b,
           w_group, b_group, w_router, b_router, w_gate, w_up, w_down, ln2_g, ln2_b):
    bsz, seq, d = x.shape
    n = bsz * seq
    assert d == 2 * PACK_SUBLANES * LANES and seq % MIX_ROWS == 0 and w_ada.shape[0] == DEPTH
    x2d = x.reshape(n, d)
    c_pad = jnp.pad(c, ((0, -bsz % SUBLANES), (0, 0)))
    for l in range(DEPTH):
        mod = _ada(c_pad, w_ada[l], b_ada[l][None, :])[:bsz].reshape(bsz, N_MOD, d)
        wr = jnp.concatenate(
            [w_router[l], w_group[l],
             jnp.zeros((d, LANES - N_EXPERTS - N_GROUPS), F32)], axis=1)
        rb = jnp.concatenate(
            [b_router[l].reshape(-1), b_group[l],
             jnp.zeros((LANES - N_EXPERTS - N_GROUPS,), F32)])[:, None]
        x1, h, loc, ewcol, tab, cnt = _mix(
            x2d, mod, w_in[l].astype(BF16), conv_w[l], w_out_conv[l].astype(BF16),
            w_pool[l].astype(BF16), pool_scale[l][None, :], w_o[l].astype(BF16),
            ln1_g[l][None, :], ln1_b[l][None, :], wr, rb, batch=bsz, seq=seq)
        pad_start, pad_row, pad_len, blk0, nblk, weight_sel, n_used, n_blocks = _block_layout(
            cnt[:, 0], n * TOP_K)
        xs = _dispatch(pad_start, pad_row, pad_len, n_used, tab, loc, h, n_blocks)
        ys = _experts(blk0, nblk, weight_sel, n_used, xs, w_gate[l], w_up[l], w_down[l])
        x2d = _combine(pad_start, tab, x1, mod, ln2_g[l][None, :], ln2_b[l][None, :],
                       ewcol, ys, seq=seq)
    return x2d.reshape(bsz, seq, d)
```

```python
import functools

import jax
import jax.numpy as jnp
from jax import lax
from jax.experimental import pallas as pl
from jax.experimental.pallas import tpu as pltpu

F32 = jnp.float32
BF16 = jnp.bfloat16
I32 = jnp.int32

LANES = 128
SUBLANES = 8
VMEM_LIMIT_BYTES = 56 * 1024 * 1024

N_POOL_GROUPS = 4
POOL_WINDOWS = (2, 4, 8, 16)
CONV_WIDTH = 3
N_GROUPS = 8
EXPERTS_PER_GROUP = 8
N_EXPERTS = N_GROUPS * EXPERTS_PER_GROUP
TOP_K = 2
N_MOD = 6
LN_EPS = 1e-5
DEPTH = 1
DEEPNORM_ALPHA = (2.0 * DEPTH) ** 0.25

MIX_ROWS = 512
EXPERT_ROWS = 256
HALO = SUBLANES
PACK_SUBLANES = 4


def _split_bf16(a):
    hi = a.astype(BF16)
    lo = (a - hi.astype(F32)).astype(BF16)
    return hi, lo


def _dot(a, b):
    return jnp.dot(a, b, preferred_element_type=F32)


def _dot3(a, b):
    a_hi, a_lo = _split_bf16(a)
    b_hi, b_lo = _split_bf16(b)
    return _dot(a_hi, b_hi) + _dot(a_lo, b_hi) + _dot(a_hi, b_lo)


def _dot3_wide(a, b):
    a_hi, a_lo = _split_bf16(a)
    n = b.shape[1]
    b_both = jnp.concatenate(_split_bf16(b), axis=1)
    hi = _dot(a_hi, b_both)
    lo = _dot(a_lo, b_both)
    return hi[:, :n] + hi[:, n:] + lo[:, :n] + lo[:, n:]


def _layer_norm(r, g, b):
    mu = jnp.mean(r, axis=-1, keepdims=True)
    d = r - mu
    var = jnp.mean(d * d, axis=-1, keepdims=True)
    return d * lax.rsqrt(var + LN_EPS) * g + b


def _packed_shape(rows):
    return (rows, PACK_SUBLANES, LANES)


def _plane(ref, s, row0, rows):
    flat = ref.reshape(ref.shape[0] * PACK_SUBLANES, LANES)
    return flat.at[pl.ds(row0 * PACK_SUBLANES + s, rows, stride=PACK_SUBLANES), :]


def _pack_rows(ref, val, row0=0, *, is_bf16_exact=False):
    rows, d = val.shape
    if not is_bf16_exact:
        val = val.astype(BF16).astype(F32)
    bits = lax.bitcast_convert_type(val, I32)
    for s in range(PACK_SUBLANES):
        hi = bits[:, s * LANES:(s + 1) * LANES]
        lo = bits[:, d // 2 + s * LANES:d // 2 + (s + 1) * LANES]
        _plane(ref, s, row0, rows)[...] = jnp.bitwise_or(hi, lax.shift_right_logical(lo, 16))


def _unpack_rows(ref, row0=0, rows=None):
    rows = ref.shape[0] - row0 if rows is None else rows
    his, los = [], []
    for s in range(PACK_SUBLANES):
        word = _plane(ref, s, row0, rows)[...]
        his.append(lax.bitcast_convert_type(jnp.bitwise_and(word, jnp.int32(-65536)), F32))
        los.append(lax.bitcast_convert_type(lax.shift_left(word, 16), F32))
    return jnp.concatenate(his + los, axis=1).astype(BF16)


def _rows(ref, row, n_rows):
    return ref.at[pl.ds(row, n_rows)]


def _ada_kernel(c_ref, w_ref, b_ref, o_ref):
    cc = c_ref[...]
    act = cc * jax.nn.sigmoid(cc)
    o_ref[...] = _dot3(act, w_ref[...]) + b_ref[...]


def _ada(c_pad, w_ada, b_ada):
    rows, d = c_pad.shape
    n_out = w_ada.shape[1]
    return pl.pallas_call(
        _ada_kernel,
        grid=(n_out // d,),
        in_specs=[
            pl.BlockSpec((rows, d), lambda j: (0, 0)),
            pl.BlockSpec((d, d), lambda j: (0, j)),
            pl.BlockSpec((1, d), lambda j: (0, j)),
        ],
        out_specs=pl.BlockSpec((rows, d), lambda j: (0, j)),
        out_shape=jax.ShapeDtypeStruct((rows, n_out), F32),
        compiler_params=pltpu.CompilerParams(
            dimension_semantics=("arbitrary",), vmem_limit_bytes=VMEM_LIMIT_BYTES),
        name="ada",
    )(c_pad, w_ada, b_ada)


def _mix_kernel(x_ref, mod_ref, win_ref, convw_ref, woc_ref, wpool_ref, pscale_ref, wo_ref,
                ln1g_ref, ln1b_ref, wr_ref, rb_ref, mod_prev_ref,
                x1_ref, h_ref, loc_ref, ewcol_ref, tab_ref, cnt_ref,
                cvbuf, zbuf, s2buf, s4buf, s8buf, base_ref, res_buf, *, tiles_per_seq):
    t_rows, d = x_ref.shape
    pg = d // N_POOL_GROUPS
    step = pl.program_id(0)
    n_tiles = pl.num_programs(0) - 1
    s = jnp.minimum(step, n_tiles - 1) % tiles_per_seq
    shift_bufs = (cvbuf, zbuf, s2buf, s4buf, s8buf)

    @pl.when(step == 0)
    def _():
        base_ref[...] = jnp.zeros_like(base_ref)
        res_buf[...] = jnp.zeros_like(res_buf)

    @pl.when(s == 0)
    def _():
        for buf in shift_bufs:
            buf[0:HALO, :] = jnp.zeros((HALO, buf.shape[1]), F32)

    @pl.when(s > 0)
    def _():
        for buf in shift_bufs:
            buf[0:HALO, :] = buf[t_rows:t_rows + HALO, :]

    def lagged_tile():
        x1 = _layer_norm(res_buf[...], ln1g_ref[...], ln1b_ref[...])
        x1_ref[...] = x1
        h = x1 * (1.0 + mod_prev_ref[0, 4:5, :]) + mod_prev_ref[0, 3:4, :]
        h_ref[...] = h.astype(BF16)
        lt = _dot3_wide(h, wr_ref[...]).T + rb_ref[...]
        _route_tile(lt, step > 0, base_ref, loc_ref, ewcol_ref, tab_ref, cnt_ref)

    @pl.when(step == n_tiles)
    def _():
        lagged_tile()

    @pl.when(step < n_tiles)
    def _():
        def shifted(buf, k):
            return buf[HALO - k:HALO - k + t_rows, :]

        xt = x_ref[...]
        shift1 = mod_ref[0, 0:1, :]
        scale1 = mod_ref[0, 1:2, :]
        gate1 = mod_ref[0, 2:3, :]

        u = (xt * (1.0 + scale1) + shift1).astype(BF16)

        def proj(i):
            return _dot(u, win_ref[:, i * d:(i + 1) * d])


        a_c = proj(2)
        a_val = proj(0)
        a_b = proj(1)
        cv = a_c * a_val

        lagged_tile()

        z = proj(3)
        cvbuf[HALO:HALO + t_rows, :] = cv
        conv = convw_ref[CONV_WIDTH - 1:CONV_WIDTH, :] * cv
        for k in range(1, CONV_WIDTH):
            conv = conv + convw_ref[CONV_WIDTH - 1 - k:CONV_WIDTH - k, :] * shifted(cvbuf, k)
        y_a = _dot((conv * a_b).astype(BF16), woc_ref[...])

        zbuf[HALO:HALO + t_rows, :] = z
        s2 = z + shifted(zbuf, 1)
        s2buf[HALO:HALO + t_rows, :] = s2[:, pg:]
        s4 = s2[:, pg:] + shifted(s2buf, 2)
        s4buf[HALO:HALO + t_rows, :] = s4[:, pg:]
        s8 = s4[:, pg:] + shifted(s4buf, 4)
        s8buf[HALO:HALO + t_rows, :] = s8[:, pg:]
        s16 = s8[:, pg:] + shifted(s8buf, 8)
        win_sums = (s2[:, :pg], s4[:, :pg], s8[:, :pg], s16)
        t_pos = s * t_rows + lax.broadcasted_iota(I32, (t_rows, 1), 0) + 1
        g_a = proj(4)
        y_b_parts = []
        for gi, w in enumerate(POOL_WINDOWS):
            count = jnp.minimum(t_pos, w).astype(F32)
            pooled = win_sums[gi] / count - z[:, gi * pg:(gi + 1) * pg]
            y_b_parts.append(_dot(pooled.astype(BF16), wpool_ref[gi]))
        g_b = proj(5)
        y_b = jnp.concatenate(y_b_parts, axis=1) * pscale_ref[...]

        merged = jax.nn.sigmoid(g_a) * y_a + jax.nn.sigmoid(g_b) * y_b
        mix = _dot(merged.astype(BF16), wo_ref[...])

        res_buf[...] = DEEPNORM_ALPHA * xt + (1.0 + gate1) * mix


def _route_tile(lt, valid, base_ref, loc_ref, ewcol_ref, tab_ref, cnt_ref):
    t_rows = lt.shape[1]
    iota8 = lax.broadcasted_iota(I32, (SUBLANES, t_rows), 0)
    lg = lt[N_EXPERTS:N_EXPERTS + N_GROUPS, :]
    g_max = jnp.max(lg, axis=0, keepdims=True)
    g_top = jnp.min(jnp.where(lg == g_max, iota8, N_GROUPS), axis=0, keepdims=True)
    g_top_p = 1.0 / jnp.sum(jnp.exp(lg - g_max), axis=0, keepdims=True)
    sel = jnp.zeros((EXPERTS_PER_GROUP, t_rows), F32)
    for g in range(N_GROUPS):
        sel = jnp.where(g_top == g, lt[g * EXPERTS_PER_GROUP:(g + 1) * EXPERTS_PER_GROUP, :], sel)
    m1 = jnp.max(sel, axis=0, keepdims=True)
    i1 = jnp.min(jnp.where(sel == m1, iota8, EXPERTS_PER_GROUP), axis=0, keepdims=True)
    sel2 = jnp.where(iota8 == i1, -jnp.inf, sel)
    m2 = jnp.max(sel2, axis=0, keepdims=True)
    i2 = jnp.min(jnp.where(sel2 == m2, iota8, EXPERTS_PER_GROUP), axis=0, keepdims=True)
    ratio = jnp.exp(m2 - m1)
    p1 = 1.0 / (1.0 + ratio)
    w1 = g_top_p * p1
    w2 = g_top_p * (ratio * p1)
    e1 = g_top * EXPERTS_PER_GROUP + i1
    e2 = g_top * EXPERTS_PER_GROUP + i2

    iota_e = lax.broadcasted_iota(I32, (N_EXPERTS, t_rows), 0)
    hit1 = jnp.logical_and(iota_e == e1, valid)
    hit2 = jnp.logical_and(iota_e == e2, valid)
    onehot = jnp.where(jnp.logical_or(hit1, hit2), 1.0, 0.0)
    before = (lax.broadcasted_iota(I32, (t_rows, t_rows), 0)
              < lax.broadcasted_iota(I32, (t_rows, t_rows), 1))
    prefix = _dot(onehot.astype(BF16), jnp.where(before, 1.0, 0.0).astype(BF16))
    tile_cnt = jnp.sum(onehot, axis=1, keepdims=True)
    lower = (lax.broadcasted_iota(I32, (N_EXPERTS, N_EXPERTS), 1)
             < lax.broadcasted_iota(I32, (N_EXPERTS, N_EXPERTS), 0))
    lower = jnp.where(lower, 1.0, 0.0).astype(BF16)
    cnt_hi, cnt_lo = _split_bf16(jnp.broadcast_to(tile_cnt, (N_EXPERTS, LANES)))
    loc_start = (_dot(lower, cnt_hi) + _dot(lower, cnt_lo))[:, 0:1]
    pos = prefix + loc_start
    loc1 = jnp.sum(jnp.where(hit1, pos, 0.0), axis=0, keepdims=True)
    loc2 = jnp.sum(jnp.where(hit2, pos, 0.0), axis=0, keepdims=True)
    base = base_ref[...]
    base_ref[...] = base + tile_cnt
    cnt_ref[...] = jnp.broadcast_to(base + tile_cnt, cnt_ref.shape).astype(I32)

    diag = (lax.broadcasted_iota(I32, (N_EXPERTS, LANES), 0)
            == lax.broadcasted_iota(I32, (N_EXPERTS, LANES), 1))

    def to_lanes(col):
        return jnp.sum(jnp.where(diag, col, 0.0), axis=0, keepdims=True)

    iota8l = lax.broadcasted_iota(I32, (SUBLANES, LANES), 0)
    min_len = jnp.min(jnp.broadcast_to(tile_cnt, (N_EXPERTS, LANES)), axis=0, keepdims=True)
    tab_ref[...] = jnp.where(
        iota8l == TAB_LEN, to_lanes(tile_cnt),
        jnp.where(iota8l == TAB_RANK, to_lanes(base),
                  jnp.where(iota8l == TAB_LOC, to_lanes(loc_start),
                            jnp.where(iota8l == TAB_MIN_LEN, min_len, 0.0)))).astype(I32)

    loc_ref[...] = jnp.where(iota8 == 0, loc1, jnp.where(iota8 == 1, loc2, 0.0)).astype(I32)
    iota_l = lax.broadcasted_iota(I32, (LANES, t_rows), 0)
    ewcol_ref[...] = jnp.where(
        iota_l == 0, w1, jnp.where(iota_l == 1, w2,
                                   jnp.where(iota_l == 2, loc1, jnp.where(iota_l == 3, loc2, 0.0)))).T


def _mix(x2d, mod, w_in, conv_w, w_oc, w_pool, pool_scale, w_o, ln1_g, ln1_b, wr, rb,
         *, batch, seq):
    n, d = x2d.shape
    t = MIX_ROWS
    n_s = seq // t
    n_tiles = batch * n_s
    const = lambda shape: pl.BlockSpec(shape, lambda i: (0,) * len(shape),
                                       pipeline_mode=pl.Buffered(1))
    cur = lambda i: jnp.minimum(i, n_tiles - 1)
    prev = lambda i: jnp.maximum(i - 1, 0)
    pg = d // N_POOL_GROUPS
    return pl.pallas_call(
        functools.partial(_mix_kernel, tiles_per_seq=n_s),
        grid=(n_tiles + 1,),
        in_specs=[
            pl.BlockSpec((t, d), lambda i: (cur(i), 0)),
            pl.BlockSpec((1, N_MOD, d), lambda i: (cur(i) // n_s, 0, 0)),
            const(w_in.shape), const(conv_w.shape), const(w_oc.shape), const(w_pool.shape),
            const(pool_scale.shape), const(w_o.shape), const(ln1_g.shape), const(ln1_b.shape),
            const(wr.shape), const(rb.shape),
            pl.BlockSpec((1, N_MOD, d), lambda i: (prev(i) // n_s, 0, 0)),
        ],
        out_specs=[
            pl.BlockSpec((t, d), lambda i: (prev(i), 0)),
            pl.BlockSpec((t, d), lambda i: (prev(i), 0)),
            pl.BlockSpec((SUBLANES, t), lambda i: (0, prev(i))),
            pl.BlockSpec((t, LANES), lambda i: (prev(i), 0)),
            pl.BlockSpec((SUBLANES, LANES), lambda i: (prev(i), 0)),
            pl.BlockSpec((N_EXPERTS, LANES), lambda i: (0, 0)),
        ],
        out_shape=[
            jax.ShapeDtypeStruct((n, d), F32),
            jax.ShapeDtypeStruct((n, d), BF16),
            jax.ShapeDtypeStruct((SUBLANES, n), I32),
            jax.ShapeDtypeStruct((n, LANES), F32),
            jax.ShapeDtypeStruct((n // t * SUBLANES, LANES), I32),
            jax.ShapeDtypeStruct((N_EXPERTS, LANES), I32),
        ],
        scratch_shapes=[
            pltpu.VMEM((t + HALO, d), F32),
            pltpu.VMEM((t + HALO, d), F32),
            pltpu.VMEM((t + HALO, 3 * pg), F32),
            pltpu.VMEM((t + HALO, 2 * pg), F32),
            pltpu.VMEM((t + HALO, pg), F32),
            pltpu.VMEM((N_EXPERTS, 1), F32),
            pltpu.VMEM((t, d), F32),
        ],
        compiler_params=pltpu.CompilerParams(
            dimension_semantics=("arbitrary",), vmem_limit_bytes=VMEM_LIMIT_BYTES),
        name="mix",
    )(x2d, mod, w_in, conv_w, w_oc, w_pool, pool_scale, w_o, ln1_g, ln1_b, wr, rb, mod)


TAB_LEN, TAB_RANK, TAB_LOC, TAB_MIN_LEN = 0, 1, 2, 3
FILL_SEM = 2


def _strip_copies(tab_ref, pstart_ref, local_ref, sorted_hbm, sem, *, to_hbm, local_row0=0):
    def start(e):
        n_rows = tab_ref[TAB_LEN, e]
        local = _rows(local_ref, local_row0 + tab_ref[TAB_LOC, e], n_rows)
        remote = _rows(sorted_hbm, pstart_ref[e] + tab_ref[TAB_RANK, e], n_rows)
        src, dst = (local, remote) if to_hbm else (remote, local)
        pltpu.make_async_copy(src, dst, sem).start()

    @pl.when(tab_ref[TAB_MIN_LEN, 0] > 0)
    def _():
        for e in range(N_EXPERTS):
            start(e)

    @pl.when(tab_ref[TAB_MIN_LEN, 0] <= 0)
    def _():
        def strip(e, carry):
            @pl.when(tab_ref[TAB_LEN, e] > 0)
            def _():
                start(e)
            return carry

        lax.fori_loop(0, N_EXPERTS, strip, 0)


def _dispatch_kernel(pstart_ref, padrow_ref, padlen_ref, nused_ref, tab_ref, loc_ref, h_ref, xs_hbm,
                     xloc0, xloc1, zero_blk, sem):
    i = pl.program_id(0)
    n_steps = pl.num_programs(0)
    t_rows = h_ref.shape[0]
    n_loc = TOP_K * t_rows
    blk_tok = zero_blk.shape[0]
    n_blocks = xs_hbm.shape[0] // blk_tok
    xlocs = (xloc0, xloc1)

    def zero_fills(act):
        def pad_rows(e, carry):
            @pl.when(padlen_ref[e] > 0)
            def _():
                act(pltpu.make_async_copy(
                    _rows(zero_blk, 0, padlen_ref[e]),
                    _rows(xs_hbm, padrow_ref[e], padlen_ref[e]), sem.at[FILL_SEM]))
            return carry

        def unused_block(b, carry):
            act(pltpu.make_async_copy(
                zero_blk, _rows(xs_hbm, b * blk_tok, blk_tok), sem.at[FILL_SEM]))
            return carry

        lax.fori_loop(0, N_EXPERTS, pad_rows, 0)
        lax.fori_loop(nused_ref[0], n_blocks, unused_block, 0)

    @pl.when(i == 0)
    def _():
        zero_blk[...] = jnp.zeros_like(zero_blk)
        zero_fills(lambda copy: copy.start())

    def strips_done(slot):
        pltpu.make_async_copy(xlocs[slot], _rows(xs_hbm, 0, n_loc), sem.at[slot]).wait()

    for slot in range(2):
        @pl.when(i % 2 == slot)
        def _():
            @pl.when(i >= 2)
            def _():
                strips_done(slot)

            iota_p = lax.broadcasted_iota(I32, (n_loc, t_rows), 0)
            pick = jnp.logical_or(iota_p == loc_ref[0:1, :], iota_p == loc_ref[1:2, :])
            xsorted = _dot(jnp.where(pick, 1.0, 0.0).astype(BF16), h_ref[...])
            _pack_rows(xlocs[slot], xsorted, is_bf16_exact=True)
            _strip_copies(tab_ref, pstart_ref, xlocs[slot], xs_hbm, sem.at[slot], to_hbm=True)

    @pl.when(i == n_steps - 1)
    def _():
        for slot in range(2):
            @pl.when(n_steps > slot)
            def _():
                strips_done(slot)
        zero_fills(lambda copy: copy.wait())


def _dispatch(pad_start, pad_row, pad_len, n_used, tab, loc, h, n_blocks):
    n, d = h.shape
    t = MIX_ROWS
    grid_spec = pltpu.PrefetchScalarGridSpec(
        num_scalar_prefetch=4,
        grid=(n // t,),
        in_specs=[
            pl.BlockSpec((SUBLANES, LANES), lambda i, *_: (i, 0), memory_space=pltpu.SMEM),
            pl.BlockSpec((SUBLANES, t), lambda i, *_: (0, i)),
            pl.BlockSpec((t, d), lambda i, *_: (i, 0)),
        ],
        out_specs=pl.BlockSpec(memory_space=pl.ANY),
        scratch_shapes=[
            pltpu.VMEM(_packed_shape(TOP_K * t), I32),
            pltpu.VMEM(_packed_shape(TOP_K * t), I32),
            pltpu.VMEM(_packed_shape(EXPERT_ROWS), I32),
            pltpu.SemaphoreType.DMA((3,)),
        ],
    )
    return pl.pallas_call(
        _dispatch_kernel,
        grid_spec=grid_spec,
        out_shape=jax.ShapeDtypeStruct(_packed_shape(n_blocks * EXPERT_ROWS), I32),
        compiler_params=pltpu.CompilerParams(
            dimension_semantics=("arbitrary",), vmem_limit_bytes=VMEM_LIMIT_BYTES,
            has_side_effects=True),
        name="dispatch",
    )(pad_start, pad_row, pad_len, n_used, tab, loc, h)


IN_SLOTS = 6
AHEAD = 4
OUT_SLOTS = 4


def _expert_kernel(blk0_ref, nblk_ref, wsel_ref, nused_ref, xs_hbm, wg_ref, wu_ref, wd_ref, ys_hbm,
                   xring, yring, wg_bf, wu_bf, wd_bf, sem_in, sem_out):
    del wsel_ref
    e = pl.program_id(0)
    blk_tok = xring.shape[0] // IN_SLOTS
    n_used = nused_ref[0]
    n_blk = nblk_ref[e]

    def in_copy(g):
        slot = g % IN_SLOTS
        return pltpu.make_async_copy(
            _rows(xs_hbm, g * blk_tok, blk_tok), _rows(xring, slot * blk_tok, blk_tok),
            sem_in.at[slot])

    def out_copy(g):
        slot = g % OUT_SLOTS
        return pltpu.make_async_copy(
            _rows(yring, slot * blk_tok, blk_tok), _rows(ys_hbm, g * blk_tok, blk_tok),
            sem_out.at[slot])

    def request(g):
        @pl.when(g < n_used)
        def _():
            in_copy(g).start()

    def retire(g):
        @pl.when(g >= OUT_SLOTS)
        def _():
            out_copy(g - OUT_SLOTS).wait()

    def mlp(*blocks):
        xbs = [_unpack_rows(xring, (g % IN_SLOTS) * blk_tok, blk_tok) for g in blocks]
        gates = [_dot(xb, wg_bf[...]) for xb in xbs]
        ups = [_dot(xb, wu_bf[...]) for xb in xbs]
        acts = [(gate * jax.nn.sigmoid(gate) * up).astype(BF16) for gate, up in zip(gates, ups)]
        for g, act in zip(blocks, acts):
            _pack_rows(yring, _dot(act, wd_bf[...]), (g % OUT_SLOTS) * blk_tok)

    @pl.when(e == 0)
    def _():
        for g in range(AHEAD):
            request(g)

    @pl.when(n_blk > 0)
    def _():
        wg_bf[...] = wg_ref[0].astype(BF16)
        wu_bf[...] = wu_ref[0].astype(BF16)
        wd_bf[...] = wd_ref[0].astype(BF16)

    def pair(p, carry):
        g = blk0_ref[e] + 2 * p
        in_copy(g).wait()
        in_copy(g + 1).wait()
        request(g + AHEAD)
        request(g + AHEAD + 1)
        retire(g)
        retire(g + 1)
        mlp(g, g + 1)
        out_copy(g).start()
        out_copy(g + 1).start()
        return carry

    lax.fori_loop(0, n_blk // 2, pair, 0)

    @pl.when(n_blk % 2 == 1)
    def _():
        g = blk0_ref[e] + n_blk - 1
        in_copy(g).wait()
        request(g + AHEAD)
        retire(g)
        mlp(g)
        out_copy(g).start()

    @pl.when(e == pl.num_programs(0) - 1)
    def _():
        for k in range(1, OUT_SLOTS + 1):
            @pl.when(n_used >= k)
            def _():
                out_copy(n_used - k).wait()


def _experts(blk0, nblk, wsel, n_used, xs, w_gate, w_up, w_down):
    bm = EXPERT_ROWS
    n_exp, d, de = w_gate.shape
    wblk = lambda e, b0, nb, ws, nu: (ws[e], 0, 0)
    grid_spec = pltpu.PrefetchScalarGridSpec(
        num_scalar_prefetch=4,
        grid=(n_exp,),
        in_specs=[
            pl.BlockSpec(memory_space=pl.ANY),
            pl.BlockSpec((1, d, de), wblk),
            pl.BlockSpec((1, d, de), wblk),
            pl.BlockSpec((1, de, d), wblk),
        ],
        out_specs=pl.BlockSpec(memory_space=pl.ANY),
        scratch_shapes=[
            pltpu.VMEM(_packed_shape(IN_SLOTS * bm), I32),
            pltpu.VMEM(_packed_shape(OUT_SLOTS * bm), I32),
            pltpu.VMEM((d, de), BF16),
            pltpu.VMEM((d, de), BF16),
            pltpu.VMEM((de, d), BF16),
            pltpu.SemaphoreType.DMA((IN_SLOTS,)),
            pltpu.SemaphoreType.DMA((OUT_SLOTS,)),
        ],
    )
    n_prefetch = 4
    return pl.pallas_call(
        _expert_kernel,
        grid_spec=grid_spec,
        out_shape=jax.ShapeDtypeStruct(xs.shape, xs.dtype),
        input_output_aliases={n_prefetch: 0},
        compiler_params=pltpu.CompilerParams(
            dimension_semantics=("arbitrary",), vmem_limit_bytes=VMEM_LIMIT_BYTES,
            has_side_effects=True),
        name="experts",
    )(blk0, nblk, wsel, n_used, xs, w_gate, w_up, w_down)


COMBINE_SLOTS = 3


def _combine_kernel(pstart_ref, tab_ref, tab1_ref, tab2_ref, x1_ref, mod_ref, ln2g_ref, ln2b_ref,
                    ewcol_ref, ys_hbm, out_ref, yring, sem):
    i = pl.program_id(0)
    n_tiles = pl.num_programs(0)
    t_rows = x1_ref.shape[0]
    n_loc = TOP_K * t_rows
    ahead = COMBINE_SLOTS - 1

    def fetch(table_ref, tile):
        slot = tile % COMBINE_SLOTS
        _strip_copies(table_ref, pstart_ref, yring, ys_hbm, sem.at[slot], to_hbm=False,
                      local_row0=slot * n_loc)

    @pl.when(i == 0)
    def _():
        fetch(tab_ref, 0)

        @pl.when(n_tiles > 1)
        def _():
            fetch(tab1_ref, 1)

    @pl.when(i + ahead < n_tiles)
    def _():
        fetch(tab2_ref, i + ahead)

    slot = i % COMBINE_SLOTS
    pltpu.make_async_copy(
        _rows(ys_hbm, 0, n_loc), _rows(yring, slot * n_loc, n_loc), sem.at[slot]).wait()
    ew = ewcol_ref[...]
    lane = lax.broadcasted_iota(I32, (t_rows, n_loc), 1)
    weights = jnp.where(lane == ew[:, 2:3].astype(I32), ew[:, 0:1],
                        jnp.where(lane == ew[:, 3:4].astype(I32), ew[:, 1:2], 0.0))
    y = _dot(weights.astype(BF16), _unpack_rows(yring, slot * n_loc, n_loc))
    gate2 = mod_ref[0, 5:6, :]
    out_ref[...] = _layer_norm(DEEPNORM_ALPHA * x1_ref[...] + (1.0 + gate2) * y,
                               ln2g_ref[...], ln2b_ref[...])


def _combine(pad_start, tab, x1, mod, ln2_g, ln2_b, ewcol, ys, *, seq):
    n, d = x1.shape
    t = MIX_ROWS
    per_seq = seq // t
    n_steps = n // t
    row_blk = lambda i, ps: (i, 0)
    later = lambda k: (lambda i, ps: (jnp.minimum(i + k, n_steps - 1), 0))
    grid_spec = pltpu.PrefetchScalarGridSpec(
        num_scalar_prefetch=1,
        grid=(n_steps,),
        in_specs=[
            pl.BlockSpec((SUBLANES, LANES), row_blk, memory_space=pltpu.SMEM),
            pl.BlockSpec((SUBLANES, LANES), later(1), memory_space=pltpu.SMEM),
            pl.BlockSpec((SUBLANES, LANES), later(COMBINE_SLOTS - 1), memory_space=pltpu.SMEM),
            pl.BlockSpec((t, d), row_blk),
            pl.BlockSpec((1, N_MOD, d), lambda i, ps: (i // per_seq, 0, 0)),
            pl.BlockSpec((1, d), lambda i, ps: (0, 0)),
            pl.BlockSpec((1, d), lambda i, ps: (0, 0)),
            pl.BlockSpec((t, LANES), row_blk),
            pl.BlockSpec(memory_space=pl.ANY),
        ],
        out_specs=pl.BlockSpec((t, d), row_blk),
        scratch_shapes=[
            pltpu.VMEM(_packed_shape(COMBINE_SLOTS * TOP_K * t), I32),
            pltpu.SemaphoreType.DMA((COMBINE_SLOTS,)),
        ],
    )
    return pl.pallas_call(
        _combine_kernel,
        grid_spec=grid_spec,
        out_shape=jax.ShapeDtypeStruct((n, d), F32),
        compiler_params=pltpu.CompilerParams(
            dimension_semantics=("arbitrary",), vmem_limit_bytes=VMEM_LIMIT_BYTES),
        name="combine",
    )(pad_start, tab, tab, tab, x1, mod, ln2_g, ln2_b, ewcol, ys)


def _block_layout(counts, n_assign):
    bm = EXPERT_ROWS
    n_blocks = -(-n_assign // bm) + N_EXPERTS
    padded = (counts + bm - 1) // bm * bm
    pad_end = jnp.cumsum(padded)
    pad_start = (pad_end - padded).astype(I32)
    n_used = (pad_end[-1] // bm).astype(I32)
    pad_row = (pad_start + counts).astype(I32)
    pad_len = (padded - counts).astype(I32)
    ids = jnp.arange(N_EXPERTS, dtype=I32)
    prev_used = lax.cummax(jnp.where(padded > 0, ids, -1))
    first_used = jnp.argmax(padded > 0).astype(I32)
    weight_sel = jnp.where(prev_used >= 0, prev_used, first_used).astype(I32)
    return (pad_start, pad_row, pad_len, pad_start // bm, (padded // bm).astype(I32), weight_sel,
            n_used.reshape(1), n_blocks)


def kernel(x, c, w_ada, b_ada, w_in, conv_w, w_out_conv, w_pool, pool_scale, w_o, ln1_g, ln1_b,
           w_group, b_group, w_router, b_router, w_gate, w_up, w_down, ln2_g, ln2_b):
    bsz, seq, d = x.shape
    n = bsz * seq
    assert d == 2 * PACK_SUBLANES * LANES and seq % MIX_ROWS == 0 and w_ada.shape[0] == DEPTH
    assert conv_w.shape[1] == CONV_WIDTH <= HALO and max(POOL_WINDOWS) <= 2 * HALO
    assert w_gate.shape[1] == N_EXPERTS and w_router.shape[2] == N_EXPERTS
    x2d = x.reshape(n, d)
    c_pad = jnp.pad(c, ((0, -bsz % SUBLANES), (0, 0)))
    for l in range(DEPTH):
        mod = _ada(c_pad, w_ada[l], b_ada[l][None, :])[:bsz].reshape(bsz, N_MOD, d)
        wr = jnp.concatenate(
            [w_router[l], w_group[l],
             jnp.zeros((d, LANES - N_EXPERTS - N_GROUPS), F32)], axis=1)
        rb = jnp.concatenate(
            [b_router[l].reshape(-1), b_group[l],
             jnp.zeros((LANES - N_EXPERTS - N_GROUPS,), F32)])[:, None]
        x1, h, loc, ewcol, tab, cnt = _mix(
            x2d, mod, w_in[l].astype(BF16), conv_w[l], w_out_conv[l].astype(BF16),
            w_pool[l].astype(BF16), pool_scale[l][None, :], w_o[l].astype(BF16),
            ln1_g[l][None, :], ln1_b[l][None, :], wr, rb, batch=bsz, seq=seq)
        pad_start, pad_row, pad_len, blk0, nblk, weight_sel, n_used, n_blocks = _block_layout(
            cnt[:, 0], n * TOP_K)
        xs = _dispatch(pad_start, pad_row, pad_len, n_used, tab, loc, h, n_blocks)
        ys = _experts(blk0, nblk, weight_sel, n_used, xs, w_gate[l], w_up[l], w_down[l])
        x2d = _combine(pad_start, tab, x1, mod, ln2_g[l][None, :], ln2_b[l][None, :],
                       ewcol, ys, seq=seq)
    return x2d.reshape(bsz, seq, d)
```

```python
import functools

import jax
import jax.numpy as jnp
from jax import lax
from jax.experimental import pallas as pl
from jax.experimental.pallas import tpu as pltpu

F32 = jnp.float32
BF16 = jnp.bfloat16
I32 = jnp.int32

LANES = 128
SUBLANES = 8
VMEM_LIMIT_BYTES = 56 * 1024 * 1024

N_POOL_GROUPS = 4
POOL_WINDOWS = (2, 4, 8, 16)
CONV_WIDTH = 3
N_GROUPS = 8
EXPERTS_PER_GROUP = 8
N_EXPERTS = N_GROUPS * EXPERTS_PER_GROUP
TOP_K = 2
N_MOD = 6
LN_EPS = 1e-5
DEPTH = 1
DEEPNORM_ALPHA = (2.0 * DEPTH) ** 0.25

MIX_ROWS = 512
EXPERT_ROWS = 256
HALO = SUBLANES
PACK_SUBLANES = 4


def _split_bf16(a):
    hi = a.astype(BF16)
    lo = (a - hi.astype(F32)).astype(BF16)
    return hi, lo


def _dot(a, b):
    return jnp.dot(a, b, preferred_element_type=F32)


def _dot3(a, b):
    a_hi, a_lo = _split_bf16(a)
    b_hi, b_lo = _split_bf16(b)
    return _dot(a_hi, b_hi) + _dot(a_lo, b_hi) + _dot(a_hi, b_lo)


def _dot3_wide(a, b):
    a_hi, a_lo = _split_bf16(a)
    n = b.shape[1]
    b_both = jnp.concatenate(_split_bf16(b), axis=1)
    hi = _dot(a_hi, b_both)
    lo = _dot(a_lo, b_both)
    return hi[:, :n] + hi[:, n:] + lo[:, :n] + lo[:, n:]


def _layer_norm(r, g, b):
    mu = jnp.mean(r, axis=-1, keepdims=True)
    d = r - mu
    var = jnp.mean(d * d, axis=-1, keepdims=True)
    return d * lax.rsqrt(var + LN_EPS) * g + b


def _packed_shape(rows):
    return (rows, PACK_SUBLANES, LANES)


def _plane(ref, s, row0, rows):
    flat = ref.reshape(ref.shape[0] * PACK_SUBLANES, LANES)
    return flat.at[pl.ds(row0 * PACK_SUBLANES + s, rows, stride=PACK_SUBLANES), :]


def _pack_rows(ref, val, row0=0, *, is_bf16_exact=False):
    rows, d = val.shape
    if not is_bf16_exact:
        val = val.astype(BF16).astype(F32)
    bits = lax.bitcast_convert_type(val, I32)
    for s in range(PACK_SUBLANES):
        hi = bits[:, s * LANES:(s + 1) * LANES]
        lo = bits[:, d // 2 + s * LANES:d // 2 + (s + 1) * LANES]
        _plane(ref, s, row0, rows)[...] = jnp.bitwise_or(hi, lax.shift_right_logical(lo, 16))


def _unpack_rows(ref, row0=0, rows=None):
    rows = ref.shape[0] - row0 if rows is None else rows
    his, los = [], []
    for s in range(PACK_SUBLANES):
        word = _plane(ref, s, row0, rows)[...]
        his.append(lax.bitcast_convert_type(jnp.bitwise_and(word, jnp.int32(-65536)), F32))
        los.append(lax.bitcast_convert_type(lax.shift_left(word, 16), F32))
    return jnp.concatenate(his + los, axis=1).astype(BF16)


def _rows(ref, row, n_rows):
    return ref.at[pl.ds(row, n_rows)]


def _ada_kernel(c_ref, w_ref, b_ref, o_ref):
    cc = c_ref[...]
    act = cc * jax.nn.sigmoid(cc)
    o_ref[...] = _dot3(act, w_ref[...]) + b_ref[...]


def _ada(c_pad, w_ada, b_ada):
    rows, d = c_pad.shape
    n_out = w_ada.shape[1]
    return pl.pallas_call(
        _ada_kernel,
        grid=(n_out // d,),
        in_specs=[
            pl.BlockSpec((rows, d), lambda j: (0, 0)),
            pl.BlockSpec((d, d), lambda j: (0, j)),
            pl.BlockSpec((1, d), lambda j: (0, j)),
        ],
        out_specs=pl.BlockSpec((rows, d), lambda j: (0, j)),
        out_shape=jax.ShapeDtypeStruct((rows, n_out), F32),
        compiler_params=pltpu.CompilerParams(
            dimension_semantics=("arbitrary",), vmem_limit_bytes=VMEM_LIMIT_BYTES),
        name="ada",
    )(c_pad, w_ada, b_ada)


def _mix_kernel(x_ref, mod_ref, win_ref, convw_ref, woc_ref, wpool_ref, pscale_ref, wo_ref,
                ln1g_ref, ln1b_ref, wr_ref, rb_ref, mod_prev_ref,
                x1_ref, h_ref, loc_ref, ewcol_ref, tab_ref, cnt_ref,
                cvbuf, zbuf, s2buf, s4buf, s8buf, base_ref, res_buf, *, tiles_per_seq):
    t_rows, d = x_ref.shape
    pg = d // N_POOL_GROUPS
    step = pl.program_id(0)
    n_tiles = pl.num_programs(0) - 1
    s = jnp.minimum(step, n_tiles - 1) % tiles_per_seq
    shift_bufs = (cvbuf, zbuf, s2buf, s4buf, s8buf)

    @pl.when(step == 0)
    def _():
        base_ref[...] = jnp.zeros_like(base_ref)
        res_buf[...] = jnp.zeros_like(res_buf)

    @pl.when(s == 0)
    def _():
        for buf in shift_bufs:
            buf[0:HALO, :] = jnp.zeros((HALO, buf.shape[1]), F32)

    @pl.when(s > 0)
    def _():
        for buf in shift_bufs:
            buf[0:HALO, :] = buf[t_rows:t_rows + HALO, :]

    def lagged_tile():
        x1 = _layer_norm(res_buf[...], ln1g_ref[...], ln1b_ref[...])
        x1_ref[...] = x1
        h = x1 * (1.0 + mod_prev_ref[0, 4:5, :]) + mod_prev_ref[0, 3:4, :]
        h_ref[...] = h.astype(BF16)
        lt = _dot3_wide(h, wr_ref[...]).T + rb_ref[...]
        _route_tile(lt, step > 0, base_ref, loc_ref, ewcol_ref, tab_ref, cnt_ref)

    @pl.when(step == n_tiles)
    def _():
        lagged_tile()

    @pl.when(step < n_tiles)
    def _():
        def shifted(buf, k):
            return buf[HALO - k:HALO - k + t_rows, :]

        xt = x_ref[...]
        shift1 = mod_ref[0, 0:1, :]
        scale1 = mod_ref[0, 1:2, :]
        gate1 = mod_ref[0, 2:3, :]

        u = (xt * (1.0 + scale1) + shift1).astype(BF16)

        def proj(i):
            return _dot(u, win_ref[:, i * d:(i + 1) * d])


        a_c = proj(2)
        a_val = proj(0)
        a_b = proj(1)
        cv = a_c * a_val

        lagged_tile()

        z = proj(3)
        cvbuf[HALO:HALO + t_rows, :] = cv
        conv = convw_ref[CONV_WIDTH - 1:CONV_WIDTH, :] * cv
        for k in range(1, CONV_WIDTH):
            conv = conv + convw_ref[CONV_WIDTH - 1 - k:CONV_WIDTH - k, :] * shifted(cvbuf, k)
        y_a = _dot((conv * a_b).astype(BF16), woc_ref[...])

        zbuf[HALO:HALO + t_rows, :] = z
        s2 = z + shifted(zbuf, 1)
        s2buf[HALO:HALO + t_rows, :] = s2[:, pg:]
        s4 = s2[:, pg:] + shifted(s2buf, 2)
        s4buf[HALO:HALO + t_rows, :] = s4[:, pg:]
        s8 = s4[:, pg:] + shifted(s4buf, 4)
        s8buf[HALO:HALO + t_rows, :] = s8[:, pg:]
        s16 = s8[:, pg:] + shifted(s8buf, 8)
        win_sums = (s2[:, :pg], s4[:, :pg], s8[:, :pg], s16)
        t_pos = s * t_rows + lax.broadcasted_iota(I32, (t_rows, 1), 0) + 1
        g_a = proj(4)
        y_b_parts = []
        for gi, w in enumerate(POOL_WINDOWS):
            count = jnp.minimum(t_pos, w).astype(F32)
            pooled = win_sums[gi] / count - z[:, gi * pg:(gi + 1) * pg]
            y_b_parts.append(_dot(pooled.astype(BF16), wpool_ref[gi]))
        g_b = proj(5)
        y_b = jnp.concatenate(y_b_parts, axis=1) * pscale_ref[...]

        merged = jax.nn.sigmoid(g_a) * y_a + jax.nn.sigmoid(g_b) * y_b
        mix = _dot(merged.astype(BF16), wo_ref[...])

        res_buf[...] = DEEPNORM_ALPHA * xt + (1.0 + gate1) * mix


def _route_tile(lt, valid, base_ref, loc_ref, ewcol_ref, tab_ref, cnt_ref):
    t_rows = lt.shape[1]
    iota8 = lax.broadcasted_iota(I32, (SUBLANES, t_rows), 0)
    lg = lt[N_EXPERTS:N_EXPERTS + N_GROUPS, :]
    g_max = jnp.max(lg, axis=0, keepdims=True)
    g_top = jnp.min(jnp.where(lg == g_max, iota8, N_GROUPS), axis=0, keepdims=True)
    g_top_p = 1.0 / jnp.sum(jnp.exp(lg - g_max), axis=0, keepdims=True)
    sel = jnp.zeros((EXPERTS_PER_GROUP, t_rows), F32)
    for g in range(N_GROUPS):
        sel = jnp.where(g_top == g, lt[g * EXPERTS_PER_GROUP:(g + 1) * EXPERTS_PER_GROUP, :], sel)
    m1 = jnp.max(sel, axis=0, keepdims=True)
    i1 = jnp.min(jnp.where(sel == m1, iota8, EXPERTS_PER_GROUP), axis=0, keepdims=True)
    sel2 = jnp.where(iota8 == i1, -jnp.inf, sel)
    m2 = jnp.max(sel2, axis=0, keepdims=True)
    i2 = jnp.min(jnp.where(sel2 == m2, iota8, EXPERTS_PER_GROUP), axis=0, keepdims=True)
    ratio = jnp.exp(m2 - m1)
    p1 = 1.0 / (1.0 + ratio)
    w1 = g_top_p * p1
    w2 = g_top_p * (ratio * p1)
    e1 = g_top * EXPERTS_PER_GROUP + i1
    e2 = g_top * EXPERTS_PER_GROUP + i2

    iota_e = lax.broadcasted_iota(I32, (N_EXPERTS, t_rows), 0)
    hit1 = jnp.logical_and(iota_e == e1, valid)
    hit2 = jnp.logical_and(iota_e == e2, valid)
    onehot = jnp.where(jnp.logical_or(hit1, hit2), 1.0, 0.0)
    before = (lax.broadcasted_iota(I32, (t_rows, t_rows), 0)
              < lax.broadcasted_iota(I32, (t_rows, t_rows), 1))
    prefix = _dot(onehot.astype(BF16), jnp.where(before, 1.0, 0.0).astype(BF16))
    tile_cnt = jnp.sum(onehot, axis=1, keepdims=True)
    lower = (lax.broadcasted_iota(I32, (N_EXPERTS, N_EXPERTS), 1)
             < lax.broadcasted_iota(I32, (N_EXPERTS, N_EXPERTS), 0))
    lower = jnp.where(lower, 1.0, 0.0).astype(BF16)
    cnt_hi, cnt_lo = _split_bf16(jnp.broadcast_to(tile_cnt, (N_EXPERTS, LANES)))
    loc_start = (_dot(lower, cnt_hi) + _dot(lower, cnt_lo))[:, 0:1]
    pos = prefix + loc_start
    loc1 = jnp.sum(jnp.where(hit1, pos, 0.0), axis=0, keepdims=True)
    loc2 = jnp.sum(jnp.where(hit2, pos, 0.0), axis=0, keepdims=True)
    base = base_ref[...]
    base_ref[...] = base + tile_cnt
    cnt_ref[...] = jnp.broadcast_to(base + tile_cnt, cnt_ref.shape).astype(I32)

    diag = (lax.broadcasted_iota(I32, (N_EXPERTS, LANES), 0)
            == lax.broadcasted_iota(I32, (N_EXPERTS, LANES), 1))

    def to_lanes(col):
        return jnp.sum(jnp.where(diag, col, 0.0), axis=0, keepdims=True)

    iota8l = lax.broadcasted_iota(I32, (SUBLANES, LANES), 0)
    min_len = jnp.min(jnp.broadcast_to(tile_cnt, (N_EXPERTS, LANES)), axis=0, keepdims=True)
    tab_ref[...] = jnp.where(
        iota8l == TAB_LEN, to_lanes(tile_cnt),
        jnp.where(iota8l == TAB_RANK, to_lanes(base),
                  jnp.where(iota8l == TAB_LOC, to_lanes(loc_start),
                            jnp.where(iota8l == TAB_MIN_LEN, min_len, 0.0)))).astype(I32)

    loc_ref[...] = jnp.where(iota8 == 0, loc1, jnp.where(iota8 == 1, loc2, 0.0)).astype(I32)
    iota_l = lax.broadcasted_iota(I32, (LANES, t_rows), 0)
    ewcol_ref[...] = jnp.where(
        iota_l == 0, w1, jnp.where(iota_l == 1, w2,
                                   jnp.where(iota_l == 2, loc1, jnp.where(iota_l == 3, loc2, 0.0)))).T


def _mix(x2d, mod, w_in, conv_w, w_oc, w_pool, pool_scale, w_o, ln1_g, ln1_b, wr, rb,
         *, batch, seq):
    n, d = x2d.shape
    t = MIX_ROWS
    n_s = seq // t
    n_tiles = batch * n_s
    const = lambda shape: pl.BlockSpec(shape, lambda i: (0,) * len(shape),
                                       pipeline_mode=pl.Buffered(1))
    cur = lambda i: jnp.minimum(i, n_tiles - 1)
    prev = lambda i: jnp.maximum(i - 1, 0)
    pg = d // N_POOL_GROUPS
    return pl.pallas_call(
        functools.partial(_mix_kernel, tiles_per_seq=n_s),
        grid=(n_tiles + 1,),
        in_specs=[
            pl.BlockSpec((t, d), lambda i: (cur(i), 0)),
            pl.BlockSpec((1, N_MOD, d), lambda i: (cur(i) // n_s, 0, 0)),
            const(w_in.shape), const(conv_w.shape), const(w_oc.shape), const(w_pool.shape),
            const(pool_scale.shape), const(w_o.shape), const(ln1_g.shape), const(ln1_b.shape),
            const(wr.shape), const(rb.shape),
            pl.BlockSpec((1, N_MOD, d), lambda i: (prev(i) // n_s, 0, 0)),
        ],
        out_specs=[
            pl.BlockSpec((t, d), lambda i: (prev(i), 0)),
            pl.BlockSpec((t, d), lambda i: (prev(i), 0)),
            pl.BlockSpec((SUBLANES, t), lambda i: (0, prev(i))),
            pl.BlockSpec((t, LANES), lambda i: (prev(i), 0)),
            pl.BlockSpec((SUBLANES, LANES), lambda i: (prev(i), 0)),
            pl.BlockSpec((N_EXPERTS, LANES), lambda i: (0, 0)),
        ],
        out_shape=[
            jax.ShapeDtypeStruct((n, d), F32),
            jax.ShapeDtypeStruct((n, d), BF16),
            jax.ShapeDtypeStruct((SUBLANES, n), I32),
            jax.ShapeDtypeStruct((n, LANES), F32),
            jax.ShapeDtypeStruct((n // t * SUBLANES, LANES), I32),
            jax.ShapeDtypeStruct((N_EXPERTS, LANES), I32),
        ],
        scratch_shapes=[
            pltpu.VMEM((t + HALO, d), F32),
            pltpu.VMEM((t + HALO, d), F32),
            pltpu.VMEM((t + HALO, 3 * pg), F32),
            pltpu.VMEM((t + HALO, 2 * pg), F32),
            pltpu.VMEM((t + HALO, pg), F32),
            pltpu.VMEM((N_EXPERTS, 1), F32),
            pltpu.VMEM((t, d), F32),
        ],
        compiler_params=pltpu.CompilerParams(
            dimension_semantics=("arbitrary",), vmem_limit_bytes=VMEM_LIMIT_BYTES),
        name="mix",
    )(x2d, mod, w_in, conv_w, w_oc, w_pool, pool_scale, w_o, ln1_g, ln1_b, wr, rb, mod)


TAB_LEN, TAB_RANK, TAB_LOC, TAB_MIN_LEN = 0, 1, 2, 3
DISPATCH_SLOTS = 3


def _strip_copies(tab_ref, pstart_ref, local_ref, sorted_hbm, sem, *, to_hbm, local_row0=0):
    def start(e):
        n_rows = tab_ref[TAB_LEN, e]
        local = _rows(local_ref, local_row0 + tab_ref[TAB_LOC, e], n_rows)
        remote = _rows(sorted_hbm, pstart_ref[e] + tab_ref[TAB_RANK, e], n_rows)
        src, dst = (local, remote) if to_hbm else (remote, local)
        pltpu.make_async_copy(src, dst, sem).start()

    @pl.when(tab_ref[TAB_MIN_LEN, 0] > 0)
    def _():
        for e in range(N_EXPERTS):
            start(e)

    @pl.when(tab_ref[TAB_MIN_LEN, 0] <= 0)
    def _():
        def strip(e, carry):
            @pl.when(tab_ref[TAB_LEN, e] > 0)
            def _():
                start(e)
            return carry

        lax.fori_loop(0, N_EXPERTS, strip, 0)


def _dispatch_kernel(pstart_ref, padrow_ref, padlen_ref, nused_ref, tab_ref, loc_ref, h_ref, xs_hbm,
                     xring, zero_blk, sem):
    i = pl.program_id(0)
    n_steps = pl.num_programs(0)
    t_rows = h_ref.shape[0]
    n_loc = TOP_K * t_rows
    blk_tok = zero_blk.shape[0]
    n_blocks = xs_hbm.shape[0] // blk_tok

    def zero_fills(act):
        def pad_rows(e, carry):
            @pl.when(padlen_ref[e] > 0)
            def _():
                act(pltpu.make_async_copy(
                    _rows(zero_blk, 0, padlen_ref[e]),
                    _rows(xs_hbm, padrow_ref[e], padlen_ref[e]), sem.at[DISPATCH_SLOTS]))
            return carry

        def unused_block(b, carry):
            act(pltpu.make_async_copy(
                zero_blk, _rows(xs_hbm, b * blk_tok, blk_tok), sem.at[DISPATCH_SLOTS]))
            return carry

        lax.fori_loop(0, N_EXPERTS, pad_rows, 0)
        lax.fori_loop(nused_ref[0], n_blocks, unused_block, 0)

    @pl.when(i == 0)
    def _():
        zero_blk[...] = jnp.zeros_like(zero_blk)
        zero_fills(lambda copy: copy.start())

    def strips_done(tile):
        slot = tile % DISPATCH_SLOTS
        pltpu.make_async_copy(
            _rows(xring, slot * n_loc, n_loc), _rows(xs_hbm, 0, n_loc), sem.at[slot]).wait()

    @pl.when(i >= DISPATCH_SLOTS)
    def _():
        strips_done(i - DISPATCH_SLOTS)

    slot = i % DISPATCH_SLOTS
    iota_p = lax.broadcasted_iota(I32, (n_loc, t_rows), 0)
    pick = jnp.logical_or(iota_p == loc_ref[0:1, :], iota_p == loc_ref[1:2, :])
    xsorted = _dot(jnp.where(pick, 1.0, 0.0).astype(BF16), h_ref[...])
    _pack_rows(xring, xsorted, slot * n_loc, is_bf16_exact=True)
    _strip_copies(tab_ref, pstart_ref, xring, xs_hbm, sem.at[slot], to_hbm=True,
                  local_row0=slot * n_loc)

    @pl.when(i == n_steps - 1)
    def _():
        for k in range(DISPATCH_SLOTS):
            @pl.when(n_steps > k)
            def _():
                strips_done(n_steps - 1 - k)
        zero_fills(lambda copy: copy.wait())


def _dispatch(pad_start, pad_row, pad_len, n_used, tab, loc, h, n_blocks):
    n, d = h.shape
    t = MIX_ROWS
    grid_spec = pltpu.PrefetchScalarGridSpec(
        num_scalar_prefetch=4,
        grid=(n // t,),
        in_specs=[
            pl.BlockSpec((SUBLANES, LANES), lambda i, *_: (i, 0), memory_space=pltpu.SMEM),
            pl.BlockSpec((SUBLANES, t), lambda i, *_: (0, i)),
            pl.BlockSpec((t, d), lambda i, *_: (i, 0)),
        ],
        out_specs=pl.BlockSpec(memory_space=pl.ANY),
        scratch_shapes=[
            pltpu.VMEM(_packed_shape(DISPATCH_SLOTS * TOP_K * t), I32),
            pltpu.VMEM(_packed_shape(EXPERT_ROWS), I32),
            pltpu.SemaphoreType.DMA((DISPATCH_SLOTS + 1,)),
        ],
    )
    return pl.pallas_call(
        _dispatch_kernel,
        grid_spec=grid_spec,
        out_shape=jax.ShapeDtypeStruct(_packed_shape(n_blocks * EXPERT_ROWS), I32),
        compiler_params=pltpu.CompilerParams(
            dimension_semantics=("arbitrary",), vmem_limit_bytes=VMEM_LIMIT_BYTES,
            has_side_effects=True),
        name="dispatch",
    )(pad_start, pad_row, pad_len, n_used, tab, loc, h)


IN_SLOTS = 6
AHEAD = 4
OUT_SLOTS = 4


def _expert_kernel(blk0_ref, nblk_ref, wsel_ref, nused_ref, xs_hbm, wg_ref, wu_ref, wd_ref, ys_hbm,
                   xring, yring, wg_bf, wu_bf, wd_bf, sem_in, sem_out):
    del wsel_ref
    e = pl.program_id(0)
    blk_tok = xring.shape[0] // IN_SLOTS
    n_used = nused_ref[0]
    n_blk = nblk_ref[e]

    def in_copy(g):
        slot = g % IN_SLOTS
        return pltpu.make_async_copy(
            _rows(xs_hbm, g * blk_tok, blk_tok), _rows(xring, slot * blk_tok, blk_tok),
            sem_in.at[slot])

    def out_copy(g):
        slot = g % OUT_SLOTS
        return pltpu.make_async_copy(
            _rows(yring, slot * blk_tok, blk_tok), _rows(ys_hbm, g * blk_tok, blk_tok),
            sem_out.at[slot])

    def request(g):
        @pl.when(g < n_used)
        def _():
            in_copy(g).start()

    def retire(g):
        @pl.when(g >= OUT_SLOTS)
        def _():
            out_copy(g - OUT_SLOTS).wait()

    def mlp(*blocks):
        xbs = [_unpack_rows(xring, (g % IN_SLOTS) * blk_tok, blk_tok) for g in blocks]
        gates = [_dot(xb, wg_bf[...]) for xb in xbs]
        ups = [_dot(xb, wu_bf[...]) for xb in xbs]
        acts = [(gate * jax.nn.sigmoid(gate) * up).astype(BF16) for gate, up in zip(gates, ups)]
        for g, act in zip(blocks, acts):
            _pack_rows(yring, _dot(act, wd_bf[...]), (g % OUT_SLOTS) * blk_tok)

    @pl.when(e == 0)
    def _():
        for g in range(AHEAD):
            request(g)

    @pl.when(n_blk > 0)
    def _():
        wg_bf[...] = wg_ref[0].astype(BF16)
        wu_bf[...] = wu_ref[0].astype(BF16)
        wd_bf[...] = wd_ref[0].astype(BF16)

    def pair(p, carry):
        g = blk0_ref[e] + 2 * p
        in_copy(g).wait()
        in_copy(g + 1).wait()
        request(g + AHEAD)
        request(g + AHEAD + 1)
        retire(g)
        retire(g + 1)
        mlp(g, g + 1)
        out_copy(g).start()
        out_copy(g + 1).start()
        return carry

    lax.fori_loop(0, n_blk // 2, pair, 0)

    @pl.when(n_blk % 2 == 1)
    def _():
        g = blk0_ref[e] + n_blk - 1
        in_copy(g).wait()
        request(g + AHEAD)
        retire(g)
        mlp(g)
        out_copy(g).start()

    @pl.when(e == pl.num_programs(0) - 1)
    def _():
        for k in range(1, OUT_SLOTS + 1):
            @pl.when(n_used >= k)
            def _():
                out_copy(n_used - k).wait()


def _experts(blk0, nblk, wsel, n_used, xs, w_gate, w_up, w_down):
    bm = EXPERT_ROWS
    n_exp, d, de = w_gate.shape
    wblk = lambda e, b0, nb, ws, nu: (ws[e], 0, 0)
    grid_spec = pltpu.PrefetchScalarGridSpec(
        num_scalar_prefetch=4,
        grid=(n_exp,),
        in_specs=[
            pl.BlockSpec(memory_space=pl.ANY),
            pl.BlockSpec((1, d, de), wblk),
            pl.BlockSpec((1, d, de), wblk),
            pl.BlockSpec((1, de, d), wblk),
        ],
        out_specs=pl.BlockSpec(memory_space=pl.ANY),
        scratch_shapes=[
            pltpu.VMEM(_packed_shape(IN_SLOTS * bm), I32),
            pltpu.VMEM(_packed_shape(OUT_SLOTS * bm), I32),
            pltpu.VMEM((d, de), BF16),
            pltpu.VMEM((d, de), BF16),
            pltpu.VMEM((de, d), BF16),
            pltpu.SemaphoreType.DMA((IN_SLOTS,)),
            pltpu.SemaphoreType.DMA((OUT_SLOTS,)),
        ],
    )
    n_prefetch = 4
    return pl.pallas_call(
        _expert_kernel,
        grid_spec=grid_spec,
        out_shape=jax.ShapeDtypeStruct(xs.shape, xs.dtype),
        input_output_aliases={n_prefetch: 0},
        compiler_params=pltpu.CompilerParams(
            dimension_semantics=("arbitrary",), vmem_limit_bytes=VMEM_LIMIT_BYTES,
            has_side_effects=True),
        name="experts",
    )(blk0, nblk, wsel, n_used, xs, w_gate, w_up, w_down)


COMBINE_SLOTS = 3


def _combine_kernel(pstart_ref, tab_ref, tab1_ref, tab2_ref, x1_ref, mod_ref, ln2g_ref, ln2b_ref,
                    ewcol_ref, ys_hbm, out_ref, yring, sem):
    i = pl.program_id(0)
    n_tiles = pl.num_programs(0)
    t_rows = x1_ref.shape[0]
    n_loc = TOP_K * t_rows
    ahead = COMBINE_SLOTS - 1

    def fetch(table_ref, tile):
        slot = tile % COMBINE_SLOTS
        _strip_copies(table_ref, pstart_ref, yring, ys_hbm, sem.at[slot], to_hbm=False,
                      local_row0=slot * n_loc)

    @pl.when(i == 0)
    def _():
        fetch(tab_ref, 0)

        @pl.when(n_tiles > 1)
        def _():
            fetch(tab1_ref, 1)

    @pl.when(i + ahead < n_tiles)
    def _():
        fetch(tab2_ref, i + ahead)

    slot = i % COMBINE_SLOTS
    pltpu.make_async_copy(
        _rows(ys_hbm, 0, n_loc), _rows(yring, slot * n_loc, n_loc), sem.at[slot]).wait()
    ew = ewcol_ref[...]
    lane = lax.broadcasted_iota(I32, (t_rows, n_loc), 1)
    weights = jnp.where(lane == ew[:, 2:3].astype(I32), ew[:, 0:1],
                        jnp.where(lane == ew[:, 3:4].astype(I32), ew[:, 1:2], 0.0))
    y = _dot(weights.astype(BF16), _unpack_rows(yring, slot * n_loc, n_loc))
    gate2 = mod_ref[0, 5:6, :]
    out_ref[...] = _layer_norm(DEEPNORM_ALPHA * x1_ref[...] + (1.0 + gate2) * y,
                               ln2g_ref[...], ln2b_ref[...])


def _combine(pad_start, tab, x1, mod, ln2_g, ln2_b, ewcol, ys, *, seq):
    n, d = x1.shape
    t = MIX_ROWS
    per_seq = seq // t
    n_steps = n // t
    row_blk = lambda i, ps: (i, 0)
    later = lambda k: (lambda i, ps: (jnp.minimum(i + k, n_steps - 1), 0))
    grid_spec = pltpu.PrefetchScalarGridSpec(
        num_scalar_prefetch=1,
        grid=(n_steps,),
        in_specs=[
            pl.BlockSpec((SUBLANES, LANES), row_blk, memory_space=pltpu.SMEM),
            pl.BlockSpec((SUBLANES, LANES), later(1), memory_space=pltpu.SMEM),
            pl.BlockSpec((SUBLANES, LANES), later(COMBINE_SLOTS - 1), memory_space=pltpu.SMEM),
            pl.BlockSpec((t, d), row_blk),
            pl.BlockSpec((1, N_MOD, d), lambda i, ps: (i // per_seq, 0, 0)),
            pl.BlockSpec((1, d), lambda i, ps: (0, 0)),
            pl.BlockSpec((1, d), lambda i, ps: (0, 0)),
            pl.BlockSpec((t, LANES), row_blk),
            pl.BlockSpec(memory_space=pl.ANY),
        ],
        out_specs=pl.BlockSpec((t, d), row_blk),
        scratch_shapes=[
            pltpu.VMEM(_packed_shape(COMBINE_SLOTS * TOP_K * t), I32),
            pltpu.SemaphoreType.DMA((COMBINE_SLOTS,)),
        ],
    )
    return pl.pallas_call(
        _combine_kernel,
        grid_spec=grid_spec,
        out_shape=jax.ShapeDtypeStruct((n, d), F32),
        compiler_params=pltpu.CompilerParams(
            dimension_semantics=("arbitrary",), vmem_limit_bytes=VMEM_LIMIT_BYTES),
        name="combine",
    )(pad_start, tab, tab, tab, x1, mod, ln2_g, ln2_b, ewcol, ys)


def _block_layout(counts, n_assign):
    bm = EXPERT_ROWS
    n_blocks = -(-n_assign // bm) + N_EXPERTS
    padded = (counts + bm - 1) // bm * bm
    pad_end = jnp.cumsum(padded)
    pad_start = (pad_end - padded).astype(I32)
    n_used = (pad_end[-1] // bm).astype(I32)
    pad_row = (pad_start + counts).astype(I32)
    pad_len = (padded - counts).astype(I32)
    ids = jnp.arange(N_EXPERTS, dtype=I32)
    prev_used = lax.cummax(jnp.where(padded > 0, ids, -1))
    first_used = jnp.argmax(padded > 0).astype(I32)
    weight_sel = jnp.where(prev_used >= 0, prev_used, first_used).astype(I32)
    return (pad_start, pad_row, pad_len, pad_start // bm, (padded // bm).astype(I32), weight_sel,
            n_used.reshape(1), n_blocks)


def kernel(x, c, w_ada, b_ada, w_in, conv_w, w_out_conv, w_pool, pool_scale, w_o, ln1_g, ln1_b,
           w_group, b_group, w_router, b_router, w_gate, w_up, w_down, ln2_g, ln2_b):
    bsz, seq, d = x.shape
    n = bsz * seq
    assert d == 2 * PACK_SUBLANES * LANES and seq % MIX_ROWS == 0 and w_ada.shape[0] == DEPTH
    assert conv_w.shape[1] == CONV_WIDTH <= HALO and max(POOL_WINDOWS) <= 2 * HALO
    assert w_gate.shape[1] == N_EXPERTS and w_router.shape[2] == N_EXPERTS
    x2d = x.reshape(n, d)
    c_pad = jnp.pad(c, ((0, -bsz % SUBLANES), (0, 0)))
    for l in range(DEPTH):
        mod = _ada(c_pad, w_ada[l], b_ada[l][None, :])[:bsz].reshape(bsz, N_MOD, d)
        wr = jnp.concatenate(
            [w_router[l], w_group[l],
             jnp.zeros((d, LANES - N_EXPERTS - N_GROUPS), F32)], axis=1)
        rb = jnp.concatenate(
            [b_router[l].reshape(-1), b_group[l],
             jnp.zeros((LANES - N_EXPERTS - N_GROUPS,), F32)])[:, None]
        x1, h, loc, ewcol, tab, cnt = _mix(
            x2d, mod, w_in[l].astype(BF16), conv_w[l], w_out_conv[l].astype(BF16),
            w_pool[l].astype(BF16), pool_scale[l][None, :], w_o[l].astype(BF16),
            ln1_g[l][None, :], ln1_b[l][None, :], wr, rb, batch=bsz, seq=seq)
        pad_start, pad_row, pad_len, blk0, nblk, weight_sel, n_used, n_blocks = _block_layout(
            cnt[:, 0], n * TOP_K)
        xs = _dispatch(pad_start, pad_row, pad_len, n_used, tab, loc, h, n_blocks)
        ys = _experts(blk0, nblk, weight_sel, n_used, xs, w_gate[l], w_up[l], w_down[l])
        x2d = _combine(pad_start, tab, x1, mod, ln2_g[l][None, :], ln2_b[l][None, :],
                       ewcol, ys, seq=seq)
    return x2d.reshape(bsz, seq, d)
```

```python
import functools

import jax
import jax.numpy as jnp
from jax import lax
from jax.experimental import pallas as pl
from jax.experimental.pallas import tpu as pltpu

F32 = jnp.float32
BF16 = jnp.bfloat16
I32 = jnp.int32

LANES = 128
SUBLANES = 8
VMEM_LIMIT_BYTES = 56 * 1024 * 1024

N_POOL_GROUPS = 4
POOL_WINDOWS = (2, 4, 8, 16)
CONV_WIDTH = 3
N_GROUPS = 8
EXPERTS_PER_GROUP = 8
N_EXPERTS = N_GROUPS * EXPERTS_PER_GROUP
TOP_K = 2
N_MOD = 6
LN_EPS = 1e-5
DEPTH = 1
DEEPNORM_ALPHA = (2.0 * DEPTH) ** 0.25

MIX_ROWS = 512
EXPERT_ROWS = 256
HALO = SUBLANES
PACK_SUBLANES = 4


def _split_bf16(a):
    hi = a.astype(BF16)
    lo = (a - hi.astype(F32)).astype(BF16)
    return hi, lo


def _dot(a, b):
    return jnp.dot(a, b, preferred_element_type=F32)


def _dot3(a, b):
    a_hi, a_lo = _split_bf16(a)
    b_hi, b_lo = _split_bf16(b)
    return _dot(a_hi, b_hi) + _dot(a_lo, b_hi) + _dot(a_hi, b_lo)


def _dot3_wide(a, b):
    a_hi, a_lo = _split_bf16(a)
    n = b.shape[1]
    b_both = jnp.concatenate(_split_bf16(b), axis=1)
    hi = _dot(a_hi, b_both)
    lo = _dot(a_lo, b_both)
    return hi[:, :n] + hi[:, n:] + lo[:, :n] + lo[:, n:]


def _layer_norm(r, g, b):
    mu = jnp.mean(r, axis=-1, keepdims=True)
    d = r - mu
    var = jnp.mean(d * d, axis=-1, keepdims=True)
    return d * lax.rsqrt(var + LN_EPS) * g + b


def _packed_shape(rows):
    return (rows, PACK_SUBLANES, LANES)


def _plane(ref, s, row0, rows):
    flat = ref.reshape(ref.shape[0] * PACK_SUBLANES, LANES)
    return flat.at[pl.ds(row0 * PACK_SUBLANES + s, rows, stride=PACK_SUBLANES), :]


def _pack_rows(ref, val, row0=0, *, is_bf16_exact=False):
    rows, d = val.shape
    if not is_bf16_exact:
        val = val.astype(BF16).astype(F32)
    bits = lax.bitcast_convert_type(val, I32)
    for s in range(PACK_SUBLANES):
        hi = bits[:, s * LANES:(s + 1) * LANES]
        lo = bits[:, d // 2 + s * LANES:d // 2 + (s + 1) * LANES]
        _plane(ref, s, row0, rows)[...] = jnp.bitwise_or(hi, lax.shift_right_logical(lo, 16))


def _unpack_rows(ref, row0=0, rows=None):
    rows = ref.shape[0] - row0 if rows is None else rows
    his, los = [], []
    for s in range(PACK_SUBLANES):
        word = _plane(ref, s, row0, rows)[...]
        his.append(lax.bitcast_convert_type(jnp.bitwise_and(word, jnp.int32(-65536)), F32))
        los.append(lax.bitcast_convert_type(lax.shift_left(word, 16), F32))
    return jnp.concatenate(his + los, axis=1).astype(BF16)


def _rows(ref, row, n_rows):
    return ref.at[pl.ds(row, n_rows)]


def _ada_kernel(c_ref, w_ref, b_ref, o_ref):
    cc = c_ref[...]
    act = cc * jax.nn.sigmoid(cc)
    o_ref[...] = _dot3(act, w_ref[...]) + b_ref[...]


def _ada(c_pad, w_ada, b_ada):
    rows, d = c_pad.shape
    n_out = w_ada.shape[1]
    return pl.pallas_call(
        _ada_kernel,
        grid=(n_out // d,),
        in_specs=[
            pl.BlockSpec((rows, d), lambda j: (0, 0)),
            pl.BlockSpec((d, d), lambda j: (0, j)),
            pl.BlockSpec((1, d), lambda j: (0, j)),
        ],
        out_specs=pl.BlockSpec((rows, d), lambda j: (0, j)),
        out_shape=jax.ShapeDtypeStruct((rows, n_out), F32),
        compiler_params=pltpu.CompilerParams(
            dimension_semantics=("arbitrary",), vmem_limit_bytes=VMEM_LIMIT_BYTES),
        name="ada",
    )(c_pad, w_ada, b_ada)


def _mix_kernel(x_ref, mod_ref, win_ref, convw_ref, woc_ref, wpool_ref, pscale_ref, wo_ref,
                ln1g_ref, ln1b_ref, wr_ref, rb_ref, mod_prev_ref,
                x1_ref, h_ref, loc_ref, ewcol_ref, tab_ref, cnt_ref,
                cvbuf, zbuf, s2buf, s4buf, s8buf, base_ref, res_buf, *, tiles_per_seq):
    t_rows, d = x_ref.shape
    pg = d // N_POOL_GROUPS
    step = pl.program_id(0)
    n_tiles = pl.num_programs(0) - 1
    s = jnp.minimum(step, n_tiles - 1) % tiles_per_seq
    shift_bufs = (cvbuf, zbuf, s2buf, s4buf, s8buf)

    @pl.when(step == 0)
    def _():
        base_ref[...] = jnp.zeros_like(base_ref)
        res_buf[...] = jnp.zeros_like(res_buf)

    @pl.when(s == 0)
    def _():
        for buf in shift_bufs:
            buf[0:HALO, :] = jnp.zeros((HALO, buf.shape[1]), F32)

    @pl.when(s > 0)
    def _():
        for buf in shift_bufs:
            buf[0:HALO, :] = buf[t_rows:t_rows + HALO, :]

    def lagged_tile():
        x1 = _layer_norm(res_buf[...], ln1g_ref[...], ln1b_ref[...])
        x1_ref[...] = x1
        h = x1 * (1.0 + mod_prev_ref[0, 4:5, :]) + mod_prev_ref[0, 3:4, :]
        h_ref[...] = h.astype(BF16)
        lt = _dot3_wide(h, wr_ref[...]).T + rb_ref[...]
        _route_tile(lt, step > 0, base_ref, loc_ref, ewcol_ref, tab_ref, cnt_ref)

    @pl.when(step == n_tiles)
    def _():
        lagged_tile()

    @pl.when(step < n_tiles)
    def _():
        def shifted(buf, k):
            return buf[HALO - k:HALO - k + t_rows, :]

        xt = x_ref[...]
        shift1 = mod_ref[0, 0:1, :]
        scale1 = mod_ref[0, 1:2, :]
        gate1 = mod_ref[0, 2:3, :]

        u = (xt * (1.0 + scale1) + shift1).astype(BF16)

        def proj(i):
            return _dot(u, win_ref[:, i * d:(i + 1) * d])


        a_c = proj(2)
        a_val = proj(0)
        a_b = proj(1)
        cv = a_c * a_val

        lagged_tile()

        z = proj(3)
        cvbuf[HALO:HALO + t_rows, :] = cv
        conv = convw_ref[CONV_WIDTH - 1:CONV_WIDTH, :] * cv
        for k in range(1, CONV_WIDTH):
            conv = conv + convw_ref[CONV_WIDTH - 1 - k:CONV_WIDTH - k, :] * shifted(cvbuf, k)
        y_a = _dot((conv * a_b).astype(BF16), woc_ref[...])

        zbuf[HALO:HALO + t_rows, :] = z
        s2 = z + shifted(zbuf, 1)
        s2buf[HALO:HALO + t_rows, :] = s2[:, pg:]
        s4 = s2[:, pg:] + shifted(s2buf, 2)
        s4buf[HALO:HALO + t_rows, :] = s4[:, pg:]
        s8 = s4[:, pg:] + shifted(s4buf, 4)
        s8buf[HALO:HALO + t_rows, :] = s8[:, pg:]
        s16 = s8[:, pg:] + shifted(s8buf, 8)
        win_sums = (s2[:, :pg], s4[:, :pg], s8[:, :pg], s16)
        t_pos = s * t_rows + lax.broadcasted_iota(I32, (t_rows, 1), 0) + 1
        g_a = proj(4)
        y_b_parts = []
        for gi, w in enumerate(POOL_WINDOWS):
            count = jnp.minimum(t_pos, w).astype(F32)
            pooled = win_sums[gi] / count - z[:, gi * pg:(gi + 1) * pg]
            y_b_parts.append(_dot(pooled.astype(BF16), wpool_ref[gi]))
        g_b = proj(5)
        y_b = jnp.concatenate(y_b_parts, axis=1) * pscale_ref[...]

        merged = jax.nn.sigmoid(g_a) * y_a + jax.nn.sigmoid(g_b) * y_b
        mix = _dot(merged.astype(BF16), wo_ref[...])

        res_buf[...] = DEEPNORM_ALPHA * xt + (1.0 + gate1) * mix


def _route_tile(lt, valid, base_ref, loc_ref, ewcol_ref, tab_ref, cnt_ref):
    t_rows = lt.shape[1]
    iota8 = lax.broadcasted_iota(I32, (SUBLANES, t_rows), 0)
    lg = lt[N_EXPERTS:N_EXPERTS + N_GROUPS, :]
    g_max = jnp.max(lg, axis=0, keepdims=True)
    g_top = jnp.min(jnp.where(lg == g_max, iota8, N_GROUPS), axis=0, keepdims=True)
    g_top_p = 1.0 / jnp.sum(jnp.exp(lg - g_max), axis=0, keepdims=True)
    sel = jnp.zeros((EXPERTS_PER_GROUP, t_rows), F32)
    for g in range(N_GROUPS):
        sel = jnp.where(g_top == g, lt[g * EXPERTS_PER_GROUP:(g + 1) * EXPERTS_PER_GROUP, :], sel)
    m1 = jnp.max(sel, axis=0, keepdims=True)
    i1 = jnp.min(jnp.where(sel == m1, iota8, EXPERTS_PER_GROUP), axis=0, keepdims=True)
    sel2 = jnp.where(iota8 == i1, -jnp.inf, sel)
    m2 = jnp.max(sel2, axis=0, keepdims=True)
    i2 = jnp.min(jnp.where(sel2 == m2, iota8, EXPERTS_PER_GROUP), axis=0, keepdims=True)
    ratio = jnp.exp(m2 - m1)
    p1 = 1.0 / (1.0 + ratio)
    w1 = g_top_p * p1
    w2 = g_top_p * (ratio * p1)
    e1 = g_top * EXPERTS_PER_GROUP + i1
    e2 = g_top * EXPERTS_PER_GROUP + i2

    iota_e = lax.broadcasted_iota(I32, (N_EXPERTS, t_rows), 0)
    hit1 = jnp.logical_and(iota_e == e1, valid)
    hit2 = jnp.logical_and(iota_e == e2, valid)
    onehot = jnp.where(jnp.logical_or(hit1, hit2), 1.0, 0.0)
    before = (lax.broadcasted_iota(I32, (t_rows, t_rows), 0)
              < lax.broadcasted_iota(I32, (t_rows, t_rows), 1))
    prefix = _dot(onehot.astype(BF16), jnp.where(before, 1.0, 0.0).astype(BF16))
    tile_cnt = jnp.sum(onehot, axis=1, keepdims=True)
    lower = (lax.broadcasted_iota(I32, (N_EXPERTS, N_EXPERTS), 1)
             < lax.broadcasted_iota(I32, (N_EXPERTS, N_EXPERTS), 0))
    lower = jnp.where(lower, 1.0, 0.0).astype(BF16)
    cnt_hi, cnt_lo = _split_bf16(jnp.broadcast_to(tile_cnt, (N_EXPERTS, LANES)))
    loc_start = (_dot(lower, cnt_hi) + _dot(lower, cnt_lo))[:, 0:1]
    pos = prefix + loc_start
    loc1 = jnp.sum(jnp.where(hit1, pos, 0.0), axis=0, keepdims=True)
    loc2 = jnp.sum(jnp.where(hit2, pos, 0.0), axis=0, keepdims=True)
    base = base_ref[...]
    base_ref[...] = base + tile_cnt
    cnt_ref[...] = jnp.broadcast_to(base + tile_cnt, cnt_ref.shape).astype(I32)

    diag = (lax.broadcasted_iota(I32, (N_EXPERTS, LANES), 0)
            == lax.broadcasted_iota(I32, (N_EXPERTS, LANES), 1))

    def to_lanes(col):
        return jnp.sum(jnp.where(diag, col, 0.0), axis=0, keepdims=True)

    iota8l = lax.broadcasted_iota(I32, (SUBLANES, LANES), 0)
    min_len = jnp.min(jnp.broadcast_to(tile_cnt, (N_EXPERTS, LANES)), axis=0, keepdims=True)
    tab_ref[...] = jnp.where(
        iota8l == TAB_LEN, to_lanes(tile_cnt),
        jnp.where(iota8l == TAB_RANK, to_lanes(base),
                  jnp.where(iota8l == TAB_LOC, to_lanes(loc_start),
                            jnp.where(iota8l == TAB_MIN_LEN, min_len, 0.0)))).astype(I32)

    loc_ref[...] = jnp.where(iota8 == 0, loc1, jnp.where(iota8 == 1, loc2, 0.0)).astype(I32)
    iota_l = lax.broadcasted_iota(I32, (LANES, t_rows), 0)
    ewcol_ref[...] = jnp.where(
        iota_l == 0, w1, jnp.where(iota_l == 1, w2,
                                   jnp.where(iota_l == 2, loc1, jnp.where(iota_l == 3, loc2, 0.0)))).T


def _mix(x2d, mod, w_in, conv_w, w_oc, w_pool, pool_scale, w_o, ln1_g, ln1_b, wr, rb,
         *, batch, seq):
    n, d = x2d.shape
    t = MIX_ROWS
    n_s = seq // t
    n_tiles = batch * n_s
    const = lambda shape: pl.BlockSpec(shape, lambda i: (0,) * len(shape),
                                       pipeline_mode=pl.Buffered(1))
    cur = lambda i: jnp.minimum(i, n_tiles - 1)
    prev = lambda i: jnp.maximum(i - 1, 0)
    pg = d // N_POOL_GROUPS
    return pl.pallas_call(
        functools.partial(_mix_kernel, tiles_per_seq=n_s),
        grid=(n_tiles + 1,),
        in_specs=[
            pl.BlockSpec((t, d), lambda i: (cur(i), 0)),
            pl.BlockSpec((1, N_MOD, d), lambda i: (cur(i) // n_s, 0, 0)),
            const(w_in.shape), const(conv_w.shape), const(w_oc.shape), const(w_pool.shape),
            const(pool_scale.shape), const(w_o.shape), const(ln1_g.shape), const(ln1_b.shape),
            const(wr.shape), const(rb.shape),
            pl.BlockSpec((1, N_MOD, d), lambda i: (prev(i) // n_s, 0, 0)),
        ],
        out_specs=[
            pl.BlockSpec((t, d), lambda i: (prev(i), 0)),
            pl.BlockSpec((t, d), lambda i: (prev(i), 0)),
            pl.BlockSpec((SUBLANES, t), lambda i: (0, prev(i))),
            pl.BlockSpec((t, LANES), lambda i: (prev(i), 0)),
            pl.BlockSpec((SUBLANES, LANES), lambda i: (prev(i), 0)),
            pl.BlockSpec((N_EXPERTS, LANES), lambda i: (0, 0)),
        ],
        out_shape=[
            jax.ShapeDtypeStruct((n, d), F32),
            jax.ShapeDtypeStruct((n, d), BF16),
            jax.ShapeDtypeStruct((SUBLANES, n), I32),
            jax.ShapeDtypeStruct((n, LANES), F32),
            jax.ShapeDtypeStruct((n // t * SUBLANES, LANES), I32),
            jax.ShapeDtypeStruct((N_EXPERTS, LANES), I32),
        ],
        scratch_shapes=[
            pltpu.VMEM((t + HALO, d), F32),
            pltpu.VMEM((t + HALO, d), F32),
            pltpu.VMEM((t + HALO, 3 * pg), F32),
            pltpu.VMEM((t + HALO, 2 * pg), F32),
            pltpu.VMEM((t + HALO, pg), F32),
            pltpu.VMEM((N_EXPERTS, 1), F32),
            pltpu.VMEM((t, d), F32),
        ],
        compiler_params=pltpu.CompilerParams(
            dimension_semantics=("arbitrary",), vmem_limit_bytes=VMEM_LIMIT_BYTES),
        name="mix",
    )(x2d, mod, w_in, conv_w, w_oc, w_pool, pool_scale, w_o, ln1_g, ln1_b, wr, rb, mod)


TAB_LEN, TAB_RANK, TAB_LOC, TAB_MIN_LEN = 0, 1, 2, 3
DISPATCH_SLOTS = 3


def _strip_copies(tab_ref, pstart_ref, local_ref, sorted_hbm, sem, *, to_hbm, local_row0=0):
    def start(e):
        n_rows = tab_ref[TAB_LEN, e]
        local = _rows(local_ref, local_row0 + tab_ref[TAB_LOC, e], n_rows)
        remote = _rows(sorted_hbm, pstart_ref[e] + tab_ref[TAB_RANK, e], n_rows)
        src, dst = (local, remote) if to_hbm else (remote, local)
        pltpu.make_async_copy(src, dst, sem).start()

    @pl.when(tab_ref[TAB_MIN_LEN, 0] > 0)
    def _():
        for e in range(N_EXPERTS):
            start(e)

    @pl.when(tab_ref[TAB_MIN_LEN, 0] <= 0)
    def _():
        def strip(e, carry):
            @pl.when(tab_ref[TAB_LEN, e] > 0)
            def _():
                start(e)
            return carry

        lax.fori_loop(0, N_EXPERTS, strip, 0)


def _dispatch_kernel(pstart_ref, padrow_ref, padlen_ref, nused_ref, tab_ref, loc_ref, h_ref, xs_hbm,
                     xring, zero_blk, sem):
    i = pl.program_id(0)
    n_steps = pl.num_programs(0)
    t_rows = h_ref.shape[0]
    n_loc = TOP_K * t_rows
    blk_tok = zero_blk.shape[0]
    n_blocks = xs_hbm.shape[0] // blk_tok

    def zero_fills(act):
        def pad_rows(e, carry):
            @pl.when(padlen_ref[e] > 0)
            def _():
                act(pltpu.make_async_copy(
                    _rows(zero_blk, 0, padlen_ref[e]),
                    _rows(xs_hbm, padrow_ref[e], padlen_ref[e]), sem.at[DISPATCH_SLOTS]))
            return carry

        def unused_block(b, carry):
            act(pltpu.make_async_copy(
                zero_blk, _rows(xs_hbm, b * blk_tok, blk_tok), sem.at[DISPATCH_SLOTS]))
            return carry

        lax.fori_loop(0, N_EXPERTS, pad_rows, 0)
        lax.fori_loop(nused_ref[0], n_blocks, unused_block, 0)

    @pl.when(i == 0)
    def _():
        zero_blk[...] = jnp.zeros_like(zero_blk)
        zero_fills(lambda copy: copy.start())

    def strips_done(tile):
        slot = tile % DISPATCH_SLOTS
        pltpu.make_async_copy(
            _rows(xring, slot * n_loc, n_loc), _rows(xs_hbm, 0, n_loc), sem.at[slot]).wait()

    @pl.when(i >= DISPATCH_SLOTS)
    def _():
        strips_done(i - DISPATCH_SLOTS)

    slot = i % DISPATCH_SLOTS
    iota_p = lax.broadcasted_iota(I32, (n_loc, t_rows), 0)
    pick = jnp.logical_or(iota_p == loc_ref[0:1, :], iota_p == loc_ref[1:2, :])
    xsorted = _dot(jnp.where(pick, 1.0, 0.0).astype(BF16), h_ref[...])
    _pack_rows(xring, xsorted, slot * n_loc, is_bf16_exact=True)
    _strip_copies(tab_ref, pstart_ref, xring, xs_hbm, sem.at[slot], to_hbm=True,
                  local_row0=slot * n_loc)

    @pl.when(i == n_steps - 1)
    def _():
        for k in range(DISPATCH_SLOTS):
            @pl.when(n_steps > k)
            def _():
                strips_done(n_steps - 1 - k)
        zero_fills(lambda copy: copy.wait())


def _dispatch(pad_start, pad_row, pad_len, n_used, tab, loc, h, n_blocks):
    n, d = h.shape
    t = MIX_ROWS
    grid_spec = pltpu.PrefetchScalarGridSpec(
        num_scalar_prefetch=4,
        grid=(n // t,),
        in_specs=[
            pl.BlockSpec((SUBLANES, LANES), lambda i, *_: (i, 0), memory_space=pltpu.SMEM),
            pl.BlockSpec((SUBLANES, t), lambda i, *_: (0, i)),
            pl.BlockSpec((t, d), lambda i, *_: (i, 0)),
        ],
        out_specs=pl.BlockSpec(memory_space=pl.ANY),
        scratch_shapes=[
            pltpu.VMEM(_packed_shape(DISPATCH_SLOTS * TOP_K * t), I32),
            pltpu.VMEM(_packed_shape(EXPERT_ROWS), I32),
            pltpu.SemaphoreType.DMA((DISPATCH_SLOTS + 1,)),
        ],
    )
    return pl.pallas_call(
        _dispatch_kernel,
        grid_spec=grid_spec,
        out_shape=jax.ShapeDtypeStruct(_packed_shape(n_blocks * EXPERT_ROWS), I32),
        compiler_params=pltpu.CompilerParams(
            dimension_semantics=("arbitrary",), vmem_limit_bytes=VMEM_LIMIT_BYTES,
            has_side_effects=True),
        name="dispatch",
    )(pad_start, pad_row, pad_len, n_used, tab, loc, h)


IN_SLOTS = 6
AHEAD = 4
OUT_SLOTS = 4


def _expert_kernel(blk0_ref, nblk_ref, wsel_ref, nused_ref, xs_hbm, wg_ref, wu_ref, wd_ref, ys_hbm,
                   xring, yring, sem_in, sem_out):
    del wsel_ref
    e = pl.program_id(0)
    blk_tok = xring.shape[0] // IN_SLOTS
    n_used = nused_ref[0]
    n_blk = nblk_ref[e]

    def in_copy(g):
        slot = g % IN_SLOTS
        return pltpu.make_async_copy(
            _rows(xs_hbm, g * blk_tok, blk_tok), _rows(xring, slot * blk_tok, blk_tok),
            sem_in.at[slot])

    def out_copy(g):
        slot = g % OUT_SLOTS
        return pltpu.make_async_copy(
            _rows(yring, slot * blk_tok, blk_tok), _rows(ys_hbm, g * blk_tok, blk_tok),
            sem_out.at[slot])

    def request(g):
        @pl.when(g < n_used)
        def _():
            in_copy(g).start()

    def retire(g):
        @pl.when(g >= OUT_SLOTS)
        def _():
            out_copy(g - OUT_SLOTS).wait()

    def mlp(*blocks):
        xbs = [_unpack_rows(xring, (g % IN_SLOTS) * blk_tok, blk_tok) for g in blocks]
        w_gate = wg_ref[0].astype(BF16)
        gates = [_dot(xb, w_gate) for xb in xbs]
        w_up = wu_ref[0].astype(BF16)
        ups = [_dot(xb, w_up) for xb in xbs]
        acts = [(gate * jax.nn.sigmoid(gate) * up).astype(BF16) for gate, up in zip(gates, ups)]
        w_down = wd_ref[0].astype(BF16)
        for g, act in zip(blocks, acts):
            _pack_rows(yring, _dot(act, w_down), (g % OUT_SLOTS) * blk_tok)

    @pl.when(e == 0)
    def _():
        for g in range(AHEAD):
            request(g)

    def pair(p, carry):
        g = blk0_ref[e] + 2 * p
        in_copy(g).wait()
        in_copy(g + 1).wait()
        request(g + AHEAD)
        request(g + AHEAD + 1)
        retire(g)
        retire(g + 1)
        mlp(g, g + 1)
        out_copy(g).start()
        out_copy(g + 1).start()
        return carry

    lax.fori_loop(0, n_blk // 2, pair, 0)

    @pl.when(n_blk % 2 == 1)
    def _():
        g = blk0_ref[e] + n_blk - 1
        in_copy(g).wait()
        request(g + AHEAD)
        retire(g)
        mlp(g)
        out_copy(g).start()

    @pl.when(e == pl.num_programs(0) - 1)
    def _():
        for k in range(1, OUT_SLOTS + 1):
            @pl.when(n_used >= k)
            def _():
                out_copy(n_used - k).wait()


def _experts(blk0, nblk, wsel, n_used, xs, w_gate, w_up, w_down):
    bm = EXPERT_ROWS
    n_exp, d, de = w_gate.shape
    wblk = lambda e, b0, nb, ws, nu: (ws[e], 0, 0)
    grid_spec = pltpu.PrefetchScalarGridSpec(
        num_scalar_prefetch=4,
        grid=(n_exp,),
        in_specs=[
            pl.BlockSpec(memory_space=pl.ANY),
            pl.BlockSpec((1, d, de), wblk),
            pl.BlockSpec((1, d, de), wblk),
            pl.BlockSpec((1, de, d), wblk),
        ],
        out_specs=pl.BlockSpec(memory_space=pl.ANY),
        scratch_shapes=[
            pltpu.VMEM(_packed_shape(IN_SLOTS * bm), I32),
            pltpu.VMEM(_packed_shape(OUT_SLOTS * bm), I32),
            pltpu.SemaphoreType.DMA((IN_SLOTS,)),
            pltpu.SemaphoreType.DMA((OUT_SLOTS,)),
        ],
    )
    n_prefetch = 4
    return pl.pallas_call(
        _expert_kernel,
        grid_spec=grid_spec,
        out_shape=jax.ShapeDtypeStruct(xs.shape, xs.dtype),
        input_output_aliases={n_prefetch: 0},
        compiler_params=pltpu.CompilerParams(
            dimension_semantics=("arbitrary",), vmem_limit_bytes=VMEM_LIMIT_BYTES,
            has_side_effects=True),
        name="experts",
    )(blk0, nblk, wsel, n_used, xs, w_gate, w_up, w_down)


COMBINE_SLOTS = 3


def _combine_kernel(pstart_ref, tab_ref, tab1_ref, tab2_ref, x1_ref, mod_ref, ln2g_ref, ln2b_ref,
                    ewcol_ref, ys_hbm, out_ref, yring, sem):
    i = pl.program_id(0)
    n_tiles = pl.num_programs(0)
    t_rows = x1_ref.shape[0]
    n_loc = TOP_K * t_rows
    ahead = COMBINE_SLOTS - 1

    def fetch(table_ref, tile):
        slot = tile % COMBINE_SLOTS
        _strip_copies(table_ref, pstart_ref, yring, ys_hbm, sem.at[slot], to_hbm=False,
                      local_row0=slot * n_loc)

    @pl.when(i == 0)
    def _():
        fetch(tab_ref, 0)

        @pl.when(n_tiles > 1)
        def _():
            fetch(tab1_ref, 1)

    @pl.when(i + ahead < n_tiles)
    def _():
        fetch(tab2_ref, i + ahead)

    slot = i % COMBINE_SLOTS
    pltpu.make_async_copy(
        _rows(ys_hbm, 0, n_loc), _rows(yring, slot * n_loc, n_loc), sem.at[slot]).wait()
    ew = ewcol_ref[...]
    lane = lax.broadcasted_iota(I32, (t_rows, n_loc), 1)
    weights = jnp.where(lane == ew[:, 2:3].astype(I32), ew[:, 0:1],
                        jnp.where(lane == ew[:, 3:4].astype(I32), ew[:, 1:2], 0.0))
    y = _dot(weights.astype(BF16), _unpack_rows(yring, slot * n_loc, n_loc))
    gate2 = mod_ref[0, 5:6, :]
    out_ref[...] = _layer_norm(DEEPNORM_ALPHA * x1_ref[...] + (1.0 + gate2) * y,
                               ln2g_ref[...], ln2b_ref[...])


def _combine(pad_start, tab, x1, mod, ln2_g, ln2_b, ewcol, ys, *, seq):
    n, d = x1.shape
    t = MIX_ROWS
    per_seq = seq // t
    n_steps = n // t
    row_blk = lambda i, ps: (i, 0)
    later = lambda k: (lambda i, ps: (jnp.minimum(i + k, n_steps - 1), 0))
    grid_spec = pltpu.PrefetchScalarGridSpec(
        num_scalar_prefetch=1,
        grid=(n_steps,),
        in_specs=[
            pl.BlockSpec((SUBLANES, LANES), row_blk, memory_space=pltpu.SMEM),
            pl.BlockSpec((SUBLANES, LANES), later(1), memory_space=pltpu.SMEM),
            pl.BlockSpec((SUBLANES, LANES), later(COMBINE_SLOTS - 1), memory_space=pltpu.SMEM),
            pl.BlockSpec((t, d), row_blk),
            pl.BlockSpec((1, N_MOD, d), lambda i, ps: (i // per_seq, 0, 0)),
            pl.BlockSpec((1, d), lambda i, ps: (0, 0)),
            pl.BlockSpec((1, d), lambda i, ps: (0, 0)),
            pl.BlockSpec((t, LANES), row_blk),
            pl.BlockSpec(memory_space=pl.ANY),
        ],
        out_specs=pl.BlockSpec((t, d), row_blk),
        scratch_shapes=[
            pltpu.VMEM(_packed_shape(COMBINE_SLOTS * TOP_K * t), I32),
            pltpu.SemaphoreType.DMA((COMBINE_SLOTS,)),
        ],
    )
    return pl.pallas_call(
        _combine_kernel,
        grid_spec=grid_spec,
        out_shape=jax.ShapeDtypeStruct((n, d), F32),
        compiler_params=pltpu.CompilerParams(
            dimension_semantics=("arbitrary",), vmem_limit_bytes=VMEM_LIMIT_BYTES),
        name="combine",
    )(pad_start, tab, tab, tab, x1, mod, ln2_g, ln2_b, ewcol, ys)


def _block_layout(counts, n_assign):
    bm = EXPERT_ROWS
    n_blocks = -(-n_assign // bm) + N_EXPERTS
    padded = (counts + bm - 1) // bm * bm
    pad_end = jnp.cumsum(padded)
    pad_start = (pad_end - padded).astype(I32)
    n_used = (pad_end[-1] // bm).astype(I32)
    pad_row = (pad_start + counts).astype(I32)
    pad_len = (padded - counts).astype(I32)
    ids = jnp.arange(N_EXPERTS, dtype=I32)
    prev_used = lax.cummax(jnp.where(padded > 0, ids, -1))
    first_used = jnp.argmax(padded > 0).astype(I32)
    weight_sel = jnp.where(prev_used >= 0, prev_used, first_used).astype(I32)
    return (pad_start, pad_row, pad_len, pad_start // bm, (padded // bm).astype(I32), weight_sel,
            n_used.reshape(1), n_blocks)


def kernel(x, c, w_ada, b_ada, w_in, conv_w, w_out_conv, w_pool, pool_scale, w_o, ln1_g, ln1_b,
           w_group, b_group, w_router, b_router, w_gate, w_up, w_down, ln2_g, ln2_b):
    bsz, seq, d = x.shape
    n = bsz * seq
    assert d == 2 * PACK_SUBLANES * LANES and seq % MIX_ROWS == 0 and w_ada.shape[0] == DEPTH
    assert conv_w.shape[1] == CONV_WIDTH <= HALO and max(POOL_WINDOWS) <= 2 * HALO
    assert w_gate.shape[1] == N_EXPERTS and w_router.shape[2] == N_EXPERTS
    x2d = x.reshape(n, d)
    c_pad = jnp.pad(c, ((0, -bsz % SUBLANES), (0, 0)))
    for l in range(DEPTH):
        mod = _ada(c_pad, w_ada[l], b_ada[l][None, :])[:bsz].reshape(bsz, N_MOD, d)
        wr = jnp.concatenate(
            [w_router[l], w_group[l],
             jnp.zeros((d, LANES - N_EXPERTS - N_GROUPS), F32)], axis=1)
        rb = jnp.concatenate(
            [b_router[l].reshape(-1), b_group[l],
             jnp.zeros((LANES - N_EXPERTS - N_GROUPS,), F32)])[:, None]
        x1, h, loc, ewcol, tab, cnt = _mix(
            x2d, mod, w_in[l].astype(BF16), conv_w[l], w_out_conv[l].astype(BF16),
            w_pool[l].astype(BF16), pool_scale[l][None, :], w_o[l].astype(BF16),
            ln1_g[l][None, :], ln1_b[l][None, :], wr, rb, batch=bsz, seq=seq)
        pad_start, pad_row, pad_len, blk0, nblk, weight_sel, n_used, n_blocks = _block_layout(
            cnt[:, 0], n * TOP_K)
        xs = _dispatch(pad_start, pad_row, pad_len, n_used, tab, loc, h, n_blocks)
        ys = _experts(blk0, nblk, weight_sel, n_used, xs, w_gate[l], w_up[l], w_down[l])
        x2d = _combine(pad_start, tab, x1, mod, ln2_g[l][None, :], ln2_b[l][None, :],
                       ewcol, ys, seq=seq)
    return x2d.reshape(bsz, seq, d)
```

```python
import functools

import jax
import jax.numpy as jnp
from jax import lax
from jax.experimental import pallas as pl
from jax.experimental.pallas import tpu as pltpu

F32 = jnp.float32
BF16 = jnp.bfloat16
I32 = jnp.int32

LANES = 128
SUBLANES = 8
VMEM_LIMIT_BYTES = 56 * 1024 * 1024

N_POOL_GROUPS = 4
POOL_WINDOWS = (2, 4, 8, 16)
CONV_WIDTH = 3
N_GROUPS = 8
EXPERTS_PER_GROUP = 8
N_EXPERTS = N_GROUPS * EXPERTS_PER_GROUP
TOP_K = 2
N_MOD = 6
LN_EPS = 1e-5
DEPTH = 1
DEEPNORM_ALPHA = (2.0 * DEPTH) ** 0.25

MIX_ROWS = 512
EXPERT_ROWS = 256
HALO = SUBLANES
PACK_SUBLANES = 4


def _split_bf16(a):
    hi = a.astype(BF16)
    lo = (a - hi.astype(F32)).astype(BF16)
    return hi, lo


def _dot(a, b):
    return jnp.dot(a, b, preferred_element_type=F32)


def _dot3(a, b):
    a_hi, a_lo = _split_bf16(a)
    b_hi, b_lo = _split_bf16(b)
    return _dot(a_hi, b_hi) + _dot(a_lo, b_hi) + _dot(a_hi, b_lo)


def _dot3_wide(a, b):
    a_hi, a_lo = _split_bf16(a)
    n = b.shape[1]
    b_both = jnp.concatenate(_split_bf16(b), axis=1)
    hi = _dot(a_hi, b_both)
    lo = _dot(a_lo, b_both)
    return hi[:, :n] + hi[:, n:] + lo[:, :n] + lo[:, n:]


def _layer_norm(r, g, b):
    mu = jnp.mean(r, axis=-1, keepdims=True)
    d = r - mu
    var = jnp.mean(d * d, axis=-1, keepdims=True)
    return d * lax.rsqrt(var + LN_EPS) * g + b


def _packed_shape(rows):
    return (rows, PACK_SUBLANES, LANES)


def _plane(ref, s, row0, rows):
    flat = ref.reshape(ref.shape[0] * PACK_SUBLANES, LANES)
    return flat.at[pl.ds(row0 * PACK_SUBLANES + s, rows, stride=PACK_SUBLANES), :]


def _pack_rows(ref, val, row0=0, *, is_bf16_exact=False):
    rows, d = val.shape
    if not is_bf16_exact:
        val = val.astype(BF16).astype(F32)
    bits = lax.bitcast_convert_type(val, I32)
    for s in range(PACK_SUBLANES):
        hi = bits[:, s * LANES:(s + 1) * LANES]
        lo = bits[:, d // 2 + s * LANES:d // 2 + (s + 1) * LANES]
        _plane(ref, s, row0, rows)[...] = jnp.bitwise_or(hi, lax.shift_right_logical(lo, 16))


def _unpack_rows(ref, row0=0, rows=None):
    rows = ref.shape[0] - row0 if rows is None else rows
    his, los = [], []
    for s in range(PACK_SUBLANES):
        word = _plane(ref, s, row0, rows)[...]
        his.append(lax.bitcast_convert_type(jnp.bitwise_and(word, jnp.int32(-65536)), F32))
        los.append(lax.bitcast_convert_type(lax.shift_left(word, 16), F32))
    return jnp.concatenate(his + los, axis=1).astype(BF16)


def _rows(ref, row, n_rows):
    return ref.at[pl.ds(row, n_rows)]


def _ada_kernel(c_ref, w_ref, b_ref, o_ref):
    cc = c_ref[...]
    act = cc * jax.nn.sigmoid(cc)
    o_ref[...] = _dot3(act, w_ref[...]) + b_ref[...]


def _ada(c_pad, w_ada, b_ada):
    rows, d = c_pad.shape
    n_out = w_ada.shape[1]
    return pl.pallas_call(
        _ada_kernel,
        grid=(n_out // d,),
        in_specs=[
            pl.BlockSpec((rows, d), lambda j: (0, 0)),
            pl.BlockSpec((d, d), lambda j: (0, j)),
            pl.BlockSpec((1, d), lambda j: (0, j)),
        ],
        out_specs=pl.BlockSpec((rows, d), lambda j: (0, j)),
        out_shape=jax.ShapeDtypeStruct((rows, n_out), F32),
        compiler_params=pltpu.CompilerParams(
            dimension_semantics=("arbitrary",), vmem_limit_bytes=VMEM_LIMIT_BYTES),
        name="ada",
    )(c_pad, w_ada, b_ada)


def _mix_kernel(x_ref, mod_ref, win_ref, convw_ref, woc_ref, wpool_ref, pscale_ref, wo_ref,
                ln1g_ref, ln1b_ref, wr_ref, rb_ref, mod_prev_ref,
                x1_ref, h_ref, loc_ref, ewcol_ref, tab_ref, cnt_ref,
                cvbuf, zbuf, s2buf, s4buf, s8buf, base_ref, res_buf, *, tiles_per_seq):
    t_rows, d = x_ref.shape
    pg = d // N_POOL_GROUPS
    step = pl.program_id(0)
    n_tiles = pl.num_programs(0) - 1
    s = jnp.minimum(step, n_tiles - 1) % tiles_per_seq
    shift_bufs = (cvbuf, zbuf, s2buf, s4buf, s8buf)

    @pl.when(step == 0)
    def _():
        base_ref[...] = jnp.zeros_like(base_ref)
        res_buf[...] = jnp.zeros_like(res_buf)

    @pl.when(s == 0)
    def _():
        for buf in shift_bufs:
            buf[0:HALO, :] = jnp.zeros((HALO, buf.shape[1]), F32)

    @pl.when(s > 0)
    def _():
        for buf in shift_bufs:
            buf[0:HALO, :] = buf[t_rows:t_rows + HALO, :]

    def lagged_tile():
        x1 = _layer_norm(res_buf[...], ln1g_ref[...], ln1b_ref[...])
        x1_ref[...] = x1
        h = x1 * (1.0 + mod_prev_ref[0, 4:5, :]) + mod_prev_ref[0, 3:4, :]
        h_ref[...] = h.astype(BF16)
        lt = _dot3_wide(h, wr_ref[...]).T + rb_ref[...]
        _route_tile(lt, step > 0, base_ref, loc_ref, ewcol_ref, tab_ref, cnt_ref)

    @pl.when(step == n_tiles)
    def _():
        lagged_tile()

    @pl.when(step < n_tiles)
    def _():
        def shifted(buf, k):
            return buf[HALO - k:HALO - k + t_rows, :]

        xt = x_ref[...]
        shift1 = mod_ref[0, 0:1, :]
        scale1 = mod_ref[0, 1:2, :]
        gate1 = mod_ref[0, 2:3, :]

        u = (xt * (1.0 + scale1) + shift1).astype(BF16)

        def proj(i):
            return _dot(u, win_ref[:, i * d:(i + 1) * d])


        a_c = proj(2)
        a_val = proj(0)
        a_b = proj(1)
        cv = a_c * a_val

        lagged_tile()

        z = proj(3)
        cvbuf[HALO:HALO + t_rows, :] = cv
        conv = convw_ref[CONV_WIDTH - 1:CONV_WIDTH, :] * cv
        for k in range(1, CONV_WIDTH):
            conv = conv + convw_ref[CONV_WIDTH - 1 - k:CONV_WIDTH - k, :] * shifted(cvbuf, k)
        y_a = _dot((conv * a_b).astype(BF16), woc_ref[...])

        zbuf[HALO:HALO + t_rows, :] = z
        s2 = z + shifted(zbuf, 1)
        s2buf[HALO:HALO + t_rows, :] = s2[:, pg:]
        s4 = s2[:, pg:] + shifted(s2buf, 2)
        s4buf[HALO:HALO + t_rows, :] = s4[:, pg:]
        s8 = s4[:, pg:] + shifted(s4buf, 4)
        s8buf[HALO:HALO + t_rows, :] = s8[:, pg:]
        s16 = s8[:, pg:] + shifted(s8buf, 8)
        win_sums = (s2[:, :pg], s4[:, :pg], s8[:, :pg], s16)
        t_pos = s * t_rows + lax.broadcasted_iota(I32, (t_rows, 1), 0) + 1
        g_a = proj(4)
        y_b_parts = []
        for gi, w in enumerate(POOL_WINDOWS):
            count = jnp.minimum(t_pos, w).astype(F32)
            pooled = win_sums[gi] / count - z[:, gi * pg:(gi + 1) * pg]
            y_b_parts.append(_dot(pooled.astype(BF16), wpool_ref[gi]))
        g_b = proj(5)
        y_b = jnp.concatenate(y_b_parts, axis=1) * pscale_ref[...]

        merged = jax.nn.sigmoid(g_a) * y_a + jax.nn.sigmoid(g_b) * y_b
        mix = _dot(merged.astype(BF16), wo_ref[...])

        res_buf[...] = DEEPNORM_ALPHA * xt + (1.0 + gate1) * mix


def _route_tile(lt, valid, base_ref, loc_ref, ewcol_ref, tab_ref, cnt_ref):
    t_rows = lt.shape[1]
    iota8 = lax.broadcasted_iota(I32, (SUBLANES, t_rows), 0)
    lg = lt[N_EXPERTS:N_EXPERTS + N_GROUPS, :]
    g_max = jnp.max(lg, axis=0, keepdims=True)
    g_top = jnp.min(jnp.where(lg == g_max, iota8, N_GROUPS), axis=0, keepdims=True)
    g_top_p = 1.0 / jnp.sum(jnp.exp(lg - g_max), axis=0, keepdims=True)
    sel = jnp.zeros((EXPERTS_PER_GROUP, t_rows), F32)
    for g in range(N_GROUPS):
        sel = jnp.where(g_top == g, lt[g * EXPERTS_PER_GROUP:(g + 1) * EXPERTS_PER_GROUP, :], sel)
    m1 = jnp.max(sel, axis=0, keepdims=True)
    i1 = jnp.min(jnp.where(sel == m1, iota8, EXPERTS_PER_GROUP), axis=0, keepdims=True)
    sel2 = jnp.where(iota8 == i1, -jnp.inf, sel)
    m2 = jnp.max(sel2, axis=0, keepdims=True)
    i2 = jnp.min(jnp.where(sel2 == m2, iota8, EXPERTS_PER_GROUP), axis=0, keepdims=True)
    ratio = jnp.exp(m2 - m1)
    p1 = 1.0 / (1.0 + ratio)
    w1 = g_top_p * p1
    w2 = g_top_p * (ratio * p1)
    e1 = g_top * EXPERTS_PER_GROUP + i1
    e2 = g_top * EXPERTS_PER_GROUP + i2

    iota_e = lax.broadcasted_iota(I32, (N_EXPERTS, t_rows), 0)
    hit1 = jnp.logical_and(iota_e == e1, valid)
    hit2 = jnp.logical_and(iota_e == e2, valid)
    onehot = jnp.where(jnp.logical_or(hit1, hit2), 1.0, 0.0)
    before = (lax.broadcasted_iota(I32, (t_rows, t_rows), 0)
              < lax.broadcasted_iota(I32, (t_rows, t_rows), 1))
    prefix = _dot(onehot.astype(BF16), jnp.where(before, 1.0, 0.0).astype(BF16))
    tile_cnt = jnp.sum(onehot, axis=1, keepdims=True)
    lower = (lax.broadcasted_iota(I32, (N_EXPERTS, N_EXPERTS), 1)
             < lax.broadcasted_iota(I32, (N_EXPERTS, N_EXPERTS), 0))
    lower = jnp.where(lower, 1.0, 0.0).astype(BF16)
    cnt_hi, cnt_lo = _split_bf16(jnp.broadcast_to(tile_cnt, (N_EXPERTS, LANES)))
    loc_start = (_dot(lower, cnt_hi) + _dot(lower, cnt_lo))[:, 0:1]
    pos = prefix + loc_start
    loc1 = jnp.sum(jnp.where(hit1, pos, 0.0), axis=0, keepdims=True)
    loc2 = jnp.sum(jnp.where(hit2, pos, 0.0), axis=0, keepdims=True)
    base = base_ref[...]
    base_ref[...] = base + tile_cnt
    cnt_ref[...] = jnp.broadcast_to(base + tile_cnt, cnt_ref.shape).astype(I32)

    diag = (lax.broadcasted_iota(I32, (N_EXPERTS, LANES), 0)
            == lax.broadcasted_iota(I32, (N_EXPERTS, LANES), 1))

    def to_lanes(col):
        return jnp.sum(jnp.where(diag, col, 0.0), axis=0, keepdims=True)

    iota8l = lax.broadcasted_iota(I32, (SUBLANES, LANES), 0)
    min_len = jnp.min(jnp.broadcast_to(tile_cnt, (N_EXPERTS, LANES)), axis=0, keepdims=True)
    tab_ref[...] = jnp.where(
        iota8l == TAB_LEN, to_lanes(tile_cnt),
        jnp.where(iota8l == TAB_RANK, to_lanes(base),
                  jnp.where(iota8l == TAB_LOC, to_lanes(loc_start),
                            jnp.where(iota8l == TAB_MIN_LEN, min_len, 0.0)))).astype(I32)

    loc_ref[...] = jnp.where(iota8 == 0, loc1, jnp.where(iota8 == 1, loc2, 0.0)).astype(I32)
    iota_l = lax.broadcasted_iota(I32, (LANES, t_rows), 0)
    ewcol_ref[...] = jnp.where(
        iota_l == 0, w1, jnp.where(iota_l == 1, w2,
                                   jnp.where(iota_l == 2, loc1, jnp.where(iota_l == 3, loc2, 0.0)))).T


def _mix(x2d, mod, w_in, conv_w, w_oc, w_pool, pool_scale, w_o, ln1_g, ln1_b, wr, rb,
         *, batch, seq):
    n, d = x2d.shape
    t = MIX_ROWS
    n_s = seq // t
    n_tiles = batch * n_s
    const = lambda shape: pl.BlockSpec(shape, lambda i: (0,) * len(shape),
                                       pipeline_mode=pl.Buffered(1))
    cur = lambda i: jnp.minimum(i, n_tiles - 1)
    prev = lambda i: jnp.maximum(i - 1, 0)
    pg = d // N_POOL_GROUPS
    return pl.pallas_call(
        functools.partial(_mix_kernel, tiles_per_seq=n_s),
        grid=(n_tiles + 1,),
        in_specs=[
            pl.BlockSpec((t, d), lambda i: (cur(i), 0)),
            pl.BlockSpec((1, N_MOD, d), lambda i: (cur(i) // n_s, 0, 0)),
            const(w_in.shape), const(conv_w.shape), const(w_oc.shape), const(w_pool.shape),
            const(pool_scale.shape), const(w_o.shape), const(ln1_g.shape), const(ln1_b.shape),
            const(wr.shape), const(rb.shape),
            pl.BlockSpec((1, N_MOD, d), lambda i: (prev(i) // n_s, 0, 0)),
        ],
        out_specs=[
            pl.BlockSpec((t, d), lambda i: (prev(i), 0)),
            pl.BlockSpec((t, d), lambda i: (prev(i), 0)),
            pl.BlockSpec((SUBLANES, t), lambda i: (0, prev(i))),
            pl.BlockSpec((t, LANES), lambda i: (prev(i), 0)),
            pl.BlockSpec((SUBLANES, LANES), lambda i: (prev(i), 0)),
            pl.BlockSpec((N_EXPERTS, LANES), lambda i: (0, 0)),
        ],
        out_shape=[
            jax.ShapeDtypeStruct((n, d), F32),
            jax.ShapeDtypeStruct((n, d), BF16),
            jax.ShapeDtypeStruct((SUBLANES, n), I32),
            jax.ShapeDtypeStruct((n, LANES), F32),
            jax.ShapeDtypeStruct((n // t * SUBLANES, LANES), I32),
            jax.ShapeDtypeStruct((N_EXPERTS, LANES), I32),
        ],
        scratch_shapes=[
            pltpu.VMEM((t + HALO, d), F32),
            pltpu.VMEM((t + HALO, d), F32),
            pltpu.VMEM((t + HALO, 3 * pg), F32),
            pltpu.VMEM((t + HALO, 2 * pg), F32),
            pltpu.VMEM((t + HALO, pg), F32),
            pltpu.VMEM((N_EXPERTS, 1), F32),
            pltpu.VMEM((t, d), F32),
        ],
        compiler_params=pltpu.CompilerParams(
            dimension_semantics=("arbitrary",), vmem_limit_bytes=VMEM_LIMIT_BYTES),
        name="mix",
    )(x2d, mod, w_in, conv_w, w_oc, w_pool, pool_scale, w_o, ln1_g, ln1_b, wr, rb, mod)


TAB_LEN, TAB_RANK, TAB_LOC, TAB_MIN_LEN = 0, 1, 2, 3
DISPATCH_SLOTS = 3


def _strip_copies(tab_ref, pstart_ref, local_ref, sorted_hbm, sem, *, to_hbm, local_row0=0):
    def start(e):
        n_rows = tab_ref[TAB_LEN, e]
        local = _rows(local_ref, local_row0 + tab_ref[TAB_LOC, e], n_rows)
        remote = _rows(sorted_hbm, pstart_ref[e] + tab_ref[TAB_RANK, e], n_rows)
        src, dst = (local, remote) if to_hbm else (remote, local)
        pltpu.make_async_copy(src, dst, sem).start()

    @pl.when(tab_ref[TAB_MIN_LEN, 0] > 0)
    def _():
        for e in range(N_EXPERTS):
            start(e)

    @pl.when(tab_ref[TAB_MIN_LEN, 0] <= 0)
    def _():
        def strip(e, carry):
            @pl.when(tab_ref[TAB_LEN, e] > 0)
            def _():
                start(e)
            return carry

        lax.fori_loop(0, N_EXPERTS, strip, 0)


def _dispatch_kernel(pstart_ref, padrow_ref, padlen_ref, nused_ref, tab_ref, loc_ref, h_ref, xs_hbm,
                     xring, zero_blk, sem):
    i = pl.program_id(0)
    n_steps = pl.num_programs(0)
    t_rows = h_ref.shape[0]
    n_loc = TOP_K * t_rows
    blk_tok = zero_blk.shape[0]
    n_blocks = xs_hbm.shape[0] // blk_tok

    def zero_fills(act):
        def pad_rows(e, carry):
            @pl.when(padlen_ref[e] > 0)
            def _():
                act(pltpu.make_async_copy(
                    _rows(zero_blk, 0, padlen_ref[e]),
                    _rows(xs_hbm, padrow_ref[e], padlen_ref[e]), sem.at[DISPATCH_SLOTS]))
            return carry

        def unused_block(b, carry):
            act(pltpu.make_async_copy(
                zero_blk, _rows(xs_hbm, b * blk_tok, blk_tok), sem.at[DISPATCH_SLOTS]))
            return carry

        lax.fori_loop(0, N_EXPERTS, pad_rows, 0)
        lax.fori_loop(nused_ref[0], n_blocks, unused_block, 0)

    @pl.when(i == 0)
    def _():
        zero_blk[...] = jnp.zeros_like(zero_blk)
        zero_fills(lambda copy: copy.start())

    def strips_done(tile):
        slot = tile % DISPATCH_SLOTS
        pltpu.make_async_copy(
            _rows(xring, slot * n_loc, n_loc), _rows(xs_hbm, 0, n_loc), sem.at[slot]).wait()

    @pl.when(i >= DISPATCH_SLOTS)
    def _():
        strips_done(i - DISPATCH_SLOTS)

    slot = i % DISPATCH_SLOTS
    iota_p = lax.broadcasted_iota(I32, (n_loc, t_rows), 0)
    pick = jnp.logical_or(iota_p == loc_ref[0:1, :], iota_p == loc_ref[1:2, :])
    xsorted = _dot(jnp.where(pick, 1.0, 0.0).astype(BF16), h_ref[...])
    _pack_rows(xring, xsorted, slot * n_loc, is_bf16_exact=True)
    _strip_copies(tab_ref, pstart_ref, xring, xs_hbm, sem.at[slot], to_hbm=True,
                  local_row0=slot * n_loc)

    @pl.when(i == n_steps - 1)
    def _():
        for k in range(DISPATCH_SLOTS):
            @pl.when(n_steps > k)
            def _():
                strips_done(n_steps - 1 - k)
        zero_fills(lambda copy: copy.wait())


def _dispatch(pad_start, pad_row, pad_len, n_used, tab, loc, h, n_blocks):
    n, d = h.shape
    t = MIX_ROWS
    grid_spec = pltpu.PrefetchScalarGridSpec(
        num_scalar_prefetch=4,
        grid=(n // t,),
        in_specs=[
            pl.BlockSpec((SUBLANES, LANES), lambda i, *_: (i, 0), memory_space=pltpu.SMEM),
            pl.BlockSpec((SUBLANES, t), lambda i, *_: (0, i)),
            pl.BlockSpec((t, d), lambda i, *_: (i, 0)),
        ],
        out_specs=pl.BlockSpec(memory_space=pl.ANY),
        scratch_shapes=[
            pltpu.VMEM(_packed_shape(DISPATCH_SLOTS * TOP_K * t), I32),
            pltpu.VMEM(_packed_shape(EXPERT_ROWS), I32),
            pltpu.SemaphoreType.DMA((DISPATCH_SLOTS + 1,)),
        ],
    )
    return pl.pallas_call(
        _dispatch_kernel,
        grid_spec=grid_spec,
        out_shape=jax.ShapeDtypeStruct(_packed_shape(n_blocks * EXPERT_ROWS), I32),
        compiler_params=pltpu.CompilerParams(
            dimension_semantics=("arbitrary",), vmem_limit_bytes=VMEM_LIMIT_BYTES,
            has_side_effects=True),
        name="dispatch",
    )(pad_start, pad_row, pad_len, n_used, tab, loc, h)


TRIP_BLOCKS = 4
AHEAD = 4
IN_SLOTS = AHEAD + TRIP_BLOCKS
OUT_SLOTS = TRIP_BLOCKS


def _expert_kernel(blk0_ref, nblk_ref, wsel_ref, nused_ref, xs_hbm, wg_ref, wu_ref, wd_ref, ys_hbm,
                   xring, yring, sem_in, sem_out):
    del wsel_ref
    e = pl.program_id(0)
    blk_tok = xring.shape[0] // IN_SLOTS
    n_used = nused_ref[0]
    n_blk = nblk_ref[e]

    def in_copy(g):
        slot = g % IN_SLOTS
        return pltpu.make_async_copy(
            _rows(xs_hbm, g * blk_tok, blk_tok), _rows(xring, slot * blk_tok, blk_tok),
            sem_in.at[slot])

    def out_copy(g):
        slot = g % OUT_SLOTS
        return pltpu.make_async_copy(
            _rows(yring, slot * blk_tok, blk_tok), _rows(ys_hbm, g * blk_tok, blk_tok),
            sem_out.at[slot])

    def request(g):
        @pl.when(g < n_used)
        def _():
            in_copy(g).start()

    def retire(g):
        @pl.when(g >= OUT_SLOTS)
        def _():
            out_copy(g - OUT_SLOTS).wait()

    def mlp(*blocks):
        xbs = [_unpack_rows(xring, (g % IN_SLOTS) * blk_tok, blk_tok) for g in blocks]
        w_gate = wg_ref[0].astype(BF16)
        gates = [_dot(xb, w_gate) for xb in xbs]
        w_up = wu_ref[0].astype(BF16)
        ups = [_dot(xb, w_up) for xb in xbs]
        acts = [(gate * jax.nn.sigmoid(gate) * up).astype(BF16) for gate, up in zip(gates, ups)]
        w_down = wd_ref[0].astype(BF16)
        for g, act in zip(blocks, acts):
            _pack_rows(yring, _dot(act, w_down), (g % OUT_SLOTS) * blk_tok)

    @pl.when(e == 0)
    def _():
        for g in range(AHEAD):
            request(g)

    def trip(g0, k):
        blocks = [g0 + j for j in range(k)]
        for g in blocks:
            in_copy(g).wait()
        for g in blocks:
            request(g + AHEAD)
        for g in blocks:
            retire(g)
        mlp(*blocks)
        for g in blocks:
            out_copy(g).start()

    def full_trip(p, carry):
        trip(blk0_ref[e] + TRIP_BLOCKS * p, TRIP_BLOCKS)
        return carry

    lax.fori_loop(0, n_blk // TRIP_BLOCKS, full_trip, 0)

    done = n_blk - n_blk % TRIP_BLOCKS
    k = TRIP_BLOCKS // 2
    while k >= 1:
        has_k = (n_blk % (2 * k)) >= k

        @pl.when(has_k)
        def _(done=done, k=k):
            trip(blk0_ref[e] + done, k)

        done = done + jnp.where(has_k, k, 0)
        k //= 2

    @pl.when(e == pl.num_programs(0) - 1)
    def _():
        for k in range(1, OUT_SLOTS + 1):
            @pl.when(n_used >= k)
            def _():
                out_copy(n_used - k).wait()


def _experts(blk0, nblk, wsel, n_used, xs, w_gate, w_up, w_down):
    bm = EXPERT_ROWS
    n_exp, d, de = w_gate.shape
    wblk = lambda e, b0, nb, ws, nu: (ws[e], 0, 0)
    grid_spec = pltpu.PrefetchScalarGridSpec(
        num_scalar_prefetch=4,
        grid=(n_exp,),
        in_specs=[
            pl.BlockSpec(memory_space=pl.ANY),
            pl.BlockSpec((1, d, de), wblk),
            pl.BlockSpec((1, d, de), wblk),
            pl.BlockSpec((1, de, d), wblk),
        ],
        out_specs=pl.BlockSpec(memory_space=pl.ANY),
        scratch_shapes=[
            pltpu.VMEM(_packed_shape(IN_SLOTS * bm), I32),
            pltpu.VMEM(_packed_shape(OUT_SLOTS * bm), I32),
            pltpu.SemaphoreType.DMA((IN_SLOTS,)),
            pltpu.SemaphoreType.DMA((OUT_SLOTS,)),
        ],
    )
    n_prefetch = 4
    return pl.pallas_call(
        _expert_kernel,
        grid_spec=grid_spec,
        out_shape=jax.ShapeDtypeStruct(xs.shape, xs.dtype),
        input_output_aliases={n_prefetch: 0},
        compiler_params=pltpu.CompilerParams(
            dimension_semantics=("arbitrary",), vmem_limit_bytes=VMEM_LIMIT_BYTES,
            has_side_effects=True),
        name="experts",
    )(blk0, nblk, wsel, n_used, xs, w_gate, w_up, w_down)


COMBINE_SLOTS = 3


def _combine_kernel(pstart_ref, tab_ref, tab1_ref, tab2_ref, x1_ref, mod_ref, ln2g_ref, ln2b_ref,
                    ewcol_ref, ys_hbm, out_ref, yring, sem):
    i = pl.program_id(0)
    n_tiles = pl.num_programs(0)
    t_rows = x1_ref.shape[0]
    n_loc = TOP_K * t_rows
    ahead = COMBINE_SLOTS - 1

    def fetch(table_ref, tile):
        slot = tile % COMBINE_SLOTS
        _strip_copies(table_ref, pstart_ref, yring, ys_hbm, sem.at[slot], to_hbm=False,
                      local_row0=slot * n_loc)

    @pl.when(i == 0)
    def _():
        fetch(tab_ref, 0)

        @pl.when(n_tiles > 1)
        def _():
            fetch(tab1_ref, 1)

    @pl.when(i + ahead < n_tiles)
    def _():
        fetch(tab2_ref, i + ahead)

    slot = i % COMBINE_SLOTS
    pltpu.make_async_copy(
        _rows(ys_hbm, 0, n_loc), _rows(yring, slot * n_loc, n_loc), sem.at[slot]).wait()
    ew = ewcol_ref[...]
    lane = lax.broadcasted_iota(I32, (t_rows, n_loc), 1)
    weights = jnp.where(lane == ew[:, 2:3].astype(I32), ew[:, 0:1],
                        jnp.where(lane == ew[:, 3:4].astype(I32), ew[:, 1:2], 0.0))
    y = _dot(weights.astype(BF16), _unpack_rows(yring, slot * n_loc, n_loc))
    gate2 = mod_ref[0, 5:6, :]
    out_ref[...] = _layer_norm(DEEPNORM_ALPHA * x1_ref[...] + (1.0 + gate2) * y,
                               ln2g_ref[...], ln2b_ref[...])


def _combine(pad_start, tab, x1, mod, ln2_g, ln2_b, ewcol, ys, *, seq):
    n, d = x1.shape
    t = MIX_ROWS
    per_seq = seq // t
    n_steps = n // t
    row_blk = lambda i, ps: (i, 0)
    later = lambda k: (lambda i, ps: (jnp.minimum(i + k, n_steps - 1), 0))
    grid_spec = pltpu.PrefetchScalarGridSpec(
        num_scalar_prefetch=1,
        grid=(n_steps,),
        in_specs=[
            pl.BlockSpec((SUBLANES, LANES), row_blk, memory_space=pltpu.SMEM),
            pl.BlockSpec((SUBLANES, LANES), later(1), memory_space=pltpu.SMEM),
            pl.BlockSpec((SUBLANES, LANES), later(COMBINE_SLOTS - 1), memory_space=pltpu.SMEM),
            pl.BlockSpec((t, d), row_blk),
            pl.BlockSpec((1, N_MOD, d), lambda i, ps: (i // per_seq, 0, 0)),
            pl.BlockSpec((1, d), lambda i, ps: (0, 0)),
            pl.BlockSpec((1, d), lambda i, ps: (0, 0)),
            pl.BlockSpec((t, LANES), row_blk),
            pl.BlockSpec(memory_space=pl.ANY),
        ],
        out_specs=pl.BlockSpec((t, d), row_blk),
        scratch_shapes=[
            pltpu.VMEM(_packed_shape(COMBINE_SLOTS * TOP_K * t), I32),
            pltpu.SemaphoreType.DMA((COMBINE_SLOTS,)),
        ],
    )
    return pl.pallas_call(
        _combine_kernel,
        grid_spec=grid_spec,
        out_shape=jax.ShapeDtypeStruct((n, d), F32),
        compiler_params=pltpu.CompilerParams(
            dimension_semantics=("arbitrary",), vmem_limit_bytes=VMEM_LIMIT_BYTES),
        name="combine",
    )(pad_start, tab, tab, tab, x1, mod, ln2_g, ln2_b, ewcol, ys)


def _block_layout(counts, n_assign):
    bm = EXPERT_ROWS
    n_blocks = -(-n_assign // bm) + N_EXPERTS
    padded = (counts + bm - 1) // bm * bm
    pad_end = jnp.cumsum(padded)
    pad_start = (pad_end - padded).astype(I32)
    n_used = (pad_end[-1] // bm).astype(I32)
    pad_row = (pad_start + counts).astype(I32)
    pad_len = (padded - counts).astype(I32)
    ids = jnp.arange(N_EXPERTS, dtype=I32)
    prev_used = lax.cummax(jnp.where(padded > 0, ids, -1))
    first_used = jnp.argmax(padded > 0).astype(I32)
    weight_sel = jnp.where(prev_used >= 0, prev_used, first_used).astype(I32)
    return (pad_start, pad_row, pad_len, pad_start // bm, (padded // bm).astype(I32), weight_sel,
            n_used.reshape(1), n_blocks)


def kernel(x, c, w_ada, b_ada, w_in, conv_w, w_out_conv, w_pool, pool_scale, w_o, ln1_g, ln1_b,
           w_group, b_group, w_router, b_router, w_gate, w_up, w_down, ln2_g, ln2_b):
    bsz, seq, d = x.shape
    n = bsz * seq
    assert d == 2 * PACK_SUBLANES * LANES and seq % MIX_ROWS == 0 and w_ada.shape[0] == DEPTH
    assert conv_w.shape[1] == CONV_WIDTH <= HALO and max(POOL_WINDOWS) <= 2 * HALO
    assert w_gate.shape[1] == N_EXPERTS and w_router.shape[2] == N_EXPERTS
    x2d = x.reshape(n, d)
    c_pad = jnp.pad(c, ((0, -bsz % SUBLANES), (0, 0)))
    for l in range(DEPTH):
        mod = _ada(c_pad, w_ada[l], b_ada[l][None, :])[:bsz].reshape(bsz, N_MOD, d)
        wr = jnp.concatenate(
            [w_router[l], w_group[l],
             jnp.zeros((d, LANES - N_EXPERTS - N_GROUPS), F32)], axis=1)
        rb = jnp.concatenate(
            [b_router[l].reshape(-1), b_group[l],
             jnp.zeros((LANES - N_EXPERTS - N_GROUPS,), F32)])[:, None]
        x1, h, loc, ewcol, tab, cnt = _mix(
            x2d, mod, w_in[l].astype(BF16), conv_w[l], w_out_conv[l].astype(BF16),
            w_pool[l].astype(BF16), pool_scale[l][None, :], w_o[l].astype(BF16),
            ln1_g[l][None, :], ln1_b[l][None, :], wr, rb, batch=bsz, seq=seq)
        pad_start, pad_row, pad_len, blk0, nblk, weight_sel, n_used, n_blocks = _block_layout(
            cnt[:, 0], n * TOP_K)
        xs = _dispatch(pad_start, pad_row, pad_len, n_used, tab, loc, h, n_blocks)
        ys = _experts(blk0, nblk, weight_sel, n_used, xs, w_gate[l], w_up[l], w_down[l])
        x2d = _combine(pad_start, tab, x1, mod, ln2_g[l][None, :], ln2_b[l][None, :],
                       ewcol, ys, seq=seq)
    return x2d.reshape(bsz, seq, d)
```

```python
import functools

import jax
import jax.numpy as jnp
from jax import lax
from jax.experimental import pallas as pl
from jax.experimental.pallas import tpu as pltpu

F32 = jnp.float32
BF16 = jnp.bfloat16
I32 = jnp.int32

LANES = 128
SUBLANES = 8
VMEM_LIMIT_BYTES = 56 * 1024 * 1024

N_POOL_GROUPS = 4
POOL_WINDOWS = (2, 4, 8, 16)
CONV_WIDTH = 3
N_GROUPS = 8
EXPERTS_PER_GROUP = 8
N_EXPERTS = N_GROUPS * EXPERTS_PER_GROUP
TOP_K = 2
N_MOD = 6
LN_EPS = 1e-5
DEPTH = 1
DEEPNORM_ALPHA = (2.0 * DEPTH) ** 0.25

MIX_ROWS = 512
EXPERT_ROWS = 256
HALO = SUBLANES
PACK_SUBLANES = 4


def _split_bf16(a):
    hi = a.astype(BF16)
    lo = (a - hi.astype(F32)).astype(BF16)
    return hi, lo


def _dot(a, b):
    return jnp.dot(a, b, preferred_element_type=F32)


def _dot3(a, b):
    a_hi, a_lo = _split_bf16(a)
    b_hi, b_lo = _split_bf16(b)
    return _dot(a_hi, b_hi) + _dot(a_lo, b_hi) + _dot(a_hi, b_lo)


def _dot3_wide(a, b):
    a_hi, a_lo = _split_bf16(a)
    n = b.shape[1]
    b_both = jnp.concatenate(_split_bf16(b), axis=1)
    hi = _dot(a_hi, b_both)
    lo = _dot(a_lo, b_both)
    return hi[:, :n] + hi[:, n:] + lo[:, :n] + lo[:, n:]


def _layer_norm(r, g, b):
    mu = jnp.mean(r, axis=-1, keepdims=True)
    d = r - mu
    var = jnp.mean(d * d, axis=-1, keepdims=True)
    return d * lax.rsqrt(var + LN_EPS) * g + b


def _packed_shape(rows):
    return (rows, PACK_SUBLANES, LANES)


def _plane(ref, s, row0, rows):
    flat = ref.reshape(ref.shape[0] * PACK_SUBLANES, LANES)
    return flat.at[pl.ds(row0 * PACK_SUBLANES + s, rows, stride=PACK_SUBLANES), :]


def _pack_rows(ref, val, row0=0, *, is_bf16_exact=False):
    rows, d = val.shape
    if not is_bf16_exact:
        val = val.astype(BF16).astype(F32)
    bits = lax.bitcast_convert_type(val, I32)
    for s in range(PACK_SUBLANES):
        hi = bits[:, s * LANES:(s + 1) * LANES]
        lo = bits[:, d // 2 + s * LANES:d // 2 + (s + 1) * LANES]
        _plane(ref, s, row0, rows)[...] = jnp.bitwise_or(hi, lax.shift_right_logical(lo, 16))


def _unpack_rows(ref, row0=0, rows=None):
    rows = ref.shape[0] - row0 if rows is None else rows
    his, los = [], []
    for s in range(PACK_SUBLANES):
        word = _plane(ref, s, row0, rows)[...]
        his.append(lax.bitcast_convert_type(jnp.bitwise_and(word, jnp.int32(-65536)), F32))
        los.append(lax.bitcast_convert_type(lax.shift_left(word, 16), F32))
    return jnp.concatenate(his + los, axis=1).astype(BF16)


def _rows(ref, row, n_rows):
    return ref.at[pl.ds(row, n_rows)]


def _ada_kernel(c_ref, w_ref, b_ref, o_ref):
    cc = c_ref[...]
    act = cc * jax.nn.sigmoid(cc)
    o_ref[...] = _dot3(act, w_ref[...]) + b_ref[...]


def _ada(c_pad, w_ada, b_ada):
    rows, d = c_pad.shape
    n_out = w_ada.shape[1]
    return pl.pallas_call(
        _ada_kernel,
        grid=(n_out // d,),
        in_specs=[
            pl.BlockSpec((rows, d), lambda j: (0, 0)),
            pl.BlockSpec((d, d), lambda j: (0, j)),
            pl.BlockSpec((1, d), lambda j: (0, j)),
        ],
        out_specs=pl.BlockSpec((rows, d), lambda j: (0, j)),
        out_shape=jax.ShapeDtypeStruct((rows, n_out), F32),
        compiler_params=pltpu.CompilerParams(
            dimension_semantics=("arbitrary",), vmem_limit_bytes=VMEM_LIMIT_BYTES),
        name="ada",
    )(c_pad, w_ada, b_ada)


def _mix_kernel(x_ref, mod_ref, win_ref, convw_ref, woc_ref, wpool_ref, pscale_ref, wo_ref,
                ln1g_ref, ln1b_ref, wr_ref, rb_ref, mod_prev_ref,
                x1_ref, h_ref, loc_ref, ewcol_ref, tab_ref, cnt_ref,
                cvbuf, zbuf, s2buf, s4buf, s8buf, base_ref, res_buf, *, tiles_per_seq):
    t_rows, d = x_ref.shape
    pg = d // N_POOL_GROUPS
    step = pl.program_id(0)
    n_tiles = pl.num_programs(0) - 1
    s = jnp.minimum(step, n_tiles - 1) % tiles_per_seq
    shift_bufs = (cvbuf, zbuf, s2buf, s4buf, s8buf)

    @pl.when(step == 0)
    def _():
        base_ref[...] = jnp.zeros_like(base_ref)
        res_buf[...] = jnp.zeros_like(res_buf)

    @pl.when(s == 0)
    def _():
        for buf in shift_bufs:
            buf[0:HALO, :] = jnp.zeros((HALO, buf.shape[1]), F32)

    @pl.when(s > 0)
    def _():
        for buf in shift_bufs:
            buf[0:HALO, :] = buf[t_rows:t_rows + HALO, :]

    def lagged_tile():
        x1 = _layer_norm(res_buf[...], ln1g_ref[...], ln1b_ref[...])
        x1_ref[...] = x1
        h = x1 * (1.0 + mod_prev_ref[0, 4:5, :]) + mod_prev_ref[0, 3:4, :]
        h_ref[...] = h.astype(BF16)
        lt = _dot3_wide(h, wr_ref[...]).T + rb_ref[...]
        _route_tile(lt, step > 0, base_ref, loc_ref, ewcol_ref, tab_ref, cnt_ref)

    @pl.when(step == n_tiles)
    def _():
        lagged_tile()

    @pl.when(step < n_tiles)
    def _():
        def shifted(buf, k):
            return buf[HALO - k:HALO - k + t_rows, :]

        xt = x_ref[...]
        shift1 = mod_ref[0, 0:1, :]
        scale1 = mod_ref[0, 1:2, :]
        gate1 = mod_ref[0, 2:3, :]

        u = (xt * (1.0 + scale1) + shift1).astype(BF16)

        def proj(i):
            return _dot(u, win_ref[:, i * d:(i + 1) * d])


        a_c = proj(2)
        a_val = proj(0)
        a_b = proj(1)
        cv = a_c * a_val

        lagged_tile()

        z = proj(3)
        cvbuf[HALO:HALO + t_rows, :] = cv
        conv = convw_ref[CONV_WIDTH - 1:CONV_WIDTH, :] * cv
        for k in range(1, CONV_WIDTH):
            conv = conv + convw_ref[CONV_WIDTH - 1 - k:CONV_WIDTH - k, :] * shifted(cvbuf, k)
        y_a = _dot((conv * a_b).astype(BF16), woc_ref[...])

        zbuf[HALO:HALO + t_rows, :] = z
        s2 = z + shifted(zbuf, 1)
        s2buf[HALO:HALO + t_rows, :] = s2[:, pg:]
        s4 = s2[:, pg:] + shifted(s2buf, 2)
        s4buf[HALO:HALO + t_rows, :] = s4[:, pg:]
        s8 = s4[:, pg:] + shifted(s4buf, 4)
        s8buf[HALO:HALO + t_rows, :] = s8[:, pg:]
        s16 = s8[:, pg:] + shifted(s8buf, 8)
        win_sums = (s2[:, :pg], s4[:, :pg], s8[:, :pg], s16)
        t_pos = s * t_rows + lax.broadcasted_iota(I32, (t_rows, 1), 0) + 1
        g_a = proj(4)
        y_b_parts = []
        for gi, w in enumerate(POOL_WINDOWS):
            count = jnp.minimum(t_pos, w).astype(F32)
            pooled = win_sums[gi] / count - z[:, gi * pg:(gi + 1) * pg]
            y_b_parts.append(_dot(pooled.astype(BF16), wpool_ref[gi]))
        g_b = proj(5)
        y_b = jnp.concatenate(y_b_parts, axis=1) * pscale_ref[...]

        merged = jax.nn.sigmoid(g_a) * y_a + jax.nn.sigmoid(g_b) * y_b
        mix = _dot(merged.astype(BF16), wo_ref[...])

        res_buf[...] = DEEPNORM_ALPHA * xt + (1.0 + gate1) * mix


def _route_tile(lt, valid, base_ref, loc_ref, ewcol_ref, tab_ref, cnt_ref):
    t_rows = lt.shape[1]
    iota8 = lax.broadcasted_iota(I32, (SUBLANES, t_rows), 0)
    lg = lt[N_EXPERTS:N_EXPERTS + N_GROUPS, :]
    g_max = jnp.max(lg, axis=0, keepdims=True)
    g_top = jnp.min(jnp.where(lg == g_max, iota8, N_GROUPS), axis=0, keepdims=True)
    g_top_p = 1.0 / jnp.sum(jnp.exp(lg - g_max), axis=0, keepdims=True)
    sel = jnp.zeros((EXPERTS_PER_GROUP, t_rows), F32)
    for g in range(N_GROUPS):
        sel = jnp.where(g_top == g, lt[g * EXPERTS_PER_GROUP:(g + 1) * EXPERTS_PER_GROUP, :], sel)
    m1 = jnp.max(sel, axis=0, keepdims=True)
    i1 = jnp.min(jnp.where(sel == m1, iota8, EXPERTS_PER_GROUP), axis=0, keepdims=True)
    sel2 = jnp.where(iota8 == i1, -jnp.inf, sel)
    m2 = jnp.max(sel2, axis=0, keepdims=True)
    i2 = jnp.min(jnp.where(sel2 == m2, iota8, EXPERTS_PER_GROUP), axis=0, keepdims=True)
    ratio = jnp.exp(m2 - m1)
    p1 = 1.0 / (1.0 + ratio)
    w1 = g_top_p * p1
    w2 = g_top_p * (ratio * p1)
    e1 = g_top * EXPERTS_PER_GROUP + i1
    e2 = g_top * EXPERTS_PER_GROUP + i2

    iota_e = lax.broadcasted_iota(I32, (N_EXPERTS, t_rows), 0)
    hit1 = jnp.logical_and(iota_e == e1, valid)
    hit2 = jnp.logical_and(iota_e == e2, valid)
    onehot = jnp.where(jnp.logical_or(hit1, hit2), 1.0, 0.0)
    before = (lax.broadcasted_iota(I32, (t_rows, t_rows), 0)
              < lax.broadcasted_iota(I32, (t_rows, t_rows), 1))
    prefix = _dot(onehot.astype(BF16), jnp.where(before, 1.0, 0.0).astype(BF16))
    tile_cnt = jnp.sum(onehot, axis=1, keepdims=True)
    lower = (lax.broadcasted_iota(I32, (N_EXPERTS, N_EXPERTS), 1)
             < lax.broadcasted_iota(I32, (N_EXPERTS, N_EXPERTS), 0))
    lower = jnp.where(lower, 1.0, 0.0).astype(BF16)
    cnt_hi, cnt_lo = _split_bf16(jnp.broadcast_to(tile_cnt, (N_EXPERTS, LANES)))
    loc_start = (_dot(lower, cnt_hi) + _dot(lower, cnt_lo))[:, 0:1]
    pos = prefix + loc_start
    loc1 = jnp.sum(jnp.where(hit1, pos, 0.0), axis=0, keepdims=True)
    loc2 = jnp.sum(jnp.where(hit2, pos, 0.0), axis=0, keepdims=True)
    base = base_ref[...]
    base_ref[...] = base + tile_cnt
    cnt_ref[...] = jnp.broadcast_to(base + tile_cnt, cnt_ref.shape).astype(I32)

    diag = (lax.broadcasted_iota(I32, (N_EXPERTS, LANES), 0)
            == lax.broadcasted_iota(I32, (N_EXPERTS, LANES), 1))

    def to_lanes(col):
        return jnp.sum(jnp.where(diag, col, 0.0), axis=0, keepdims=True)

    iota8l = lax.broadcasted_iota(I32, (SUBLANES, LANES), 0)
    min_len = jnp.min(jnp.broadcast_to(tile_cnt, (N_EXPERTS, LANES)), axis=0, keepdims=True)
    tab_ref[...] = jnp.where(
        iota8l == TAB_LEN, to_lanes(tile_cnt),
        jnp.where(iota8l == TAB_RANK, to_lanes(base),
                  jnp.where(iota8l == TAB_LOC, to_lanes(loc_start),
                            jnp.where(iota8l == TAB_MIN_LEN, min_len, 0.0)))).astype(I32)

    loc_ref[...] = jnp.where(iota8 == 0, loc1, jnp.where(iota8 == 1, loc2, 0.0)).astype(I32)
    ewcol_ref[...] = jnp.where(
        iota8 == 0, w1, jnp.where(iota8 == 1, w2,
                                  jnp.where(iota8 == 2, loc1, jnp.where(iota8 == 3, loc2, 0.0))))


def _mix(x2d, mod, w_in, conv_w, w_oc, w_pool, pool_scale, w_o, ln1_g, ln1_b, wr, rb,
         *, batch, seq):
    n, d = x2d.shape
    t = MIX_ROWS
    n_s = seq // t
    n_tiles = batch * n_s
    const = lambda shape: pl.BlockSpec(shape, lambda i: (0,) * len(shape),
                                       pipeline_mode=pl.Buffered(1))
    cur = lambda i: jnp.minimum(i, n_tiles - 1)
    prev = lambda i: jnp.maximum(i - 1, 0)
    pg = d // N_POOL_GROUPS
    return pl.pallas_call(
        functools.partial(_mix_kernel, tiles_per_seq=n_s),
        grid=(n_tiles + 1,),
        in_specs=[
            pl.BlockSpec((t, d), lambda i: (cur(i), 0)),
            pl.BlockSpec((1, N_MOD, d), lambda i: (cur(i) // n_s, 0, 0)),
            const(w_in.shape), const(conv_w.shape), const(w_oc.shape), const(w_pool.shape),
            const(pool_scale.shape), const(w_o.shape), const(ln1_g.shape), const(ln1_b.shape),
            const(wr.shape), const(rb.shape),
            pl.BlockSpec((1, N_MOD, d), lambda i: (prev(i) // n_s, 0, 0)),
        ],
        out_specs=[
            pl.BlockSpec((t, d), lambda i: (prev(i), 0)),
            pl.BlockSpec((t, d), lambda i: (prev(i), 0)),
            pl.BlockSpec((SUBLANES, t), lambda i: (0, prev(i))),
            pl.BlockSpec((SUBLANES, t), lambda i: (0, prev(i))),
            pl.BlockSpec((SUBLANES, LANES), lambda i: (prev(i), 0)),
            pl.BlockSpec((N_EXPERTS, LANES), lambda i: (0, 0)),
        ],
        out_shape=[
            jax.ShapeDtypeStruct((n, d), F32),
            jax.ShapeDtypeStruct((n, d), BF16),
            jax.ShapeDtypeStruct((SUBLANES, n), I32),
            jax.ShapeDtypeStruct((SUBLANES, n), F32),
            jax.ShapeDtypeStruct((n // t * SUBLANES, LANES), I32),
            jax.ShapeDtypeStruct((N_EXPERTS, LANES), I32),
        ],
        scratch_shapes=[
            pltpu.VMEM((t + HALO, d), F32),
            pltpu.VMEM((t + HALO, d), F32),
            pltpu.VMEM((t + HALO, 3 * pg), F32),
            pltpu.VMEM((t + HALO, 2 * pg), F32),
            pltpu.VMEM((t + HALO, pg), F32),
            pltpu.VMEM((N_EXPERTS, 1), F32),
            pltpu.VMEM((t, d), F32),
        ],
        compiler_params=pltpu.CompilerParams(
            dimension_semantics=("arbitrary",), vmem_limit_bytes=VMEM_LIMIT_BYTES),
        name="mix",
    )(x2d, mod, w_in, conv_w, w_oc, w_pool, pool_scale, w_o, ln1_g, ln1_b, wr, rb, mod)


TAB_LEN, TAB_RANK, TAB_LOC, TAB_MIN_LEN = 0, 1, 2, 3
DISPATCH_SLOTS = 3


def _strip_copies(tab_ref, pstart_ref, local_ref, sorted_hbm, sem, *, to_hbm, local_row0=0):
    def start(e):
        n_rows = tab_ref[TAB_LEN, e]
        local = _rows(local_ref, local_row0 + tab_ref[TAB_LOC, e], n_rows)
        remote = _rows(sorted_hbm, pstart_ref[e] + tab_ref[TAB_RANK, e], n_rows)
        src, dst = (local, remote) if to_hbm else (remote, local)
        pltpu.make_async_copy(src, dst, sem).start()

    @pl.when(tab_ref[TAB_MIN_LEN, 0] > 0)
    def _():
        for e in range(N_EXPERTS):
            start(e)

    @pl.when(tab_ref[TAB_MIN_LEN, 0] <= 0)
    def _():
        def strip(e, carry):
            @pl.when(tab_ref[TAB_LEN, e] > 0)
            def _():
                start(e)
            return carry

        lax.fori_loop(0, N_EXPERTS, strip, 0)


def _dispatch_kernel(pstart_ref, padrow_ref, padlen_ref, nused_ref, tab_ref, loc_ref, h_ref, xs_hbm,
                     xring, zero_blk, sem):
    i = pl.program_id(0)
    n_steps = pl.num_programs(0)
    t_rows = h_ref.shape[0]
    n_loc = TOP_K * t_rows
    blk_tok = zero_blk.shape[0]
    n_blocks = xs_hbm.shape[0] // blk_tok

    def zero_fills(act):
        def pad_rows(e, carry):
            @pl.when(padlen_ref[e] > 0)
            def _():
                act(pltpu.make_async_copy(
                    _rows(zero_blk, 0, padlen_ref[e]),
                    _rows(xs_hbm, padrow_ref[e], padlen_ref[e]), sem.at[DISPATCH_SLOTS]))
            return carry

        def unused_block(b, carry):
            act(pltpu.make_async_copy(
                zero_blk, _rows(xs_hbm, b * blk_tok, blk_tok), sem.at[DISPATCH_SLOTS]))
            return carry

        lax.fori_loop(0, N_EXPERTS, pad_rows, 0)
        lax.fori_loop(nused_ref[0], n_blocks, unused_block, 0)

    @pl.when(i == 0)
    def _():
        zero_blk[...] = jnp.zeros_like(zero_blk)
        zero_fills(lambda copy: copy.start())

    def strips_done(tile):
        slot = tile % DISPATCH_SLOTS
        pltpu.make_async_copy(
            _rows(xring, slot * n_loc, n_loc), _rows(xs_hbm, 0, n_loc), sem.at[slot]).wait()

    @pl.when(i >= DISPATCH_SLOTS)
    def _():
        strips_done(i - DISPATCH_SLOTS)

    slot = i % DISPATCH_SLOTS
    iota_p = lax.broadcasted_iota(I32, (n_loc, t_rows), 0)
    pick = jnp.logical_or(iota_p == loc_ref[0:1, :], iota_p == loc_ref[1:2, :])
    xsorted = _dot(jnp.where(pick, 1.0, 0.0).astype(BF16), h_ref[...])
    _pack_rows(xring, xsorted, slot * n_loc, is_bf16_exact=True)
    _strip_copies(tab_ref, pstart_ref, xring, xs_hbm, sem.at[slot], to_hbm=True,
                  local_row0=slot * n_loc)

    @pl.when(i == n_steps - 1)
    def _():
        for k in range(DISPATCH_SLOTS):
            @pl.when(n_steps > k)
            def _():
                strips_done(n_steps - 1 - k)
        zero_fills(lambda copy: copy.wait())


def _dispatch(pad_start, pad_row, pad_len, n_used, tab, loc, h, n_blocks):
    n, d = h.shape
    t = MIX_ROWS
    grid_spec = pltpu.PrefetchScalarGridSpec(
        num_scalar_prefetch=4,
        grid=(n // t,),
        in_specs=[
            pl.BlockSpec((SUBLANES, LANES), lambda i, *_: (i, 0), memory_space=pltpu.SMEM),
            pl.BlockSpec((SUBLANES, t), lambda i, *_: (0, i)),
            pl.BlockSpec((t, d), lambda i, *_: (i, 0)),
        ],
        out_specs=pl.BlockSpec(memory_space=pl.ANY),
        scratch_shapes=[
            pltpu.VMEM(_packed_shape(DISPATCH_SLOTS * TOP_K * t), I32),
            pltpu.VMEM(_packed_shape(EXPERT_ROWS), I32),
            pltpu.SemaphoreType.DMA((DISPATCH_SLOTS + 1,)),
        ],
    )
    return pl.pallas_call(
        _dispatch_kernel,
        grid_spec=grid_spec,
        out_shape=jax.ShapeDtypeStruct(_packed_shape(n_blocks * EXPERT_ROWS), I32),
        compiler_params=pltpu.CompilerParams(
            dimension_semantics=("arbitrary",), vmem_limit_bytes=VMEM_LIMIT_BYTES,
            has_side_effects=True),
        name="dispatch",
    )(pad_start, pad_row, pad_len, n_used, tab, loc, h)


IN_SLOTS = 6
AHEAD = 4
OUT_SLOTS = 4


def _expert_kernel(blk0_ref, nblk_ref, wsel_ref, nused_ref, xs_hbm, wg_ref, wu_ref, wd_ref, ys_hbm,
                   xring, yring, sem_in, sem_out):
    del wsel_ref
    e = pl.program_id(0)
    blk_tok = xring.shape[0] // IN_SLOTS
    n_used = nused_ref[0]
    n_blk = nblk_ref[e]

    def in_copy(g):
        slot = g % IN_SLOTS
        return pltpu.make_async_copy(
            _rows(xs_hbm, g * blk_tok, blk_tok), _rows(xring, slot * blk_tok, blk_tok),
            sem_in.at[slot])

    def out_copy(g):
        slot = g % OUT_SLOTS
        return pltpu.make_async_copy(
            _rows(yring, slot * blk_tok, blk_tok), _rows(ys_hbm, g * blk_tok, blk_tok),
            sem_out.at[slot])

    def request(g):
        @pl.when(g < n_used)
        def _():
            in_copy(g).start()

    def retire(g):
        @pl.when(g >= OUT_SLOTS)
        def _():
            out_copy(g - OUT_SLOTS).wait()

    def mlp(*blocks):
        xbs = [_unpack_rows(xring, (g % IN_SLOTS) * blk_tok, blk_tok) for g in blocks]
        w_gate = wg_ref[0].astype(BF16)
        gates = [_dot(xb, w_gate) for xb in xbs]
        w_up = wu_ref[0].astype(BF16)
        ups = [_dot(xb, w_up) for xb in xbs]
        acts = [(gate * jax.nn.sigmoid(gate) * up).astype(BF16) for gate, up in zip(gates, ups)]
        w_down = wd_ref[0].astype(BF16)
        for g, act in zip(blocks, acts):
            _pack_rows(yring, _dot(act, w_down), (g % OUT_SLOTS) * blk_tok)

    @pl.when(e == 0)
    def _():
        for g in range(AHEAD):
            request(g)

    def pair(p, carry):
        g = blk0_ref[e] + 2 * p
        in_copy(g).wait()
        in_copy(g + 1).wait()
        request(g + AHEAD)
        request(g + AHEAD + 1)
        retire(g)
        retire(g + 1)
        mlp(g, g + 1)
        out_copy(g).start()
        out_copy(g + 1).start()
        return carry

    lax.fori_loop(0, n_blk // 2, pair, 0)

    @pl.when(n_blk % 2 == 1)
    def _():
        g = blk0_ref[e] + n_blk - 1
        in_copy(g).wait()
        request(g + AHEAD)
        retire(g)
        mlp(g)
        out_copy(g).start()

    @pl.when(e == pl.num_programs(0) - 1)
    def _():
        for k in range(1, OUT_SLOTS + 1):
            @pl.when(n_used >= k)
            def _():
                out_copy(n_used - k).wait()


def _experts(blk0, nblk, wsel, n_used, xs, w_gate, w_up, w_down):
    bm = EXPERT_ROWS
    n_exp, d, de = w_gate.shape
    wblk = lambda e, b0, nb, ws, nu: (ws[e], 0, 0)
    grid_spec = pltpu.PrefetchScalarGridSpec(
        num_scalar_prefetch=4,
        grid=(n_exp,),
        in_specs=[
            pl.BlockSpec(memory_space=pl.ANY),
            pl.BlockSpec((1, d, de), wblk),
            pl.BlockSpec((1, d, de), wblk),
            pl.BlockSpec((1, de, d), wblk),
        ],
        out_specs=pl.BlockSpec(memory_space=pl.ANY),
        scratch_shapes=[
            pltpu.VMEM(_packed_shape(IN_SLOTS * bm), I32),
            pltpu.VMEM(_packed_shape(OUT_SLOTS * bm), I32),
            pltpu.SemaphoreType.DMA((IN_SLOTS,)),
            pltpu.SemaphoreType.DMA((OUT_SLOTS,)),
        ],
    )
    n_prefetch = 4
    return pl.pallas_call(
        _expert_kernel,
        grid_spec=grid_spec,
        out_shape=jax.ShapeDtypeStruct(xs.shape, xs.dtype),
        input_output_aliases={n_prefetch: 0},
        compiler_params=pltpu.CompilerParams(
            dimension_semantics=("arbitrary",), vmem_limit_bytes=VMEM_LIMIT_BYTES,
            has_side_effects=True),
        name="experts",
    )(blk0, nblk, wsel, n_used, xs, w_gate, w_up, w_down)


COMBINE_SLOTS = 3


def _combine_kernel(pstart_ref, tab_ref, tab1_ref, tab2_ref, x1_ref, mod_ref, ln2g_ref, ln2b_ref,
                    ewcol_ref, ys_hbm, out_ref, yring, sem):
    i = pl.program_id(0)
    n_tiles = pl.num_programs(0)
    t_rows = x1_ref.shape[0]
    n_loc = TOP_K * t_rows
    ahead = COMBINE_SLOTS - 1

    def fetch(table_ref, tile):
        slot = tile % COMBINE_SLOTS
        _strip_copies(table_ref, pstart_ref, yring, ys_hbm, sem.at[slot], to_hbm=False,
                      local_row0=slot * n_loc)

    @pl.when(i == 0)
    def _():
        fetch(tab_ref, 0)

        @pl.when(n_tiles > 1)
        def _():
            fetch(tab1_ref, 1)

    @pl.when(i + ahead < n_tiles)
    def _():
        fetch(tab2_ref, i + ahead)

    slot = i % COMBINE_SLOTS
    pltpu.make_async_copy(
        _rows(ys_hbm, 0, n_loc), _rows(yring, slot * n_loc, n_loc), sem.at[slot]).wait()
    ew = jnp.concatenate(
        [ewcol_ref[...], jnp.zeros((LANES - SUBLANES, t_rows), F32)], axis=0).T
    lane = lax.broadcasted_iota(I32, (t_rows, n_loc), 1)
    weights = jnp.where(lane == ew[:, 2:3].astype(I32), ew[:, 0:1],
                        jnp.where(lane == ew[:, 3:4].astype(I32), ew[:, 1:2], 0.0))
    y = _dot(weights.astype(BF16), _unpack_rows(yring, slot * n_loc, n_loc))
    gate2 = mod_ref[0, 5:6, :]
    out_ref[...] = _layer_norm(DEEPNORM_ALPHA * x1_ref[...] + (1.0 + gate2) * y,
                               ln2g_ref[...], ln2b_ref[...])


def _combine(pad_start, tab, x1, mod, ln2_g, ln2_b, ewcol, ys, *, seq):
    n, d = x1.shape
    t = MIX_ROWS
    per_seq = seq // t
    n_steps = n // t
    row_blk = lambda i, ps: (i, 0)
    later = lambda k: (lambda i, ps: (jnp.minimum(i + k, n_steps - 1), 0))
    grid_spec = pltpu.PrefetchScalarGridSpec(
        num_scalar_prefetch=1,
        grid=(n_steps,),
        in_specs=[
            pl.BlockSpec((SUBLANES, LANES), row_blk, memory_space=pltpu.SMEM),
            pl.BlockSpec((SUBLANES, LANES), later(1), memory_space=pltpu.SMEM),
            pl.BlockSpec((SUBLANES, LANES), later(COMBINE_SLOTS - 1), memory_space=pltpu.SMEM),
            pl.BlockSpec((t, d), row_blk),
            pl.BlockSpec((1, N_MOD, d), lambda i, ps: (i // per_seq, 0, 0)),
            pl.BlockSpec((1, d), lambda i, ps: (0, 0)),
            pl.BlockSpec((1, d), lambda i, ps: (0, 0)),
            pl.BlockSpec((SUBLANES, t), lambda i, ps: (0, i)),
            pl.BlockSpec(memory_space=pl.ANY),
        ],
        out_specs=pl.BlockSpec((t, d), row_blk),
        scratch_shapes=[
            pltpu.VMEM(_packed_shape(COMBINE_SLOTS * TOP_K * t), I32),
            pltpu.SemaphoreType.DMA((COMBINE_SLOTS,)),
        ],
    )
    return pl.pallas_call(
        _combine_kernel,
        grid_spec=grid_spec,
        out_shape=jax.ShapeDtypeStruct((n, d), F32),
        compiler_params=pltpu.CompilerParams(
            dimension_semantics=("arbitrary",), vmem_limit_bytes=VMEM_LIMIT_BYTES),
        name="combine",
    )(pad_start, tab, tab, tab, x1, mod, ln2_g, ln2_b, ewcol, ys)


def _block_layout(counts, n_assign):
    bm = EXPERT_ROWS
    n_blocks = -(-n_assign // bm) + N_EXPERTS
    padded = (counts + bm - 1) // bm * bm
    pad_end = jnp.cumsum(padded)
    pad_start = (pad_end - padded).astype(I32)
    n_used = (pad_end[-1] // bm).astype(I32)
    pad_row = (pad_start + counts).astype(I32)
    pad_len = (padded - counts).astype(I32)
    ids = jnp.arange(N_EXPERTS, dtype=I32)
    prev_used = lax.cummax(jnp.where(padded > 0, ids, -1))
    first_used = jnp.argmax(padded > 0).astype(I32)
    weight_sel = jnp.where(prev_used >= 0, prev_used, first_used).astype(I32)
    return (pad_start, pad_row, pad_len, pad_start // bm, (padded // bm).astype(I32), weight_sel,
            n_used.reshape(1), n_blocks)


def kernel(x, c, w_ada, b_ada, w_in, conv_w, w_out_conv, w_pool, pool_scale, w_o, ln1_g, ln1_b,
           w_group, b_group, w_router, b_router, w_gate, w_up, w_down, ln2_g, ln2_b):
    bsz, seq, d = x.shape
    n = bsz * seq
    assert d == 2 * PACK_SUBLANES * LANES and seq % MIX_ROWS == 0 and w_ada.shape[0] == DEPTH
    assert conv_w.shape[1] == CONV_WIDTH <= HALO and max(POOL_WINDOWS) <= 2 * HALO
    assert w_gate.shape[1] == N_EXPERTS and w_router.shape[2] == N_EXPERTS
    x2d = x.reshape(n, d)
    c_pad = jnp.pad(c, ((0, -bsz % SUBLANES), (0, 0)))
    for l in range(DEPTH):
        mod = _ada(c_pad, w_ada[l], b_ada[l][None, :])[:bsz].reshape(bsz, N_MOD, d)
        wr = jnp.concatenate(
            [w_router[l], w_group[l],
             jnp.zeros((d, LANES - N_EXPERTS - N_GROUPS), F32)], axis=1)
        rb = jnp.concatenate(
            [b_router[l].reshape(-1), b_group[l],
             jnp.zeros((LANES - N_EXPERTS - N_GROUPS,), F32)])[:, None]
        x1, h, loc, ewcol, tab, cnt = _mix(
            x2d, mod, w_in[l].astype(BF16), conv_w[l], w_out_conv[l].astype(BF16),
            w_pool[l].astype(BF16), pool_scale[l][None, :], w_o[l].astype(BF16),
            ln1_g[l][None, :], ln1_b[l][None, :], wr, rb, batch=bsz, seq=seq)
        pad_start, pad_row, pad_len, blk0, nblk, weight_sel, n_used, n_blocks = _block_layout(
            cnt[:, 0], n * TOP_K)
        xs = _dispatch(pad_start, pad_row, pad_len, n_used, tab, loc, h, n_blocks)
        ys = _experts(blk0, nblk, weight_sel, n_used, xs, w_gate[l], w_up[l], w_down[l])
        x2d = _combine(pad_start, tab, x1, mod, ln2_g[l][None, :], ln2_b[l][None, :],
                       ewcol, ys, seq=seq)
    return x2d.reshape(bsz, seq, d)
```

```python
import functools

import jax
import jax.numpy as jnp
from jax import lax
from jax.experimental import pallas as pl
from jax.experimental.pallas import tpu as pltpu

F32 = jnp.float32
BF16 = jnp.bfloat16
I32 = jnp.int32

LANES = 128
SUBLANES = 8
VMEM_LIMIT_BYTES = 56 * 1024 * 1024

N_POOL_GROUPS = 4
POOL_WINDOWS = (2, 4, 8, 16)
CONV_WIDTH = 3
N_GROUPS = 8
EXPERTS_PER_GROUP = 8
N_EXPERTS = N_GROUPS * EXPERTS_PER_GROUP
TOP_K = 2
N_MOD = 6
LN_EPS = 1e-5
DEPTH = 1
DEEPNORM_ALPHA = (2.0 * DEPTH) ** 0.25

MIX_ROWS = 512
EXPERT_ROWS = 256
HALO = SUBLANES
PACK_SUBLANES = 4


def _split_bf16(a):
    hi = a.astype(BF16)
    lo = (a - hi.astype(F32)).astype(BF16)
    return hi, lo


def _dot(a, b):
    return jnp.dot(a, b, preferred_element_type=F32)


def _dot3(a, b):
    a_hi, a_lo = _split_bf16(a)
    b_hi, b_lo = _split_bf16(b)
    return _dot(a_hi, b_hi) + _dot(a_lo, b_hi) + _dot(a_hi, b_lo)


def _dot3_wide(a, b):
    a_hi, a_lo = _split_bf16(a)
    n = b.shape[1]
    b_both = jnp.concatenate(_split_bf16(b), axis=1)
    hi = _dot(a_hi, b_both)
    lo = _dot(a_lo, b_both)
    return hi[:, :n] + hi[:, n:] + lo[:, :n] + lo[:, n:]


def _layer_norm(r, g, b):
    mu = jnp.mean(r, axis=-1, keepdims=True)
    d = r - mu
    var = jnp.mean(d * d, axis=-1, keepdims=True)
    return d * lax.rsqrt(var + LN_EPS) * g + b


def _packed_shape(rows):
    return (rows, PACK_SUBLANES, LANES)


def _plane(ref, s, row0, rows):
    flat = ref.reshape(ref.shape[0] * PACK_SUBLANES, LANES)
    return flat.at[pl.ds(row0 * PACK_SUBLANES + s, rows, stride=PACK_SUBLANES), :]


def _pack_rows(ref, val, row0=0, *, is_bf16_exact=False):
    rows, d = val.shape
    if not is_bf16_exact:
        val = val.astype(BF16).astype(F32)
    bits = lax.bitcast_convert_type(val, I32)
    for s in range(PACK_SUBLANES):
        hi = bits[:, s * LANES:(s + 1) * LANES]
        lo = bits[:, d // 2 + s * LANES:d // 2 + (s + 1) * LANES]
        _plane(ref, s, row0, rows)[...] = jnp.bitwise_or(hi, lax.shift_right_logical(lo, 16))


def _unpack_rows(ref, row0=0, rows=None):
    rows = ref.shape[0] - row0 if rows is None else rows
    his, los = [], []
    for s in range(PACK_SUBLANES):
        word = _plane(ref, s, row0, rows)[...]
        his.append(lax.bitcast_convert_type(jnp.bitwise_and(word, jnp.int32(-65536)), F32))
        los.append(lax.bitcast_convert_type(lax.shift_left(word, 16), F32))
    return jnp.concatenate(his + los, axis=1).astype(BF16)


def _rows(ref, row, n_rows):
    return ref.at[pl.ds(row, n_rows)]


def _ada_kernel(c_ref, w_ref, b_ref, o_ref):
    cc = c_ref[...]
    act = cc * jax.nn.sigmoid(cc)
    o_ref[...] = _dot3(act, w_ref[...]) + b_ref[...]


def _ada(c_pad, w_ada, b_ada):
    rows, d = c_pad.shape
    n_out = w_ada.shape[1]
    return pl.pallas_call(
        _ada_kernel,
        grid=(n_out // d,),
        in_specs=[
            pl.BlockSpec((rows, d), lambda j: (0, 0)),
            pl.BlockSpec((d, d), lambda j: (0, j)),
            pl.BlockSpec((1, d), lambda j: (0, j)),
        ],
        out_specs=pl.BlockSpec((rows, d), lambda j: (0, j)),
        out_shape=jax.ShapeDtypeStruct((rows, n_out), F32),
        compiler_params=pltpu.CompilerParams(
            dimension_semantics=("arbitrary",), vmem_limit_bytes=VMEM_LIMIT_BYTES),
        name="ada",
    )(c_pad, w_ada, b_ada)


def _mix_kernel(x_ref, mod_ref, win_ref, convw_ref, woc_ref, wpool_ref, pscale_ref, wo_ref,
                ln1g_ref, ln1b_ref, wr_ref, rb_ref, mod_prev_ref,
                x1_ref, h_ref, loc_ref, ewcol_ref, tab_ref, cnt_ref,
                cvbuf, zbuf, s2buf, s4buf, s8buf, base_ref, res_buf, *, tiles_per_seq):
    t_rows, d = x_ref.shape
    pg = d // N_POOL_GROUPS
    step = pl.program_id(0)
    n_tiles = pl.num_programs(0) - 1
    s = jnp.minimum(step, n_tiles - 1) % tiles_per_seq
    shift_bufs = (cvbuf, zbuf, s2buf, s4buf, s8buf)

    @pl.when(step == 0)
    def _():
        base_ref[...] = jnp.zeros_like(base_ref)
        res_buf[...] = jnp.zeros_like(res_buf)

    @pl.when(s == 0)
    def _():
        for buf in shift_bufs:
            buf[0:HALO, :] = jnp.zeros((HALO, buf.shape[1]), F32)

    @pl.when(s > 0)
    def _():
        for buf in shift_bufs:
            buf[0:HALO, :] = buf[t_rows:t_rows + HALO, :]

    def lagged_tile():
        x1 = _layer_norm(res_buf[...], ln1g_ref[...], ln1b_ref[...])
        x1_ref[...] = x1
        h = x1 * (1.0 + mod_prev_ref[0, 4:5, :]) + mod_prev_ref[0, 3:4, :]
        h_ref[...] = h.astype(BF16)
        lt = _dot3_wide(h, wr_ref[...]).T + rb_ref[...]
        _route_tile(lt, step > 0, base_ref, loc_ref, ewcol_ref, tab_ref, cnt_ref)

    @pl.when(step == n_tiles)
    def _():
        lagged_tile()

    @pl.when(step < n_tiles)
    def _():
        def shifted(buf, k):
            return buf[HALO - k:HALO - k + t_rows, :]

        xt = x_ref[...]
        shift1 = mod_ref[0, 0:1, :]
        scale1 = mod_ref[0, 1:2, :]
        gate1 = mod_ref[0, 2:3, :]

        u = (xt * (1.0 + scale1) + shift1).astype(BF16)

        def proj(i):
            return _dot(u, win_ref[:, i * d:(i + 1) * d])


        a_c = proj(2)
        a_val = proj(0)
        a_b = proj(1)
        cv = a_c * a_val

        lagged_tile()

        z = proj(3)
        cvbuf[HALO:HALO + t_rows, :] = cv
        conv = convw_ref[CONV_WIDTH - 1:CONV_WIDTH, :] * cv
        for k in range(1, CONV_WIDTH):
            conv = conv + convw_ref[CONV_WIDTH - 1 - k:CONV_WIDTH - k, :] * shifted(cvbuf, k)
        y_a = _dot((conv * a_b).astype(BF16), woc_ref[...])

        zbuf[HALO:HALO + t_rows, :] = z
        s2 = z + shifted(zbuf, 1)
        s2buf[HALO:HALO + t_rows, :] = s2[:, pg:]
        s4 = s2[:, pg:] + shifted(s2buf, 2)
        s4buf[HALO:HALO + t_rows, :] = s4[:, pg:]
        s8 = s4[:, pg:] + shifted(s4buf, 4)
        s8buf[HALO:HALO + t_rows, :] = s8[:, pg:]
        s16 = s8[:, pg:] + shifted(s8buf, 8)
        win_sums = (s2[:, :pg], s4[:, :pg], s8[:, :pg], s16)
        t_pos = s * t_rows + lax.broadcasted_iota(I32, (t_rows, 1), 0) + 1
        g_a = proj(4)
        y_b_parts = []
        for gi, w in enumerate(POOL_WINDOWS):
            count = jnp.minimum(t_pos, w).astype(F32)
            pooled = win_sums[gi] / count - z[:, gi * pg:(gi + 1) * pg]
            y_b_parts.append(_dot(pooled.astype(BF16), wpool_ref[gi]))
        g_b = proj(5)
        y_b = jnp.concatenate(y_b_parts, axis=1) * pscale_ref[...]

        merged = jax.nn.sigmoid(g_a) * y_a + jax.nn.sigmoid(g_b) * y_b
        mix = _dot(merged.astype(BF16), wo_ref[...])

        res_buf[...] = DEEPNORM_ALPHA * xt + (1.0 + gate1) * mix


def _route_tile(lt, valid, base_ref, loc_ref, ewcol_ref, tab_ref, cnt_ref):
    t_rows = lt.shape[1]
    iota8 = lax.broadcasted_iota(I32, (SUBLANES, t_rows), 0)
    lg = lt[N_EXPERTS:N_EXPERTS + N_GROUPS, :]
    g_max = jnp.max(lg, axis=0, keepdims=True)
    g_top = jnp.min(jnp.where(lg == g_max, iota8, N_GROUPS), axis=0, keepdims=True)
    g_top_p = 1.0 / jnp.sum(jnp.exp(lg - g_max), axis=0, keepdims=True)
    sel = jnp.zeros((EXPERTS_PER_GROUP, t_rows), F32)
    for g in range(N_GROUPS):
        sel = jnp.where(g_top == g, lt[g * EXPERTS_PER_GROUP:(g + 1) * EXPERTS_PER_GROUP, :], sel)
    m1 = jnp.max(sel, axis=0, keepdims=True)
    i1 = jnp.min(jnp.where(sel == m1, iota8, EXPERTS_PER_GROUP), axis=0, keepdims=True)
    sel2 = jnp.where(iota8 == i1, -jnp.inf, sel)
    m2 = jnp.max(sel2, axis=0, keepdims=True)
    i2 = jnp.min(jnp.where(sel2 == m2, iota8, EXPERTS_PER_GROUP), axis=0, keepdims=True)
    ratio = jnp.exp(m2 - m1)
    p1 = 1.0 / (1.0 + ratio)
    w1 = g_top_p * p1
    w2 = g_top_p * (ratio * p1)
    e1 = g_top * EXPERTS_PER_GROUP + i1
    e2 = g_top * EXPERTS_PER_GROUP + i2

    iota_e = lax.broadcasted_iota(I32, (N_EXPERTS, t_rows), 0)
    hit1 = jnp.logical_and(iota_e == e1, valid)
    hit2 = jnp.logical_and(iota_e == e2, valid)
    onehot = jnp.where(jnp.logical_or(hit1, hit2), 1.0, 0.0)
    before = (lax.broadcasted_iota(I32, (t_rows, t_rows), 0)
              < lax.broadcasted_iota(I32, (t_rows, t_rows), 1))
    prefix = _dot(onehot.astype(BF16), jnp.where(before, 1.0, 0.0).astype(BF16))
    tile_cnt = jnp.sum(onehot, axis=1, keepdims=True)
    lower = (lax.broadcasted_iota(I32, (N_EXPERTS, N_EXPERTS), 1)
             < lax.broadcasted_iota(I32, (N_EXPERTS, N_EXPERTS), 0))
    lower = jnp.where(lower, 1.0, 0.0).astype(BF16)
    cnt_hi, cnt_lo = _split_bf16(jnp.broadcast_to(tile_cnt, (N_EXPERTS, LANES)))
    loc_start = (_dot(lower, cnt_hi) + _dot(lower, cnt_lo))[:, 0:1]
    pos = prefix + loc_start
    loc1 = jnp.sum(jnp.where(hit1, pos, 0.0), axis=0, keepdims=True)
    loc2 = jnp.sum(jnp.where(hit2, pos, 0.0), axis=0, keepdims=True)
    base = base_ref[...]
    base_ref[...] = base + tile_cnt
    cnt_ref[...] = jnp.broadcast_to(base + tile_cnt, cnt_ref.shape).astype(I32)

    diag = (lax.broadcasted_iota(I32, (N_EXPERTS, LANES), 0)
            == lax.broadcasted_iota(I32, (N_EXPERTS, LANES), 1))

    def to_lanes(col):
        return jnp.sum(jnp.where(diag, col, 0.0), axis=0, keepdims=True)

    iota8l = lax.broadcasted_iota(I32, (SUBLANES, LANES), 0)
    min_len = jnp.min(jnp.broadcast_to(tile_cnt, (N_EXPERTS, LANES)), axis=0, keepdims=True)
    tab_ref[...] = jnp.where(
        iota8l == TAB_LEN, to_lanes(tile_cnt),
        jnp.where(iota8l == TAB_RANK, to_lanes(base),
                  jnp.where(iota8l == TAB_LOC, to_lanes(loc_start),
                            jnp.where(iota8l == TAB_MIN_LEN, min_len, 0.0)))).astype(I32)

    loc_ref[...] = jnp.where(iota8 == 0, loc1, jnp.where(iota8 == 1, loc2, 0.0)).astype(I32)
    iota_l = lax.broadcasted_iota(I32, (LANES, t_rows), 0)
    ewcol_ref[...] = jnp.where(
        iota_l == 0, w1, jnp.where(iota_l == 1, w2,
                                   jnp.where(iota_l == 2, loc1, jnp.where(iota_l == 3, loc2, 0.0)))).T


def _mix(x2d, mod, w_in, conv_w, w_oc, w_pool, pool_scale, w_o, ln1_g, ln1_b, wr, rb,
         *, batch, seq):
    n, d = x2d.shape
    t = MIX_ROWS
    n_s = seq // t
    n_tiles = batch * n_s
    const = lambda shape: pl.BlockSpec(shape, lambda i: (0,) * len(shape),
                                       pipeline_mode=pl.Buffered(1))
    cur = lambda i: jnp.minimum(i, n_tiles - 1)
    prev = lambda i: jnp.maximum(i - 1, 0)
    pg = d // N_POOL_GROUPS
    return pl.pallas_call(
        functools.partial(_mix_kernel, tiles_per_seq=n_s),
        grid=(n_tiles + 1,),
        in_specs=[
            pl.BlockSpec((t, d), lambda i: (cur(i), 0)),
            pl.BlockSpec((1, N_MOD, d), lambda i: (cur(i) // n_s, 0, 0)),
            const(w_in.shape), const(conv_w.shape), const(w_oc.shape), const(w_pool.shape),
            const(pool_scale.shape), const(w_o.shape), const(ln1_g.shape), const(ln1_b.shape),
            const(wr.shape), const(rb.shape),
            pl.BlockSpec((1, N_MOD, d), lambda i: (prev(i) // n_s, 0, 0)),
        ],
        out_specs=[
            pl.BlockSpec((t, d), lambda i: (prev(i), 0)),
            pl.BlockSpec((t, d), lambda i: (prev(i), 0)),
            pl.BlockSpec((SUBLANES, t), lambda i: (0, prev(i))),
            pl.BlockSpec((t, LANES), lambda i: (prev(i), 0)),
            pl.BlockSpec((SUBLANES, LANES), lambda i: (prev(i), 0)),
            pl.BlockSpec((N_EXPERTS, LANES), lambda i: (0, 0)),
        ],
        out_shape=[
            jax.ShapeDtypeStruct((n, d), F32),
            jax.ShapeDtypeStruct((n, d), BF16),
            jax.ShapeDtypeStruct((SUBLANES, n), I32),
            jax.ShapeDtypeStruct((n, LANES), F32),
            jax.ShapeDtypeStruct((n // t * SUBLANES, LANES), I32),
            jax.ShapeDtypeStruct((N_EXPERTS, LANES), I32),
        ],
        scratch_shapes=[
            pltpu.VMEM((t + HALO, d), F32),
            pltpu.VMEM((t + HALO, d), F32),
            pltpu.VMEM((t + HALO, 3 * pg), F32),
            pltpu.VMEM((t + HALO, 2 * pg), F32),
            pltpu.VMEM((t + HALO, pg), F32),
            pltpu.VMEM((N_EXPERTS, 1), F32),
            pltpu.VMEM((t, d), F32),
        ],
        compiler_params=pltpu.CompilerParams(
            dimension_semantics=("arbitrary",), vmem_limit_bytes=VMEM_LIMIT_BYTES),
        name="mix",
    )(x2d, mod, w_in, conv_w, w_oc, w_pool, pool_scale, w_o, ln1_g, ln1_b, wr, rb, mod)


TAB_LEN, TAB_RANK, TAB_LOC, TAB_MIN_LEN = 0, 1, 2, 3
DISPATCH_SLOTS = 3


def _strip_copies(tab_ref, pstart_ref, local_ref, sorted_hbm, sem, *, to_hbm, local_row0=0):
    def start(e):
        n_rows = tab_ref[TAB_LEN, e]
        local = _rows(local_ref, local_row0 + tab_ref[TAB_LOC, e], n_rows)
        remote = _rows(sorted_hbm, pstart_ref[e] + tab_ref[TAB_RANK, e], n_rows)
        src, dst = (local, remote) if to_hbm else (remote, local)
        pltpu.make_async_copy(src, dst, sem).start()

    @pl.when(tab_ref[TAB_MIN_LEN, 0] > 0)
    def _():
        for e in range(N_EXPERTS):
            start(e)

    @pl.when(tab_ref[TAB_MIN_LEN, 0] <= 0)
    def _():
        def strip(e, carry):
            @pl.when(tab_ref[TAB_LEN, e] > 0)
            def _():
                start(e)
            return carry

        lax.fori_loop(0, N_EXPERTS, strip, 0)


def _dispatch_kernel(pstart_ref, padrow_ref, padlen_ref, nused_ref, tab_ref, loc_ref, h_ref, xs_hbm,
                     xring, zero_blk, sem):
    i = pl.program_id(0)
    n_steps = pl.num_programs(0)
    t_rows = h_ref.shape[0]
    n_loc = TOP_K * t_rows
    blk_tok = zero_blk.shape[0]
    n_blocks = xs_hbm.shape[0] // blk_tok

    def zero_fills(act):
        def pad_rows(e, carry):
            @pl.when(padlen_ref[e] > 0)
            def _():
                act(pltpu.make_async_copy(
                    _rows(zero_blk, 0, padlen_ref[e]),
                    _rows(xs_hbm, padrow_ref[e], padlen_ref[e]), sem.at[DISPATCH_SLOTS]))
            return carry

        def unused_block(b, carry):
            act(pltpu.make_async_copy(
                zero_blk, _rows(xs_hbm, b * blk_tok, blk_tok), sem.at[DISPATCH_SLOTS]))
            return carry

        lax.fori_loop(0, N_EXPERTS, pad_rows, 0)
        lax.fori_loop(nused_ref[0], n_blocks, unused_block, 0)

    @pl.when(i == 0)
    def _():
        zero_blk[...] = jnp.zeros_like(zero_blk)
        zero_fills(lambda copy: copy.start())

    def strips_done(tile):
        slot = tile % DISPATCH_SLOTS
        pltpu.make_async_copy(
            _rows(xring, slot * n_loc, n_loc), _rows(xs_hbm, 0, n_loc), sem.at[slot]).wait()

    @pl.when(i >= DISPATCH_SLOTS)
    def _():
        strips_done(i - DISPATCH_SLOTS)

    slot = i % DISPATCH_SLOTS
    iota_p = lax.broadcasted_iota(I32, (n_loc, t_rows), 0)
    pick = jnp.logical_or(iota_p == loc_ref[0:1, :], iota_p == loc_ref[1:2, :])
    xsorted = _dot(jnp.where(pick, 1.0, 0.0).astype(BF16), h_ref[...])
    _pack_rows(xring, xsorted, slot * n_loc, is_bf16_exact=True)
    _strip_copies(tab_ref, pstart_ref, xring, xs_hbm, sem.at[slot], to_hbm=True,
                  local_row0=slot * n_loc)

    @pl.when(i == n_steps - 1)
    def _():
        for k in range(DISPATCH_SLOTS):
            @pl.when(n_steps > k)
            def _():
                strips_done(n_steps - 1 - k)
        zero_fills(lambda copy: copy.wait())


def _dispatch(pad_start, pad_row, pad_len, n_used, tab, loc, h, n_blocks):
    n, d = h.shape
    t = MIX_ROWS
    grid_spec = pltpu.PrefetchScalarGridSpec(
        num_scalar_prefetch=4,
        grid=(n // t,),
        in_specs=[
            pl.BlockSpec((SUBLANES, LANES), lambda i, *_: (i, 0), memory_space=pltpu.SMEM),
            pl.BlockSpec((SUBLANES, t), lambda i, *_: (0, i)),
            pl.BlockSpec((t, d), lambda i, *_: (i, 0)),
        ],
        out_specs=pl.BlockSpec(memory_space=pl.ANY),
        scratch_shapes=[
            pltpu.VMEM(_packed_shape(DISPATCH_SLOTS * TOP_K * t), I32),
            pltpu.VMEM(_packed_shape(EXPERT_ROWS), I32),
            pltpu.SemaphoreType.DMA((DISPATCH_SLOTS + 1,)),
        ],
    )
    return pl.pallas_call(
        _dispatch_kernel,
        grid_spec=grid_spec,
        out_shape=jax.ShapeDtypeStruct(_packed_shape(n_blocks * EXPERT_ROWS), I32),
        compiler_params=pltpu.CompilerParams(
            dimension_semantics=("arbitrary",), vmem_limit_bytes=VMEM_LIMIT_BYTES,
            has_side_effects=True),
        name="dispatch",
    )(pad_start, pad_row, pad_len, n_used, tab, loc, h)


IN_SLOTS = 6
AHEAD = 4
OUT_SLOTS = 4


def _expert_kernel(blk0_ref, nblk_ref, wsel_ref, nused_ref, xs_hbm, wg_ref, wu_ref, wd_ref, ys_hbm,
                   xring, yring, sem_in, sem_out):
    del wsel_ref
    e = pl.program_id(0)
    blk_tok = xring.shape[0] // IN_SLOTS
    n_used = nused_ref[0]
    n_blk = nblk_ref[e]

    def in_copy(g):
        slot = g % IN_SLOTS
        return pltpu.make_async_copy(
            _rows(xs_hbm, g * blk_tok, blk_tok), _rows(xring, slot * blk_tok, blk_tok),
            sem_in.at[slot])

    def out_copy(g):
        slot = g % OUT_SLOTS
        return pltpu.make_async_copy(
            _rows(yring, slot * blk_tok, blk_tok), _rows(ys_hbm, g * blk_tok, blk_tok),
            sem_out.at[slot])

    def request(g):
        @pl.when(g < n_used)
        def _():
            in_copy(g).start()

    def retire(g):
        @pl.when(g >= OUT_SLOTS)
        def _():
            out_copy(g - OUT_SLOTS).wait()

    def mlp(*blocks):
        xbs = [_unpack_rows(xring, (g % IN_SLOTS) * blk_tok, blk_tok) for g in blocks]
        w_gate = wg_ref[0].astype(BF16)
        gates = [_dot(xb, w_gate) for xb in xbs]
        w_up = wu_ref[0].astype(BF16)
        ups = [_dot(xb, w_up) for xb in xbs]
        acts = [(gate * jax.nn.sigmoid(gate) * up).astype(BF16) for gate, up in zip(gates, ups)]
        w_down = wd_ref[0].astype(BF16)
        for g, act in zip(blocks, acts):
            _pack_rows(yring, _dot(act, w_down), (g % OUT_SLOTS) * blk_tok)

    @pl.when(e == 0)
    def _():
        for g in range(AHEAD):
            request(g)

    def pair(p, carry):
        g = blk0_ref[e] + 2 * p
        in_copy(g).wait()
        in_copy(g + 1).wait()
        request(g + AHEAD)
        request(g + AHEAD + 1)
        retire(g)
        retire(g + 1)
        mlp(g, g + 1)
        out_copy(g).start()
        out_copy(g + 1).start()
        return carry

    lax.fori_loop(0, n_blk // 2, pair, 0)

    @pl.when(n_blk % 2 == 1)
    def _():
        g = blk0_ref[e] + n_blk - 1
        in_copy(g).wait()
        request(g + AHEAD)
        retire(g)
        mlp(g)
        out_copy(g).start()

    @pl.when(e == pl.num_programs(0) - 1)
    def _():
        for k in range(1, OUT_SLOTS + 1):
            @pl.when(n_used >= k)
            def _():
                out_copy(n_used - k).wait()


def _experts(blk0, nblk, wsel, n_used, xs, w_gate, w_up, w_down):
    bm = EXPERT_ROWS
    n_exp, d, de = w_gate.shape
    wblk = lambda e, b0, nb, ws, nu: (ws[e], 0, 0)
    grid_spec = pltpu.PrefetchScalarGridSpec(
        num_scalar_prefetch=4,
        grid=(n_exp,),
        in_specs=[
            pl.BlockSpec(memory_space=pl.ANY),
            pl.BlockSpec((1, d, de), wblk),
            pl.BlockSpec((1, d, de), wblk),
            pl.BlockSpec((1, de, d), wblk),
        ],
        out_specs=pl.BlockSpec(memory_space=pl.ANY),
        scratch_shapes=[
            pltpu.VMEM(_packed_shape(IN_SLOTS * bm), I32),
            pltpu.VMEM(_packed_shape(OUT_SLOTS * bm), I32),
            pltpu.SemaphoreType.DMA((IN_SLOTS,)),
            pltpu.SemaphoreType.DMA((OUT_SLOTS,)),
        ],
    )
    n_prefetch = 4
    return pl.pallas_call(
        _expert_kernel,
        grid_spec=grid_spec,
        out_shape=jax.ShapeDtypeStruct(xs.shape, xs.dtype),
        input_output_aliases={n_prefetch: 0},
        compiler_params=pltpu.CompilerParams(
            dimension_semantics=("arbitrary",), vmem_limit_bytes=VMEM_LIMIT_BYTES,
            has_side_effects=True),
        name="experts",
    )(blk0, nblk, wsel, n_used, xs, w_gate, w_up, w_down)


COMBINE_SLOTS = 4


def _combine_kernel(pstart_ref, *refs):
    tabs = refs[:COMBINE_SLOTS]
    x1_ref, mod_ref, ln2g_ref, ln2b_ref, ewcol_ref, ys_hbm, out_ref, yring, sem = (
        refs[COMBINE_SLOTS:])
    i = pl.program_id(0)
    n_tiles = pl.num_programs(0)
    t_rows = x1_ref.shape[0]
    n_loc = TOP_K * t_rows
    ahead = COMBINE_SLOTS - 1

    def fetch(table_ref, tile):
        slot = tile % COMBINE_SLOTS
        _strip_copies(table_ref, pstart_ref, yring, ys_hbm, sem.at[slot], to_hbm=False,
                      local_row0=slot * n_loc)

    @pl.when(i == 0)
    def _():
        for k in range(ahead):
            @pl.when(n_tiles > k)
            def _(k=k):
                fetch(tabs[k], k)

    @pl.when(i + ahead < n_tiles)
    def _():
        fetch(tabs[ahead], i + ahead)

    slot = i % COMBINE_SLOTS
    pltpu.make_async_copy(
        _rows(ys_hbm, 0, n_loc), _rows(yring, slot * n_loc, n_loc), sem.at[slot]).wait()
    ew = ewcol_ref[...]
    lane = lax.broadcasted_iota(I32, (t_rows, n_loc), 1)
    weights = jnp.where(lane == ew[:, 2:3].astype(I32), ew[:, 0:1],
                        jnp.where(lane == ew[:, 3:4].astype(I32), ew[:, 1:2], 0.0))
    y = _dot(weights.astype(BF16), _unpack_rows(yring, slot * n_loc, n_loc))
    gate2 = mod_ref[0, 5:6, :]
    out_ref[...] = _layer_norm(DEEPNORM_ALPHA * x1_ref[...] + (1.0 + gate2) * y,
                               ln2g_ref[...], ln2b_ref[...])


def _combine(pad_start, tab, x1, mod, ln2_g, ln2_b, ewcol, ys, *, seq):
    n, d = x1.shape
    t = MIX_ROWS
    per_seq = seq // t
    n_steps = n // t
    row_blk = lambda i, ps: (i, 0)
    later = lambda k: (lambda i, ps: (jnp.minimum(i + k, n_steps - 1), 0))
    grid_spec = pltpu.PrefetchScalarGridSpec(
        num_scalar_prefetch=1,
        grid=(n_steps,),
        in_specs=[
            pl.BlockSpec((SUBLANES, LANES), later(k), memory_space=pltpu.SMEM)
            for k in range(COMBINE_SLOTS)
        ] + [
            pl.BlockSpec((t, d), row_blk),
            pl.BlockSpec((1, N_MOD, d), lambda i, ps: (i // per_seq, 0, 0)),
            pl.BlockSpec((1, d), lambda i, ps: (0, 0)),
            pl.BlockSpec((1, d), lambda i, ps: (0, 0)),
            pl.BlockSpec((t, LANES), row_blk),
            pl.BlockSpec(memory_space=pl.ANY),
        ],
        out_specs=pl.BlockSpec((t, d), row_blk),
        scratch_shapes=[
            pltpu.VMEM(_packed_shape(COMBINE_SLOTS * TOP_K * t), I32),
            pltpu.SemaphoreType.DMA((COMBINE_SLOTS,)),
        ],
    )
    return pl.pallas_call(
        _combine_kernel,
        grid_spec=grid_spec,
        out_shape=jax.ShapeDtypeStruct((n, d), F32),
        compiler_params=pltpu.CompilerParams(
            dimension_semantics=("arbitrary",), vmem_limit_bytes=VMEM_LIMIT_BYTES),
        name="combine",
    )(pad_start, *([tab] * COMBINE_SLOTS), x1, mod, ln2_g, ln2_b, ewcol, ys)


def _block_layout(counts, n_assign):
    bm = EXPERT_ROWS
    n_blocks = -(-n_assign // bm) + N_EXPERTS
    padded = (counts + bm - 1) // bm * bm
    pad_end = jnp.cumsum(padded)
    pad_start = (pad_end - padded).astype(I32)
    n_used = (pad_end[-1] // bm).astype(I32)
    pad_row = (pad_start + counts).astype(I32)
    pad_len = (padded - counts).astype(I32)
    ids = jnp.arange(N_EXPERTS, dtype=I32)
    prev_used = lax.cummax(jnp.where(padded > 0, ids, -1))
    first_used = jnp.argmax(padded > 0).astype(I32)
    weight_sel = jnp.where(prev_used >= 0, prev_used, first_used).astype(I32)
    return (pad_start, pad_row, pad_len, pad_start // bm, (padded // bm).astype(I32), weight_sel,
            n_used.reshape(1), n_blocks)


def kernel(x, c, w_ada, b_ada, w_in, conv_w, w_out_conv, w_pool, pool_scale, w_o, ln1_g, ln1_b,
           w_group, b_group, w_router, b_router, w_gate, w_up, w_down, ln2_g, ln2_b):
    bsz, seq, d = x.shape
    n = bsz * seq
    assert d == 2 * PACK_SUBLANES * LANES and seq % MIX_ROWS == 0 and w_ada.shape[0] == DEPTH
    assert conv_w.shape[1] == CONV_WIDTH <= HALO and max(POOL_WINDOWS) <= 2 * HALO
    assert w_gate.shape[1] == N_EXPERTS and w_router.shape[2] == N_EXPERTS
    x2d = x.reshape(n, d)
    c_pad = jnp.pad(c, ((0, -bsz % SUBLANES), (0, 0)))
    for l in range(DEPTH):
        mod = _ada(c_pad, w_ada[l], b_ada[l][None, :])[:bsz].reshape(bsz, N_MOD, d)
        wr = jnp.concatenate(
            [w_router[l], w_group[l],
             jnp.zeros((d, LANES - N_EXPERTS - N_GROUPS), F32)], axis=1)
        rb = jnp.concatenate(
            [b_router[l].reshape(-1), b_group[l],
             jnp.zeros((LANES - N_EXPERTS - N_GROUPS,), F32)])[:, None]
        x1, h, loc, ewcol, tab, cnt = _mix(
            x2d, mod, w_in[l].astype(BF16), conv_w[l], w_out_conv[l].astype(BF16),
            w_pool[l].astype(BF16), pool_scale[l][None, :], w_o[l].astype(BF16),
            ln1_g[l][None, :], ln1_b[l][None, :], wr, rb, batch=bsz, seq=seq)
        pad_start, pad_row, pad_len, blk0, nblk, weight_sel, n_used, n_blocks = _block_layout(
            cnt[:, 0], n * TOP_K)
        xs = _dispatch(pad_start, pad_row, pad_len, n_used, tab, loc, h, n_blocks)
        ys = _experts(blk0, nblk, weight_sel, n_used, xs, w_gate[l], w_up[l], w_down[l])
        x2d = _combine(pad_start, tab, x1, mod, ln2_g[l][None, :], ln2_b[l][None, :],
                       ewcol, ys, seq=seq)
    return x2d.reshape(bsz, seq, d)
```

```python
import functools

import jax
import jax.numpy as jnp
from jax import lax
from jax.experimental import pallas as pl
from jax.experimental.pallas import tpu as pltpu

F32 = jnp.float32
BF16 = jnp.bfloat16
I32 = jnp.int32

LANES = 128
SUBLANES = 8
VMEM_LIMIT_BYTES = 56 * 1024 * 1024

N_POOL_GROUPS = 4
POOL_WINDOWS = (2, 4, 8, 16)
CONV_WIDTH = 3
N_GROUPS = 8
EXPERTS_PER_GROUP = 8
N_EXPERTS = N_GROUPS * EXPERTS_PER_GROUP
TOP_K = 2
N_MOD = 6
LN_EPS = 1e-5
DEPTH = 1
DEEPNORM_ALPHA = (2.0 * DEPTH) ** 0.25

MIX_ROWS = 512
EXPERT_ROWS = 256
HALO = SUBLANES
PACK_SUBLANES = 4


def _split_bf16(a):
    hi = a.astype(BF16)
    lo = (a - hi.astype(F32)).astype(BF16)
    return hi, lo


def _dot(a, b):
    return jnp.dot(a, b, preferred_element_type=F32)


def _dot3(a, b):
    a_hi, a_lo = _split_bf16(a)
    b_hi, b_lo = _split_bf16(b)
    return _dot(a_hi, b_hi) + _dot(a_lo, b_hi) + _dot(a_hi, b_lo)


def _dot3_wide(a, b):
    a_hi, a_lo = _split_bf16(a)
    n = b.shape[1]
    b_both = jnp.concatenate(_split_bf16(b), axis=1)
    hi = _dot(a_hi, b_both)
    lo = _dot(a_lo, b_both)
    return hi[:, :n] + hi[:, n:] + lo[:, :n] + lo[:, n:]


def _layer_norm(r, g, b):
    mu = jnp.mean(r, axis=-1, keepdims=True)
    d = r - mu
    var = jnp.mean(d * d, axis=-1, keepdims=True)
    return d * lax.rsqrt(var + LN_EPS) * g + b


def _packed_shape(rows):
    return (rows, PACK_SUBLANES, LANES)


def _plane(ref, s, row0, rows):
    flat = ref.reshape(ref.shape[0] * PACK_SUBLANES, LANES)
    return flat.at[pl.ds(row0 * PACK_SUBLANES + s, rows, stride=PACK_SUBLANES), :]


def _pack_rows(ref, val, row0=0, *, is_bf16_exact=False):
    rows, d = val.shape
    if not is_bf16_exact:
        val = val.astype(BF16).astype(F32)
    bits = lax.bitcast_convert_type(val, I32)
    for s in range(PACK_SUBLANES):
        hi = bits[:, s * LANES:(s + 1) * LANES]
        lo = bits[:, d // 2 + s * LANES:d // 2 + (s + 1) * LANES]
        _plane(ref, s, row0, rows)[...] = jnp.bitwise_or(hi, lax.shift_right_logical(lo, 16))


def _unpack_rows(ref, row0=0, rows=None):
    rows = ref.shape[0] - row0 if rows is None else rows
    his, los = [], []
    for s in range(PACK_SUBLANES):
        word = _plane(ref, s, row0, rows)[...]
        his.append(lax.bitcast_convert_type(jnp.bitwise_and(word, jnp.int32(-65536)), F32))
        los.append(lax.bitcast_convert_type(lax.shift_left(word, 16), F32))
    return jnp.concatenate(his + los, axis=1).astype(BF16)


def _rows(ref, row, n_rows):
    return ref.at[pl.ds(row, n_rows)]


def _ada_kernel(c_ref, w_ref, b_ref, o_ref):
    cc = c_ref[...]
    act = cc * jax.nn.sigmoid(cc)
    o_ref[...] = _dot3(act, w_ref[...]) + b_ref[...]


def _ada(c_pad, w_ada, b_ada):
    rows, d = c_pad.shape
    n_out = w_ada.shape[1]
    return pl.pallas_call(
        _ada_kernel,
        grid=(n_out // d,),
        in_specs=[
            pl.BlockSpec((rows, d), lambda j: (0, 0)),
            pl.BlockSpec((d, d), lambda j: (0, j)),
            pl.BlockSpec((1, d), lambda j: (0, j)),
        ],
        out_specs=pl.BlockSpec((rows, d), lambda j: (0, j)),
        out_shape=jax.ShapeDtypeStruct((rows, n_out), F32),
        compiler_params=pltpu.CompilerParams(
            dimension_semantics=("arbitrary",), vmem_limit_bytes=VMEM_LIMIT_BYTES),
        name="ada",
    )(c_pad, w_ada, b_ada)


def _mix_kernel(x_ref, mod_ref, win_ref, convw_ref, woc_ref, wpool_ref, pscale_ref, wo_ref,
                ln1g_ref, ln1b_ref, wr_ref, rb_ref, mod_prev_ref,
                x1_ref, h_ref, loc_ref, ewcol_ref, tab_ref, cnt_ref,
                cvbuf, zbuf, s2buf, s4buf, s8buf, base_ref, res_buf, *, tiles_per_seq):
    t_rows, d = x_ref.shape
    pg = d // N_POOL_GROUPS
    step = pl.program_id(0)
    n_tiles = pl.num_programs(0) - 1
    s = jnp.minimum(step, n_tiles - 1) % tiles_per_seq
    shift_bufs = (cvbuf, zbuf, s2buf, s4buf, s8buf)

    @pl.when(step == 0)
    def _():
        base_ref[...] = jnp.zeros_like(base_ref)
        res_buf[...] = jnp.zeros_like(res_buf)

    @pl.when(s == 0)
    def _():
        for buf in shift_bufs:
            buf[0:HALO, :] = jnp.zeros((HALO, buf.shape[1]), F32)

    @pl.when(s > 0)
    def _():
        for buf in shift_bufs:
            buf[0:HALO, :] = buf[t_rows:t_rows + HALO, :]

    def lagged_tile():
        x1 = _layer_norm(res_buf[...], ln1g_ref[...], ln1b_ref[...])
        x1_ref[...] = x1
        h = x1 * (1.0 + mod_prev_ref[0, 4:5, :]) + mod_prev_ref[0, 3:4, :]
        h_ref[...] = h.astype(BF16)
        lt = _dot3_wide(h, wr_ref[...]).T + rb_ref[...]
        _route_tile(lt, step > 0, base_ref, loc_ref, ewcol_ref, tab_ref, cnt_ref)

    @pl.when(step == n_tiles)
    def _():
        lagged_tile()

    @pl.when(step < n_tiles)
    def _():
        def shifted(buf, k):
            return buf[HALO - k:HALO - k + t_rows, :]

        xt = x_ref[...]
        shift1 = mod_ref[0, 0:1, :]
        scale1 = mod_ref[0, 1:2, :]
        gate1 = mod_ref[0, 2:3, :]

        u = (xt * (1.0 + scale1) + shift1).astype(BF16)

        def proj(i):
            return _dot(u, win_ref[:, i * d:(i + 1) * d])


        a_c = proj(2)
        a_val = proj(0)
        a_b = proj(1)
        cv = a_c * a_val

        lagged_tile()

        z = proj(3)
        cvbuf[HALO:HALO + t_rows, :] = cv
        conv = convw_ref[CONV_WIDTH - 1:CONV_WIDTH, :] * cv
        for k in range(1, CONV_WIDTH):
            conv = conv + convw_ref[CONV_WIDTH - 1 - k:CONV_WIDTH - k, :] * shifted(cvbuf, k)
        y_a = _dot((conv * a_b).astype(BF16), woc_ref[...])

        zbuf[HALO:HALO + t_rows, :] = z
        s2 = z + shifted(zbuf, 1)
        s2buf[HALO:HALO + t_rows, :] = s2[:, pg:]
        s4 = s2[:, pg:] + shifted(s2buf, 2)
        s4buf[HALO:HALO + t_rows, :] = s4[:, pg:]
        s8 = s4[:, pg:] + shifted(s4buf, 4)
        s8buf[HALO:HALO + t_rows, :] = s8[:, pg:]
        s16 = s8[:, pg:] + shifted(s8buf, 8)
        win_sums = (s2[:, :pg], s4[:, :pg], s8[:, :pg], s16)
        t_pos = s * t_rows + lax.broadcasted_iota(I32, (t_rows, 1), 0) + 1
        g_a = proj(4)
        y_b_parts = []
        for gi, w in enumerate(POOL_WINDOWS):
            count = jnp.minimum(t_pos, w).astype(F32)
            pooled = win_sums[gi] / count - z[:, gi * pg:(gi + 1) * pg]
            y_b_parts.append(_dot(pooled.astype(BF16), wpool_ref[gi]))
        g_b = proj(5)
        y_b = jnp.concatenate(y_b_parts, axis=1) * pscale_ref[...]

        merged = jax.nn.sigmoid(g_a) * y_a + jax.nn.sigmoid(g_b) * y_b
        mix = _dot(merged.astype(BF16), wo_ref[...])

        res_buf[...] = DEEPNORM_ALPHA * xt + (1.0 + gate1) * mix


def _route_tile(lt, valid, base_ref, loc_ref, ewcol_ref, tab_ref, cnt_ref):
    t_rows = lt.shape[1]
    iota8 = lax.broadcasted_iota(I32, (SUBLANES, t_rows), 0)
    lg = lt[N_EXPERTS:N_EXPERTS + N_GROUPS, :]
    g_max = jnp.max(lg, axis=0, keepdims=True)
    g_top = jnp.min(jnp.where(lg == g_max, iota8, N_GROUPS), axis=0, keepdims=True)
    g_top_p = 1.0 / jnp.sum(jnp.exp(lg - g_max), axis=0, keepdims=True)
    sel = jnp.zeros((EXPERTS_PER_GROUP, t_rows), F32)
    for g in range(N_GROUPS):
        sel = jnp.where(g_top == g, lt[g * EXPERTS_PER_GROUP:(g + 1) * EXPERTS_PER_GROUP, :], sel)
    m1 = jnp.max(sel, axis=0, keepdims=True)
    i1 = jnp.min(jnp.where(sel == m1, iota8, EXPERTS_PER_GROUP), axis=0, keepdims=True)
    sel2 = jnp.where(iota8 == i1, -jnp.inf, sel)
    m2 = jnp.max(sel2, axis=0, keepdims=True)
    i2 = jnp.min(jnp.where(sel2 == m2, iota8, EXPERTS_PER_GROUP), axis=0, keepdims=True)
    ratio = jnp.exp(m2 - m1)
    p1 = 1.0 / (1.0 + ratio)
    w1 = g_top_p * p1
    w2 = g_top_p * (ratio * p1)
    e1 = g_top * EXPERTS_PER_GROUP + i1
    e2 = g_top * EXPERTS_PER_GROUP + i2

    iota_e = lax.broadcasted_iota(I32, (N_EXPERTS, t_rows), 0)
    hit1 = jnp.logical_and(iota_e == e1, valid)
    hit2 = jnp.logical_and(iota_e == e2, valid)
    onehot = jnp.where(jnp.logical_or(hit1, hit2), 1.0, 0.0)
    before = (lax.broadcasted_iota(I32, (t_rows, t_rows), 0)
              < lax.broadcasted_iota(I32, (t_rows, t_rows), 1))
    prefix = _dot(onehot.astype(BF16), jnp.where(before, 1.0, 0.0).astype(BF16))
    tile_cnt = jnp.sum(onehot, axis=1, keepdims=True)
    lower = (lax.broadcasted_iota(I32, (N_EXPERTS, N_EXPERTS), 1)
             < lax.broadcasted_iota(I32, (N_EXPERTS, N_EXPERTS), 0))
    lower = jnp.where(lower, 1.0, 0.0).astype(BF16)
    cnt_hi, cnt_lo = _split_bf16(jnp.broadcast_to(tile_cnt, (N_EXPERTS, LANES)))
    loc_start = (_dot(lower, cnt_hi) + _dot(lower, cnt_lo))[:, 0:1]
    pos = prefix + loc_start
    loc1 = jnp.sum(jnp.where(hit1, pos, 0.0), axis=0, keepdims=True)
    loc2 = jnp.sum(jnp.where(hit2, pos, 0.0), axis=0, keepdims=True)
    base = base_ref[...]
    base_ref[...] = base + tile_cnt
    cnt_ref[...] = jnp.broadcast_to(base + tile_cnt, cnt_ref.shape).astype(I32)

    diag = (lax.broadcasted_iota(I32, (N_EXPERTS, LANES), 0)
            == lax.broadcasted_iota(I32, (N_EXPERTS, LANES), 1))

    def to_lanes(col):
        return jnp.sum(jnp.where(diag, col, 0.0), axis=0, keepdims=True)

    iota8l = lax.broadcasted_iota(I32, (SUBLANES, LANES), 0)
    min_len = jnp.min(jnp.broadcast_to(tile_cnt, (N_EXPERTS, LANES)), axis=0, keepdims=True)
    tab_ref[...] = jnp.where(
        iota8l == TAB_LEN, to_lanes(tile_cnt),
        jnp.where(iota8l == TAB_RANK, to_lanes(base),
                  jnp.where(iota8l == TAB_LOC, to_lanes(loc_start),
                            jnp.where(iota8l == TAB_MIN_LEN, min_len, 0.0)))).astype(I32)

    loc_ref[...] = jnp.where(iota8 == 0, loc1, jnp.where(iota8 == 1, loc2, 0.0)).astype(I32)
    iota_l = lax.broadcasted_iota(I32, (LANES, t_rows), 0)
    ewcol_ref[...] = jnp.where(
        iota_l == 0, w1, jnp.where(iota_l == 1, w2,
                                   jnp.where(iota_l == 2, loc1, jnp.where(iota_l == 3, loc2, 0.0)))).T


def _mix(x2d, mod, w_in, conv_w, w_oc, w_pool, pool_scale, w_o, ln1_g, ln1_b, wr, rb,
         *, batch, seq):
    n, d = x2d.shape
    t = MIX_ROWS
    n_s = seq // t
    n_tiles = batch * n_s
    const = lambda shape: pl.BlockSpec(shape, lambda i: (0,) * len(shape),
                                       pipeline_mode=pl.Buffered(1))
    cur = lambda i: jnp.minimum(i, n_tiles - 1)
    prev = lambda i: jnp.maximum(i - 1, 0)
    pg = d // N_POOL_GROUPS
    return pl.pallas_call(
        functools.partial(_mix_kernel, tiles_per_seq=n_s),
        grid=(n_tiles + 1,),
        in_specs=[
            pl.BlockSpec((t, d), lambda i: (cur(i), 0)),
            pl.BlockSpec((1, N_MOD, d), lambda i: (cur(i) // n_s, 0, 0)),
            const(w_in.shape), const(conv_w.shape), const(w_oc.shape), const(w_pool.shape),
            const(pool_scale.shape), const(w_o.shape), const(ln1_g.shape), const(ln1_b.shape),
            const(wr.shape), const(rb.shape),
            pl.BlockSpec((1, N_MOD, d), lambda i: (prev(i) // n_s, 0, 0)),
        ],
        out_specs=[
            pl.BlockSpec((t, d), lambda i: (prev(i), 0)),
            pl.BlockSpec((t, d), lambda i: (prev(i), 0)),
            pl.BlockSpec((SUBLANES, t), lambda i: (0, prev(i))),
            pl.BlockSpec((t, LANES), lambda i: (prev(i), 0)),
            pl.BlockSpec((SUBLANES, LANES), lambda i: (prev(i), 0)),
            pl.BlockSpec((N_EXPERTS, LANES), lambda i: (0, 0)),
        ],
        out_shape=[
            jax.ShapeDtypeStruct((n, d), F32),
            jax.ShapeDtypeStruct((n, d), BF16),
            jax.ShapeDtypeStruct((SUBLANES, n), I32),
            jax.ShapeDtypeStruct((n, LANES), F32),
            jax.ShapeDtypeStruct((n // t * SUBLANES, LANES), I32),
            jax.ShapeDtypeStruct((N_EXPERTS, LANES), I32),
        ],
        scratch_shapes=[
            pltpu.VMEM((t + HALO, d), F32),
            pltpu.VMEM((t + HALO, d), F32),
            pltpu.VMEM((t + HALO, 3 * pg), F32),
            pltpu.VMEM((t + HALO, 2 * pg), F32),
            pltpu.VMEM((t + HALO, pg), F32),
            pltpu.VMEM((N_EXPERTS, 1), F32),
            pltpu.VMEM((t, d), F32),
        ],
        compiler_params=pltpu.CompilerParams(
            dimension_semantics=("arbitrary",), vmem_limit_bytes=VMEM_LIMIT_BYTES),
        name="mix",
    )(x2d, mod, w_in, conv_w, w_oc, w_pool, pool_scale, w_o, ln1_g, ln1_b, wr, rb, mod)


TAB_LEN, TAB_RANK, TAB_LOC, TAB_MIN_LEN = 0, 1, 2, 3
DISPATCH_SLOTS = 3


def _strip_copies(tab_ref, pstart_ref, local_ref, sorted_hbm, sem, *, to_hbm, local_row0=0):
    def start(e, priority=0):
        n_rows = tab_ref[TAB_LEN, e]
        local = _rows(local_ref, local_row0 + tab_ref[TAB_LOC, e], n_rows)
        remote = _rows(sorted_hbm, pstart_ref[e] + tab_ref[TAB_RANK, e], n_rows)
        src, dst = (local, remote) if to_hbm else (remote, local)
        pltpu.make_async_copy(src, dst, sem).start(priority)

    @pl.when(tab_ref[TAB_MIN_LEN, 0] > 0)
    def _():
        for e in range(N_EXPERTS):
            start(e, priority=e % 2)

    @pl.when(tab_ref[TAB_MIN_LEN, 0] <= 0)
    def _():
        def strip(e, carry):
            @pl.when(tab_ref[TAB_LEN, e] > 0)
            def _():
                start(e)
            return carry

        lax.fori_loop(0, N_EXPERTS, strip, 0)


def _dispatch_kernel(pstart_ref, padrow_ref, padlen_ref, nused_ref, tab_ref, loc_ref, h_ref, xs_hbm,
                     xring, zero_blk, sem):
    i = pl.program_id(0)
    n_steps = pl.num_programs(0)
    t_rows = h_ref.shape[0]
    n_loc = TOP_K * t_rows
    blk_tok = zero_blk.shape[0]
    n_blocks = xs_hbm.shape[0] // blk_tok

    def zero_fills(act):
        def pad_rows(e, carry):
            @pl.when(padlen_ref[e] > 0)
            def _():
                act(pltpu.make_async_copy(
                    _rows(zero_blk, 0, padlen_ref[e]),
                    _rows(xs_hbm, padrow_ref[e], padlen_ref[e]), sem.at[DISPATCH_SLOTS]))
            return carry

        def unused_block(b, carry):
            act(pltpu.make_async_copy(
                zero_blk, _rows(xs_hbm, b * blk_tok, blk_tok), sem.at[DISPATCH_SLOTS]))
            return carry

        lax.fori_loop(0, N_EXPERTS, pad_rows, 0)
        lax.fori_loop(nused_ref[0], n_blocks, unused_block, 0)

    @pl.when(i == 0)
    def _():
        zero_blk[...] = jnp.zeros_like(zero_blk)
        zero_fills(lambda copy: copy.start())

    def strips_done(tile):
        slot = tile % DISPATCH_SLOTS
        pltpu.make_async_copy(
            _rows(xring, slot * n_loc, n_loc), _rows(xs_hbm, 0, n_loc), sem.at[slot]).wait()

    @pl.when(i >= DISPATCH_SLOTS)
    def _():
        strips_done(i - DISPATCH_SLOTS)

    slot = i % DISPATCH_SLOTS
    iota_p = lax.broadcasted_iota(I32, (n_loc, t_rows), 0)
    pick = jnp.logical_or(iota_p == loc_ref[0:1, :], iota_p == loc_ref[1:2, :])
    xsorted = _dot(jnp.where(pick, 1.0, 0.0).astype(BF16), h_ref[...])
    _pack_rows(xring, xsorted, slot * n_loc, is_bf16_exact=True)
    _strip_copies(tab_ref, pstart_ref, xring, xs_hbm, sem.at[slot], to_hbm=True,
                  local_row0=slot * n_loc)

    @pl.when(i == n_steps - 1)
    def _():
        for k in range(DISPATCH_SLOTS):
            @pl.when(n_steps > k)
            def _():
                strips_done(n_steps - 1 - k)
        zero_fills(lambda copy: copy.wait())


def _dispatch(pad_start, pad_row, pad_len, n_used, tab, loc, h, n_blocks):
    n, d = h.shape
    t = MIX_ROWS
    grid_spec = pltpu.PrefetchScalarGridSpec(
        num_scalar_prefetch=4,
        grid=(n // t,),
        in_specs=[
            pl.BlockSpec((SUBLANES, LANES), lambda i, *_: (i, 0), memory_space=pltpu.SMEM),
            pl.BlockSpec((SUBLANES, t), lambda i, *_: (0, i)),
            pl.BlockSpec((t, d), lambda i, *_: (i, 0)),
        ],
        out_specs=pl.BlockSpec(memory_space=pl.ANY),
        scratch_shapes=[
            pltpu.VMEM(_packed_shape(DISPATCH_SLOTS * TOP_K * t), I32),
            pltpu.VMEM(_packed_shape(EXPERT_ROWS), I32),
            pltpu.SemaphoreType.DMA((DISPATCH_SLOTS + 1,)),
        ],
    )
    return pl.pallas_call(
        _dispatch_kernel,
        grid_spec=grid_spec,
        out_shape=jax.ShapeDtypeStruct(_packed_shape(n_blocks * EXPERT_ROWS), I32),
        compiler_params=pltpu.CompilerParams(
            dimension_semantics=("arbitrary",), vmem_limit_bytes=VMEM_LIMIT_BYTES,
            has_side_effects=True),
        name="dispatch",
    )(pad_start, pad_row, pad_len, n_used, tab, loc, h)


IN_SLOTS = 6
AHEAD = 4
OUT_SLOTS = 4


def _expert_kernel(blk0_ref, nblk_ref, wsel_ref, nused_ref, xs_hbm, wg_ref, wu_ref, wd_ref, ys_hbm,
                   xring, yring, sem_in, sem_out):
    del wsel_ref
    e = pl.program_id(0)
    blk_tok = xring.shape[0] // IN_SLOTS
    n_used = nused_ref[0]
    n_blk = nblk_ref[e]

    def in_copy(g):
        slot = g % IN_SLOTS
        return pltpu.make_async_copy(
            _rows(xs_hbm, g * blk_tok, blk_tok), _rows(xring, slot * blk_tok, blk_tok),
            sem_in.at[slot])

    def out_copy(g):
        slot = g % OUT_SLOTS
        return pltpu.make_async_copy(
            _rows(yring, slot * blk_tok, blk_tok), _rows(ys_hbm, g * blk_tok, blk_tok),
            sem_out.at[slot])

    def request(g):
        @pl.when(g < n_used)
        def _():
            in_copy(g).start()

    def retire(g):
        @pl.when(g >= OUT_SLOTS)
        def _():
            out_copy(g - OUT_SLOTS).wait()

    def mlp(*blocks):
        xbs = [_unpack_rows(xring, (g % IN_SLOTS) * blk_tok, blk_tok) for g in blocks]
        w_gate = wg_ref[0].astype(BF16)
        gates = [_dot(xb, w_gate) for xb in xbs]
        w_up = wu_ref[0].astype(BF16)
        ups = [_dot(xb, w_up) for xb in xbs]
        acts = [(gate * jax.nn.sigmoid(gate) * up).astype(BF16) for gate, up in zip(gates, ups)]
        w_down = wd_ref[0].astype(BF16)
        for g, act in zip(blocks, acts):
            _pack_rows(yring, _dot(act, w_down), (g % OUT_SLOTS) * blk_tok)

    @pl.when(e == 0)
    def _():
        for g in range(AHEAD):
            request(g)

    def pair(p, carry):
        g = blk0_ref[e] + 2 * p
        in_copy(g).wait()
        in_copy(g + 1).wait()
        request(g + AHEAD)
        request(g + AHEAD + 1)
        retire(g)
        retire(g + 1)
        mlp(g, g + 1)
        out_copy(g).start()
        out_copy(g + 1).start()
        return carry

    lax.fori_loop(0, n_blk // 2, pair, 0)

    @pl.when(n_blk % 2 == 1)
    def _():
        g = blk0_ref[e] + n_blk - 1
        in_copy(g).wait()
        request(g + AHEAD)
        retire(g)
        mlp(g)
        out_copy(g).start()

    @pl.when(e == pl.num_programs(0) - 1)
    def _():
        for k in range(1, OUT_SLOTS + 1):
            @pl.when(n_used >= k)
            def _():
                out_copy(n_used - k).wait()


def _experts(blk0, nblk, wsel, n_used, xs, w_gate, w_up, w_down):
    bm = EXPERT_ROWS
    n_exp, d, de = w_gate.shape
    wblk = lambda e, b0, nb, ws, nu: (ws[e], 0, 0)
    grid_spec = pltpu.PrefetchScalarGridSpec(
        num_scalar_prefetch=4,
        grid=(n_exp,),
        in_specs=[
            pl.BlockSpec(memory_space=pl.ANY),
            pl.BlockSpec((1, d, de), wblk),
            pl.BlockSpec((1, d, de), wblk),
            pl.BlockSpec((1, de, d), wblk),
        ],
        out_specs=pl.BlockSpec(memory_space=pl.ANY),
        scratch_shapes=[
            pltpu.VMEM(_packed_shape(IN_SLOTS * bm), I32),
            pltpu.VMEM(_packed_shape(OUT_SLOTS * bm), I32),
            pltpu.SemaphoreType.DMA((IN_SLOTS,)),
            pltpu.SemaphoreType.DMA((OUT_SLOTS,)),
        ],
    )
    n_prefetch = 4
    return pl.pallas_call(
        _expert_kernel,
        grid_spec=grid_spec,
        out_shape=jax.ShapeDtypeStruct(xs.shape, xs.dtype),
        input_output_aliases={n_prefetch: 0},
        compiler_params=pltpu.CompilerParams(
            dimension_semantics=("arbitrary",), vmem_limit_bytes=VMEM_LIMIT_BYTES,
            has_side_effects=True),
        name="experts",
    )(blk0, nblk, wsel, n_used, xs, w_gate, w_up, w_down)


COMBINE_SLOTS = 3


def _combine_kernel(pstart_ref, tab_ref, tab1_ref, tab2_ref, x1_ref, mod_ref, ln2g_ref, ln2b_ref,
                    ewcol_ref, ys_hbm, out_ref, yring, sem):
    i = pl.program_id(0)
    n_tiles = pl.num_programs(0)
    t_rows = x1_ref.shape[0]
    n_loc = TOP_K * t_rows
    ahead = COMBINE_SLOTS - 1

    def fetch(table_ref, tile):
        slot = tile % COMBINE_SLOTS
        _strip_copies(table_ref, pstart_ref, yring, ys_hbm, sem.at[slot], to_hbm=False,
                      local_row0=slot * n_loc)

    @pl.when(i == 0)
    def _():
        fetch(tab_ref, 0)

        @pl.when(n_tiles > 1)
        def _():
            fetch(tab1_ref, 1)

    @pl.when(i + ahead < n_tiles)
    def _():
        fetch(tab2_ref, i + ahead)

    slot = i % COMBINE_SLOTS
    pltpu.make_async_copy(
        _rows(ys_hbm, 0, n_loc), _rows(yring, slot * n_loc, n_loc), sem.at[slot]).wait()
    ew = ewcol_ref[...]
    lane = lax.broadcasted_iota(I32, (t_rows, n_loc), 1)
    weights = jnp.where(lane == ew[:, 2:3].astype(I32), ew[:, 0:1],
                        jnp.where(lane == ew[:, 3:4].astype(I32), ew[:, 1:2], 0.0))
    y = _dot(weights.astype(BF16), _unpack_rows(yring, slot * n_loc, n_loc))
    gate2 = mod_ref[0, 5:6, :]
    out_ref[...] = _layer_norm(DEEPNORM_ALPHA * x1_ref[...] + (1.0 + gate2) * y,
                               ln2g_ref[...], ln2b_ref[...])


def _combine(pad_start, tab, x1, mod, ln2_g, ln2_b, ewcol, ys, *, seq):
    n, d = x1.shape
    t = MIX_ROWS
    per_seq = seq // t
    n_steps = n // t
    row_blk = lambda i, ps: (i, 0)
    later = lambda k: (lambda i, ps: (jnp.minimum(i + k, n_steps - 1), 0))
    grid_spec = pltpu.PrefetchScalarGridSpec(
        num_scalar_prefetch=1,
        grid=(n_steps,),
        in_specs=[
            pl.BlockSpec((SUBLANES, LANES), row_blk, memory_space=pltpu.SMEM),
            pl.BlockSpec((SUBLANES, LANES), later(1), memory_space=pltpu.SMEM),
            pl.BlockSpec((SUBLANES, LANES), later(COMBINE_SLOTS - 1), memory_space=pltpu.SMEM),
            pl.BlockSpec((t, d), row_blk),
            pl.BlockSpec((1, N_MOD, d), lambda i, ps: (i // per_seq, 0, 0)),
            pl.BlockSpec((1, d), lambda i, ps: (0, 0)),
            pl.BlockSpec((1, d), lambda i, ps: (0, 0)),
            pl.BlockSpec((t, LANES), row_blk),
            pl.BlockSpec(memory_space=pl.ANY),
        ],
        out_specs=pl.BlockSpec((t, d), row_blk),
        scratch_shapes=[
            pltpu.VMEM(_packed_shape(COMBINE_SLOTS * TOP_K * t), I32),
            pltpu.SemaphoreType.DMA((COMBINE_SLOTS,)),
        ],
    )
    return pl.pallas_call(
        _combine_kernel,
        grid_spec=grid_spec,
        out_shape=jax.ShapeDtypeStruct((n, d), F32),
        compiler_params=pltpu.CompilerParams(
            dimension_semantics=("arbitrary",), vmem_limit_bytes=VMEM_LIMIT_BYTES),
        name="combine",
    )(pad_start, tab, tab, tab, x1, mod, ln2_g, ln2_b, ewcol, ys)


def _block_layout(counts, n_assign):
    bm = EXPERT_ROWS
    n_blocks = -(-n_assign // bm) + N_EXPERTS
    padded = (counts + bm - 1) // bm * bm
    pad_end = jnp.cumsum(padded)
    pad_start = (pad_end - padded).astype(I32)
    n_used = (pad_end[-1] // bm).astype(I32)
    pad_row = (pad_start + counts).astype(I32)
    pad_len = (padded - counts).astype(I32)
    ids = jnp.arange(N_EXPERTS, dtype=I32)
    prev_used = lax.cummax(jnp.where(padded > 0, ids, -1))
    first_used = jnp.argmax(padded > 0).astype(I32)
    weight_sel = jnp.where(prev_used >= 0, prev_used, first_used).astype(I32)
    return (pad_start, pad_row, pad_len, pad_start // bm, (padded // bm).astype(I32), weight_sel,
            n_used.reshape(1), n_blocks)


def kernel(x, c, w_ada, b_ada, w_in, conv_w, w_out_conv, w_pool, pool_scale, w_o, ln1_g, ln1_b,
           w_group, b_group, w_router, b_router, w_gate, w_up, w_down, ln2_g, ln2_b):
    bsz, seq, d = x.shape
    n = bsz * seq
    assert d == 2 * PACK_SUBLANES * LANES and seq % MIX_ROWS == 0 and w_ada.shape[0] == DEPTH
    assert conv_w.shape[1] == CONV_WIDTH <= HALO and max(POOL_WINDOWS) <= 2 * HALO
    assert w_gate.shape[1] == N_EXPERTS and w_router.shape[2] == N_EXPERTS
    x2d = x.reshape(n, d)
    c_pad = jnp.pad(c, ((0, -bsz % SUBLANES), (0, 0)))
    for l in range(DEPTH):
        mod = _ada(c_pad, w_ada[l], b_ada[l][None, :])[:bsz].reshape(bsz, N_MOD, d)
        wr = jnp.concatenate(
            [w_router[l], w_group[l],
             jnp.zeros((d, LANES - N_EXPERTS - N_GROUPS), F32)], axis=1)
        rb = jnp.concatenate(
            [b_router[l].reshape(-1), b_group[l],
             jnp.zeros((LANES - N_EXPERTS - N_GROUPS,), F32)])[:, None]
        x1, h, loc, ewcol, tab, cnt = _mix(
            x2d, mod, w_in[l].astype(BF16), conv_w[l], w_out_conv[l].astype(BF16),
            w_pool[l].astype(BF16), pool_scale[l][None, :], w_o[l].astype(BF16),
            ln1_g[l][None, :], ln1_b[l][None, :], wr, rb, batch=bsz, seq=seq)
        pad_start, pad_row, pad_len, blk0, nblk, weight_sel, n_used, n_blocks = _block_layout(
            cnt[:, 0], n * TOP_K)
        xs = _dispatch(pad_start, pad_row, pad_len, n_used, tab, loc, h, n_blocks)
        ys = _experts(blk0, nblk, weight_sel, n_used, xs, w_gate[l], w_up[l], w_down[l])
        x2d = _combine(pad_start, tab, x1, mod, ln2_g[l][None, :], ln2_b[l][None, :],
                       ewcol, ys, seq=seq)
    return x2d.reshape(bsz, seq, d)
```
